```python
import math
import jax
import jax.numpy as jnp
from jax import lax
import numpy as np

D_MODEL = 2048
BATCH = 4
SEQ = 2048
DEPTH = 2
DEC_BATCH = 32
DEC_SEQ = 1
PAST_LEN = 16384
PAGE_SIZE = 128

HEAD_DIM = 64
ROPE_THETA = 10000.0
MIX_WIDTH = D_MODEL
D_FF = ((8 * D_MODEL // 3 + 127) // 128) * 128
NORM_EPS = 1e-6
NEG_BIG = -1e30
LB_FLOOR = 1e-30

SSM_GROUP = 16
SSM_STATE = 64
SSM_WIDTH = 7 * D_MODEL // 32
SSM_GROUPS = SSM_WIDTH // SSM_GROUP
DT_MIN = 0.001
DT_MAX = 0.1

SWA_HEADS = D_MODEL // 256
SWA_KV_HEADS = SWA_HEADS // 4
SWA_GROUP = SWA_HEADS // SWA_KV_HEADS
SWA_WIDTH = SWA_HEADS * HEAD_DIM
SWA_KV_WIDTH = SWA_KV_HEADS * HEAD_DIM
SWA_WINDOW = 128

DIL_PAIRS = ((128, 1), (512, 4), (2048, 16))
DIL_GROUPS = 3
DIL_HEADS_PER_GROUP = 3
DIL_HEADS = DIL_GROUPS * DIL_HEADS_PER_GROUP
DIL_WIDTH = DIL_HEADS * HEAD_DIM
DIL_KV_WIDTH = DIL_GROUPS * HEAD_DIM

HGRN_HEAD_DIM = 128
HGRN_WIDTH = MIX_WIDTH - SSM_WIDTH - SWA_WIDTH - DIL_WIDTH
HGRN_HEADS = HGRN_WIDTH // HGRN_HEAD_DIM
HGRN_CHUNK = 16

IN_COLS = SSM_WIDTH + SWA_WIDTH + 2 * SWA_KV_WIDTH + DIL_WIDTH + 2 * DIL_KV_WIDTH + 4 * HGRN_WIDTH

kernel_name = 'hymba_s5_swa_dilated_hgrn2_macaron_step'


def rms_norm(x, g):
    xf = x.astype(jnp.float32)
    y = xf * lax.rsqrt(jnp.mean(xf * xf, axis=-1, keepdims=True) + NORM_EPS)
    return (y * g.astype(jnp.float32)).astype(x.dtype)


def swiglu(x, wg, wu, wd):
    return (jax.nn.silu(x @ wg) * (x @ wu)) @ wd


def rotary(x, pos):
    half = HEAD_DIM // 2
    inv_freq = ROPE_THETA ** (-jnp.arange(half, dtype=jnp.float32) / half)
    ang = pos.astype(jnp.float32)[:, None] * inv_freq[None, :]
    cos = jnp.cos(ang)[None, :, None, :]
    sin = jnp.sin(ang)[None, :, None, :]
    xf = x.astype(jnp.float32)
    x1, x2 = xf[..., :half], xf[..., half:]
    return jnp.concatenate([x1 * cos - x2 * sin, x2 * cos + x1 * sin], axis=-1).astype(x.dtype)


def sink_softmax(s, sink):
    m = jnp.max(s, axis=-1)
    if sink is not None:
        m = jnp.maximum(m, sink)
    p = jnp.exp(s - m[..., None])
    l = jnp.sum(p, axis=-1)
    if sink is not None:
        l = l + jnp.exp(sink - m)
    return p, l, m + jnp.log(l)


def banded_attention(q, k, v, window, sink=None):
    n, t, hkv, grp, hd = q.shape
    nb = -(-t // window)
    pad = nb * window - t
    q = jnp.pad(q, ((0, 0), (0, pad), (0, 0), (0, 0), (0, 0)))
    k = jnp.pad(k, ((0, 0), (0, pad), (0, 0), (0, 0)))
    v = jnp.pad(v, ((0, 0), (0, pad), (0, 0), (0, 0)))
    qb = q.reshape(n, nb, window, hkv, grp, hd).astype(jnp.float32)
    kb = k.reshape(n, nb, window, hkv, hd).astype(jnp.float32)
    vb = v.reshape(n, nb, window, hkv, hd).astype(jnp.float32)
    prev = ((0, 0), (1, 0), (0, 0), (0, 0), (0, 0))
    kk = jnp.concatenate([jnp.pad(kb, prev)[:, :-1], kb], axis=2)
    vv = jnp.concatenate([jnp.pad(vb, prev)[:, :-1], vb], axis=2)
    s = jnp.einsum('nbqhgd,nbkhd->nbhgqk', qb, kk) * (hd ** -0.5)
    qi = jnp.arange(window)[:, None]
    kj = jnp.arange(2 * window)[None, :]
    dist = window + qi - kj
    kpos = (jnp.arange(nb)[:, None, None] - 1) * window + kj[None]
    mask = (dist >= 0) & (dist <= window) & (kpos >= 0)
    s = jnp.where(mask[None, :, None, None], s, NEG_BIG)
    sink_b = None if sink is None else sink[None, None, :, :, None]
    p, l, lse = sink_softmax(s, sink_b)
    o = jnp.einsum('nbhgqk,nbkhd->nbqhgd', p, vv) / l.transpose(0, 1, 4, 2, 3)[..., None]
    o = o.reshape(n, nb * window, hkv, grp, hd)[:, :t]
    lse = lse.transpose(0, 1, 4, 2, 3).reshape(n, nb * window, hkv, grp)[:, :t]
    return o, lse


def gathered_attention(q, k_all, v_all, q_start, window, dilation, sink=None):
    tq, hd = q.shape[1], q.shape[-1]
    nk = window // dilation + 1
    idx = q_start + jnp.arange(tq)[:, None] - dilation * jnp.arange(nk)[None, :]
    valid = idx >= 0
    idx = jnp.maximum(idx, 0)
    kg = k_all[:, idx].astype(jnp.float32)
    vg = v_all[:, idx].astype(jnp.float32)
    s = jnp.einsum('nqhgd,nqkhd->nhgqk', q.astype(jnp.float32), kg) * (hd ** -0.5)
    s = jnp.where(valid[None, None, None], s, NEG_BIG)
    sink_b = None if sink is None else sink[None, :, :, None]
    p, l, lse = sink_softmax(s, sink_b)
    o = jnp.einsum('nhgqk,nqkhd->nqhgd', p, vg) / l.transpose(0, 3, 1, 2)[..., None]
    return o, lse.transpose(0, 3, 1, 2)


def s5_scan(u, x0, a_re, a_im, log_dt, b_re, b_im, c_re, c_im, d_skip):
    f32 = jnp.float32
    a_re, a_im = a_re.astype(f32), a_im.astype(f32)
    dt = jnp.exp(log_dt.astype(f32))[:, None]
    mag = jnp.exp(a_re * dt)
    lam_re, lam_im = mag * jnp.cos(a_im * dt), mag * jnp.sin(a_im * dt)
    den = a_re * a_re + a_im * a_im
    z_re = ((lam_re - 1.0) * a_re + lam_im * a_im) / den
    z_im = (lam_im * a_re - (lam_re - 1.0) * a_im) / den
    b_re, b_im = b_re.astype(f32), b_im.astype(f32)
    bb_re = z_re[..., None] * b_re - z_im[..., None] * b_im
    bb_im = z_re[..., None] * b_im + z_im[..., None] * b_re
    bu_re = jnp.einsum('ntgj,gsj->ntgs', u, bb_re)
    bu_im = jnp.einsum('ntgj,gsj->ntgs', u, bb_im)
    if x0 is not None:
        bu_re = bu_re.at[:, 0].add(lam_re * x0[..., 0] - lam_im * x0[..., 1])
        bu_im = bu_im.at[:, 0].add(lam_re * x0[..., 1] + lam_im * x0[..., 0])
    ar = jnp.broadcast_to(lam_re, bu_re.shape)
    ai = jnp.broadcast_to(lam_im, bu_re.shape)

    def combine(e1, e2):
        ar1, ai1, br1, bi1 = e1
        ar2, ai2, br2, bi2 = e2
        return (ar1 * ar2 - ai1 * ai2, ar1 * ai2 + ai1 * ar2,
                ar2 * br1 - ai2 * bi1 + br2, ar2 * bi1 + ai2 * br1 + bi2)

    _, _, xr, xi = lax.associative_scan(combine, (ar, ai, bu_re, bu_im), axis=1)
    y = (jnp.einsum('ntgs,gjs->ntgj', xr, c_re.astype(f32))
         - jnp.einsum('ntgs,gjs->ntgj', xi, c_im.astype(f32))
         + d_skip.astype(f32) * u)
    return y, jnp.stack([xr[:, -1], xi[:, -1]], axis=-1)


def hgrn2_recurrence(q, k, v, logf, s0):
    n, t, h, _ = q.shape
    nc = -(-t // HGRN_CHUNK)
    pad = nc * HGRN_CHUNK - t

    def chunks(a):
        a = jnp.pad(a, ((0, 0), (0, pad), (0, 0), (0, 0)))
        return jnp.moveaxis(a.reshape(n, nc, HGRN_CHUNK, h, a.shape[-1]), 1, 0)

    tri = jnp.tril(jnp.ones((HGRN_CHUNK, HGRN_CHUNK), dtype=bool))

    def step(S, inp):
        qc, kc, vc, gc = inp
        b = jnp.cumsum(gc, axis=1)
        diff = b[:, :, None] - b[:, None, :]
        dec = jnp.exp(jnp.where(tri[None, :, :, None, None], diff, NEG_BIG))
        att = jnp.einsum('nthk,nshk,ntshk->nhts', qc, kc, dec)
        o = (jnp.einsum('nhts,nshv->nthv', att, vc)
             + jnp.einsum('nthk,nhkv->nthv', qc * jnp.exp(b), S))
        bl = b[:, -1]
        S = (jnp.exp(bl)[..., None] * S
             + jnp.einsum('nshk,nshv->nhkv', kc * jnp.exp(bl[:, None] - b), vc))
        return S, o

    S, o = lax.scan(step, s0, (chunks(q), chunks(k), chunks(v), chunks(logf)))
    o = jnp.moveaxis(o, 0, 1).reshape(n, nc * HGRN_CHUNK, h, v.shape[-1])[:, :t]
    return o, S


def token_mixing(xn, pos, lp, lower_bound, cache):
    n, t, _ = xn.shape
    f32 = jnp.float32
    sizes = (SSM_WIDTH, SWA_WIDTH, SWA_KV_WIDTH, SWA_KV_WIDTH, DIL_WIDTH, DIL_KV_WIDTH, DIL_KV_WIDTH,
             HGRN_WIDTH, HGRN_WIDTH, HGRN_WIDTH, HGRN_WIDTH)
    cuts, acc = [], 0
    for s in sizes[:-1]:
        acc += s
        cuts.append(acc)
    u, q_b, k_b, v_b, q_c, k_c, v_c, q_d, f_d, i_d, g_d = jnp.split(xn @ lp['w_in'], cuts, axis=-1)

    y_a, ssm_new = s5_scan(u.astype(f32).reshape(n, t, SSM_GROUPS, SSM_GROUP),
                           None if cache is None else cache[0].astype(f32),
                           lp['ssm_a_re'], lp['ssm_a_im'], lp['ssm_log_dt'], lp['ssm_b_re'], lp['ssm_b_im'],
                           lp['ssm_c_re'], lp['ssm_c_im'], lp['ssm_d'])
    z = jax.nn.gelu(y_a.reshape(n, t, SSM_WIDTH))
    out_a = rms_norm(z * jax.nn.sigmoid(z @ lp['ssm_w_glu'].astype(f32) + lp['ssm_b_glu'].astype(f32)),
                     lp['out_norm_a'])

    qb = rotary(q_b.reshape(n, t, SWA_HEADS, HEAD_DIM), pos).reshape(n, t, SWA_KV_HEADS, SWA_GROUP, HEAD_DIM)
    kb = rotary(k_b.reshape(n, t, SWA_KV_HEADS, HEAD_DIM), pos)
    vb = v_b.reshape(n, t, SWA_KV_HEADS, HEAD_DIM)
    sink = lp['swa_sinks'].astype(f32).reshape(SWA_KV_HEADS, SWA_GROUP)
    if cache is None:
        ob, _ = banded_attention(qb, kb, vb, SWA_WINDOW, sink)
        keep = min(SWA_WINDOW, t)
        swa_new = jnp.stack([kb[:, t - keep:], vb[:, t - keep:]], axis=2)
    else:
        buf = cache[1]
        k_all = jnp.concatenate([buf[:, :, 0].astype(kb.dtype), kb], axis=1)
        v_all = jnp.concatenate([buf[:, :, 1].astype(vb.dtype), vb], axis=1)
        ob, _ = gathered_attention(qb, k_all, v_all, buf.shape[1], SWA_WINDOW, 1, sink)
        swa_new = jnp.stack([kb, vb], axis=2)
    out_b = rms_norm(ob.reshape(n, t, SWA_WIDTH), lp['out_norm_b'])

    qc = rotary(q_c.reshape(n, t, DIL_HEADS, HEAD_DIM), pos).reshape(n, t, DIL_GROUPS, DIL_HEADS_PER_GROUP, HEAD_DIM)
    kc = rotary(k_c.reshape(n, t, DIL_GROUPS, HEAD_DIM), pos)
    vc = v_c.reshape(n, t, DIL_GROUPS, HEAD_DIM)
    outs, lses, dil_new = [], [], []
    for gi, (win, dil) in enumerate(DIL_PAIRS):
        qg, kg, vg = qc[:, :, gi:gi + 1], kc[:, :, gi:gi + 1], vc[:, :, gi:gi + 1]
        if cache is None:
            def sub(a):
                a = a.reshape((n, t // dil, dil) + a.shape[2:])
                return jnp.moveaxis(a, 2, 1).reshape((n * dil, t // dil) + a.shape[3:])

            def unsub(a):
                a = a.reshape((n, dil, t // dil) + a.shape[2:])
                return jnp.moveaxis(a, 1, 2).reshape((n, t) + a.shape[3:])

            o, lse = banded_attention(sub(qg), sub(kg), sub(vg), win // dil)
            o, lse = unsub(o), unsub(lse)
            keep = min(win, t)
            dil_new.append(jnp.stack([kg[:, t - keep:, 0], vg[:, t - keep:, 0]], axis=2))
        else:
            buf = cache[2 + gi]
            k_all = jnp.concatenate([buf[:, :, 0][:, :, None].astype(kg.dtype), kg], axis=1)
            v_all = jnp.concatenate([buf[:, :, 1][:, :, None].astype(vg.dtype), vg], axis=1)
            o, lse = gathered_attention(qg, k_all, v_all, buf.shape[1], win, dil)
            dil_new.append(jnp.stack([kg[:, :, 0], vg[:, :, 0]], axis=2))
        outs.append(o[:, :, 0])
        lses.append(lse[:, :, 0])
    wts = jax.nn.softmax(jnp.stack(lses, axis=0), axis=0)
    oc = jnp.moveaxis(jnp.stack(outs, axis=0) * wts[..., None], 0, 2).reshape(n, t, DIL_WIDTH)
    out_c = rms_norm(oc, lp['out_norm_c'])

    lb = lower_bound.astype(f32).reshape(HGRN_HEADS, HGRN_HEAD_DIM)
    fpre = f_d.astype(f32).reshape(n, t, HGRN_HEADS, HGRN_HEAD_DIM)
    log_f = jnp.logaddexp(jnp.log(jnp.maximum(lb, LB_FLOOR)), jnp.log1p(-lb) + jax.nn.log_sigmoid(fpre))
    k_d = (1.0 - lb) * jax.nn.sigmoid(-fpre)
    q_dd = jax.nn.silu(q_d.astype(f32)).reshape(n, t, HGRN_HEADS, HGRN_HEAD_DIM)
    v_d = i_d.astype(f32).reshape(n, t, HGRN_HEADS, HGRN_HEAD_DIM)
    if cache is None:
        s0 = jnp.zeros((n, HGRN_HEADS, HGRN_HEAD_DIM, HGRN_HEAD_DIM), f32)
    else:
        s0 = cache[5].astype(f32)
    o_d, hgrn_new = hgrn2_recurrence(q_dd, k_d, v_d, log_f, s0)
    o_d = o_d * lax.rsqrt(jnp.mean(o_d * o_d, axis=-1, keepdims=True) + NORM_EPS)
    out_d = o_d.reshape(n, t, HGRN_WIDTH) * lp['out_norm_d'].astype(f32) * jax.nn.silu(g_d.astype(f32))

    mixed = jnp.concatenate([out_a.astype(xn.dtype), out_b.astype(xn.dtype),
                             out_c.astype(xn.dtype), out_d.astype(xn.dtype)], axis=-1)
    new_state = (ssm_new, swa_new, dil_new[0], dil_new[1], dil_new[2], hgrn_new)
    return mixed @ lp['w_out'], new_state


def trunk_layer(h, pos, lp, lower_bound, cache):
    h = h + 0.5 * rms_norm(swiglu(rms_norm(h, lp['ffn1_norm_pre']), lp['ffn1_w_gate'], lp['ffn1_w_up'],
                                  lp['ffn1_w_down']), lp['ffn1_norm_post'])
    mix, new_state = token_mixing(rms_norm(h, lp['mix_norm_pre']), pos, lp, lower_bound, cache)
    h = h + rms_norm(mix, lp['mix_norm_post'])
    h = h + 0.5 * rms_norm(swiglu(rms_norm(h, lp['ffn2_norm_pre']), lp['ffn2_w_gate'], lp['ffn2_w_up'],
                                  lp['ffn2_w_down']), lp['ffn2_norm_post'])
    return h, new_state


def setup_inputs(seed: int = 0) -> dict:
    key = jax.random.key(seed)
    keys = iter(jax.random.split(key, 64))
    f32 = jnp.float32

    def nrm(shape, scale):
        return scale * jax.random.normal(next(keys), shape, f32)

    def gain(width):
        return 1.0 + nrm((DEPTH, width), 0.05)

    inp = {}
    inp['x_prompt'] = nrm((BATCH, SEQ, D_MODEL), 1.0)
    inp['x_sample'] = nrm((DEC_BATCH, DEC_SEQ, D_MODEL), 1.0)
    inp['state_ssm'] = nrm((DEPTH, DEC_BATCH, SSM_GROUPS, SSM_STATE, 2), 0.1)
    inp['cache_swa_kv'] = nrm((DEPTH, DEC_BATCH, min(SWA_WINDOW, PAST_LEN), 2, SWA_KV_HEADS, HEAD_DIM), 1.0)
    inp['cache_dil0_kv'] = nrm((DEPTH, DEC_BATCH, min(DIL_PAIRS[0][0], PAST_LEN), 2, HEAD_DIM), 1.0)
    inp['cache_dil1_kv'] = nrm((DEPTH, DEC_BATCH, min(DIL_PAIRS[1][0], PAST_LEN), 2, HEAD_DIM), 1.0)
    inp['cache_dil2_kv'] = nrm((DEPTH, DEC_BATCH, min(DIL_PAIRS[2][0], PAST_LEN), 2, HEAD_DIM), 1.0)
    inp['state_hgrn'] = nrm((DEPTH, DEC_BATCH, HGRN_HEADS, HGRN_HEAD_DIM, HGRN_HEAD_DIM), 0.5)
    inp['ffn1_norm_pre'] = gain(D_MODEL)
    inp['ffn1_w_gate'] = nrm((DEPTH, D_MODEL, D_FF), D_MODEL ** -0.5)
    inp['ffn1_w_up'] = nrm((DEPTH, D_MODEL, D_FF), D_MODEL ** -0.5)
    inp['ffn1_w_down'] = nrm((DEPTH, D_FF, D_MODEL), D_FF ** -0.5)
    inp['ffn1_norm_post'] = gain(D_MODEL)
    inp['mix_norm_pre'] = gain(D_MODEL)
    inp['w_in'] = nrm((DEPTH, D_MODEL, IN_COLS), D_MODEL ** -0.5)
    inp['ssm_a_re'] = -0.5 + nrm((DEPTH, SSM_GROUPS, SSM_STATE), 0.01)
    inp['ssm_a_im'] = (math.pi * jnp.arange(SSM_STATE, dtype=f32))[None, None, :] + nrm((DEPTH, SSM_GROUPS, SSM_STATE), 0.01)
    inp['ssm_log_dt'] = jax.random.uniform(next(keys), (DEPTH, SSM_GROUPS), f32, math.log(DT_MIN), math.log(DT_MAX))
    inp['ssm_b_re'] = nrm((DEPTH, SSM_GROUPS, SSM_STATE, SSM_GROUP), SSM_GROUP ** -0.5)
    inp['ssm_b_im'] = nrm((DEPTH, SSM_GROUPS, SSM_STATE, SSM_GROUP), SSM_GROUP ** -0.5)
    inp['ssm_c_re'] = nrm((DEPTH, SSM_GROUPS, SSM_GROUP, SSM_STATE), SSM_STATE ** -0.5)
    inp['ssm_c_im'] = nrm((DEPTH, SSM_GROUPS, SSM_GROUP, SSM_STATE), SSM_STATE ** -0.5)
    inp['ssm_d'] = nrm((DEPTH, SSM_GROUPS, SSM_GROUP), 1.0)
    inp['ssm_w_glu'] = nrm((DEPTH, SSM_WIDTH, SSM_WIDTH), SSM_WIDTH ** -0.5)
    inp['ssm_b_glu'] = nrm((DEPTH, SSM_WIDTH), 0.01)
    inp['swa_sinks'] = nrm((DEPTH, SWA_HEADS), 0.5)
    inp['hgrn_lower_bounds'] = nrm((DEPTH, HGRN_WIDTH), 0.5)
    inp['out_norm_a'] = gain(SSM_WIDTH)
    inp['out_norm_b'] = gain(SWA_WIDTH)
    inp['out_norm_c'] = gain(DIL_WIDTH)
    inp['out_norm_d'] = gain(HGRN_WIDTH)
    inp['w_out'] = nrm((DEPTH, MIX_WIDTH, D_MODEL), MIX_WIDTH ** -0.5)
    inp['mix_norm_post'] = gain(D_MODEL)
    inp['ffn2_norm_pre'] = gain(D_MODEL)
    inp['ffn2_w_gate'] = nrm((DEPTH, D_MODEL, D_FF), D_MODEL ** -0.5)
    inp['ffn2_w_up'] = nrm((DEPTH, D_MODEL, D_FF), D_MODEL ** -0.5)
    inp['ffn2_w_down'] = nrm((DEPTH, D_FF, D_MODEL), D_FF ** -0.5)
    inp['ffn2_norm_post'] = gain(D_MODEL)
    return inp


def reference(x_prompt, x_sample, state_ssm, cache_swa_kv, cache_dil0_kv, cache_dil1_kv, cache_dil2_kv,
              state_hgrn, ffn1_norm_pre, ffn1_w_gate, ffn1_w_up, ffn1_w_down, ffn1_norm_post, mix_norm_pre,
              w_in, ssm_a_re, ssm_a_im, ssm_log_dt, ssm_b_re, ssm_b_im, ssm_c_re, ssm_c_im, ssm_d, ssm_w_glu,
              ssm_b_glu, swa_sinks, hgrn_lower_bounds, out_norm_a, out_norm_b, out_norm_c, out_norm_d, w_out,
              mix_norm_post, ffn2_norm_pre, ffn2_w_gate, ffn2_w_up, ffn2_w_down, ffn2_norm_post):
    lbp = jax.nn.softmax(hgrn_lower_bounds.astype(jnp.float32), axis=0)
    lower = jnp.cumsum(lbp, axis=0) - lbp[0:1]
    pos_p = jnp.arange(SEQ, dtype=jnp.int32)
    pos_s = PAST_LEN + jnp.arange(DEC_SEQ, dtype=jnp.int32)
    h_p, h_s = x_prompt, x_sample
    new_p = [[] for _ in range(6)]
    new_s = [[] for _ in range(6)]
    for l in range(DEPTH):
        lp = {
            'ffn1_norm_pre': ffn1_norm_pre[l], 'ffn1_w_gate': ffn1_w_gate[l], 'ffn1_w_up': ffn1_w_up[l],
            'ffn1_w_down': ffn1_w_down[l], 'ffn1_norm_post': ffn1_norm_post[l], 'mix_norm_pre': mix_norm_pre[l],
            'w_in': w_in[l], 'ssm_a_re': ssm_a_re[l], 'ssm_a_im': ssm_a_im[l], 'ssm_log_dt': ssm_log_dt[l],
            'ssm_b_re': ssm_b_re[l], 'ssm_b_im': ssm_b_im[l], 'ssm_c_re': ssm_c_re[l], 'ssm_c_im': ssm_c_im[l],
            'ssm_d': ssm_d[l], 'ssm_w_glu': ssm_w_glu[l], 'ssm_b_glu': ssm_b_glu[l], 'swa_sinks': swa_sinks[l],
            'out_norm_a': out_norm_a[l], 'out_norm_b': out_norm_b[l], 'out_norm_c': out_norm_c[l],
            'out_norm_d': out_norm_d[l], 'w_out': w_out[l], 'mix_norm_post': mix_norm_post[l],
            'ffn2_norm_pre': ffn2_norm_pre[l], 'ffn2_w_gate': ffn2_w_gate[l], 'ffn2_w_up': ffn2_w_up[l],
            'ffn2_w_down': ffn2_w_down[l], 'ffn2_norm_post': ffn2_norm_post[l],
        }
        h_p, st_p = trunk_layer(h_p, pos_p, lp, lower[l], None)
        cache = (state_ssm[l], cache_swa_kv[l], cache_dil0_kv[l], cache_dil1_kv[l], cache_dil2_kv[l], state_hgrn[l])
        h_s, st_s = trunk_layer(h_s, pos_s, lp, lower[l], cache)
        for i in range(6):
            new_p[i].append(st_p[i])
            new_s[i].append(st_s[i])
    new_p = [jnp.stack(a, axis=0) for a in new_p]
    new_s = [jnp.stack(a, axis=0) for a in new_s]
    return (h_p, h_s, new_p[0], new_s[0], new_p[1], new_s[1], new_p[2], new_s[2],
            new_p[3], new_s[3], new_p[4], new_s[4], new_p[5], new_s[5])
```

```python
import functools
import math

import numpy as np
import jax
import jax.numpy as jnp
from jax import lax
from jax.experimental import pallas as pl
from jax.experimental.pallas import tpu as pltpu

F32 = jnp.float32
BF16 = jnp.bfloat16

D_MODEL = 2048
SEQ = 2048
DEPTH = 2
PAST_LEN = 16384
HEAD_DIM = 64
ROPE_THETA = 10000.0
D_FF = 5504
NORM_EPS = 1e-6
NEG_BIG = -1e30
LB_FLOOR = 1e-30
SSM_GROUP = 16
SSM_STATE = 64
SSM_WIDTH = 448
SSM_GROUPS = 28
SWA_HEADS = 8
SWA_KV_HEADS = 2
SWA_GROUP = 4
SWA_WIDTH = 512
SWA_KV_WIDTH = 128
WINDOW = 128
DIL_PAIRS = ((128, 1), (512, 4), (2048, 16))
DIL_HPG = 3
DIL_GW = DIL_HPG * HEAD_DIM
DIL_WIDTH = 576
DIL_KV_WIDTH = 192
HGRN_HEAD_DIM = 128
HGRN_WIDTH = 512
HGRN_HEADS = 4
IN_COLS = 4224
ATTN_SCALE = HEAD_DIM ** -0.5

LANES = 128
FF_TILE = 512
D_FF_PAD = ((D_FF + FF_TILE - 1) // FF_TILE) * FF_TILE
ROT_COLS = SWA_WIDTH + SWA_KV_WIDTH + DIL_WIDTH + DIL_KV_WIDTH
WIN_TILE = ROT_COLS
VMEM_LIMIT = 56 * 1024 * 1024

OFF_QB = 0
OFF_KB = OFF_QB + SWA_WIDTH
OFF_QC = OFF_KB + SWA_KV_WIDTH
OFF_KC = OFF_QC + DIL_WIDTH
OFF_U = OFF_KC + DIL_KV_WIDTH
OFF_VB = OFF_U + SSM_WIDTH
OFF_VC = OFF_VB + SWA_KV_WIDTH
OFF_QD = OFF_VC + DIL_KV_WIDTH
OFF_FD = OFF_QD + HGRN_WIDTH
OFF_ID = OFF_FD + HGRN_WIDTH
OFF_GD = OFF_ID + HGRN_WIDTH

S5_CHUNK = 16
S5_NPOW = S5_CHUNK + 1
HGRN_CHUNK = 128
HGRN_LEVELS = (64, 32, 16, 8, 4, 2, 1)


def _cparams(sem):
    return pltpu.CompilerParams(dimension_semantics=sem, vmem_limit_bytes=VMEM_LIMIT)


def _rms(x, g):
    return x * lax.rsqrt(jnp.mean(x * x, axis=-1, keepdims=True) + NORM_EPS) * g


def _sigmoid(x):
    return 1.0 / (1.0 + jnp.exp(-x))


def _silu(x):
    return x * _sigmoid(x)


def _dot(a, b):
    return jnp.dot(a, b, preferred_element_type=F32)


def _dot_nt(a, b):
    return lax.dot_general(a, b, (((1,), (1,)), ((), ())), preferred_element_type=F32)


def _dot_tn(a, b):
    return lax.dot_general(a, b, (((0,), (0,)), ((), ())), preferred_element_type=F32)


def _ffn_kernel(x_ref, gpre_ref, wg_ref, wu_ref, wd_ref, gpost_ref, o_ref, xn_ref, acc_ref):
    f = pl.program_id(1)

    @pl.when(f == 0)
    def _():
        xn_ref[...] = _rms(x_ref[...], gpre_ref[...]).astype(BF16)
        acc_ref[...] = jnp.zeros_like(acc_ref)

    xn = xn_ref[...]
    gate = _dot(xn, wg_ref[...])
    up = _dot(xn, wu_ref[...])
    acc_ref[...] += _dot((_silu(gate) * up).astype(BF16), wd_ref[...])

    @pl.when(f == pl.num_programs(1) - 1)
    def _():
        o_ref[...] = x_ref[...] + 0.5 * _rms(acc_ref[...], gpost_ref[...])


def _ffn(h, gpre, wg, wu, wd, gpost, tm):
    m = h.shape[0]
    nf = D_FF_PAD // FF_TILE
    return pl.pallas_call(
        _ffn_kernel,
        grid=(m // tm, nf),
        in_specs=[
            pl.BlockSpec((tm, D_MODEL), lambda i, f: (i, 0)),
            pl.BlockSpec((1, D_MODEL), lambda i, f: (0, 0)),
            pl.BlockSpec((D_MODEL, FF_TILE), lambda i, f: (0, f)),
            pl.BlockSpec((D_MODEL, FF_TILE), lambda i, f: (0, f)),
            pl.BlockSpec((FF_TILE, D_MODEL), lambda i, f: (f, 0)),
            pl.BlockSpec((1, D_MODEL), lambda i, f: (0, 0)),
        ],
        out_specs=pl.BlockSpec((tm, D_MODEL), lambda i, f: (i, 0)),
        out_shape=jax.ShapeDtypeStruct((m, D_MODEL), F32),
        scratch_shapes=[pltpu.VMEM((tm, D_MODEL), BF16), pltpu.VMEM((tm, D_MODEL), F32)],
        compiler_params=_cparams(("parallel", "arbitrary")),
    )(h, gpre, wg, wu, wd, gpost)


def _win_kernel(x_ref, g_ref, w_ref, cos_ref, sin_ref, o_ref, xn_ref):
    j = pl.program_id(1)

    @pl.when(j == 0)
    def _():
        xn_ref[...] = _rms(x_ref[...], g_ref[...]).astype(BF16)

    y = _dot(xn_ref[...], w_ref[...])

    @pl.when(j == 0)
    def _():
        cos = cos_ref[...]
        sin = sin_ref[...]
        first_half = (lax.broadcasted_iota(jnp.int32, cos.shape, 1) % HEAD_DIM) < (HEAD_DIM // 2)
        for c in range(WIN_TILE // LANES):
            x = y[:, c * LANES:(c + 1) * LANES]
            partner = jnp.where(first_half,
                                pltpu.roll(x, LANES - HEAD_DIM // 2, axis=1),
                                pltpu.roll(x, HEAD_DIM // 2, axis=1))
            o_ref[:, c * LANES:(c + 1) * LANES] = x * cos + partner * sin

    @pl.when(j != 0)
    def _():
        o_ref[...] = y


def _win(h, g, w, cos_t, sin_t, tm):
    m = h.shape[0]
    return pl.pallas_call(
        _win_kernel,
        grid=(m // tm, IN_COLS // WIN_TILE),
        in_specs=[
            pl.BlockSpec((tm, D_MODEL), lambda i, j: (i, 0)),
            pl.BlockSpec((1, D_MODEL), lambda i, j: (0, 0)),
            pl.BlockSpec((D_MODEL, WIN_TILE), lambda i, j: (0, j)),
            pl.BlockSpec((tm, LANES), lambda i, j: (i, 0)),
            pl.BlockSpec((tm, LANES), lambda i, j: (i, 0)),
        ],
        out_specs=pl.BlockSpec((tm, WIN_TILE), lambda i, j: (i, j)),
        out_shape=jax.ShapeDtypeStruct((m, IN_COLS), F32),
        scratch_shapes=[pltpu.VMEM((tm, D_MODEL), BF16)],
        compiler_params=_cparams(("parallel", "arbitrary")),
    )(h, g, w, cos_t, sin_t)


def _rope_tables(pos):
    half = HEAD_DIM // 2
    inv_freq = ROPE_THETA ** (-jnp.arange(half, dtype=F32) / half)
    ang = pos.astype(F32)[:, None] * inv_freq[None, :]
    cos, sin = jnp.cos(ang), jnp.sin(ang)
    reps = LANES // HEAD_DIM
    cos_t = jnp.tile(jnp.concatenate([cos, cos], axis=1), (1, reps))
    sin_t = jnp.tile(jnp.concatenate([-sin, sin], axis=1), (1, reps))
    return cos_t, sin_t


def _band_attn_kernel(q_ref, kc_ref, kp_ref, vc_ref, vp_ref, sink_ref, o_ref, lse_ref, *, group, use_sink):
    i = pl.program_id(1)
    w = WINDOW
    q = q_ref[0]
    kk = jnp.concatenate([kp_ref[0], kc_ref[0]], axis=0).astype(BF16)
    vv = jnp.concatenate([vp_ref[0], vc_ref[0]], axis=0).astype(BF16)
    r = lax.broadcasted_iota(jnp.int32, (w, 2 * w), 0)
    c = lax.broadcasted_iota(jnp.int32, (w, 2 * w), 1)
    first_key = jnp.where(i == 0, w, 0)
    valid = (c >= r) & (c <= r + w) & (c >= first_key)
    outs, lses = [], []
    for g in range(group):
        qg = q[:, g * HEAD_DIM:(g + 1) * HEAD_DIM].astype(BF16)
        s = _dot_nt(qg, kk) * ATTN_SCALE
        s = jnp.where(valid, s, NEG_BIG)
        m = jnp.max(s, axis=-1, keepdims=True)
        if use_sink:
            sk = sink_ref[0][:, g * HEAD_DIM:g * HEAD_DIM + 1]
            m = jnp.maximum(m, sk)
        p = jnp.exp(s - m)
        l = jnp.sum(p, axis=-1, keepdims=True)
        if use_sink:
            l = l + jnp.exp(sk - m)
        outs.append(_dot(p.astype(BF16), vv) / l)
        lses.append(jnp.broadcast_to(m + jnp.log(l), (w, HEAD_DIM)))
    o_ref[0] = jnp.concatenate(outs, axis=1)
    lse_ref[0] = jnp.concatenate(lses, axis=1)


def _band_attn(q, k, v, sink, group, use_sink):
    nb, t, gw = q.shape
    ns = sink.shape[0]
    cur = lambda b, i: (b, i, 0)
    prev = lambda b, i: (b, jnp.maximum(i - 1, 0), 0)
    return pl.pallas_call(
        functools.partial(_band_attn_kernel, group=group, use_sink=use_sink),
        grid=(nb, t // WINDOW),
        in_specs=[
            pl.BlockSpec((1, WINDOW, gw), cur),
            pl.BlockSpec((1, WINDOW, HEAD_DIM), cur),
            pl.BlockSpec((1, WINDOW, HEAD_DIM), prev),
            pl.BlockSpec((1, WINDOW, HEAD_DIM), cur),
            pl.BlockSpec((1, WINDOW, HEAD_DIM), prev),
            pl.BlockSpec((1, 1, gw), lambda b, i: (b % ns, 0, 0)),
        ],
        out_specs=[pl.BlockSpec((1, WINDOW, gw), cur), pl.BlockSpec((1, WINDOW, gw), cur)],
        out_shape=[jax.ShapeDtypeStruct((nb, t, gw), F32), jax.ShapeDtypeStruct((nb, t, gw), F32)],
        compiler_params=_cparams(("parallel", "arbitrary")),
    )(q, k, k, v, v, sink)


def _decode_attn_kernel(q_ref, kn_ref, vn_ref, cache_ref, sink_ref, o_ref, lse_ref, *,
                        bn, kv_heads, group, k_off, v_off, use_sink):
    for b in range(bn):
        q = q_ref[b]
        kn = kn_ref[b]
        vn = vn_ref[b]
        cache = cache_ref[b]
        o_rows, l_rows = [], []
        for h in range(kv_heads):
            kc = cache[:, k_off[h]:k_off[h] + HEAD_DIM]
            vc = cache[:, v_off[h]:v_off[h] + HEAD_DIM]
            knh = kn[:, h * HEAD_DIM:(h + 1) * HEAD_DIM]
            vnh = vn[:, h * HEAD_DIM:(h + 1) * HEAD_DIM]
            base = h * group * HEAD_DIM
            qh = jnp.concatenate([q[:, base + g * HEAD_DIM:base + (g + 1) * HEAD_DIM] for g in range(group)], axis=0)
            s = _dot_nt(qh.astype(BF16), kc.astype(BF16)) * ATTN_SCALE
            s_new = jnp.sum(qh * knh, axis=-1, keepdims=True) * ATTN_SCALE
            m = jnp.maximum(jnp.max(s, axis=-1, keepdims=True), s_new)
            if use_sink:
                sk = jnp.concatenate([sink_ref[:, base + g * HEAD_DIM:base + g * HEAD_DIM + 1] for g in range(group)], axis=0)
                m = jnp.maximum(m, sk)
            p = jnp.exp(s - m)
            p_new = jnp.exp(s_new - m)
            l = jnp.sum(p, axis=-1, keepdims=True) + p_new
            if use_sink:
                l = l + jnp.exp(sk - m)
            o = (_dot(p.astype(BF16), vc.astype(BF16)) + p_new * vnh) / l
            lse = jnp.broadcast_to(m + jnp.log(l), (group, HEAD_DIM))
            o_rows += [o[g:g + 1] for g in range(group)]
            l_rows += [lse[g:g + 1] for g in range(group)]
        o_ref[b] = jnp.concatenate(o_rows, axis=1)
        lse_ref[b] = jnp.concatenate(l_rows, axis=1)


def _decode_attn(q, kn, vn, cache, sink, kv_heads, group, k_off, v_off, use_sink, cache_lanes):
    n = q.shape[0]
    bn = 8
    qw = q.shape[2]
    kw = kn.shape[2]
    blk = lambda i: (i, 0, 0)
    return pl.pallas_call(
        functools.partial(_decode_attn_kernel, bn=bn, kv_heads=kv_heads, group=group, k_off=k_off, v_off=v_off,
                          use_sink=use_sink),
        grid=(n // bn,),
        in_specs=[
            pl.BlockSpec((bn, 1, qw), blk),
            pl.BlockSpec((bn, 1, kw), blk),
            pl.BlockSpec((bn, 1, kw), blk),
            pl.BlockSpec((bn, WINDOW, cache_lanes), blk),
            pl.BlockSpec((1, qw), lambda i: (0, 0)),
        ],
        out_specs=[pl.BlockSpec((bn, 1, qw), blk), pl.BlockSpec((bn, 1, qw), blk)],
        out_shape=[jax.ShapeDtypeStruct((n, 1, qw), F32), jax.ShapeDtypeStruct((n, 1, qw), F32)],
        compiler_params=_cparams(("parallel",)),
    )(q, kn, vn, cache, sink)


def _s5_param_kernel(are_ref, aim_ref, ldt_ref, bre_ref, bim_ref, cre_ref, cim_ref,
                     lam_re_ref, lam_im_ref, cl_re_ref, cl_im_ref, r_re_ref, r_im_ref, k_ref):
    ar = are_ref[0]
    ai = aim_ref[0]
    dt = jnp.exp(ldt_ref[0])
    nrow = lam_re_ref.shape[1]
    ri = lax.broadcasted_iota(jnp.int32, (nrow, 1), 0)
    pw = jnp.where(ri < S5_NPOW, ri, jnp.left_shift(S5_CHUNK, jnp.maximum(ri - S5_NPOW, 0))).astype(F32)
    mag = jnp.exp(pw * (ar * dt))
    ang = pw * (ai * dt)
    lam_re = mag * jnp.cos(ang)
    lam_im = mag * jnp.sin(ang)
    lam_re_ref[0] = lam_re
    lam_im_ref[0] = lam_im
    l1r, l1i = lam_re[1:2], lam_im[1:2]
    den = ar * ar + ai * ai
    z_re = ((l1r - 1.0) * ar + l1i * ai) / den
    z_im = (l1i * ar - (l1r - 1.0) * ai) / den
    b_re, b_im = bre_ref[0], bim_ref[0]
    bb_re = z_re * b_re - z_im * b_im
    bb_im = z_re * b_im + z_im * b_re
    c_re, c_im = cre_ref[0], cim_ref[0]
    cls_re, cls_im = [], []
    for d in range(S5_NPOW):
        lr, li = lam_re[d:d + 1], lam_im[d:d + 1]
        cr = c_re * lr - c_im * li
        ci = c_re * li + c_im * lr
        cl_re_ref[0, d * SSM_GROUP:(d + 1) * SSM_GROUP, :] = cr
        cl_im_ref[0, d * SSM_GROUP:(d + 1) * SSM_GROUP, :] = ci
        if d < S5_CHUNK:
            cls_re.append(cr)
            cls_im.append(ci)
    for s in range(S5_CHUNK):
        lr = lam_re[S5_CHUNK - 1 - s:S5_CHUNK - s]
        li = lam_im[S5_CHUNK - 1 - s:S5_CHUNK - s]
        r_re_ref[0, s * SSM_GROUP:(s + 1) * SSM_GROUP, :] = lr * bb_re - li * bb_im
        r_im_ref[0, s * SSM_GROUP:(s + 1) * SSM_GROUP, :] = lr * bb_im + li * bb_re
    call_re = jnp.concatenate(cls_re, axis=0)
    call_im = jnp.concatenate(cls_im, axis=0)
    hp = lax.Precision.HIGHEST
    nt = (((1,), (1,)), ((), ()))
    k_ref[0] = (lax.dot_general(call_re, bb_re, nt, precision=hp, preferred_element_type=F32)
                - lax.dot_general(call_im, bb_im, nt, precision=hp, preferred_element_type=F32))


def _s5_params(a_re, a_im, log_dt, b_re, b_im, c_re, c_im, nsteps):
    g = SSM_GROUPS
    nrow = ((S5_NPOW + nsteps + 7) // 8) * 8
    row = lambda i: (i, 0, 0)
    shp = lambda r, c: jax.ShapeDtypeStruct((g, r, c), F32)
    ncl = S5_NPOW * SSM_GROUP
    nr = S5_CHUNK * SSM_GROUP
    return pl.pallas_call(
        _s5_param_kernel,
        grid=(g,),
        in_specs=[
            pl.BlockSpec((1, 1, SSM_STATE), row),
            pl.BlockSpec((1, 1, SSM_STATE), row),
            pl.BlockSpec((1, 1, 1), row),
            pl.BlockSpec((1, SSM_GROUP, SSM_STATE), row),
            pl.BlockSpec((1, SSM_GROUP, SSM_STATE), row),
            pl.BlockSpec((1, SSM_GROUP, SSM_STATE), row),
            pl.BlockSpec((1, SSM_GROUP, SSM_STATE), row),
        ],
        out_specs=[
            pl.BlockSpec((1, nrow, SSM_STATE), row), pl.BlockSpec((1, nrow, SSM_STATE), row),
            pl.BlockSpec((1, ncl, SSM_STATE), row), pl.BlockSpec((1, ncl, SSM_STATE), row),
            pl.BlockSpec((1, nr, SSM_STATE), row), pl.BlockSpec((1, nr, SSM_STATE), row),
            pl.BlockSpec((1, nr, SSM_GROUP), row),
        ],
        out_shape=[shp(nrow, SSM_STATE), shp(nrow, SSM_STATE), shp(ncl, SSM_STATE), shp(ncl, SSM_STATE),
                   shp(nr, SSM_STATE), shp(nr, SSM_STATE), shp(nr, SSM_GROUP)],
        compiler_params=_cparams(("parallel",)),
    )(a_re.reshape(g, 1, SSM_STATE), a_im.reshape(g, 1, SSM_STATE), log_dt.reshape(g, 1, 1),
      jnp.swapaxes(b_re, 1, 2), jnp.swapaxes(b_im, 1, 2), c_re, c_im)


def _s5_prompt_kernel(u_ref, mt_ref, r_re_ref, r_im_ref, cl_re_ref, cl_im_ref, lam_re_ref, lam_im_ref, d_ref,
                      y_ref, xre_ref, xim_ref, *, nseq, nchunk):
    u = u_ref[0]
    ub = u.astype(BF16)
    y = _dot(ub, mt_ref[0].astype(BF16))
    xr = _dot(ub, r_re_ref[0].astype(BF16))
    xi = _dot(ub, r_im_ref[0].astype(BF16))
    rows = nseq * nchunk
    cidx = lax.broadcasted_iota(jnp.int32, (rows, SSM_STATE), 0) % nchunk
    lam_re, lam_im = lam_re_ref[0], lam_im_ref[0]
    for k in range(int(math.log2(nchunk))):
        sh = 1 << k
        lr = lam_re[S5_NPOW + k:S5_NPOW + k + 1]
        li = lam_im[S5_NPOW + k:S5_NPOW + k + 1]
        ok = cidx >= sh
        pr = jnp.where(ok, pltpu.roll(xr, sh, axis=0), 0.0)
        pi = jnp.where(ok, pltpu.roll(xi, sh, axis=0), 0.0)
        xr, xi = xr + lr * pr - li * pi, xi + lr * pi + li * pr
    ok = cidx >= 1
    er = jnp.where(ok, pltpu.roll(xr, 1, axis=0), 0.0)
    ei = jnp.where(ok, pltpu.roll(xi, 1, axis=0), 0.0)
    cl_re = cl_re_ref[0][SSM_GROUP:, :]
    cl_im = cl_im_ref[0][SSM_GROUP:, :]
    y = y + _dot_nt(er.astype(BF16), cl_re.astype(BF16)) - _dot_nt(ei.astype(BF16), cl_im.astype(BF16))
    y_ref[0] = y + d_ref[0] * u
    last = [s * nchunk + nchunk - 1 for s in range(nseq)]
    xre_ref[0] = jnp.concatenate([xr[i:i + 1] for i in last], axis=0)
    xim_ref[0] = jnp.concatenate([xi[i:i + 1] for i in last], axis=0)


def _s5_prompt(u, mt, r_re, r_im, cl_re, cl_im, lam_re, lam_im, d_t, nseq, nchunk):
    g = SSM_GROUPS
    rows = nseq * nchunk
    cw = S5_CHUNK * SSM_GROUP
    row = lambda i: (i, 0, 0)
    full = lambda a: pl.BlockSpec((1,) + a.shape[1:], row)
    return pl.pallas_call(
        functools.partial(_s5_prompt_kernel, nseq=nseq, nchunk=nchunk),
        grid=(g,),
        in_specs=[full(u), full(mt), full(r_re), full(r_im), full(cl_re), full(cl_im), full(lam_re), full(lam_im),
                  full(d_t)],
        out_specs=[pl.BlockSpec((1, rows, cw), row), pl.BlockSpec((1, nseq, SSM_STATE), row),
                   pl.BlockSpec((1, nseq, SSM_STATE), row)],
        out_shape=[jax.ShapeDtypeStruct((g, rows, cw), F32), jax.ShapeDtypeStruct((g, nseq, SSM_STATE), F32),
                   jax.ShapeDtypeStruct((g, nseq, SSM_STATE), F32)],
        compiler_params=_cparams(("parallel",)),
    )(u, mt, r_re, r_im, cl_re, cl_im, lam_re, lam_im, d_t)


def _s5_sample_kernel(u_ref, x0r_ref, x0i_ref, r_re_ref, r_im_ref, cl_re_ref, cl_im_ref, lam_re_ref, lam_im_ref,
                      d_ref, y_ref, xre_ref, xim_ref):
    u = u_ref[0]
    ub = u.astype(BF16)
    lo = (S5_CHUNK - 1) * SSM_GROUP
    bb_re = r_re_ref[0][lo:lo + SSM_GROUP, :]
    bb_im = r_im_ref[0][lo:lo + SSM_GROUP, :]
    lr, li = lam_re_ref[0][1:2], lam_im_ref[0][1:2]
    x0r, x0i = x0r_ref[0], x0i_ref[0]
    xr = _dot(ub, bb_re.astype(BF16)) + (lr * x0r - li * x0i)
    xi = _dot(ub, bb_im.astype(BF16)) + (lr * x0i + li * x0r)
    c_re = cl_re_ref[0][:SSM_GROUP, :]
    c_im = cl_im_ref[0][:SSM_GROUP, :]
    y_ref[0] = (_dot_nt(xr.astype(BF16), c_re.astype(BF16)) - _dot_nt(xi.astype(BF16), c_im.astype(BF16))
                + d_ref[0] * u)
    xre_ref[0] = xr
    xim_ref[0] = xi


def _s5_sample(u, x0r, x0i, r_re, r_im, cl_re, cl_im, lam_re, lam_im, d):
    g, n = u.shape[0], u.shape[1]
    row = lambda i: (i, 0, 0)
    full = lambda a: pl.BlockSpec((1,) + a.shape[1:], row)
    return pl.pallas_call(
        _s5_sample_kernel,
        grid=(g,),
        in_specs=[full(u), full(x0r), full(x0i), full(r_re), full(r_im), full(cl_re), full(cl_im), full(lam_re),
                  full(lam_im), full(d)],
        out_specs=[pl.BlockSpec((1, n, SSM_GROUP), row), pl.BlockSpec((1, n, SSM_STATE), row),
                   pl.BlockSpec((1, n, SSM_STATE), row)],
        out_shape=[jax.ShapeDtypeStruct((g, n, SSM_GROUP), F32), jax.ShapeDtypeStruct((g, n, SSM_STATE), F32),
                   jax.ShapeDtypeStruct((g, n, SSM_STATE), F32)],
        compiler_params=_cparams(("parallel",)),
    )(u, x0r, x0i, r_re, r_im, cl_re, cl_im, lam_re, lam_im, d)


def _log_sigmoid(x):
    return jnp.minimum(x, 0.0) - jnp.log1p(jnp.exp(-jnp.abs(x)))


def _logaddexp(a, b):
    return jnp.maximum(a, b) + jnp.log1p(jnp.exp(-jnp.abs(a - b)))


def _hgrn_gates(fpre, lb):
    log_f = _logaddexp(jnp.log(jnp.maximum(lb, LB_FLOOR)), jnp.log1p(-lb) + _log_sigmoid(fpre))
    k = (1.0 - lb) * _sigmoid(-fpre)
    return log_f, k


def _hgrn_scan_matrix():
    c = HGRN_CHUNK
    tri = np.tril(np.ones((c, c), np.float32))
    t = np.arange(c)
    mats = [tri]
    for m in HGRN_LEVELS:
        mats.append(tri[(t // (2 * m)) * (2 * m) + m - 1])
    return np.concatenate(mats, axis=0)


def _hgrn_prompt_kernel(q_ref, f_ref, i_ref, g_ref, lbraw_ref, gn_ref, scan_ref, o_ref, s_ref, st_ref, *, layer):
    cidx = pl.program_id(2)
    c = HGRN_CHUNK

    @pl.when(cidx == 0)
    def _():
        st_ref[...] = jnp.zeros_like(st_ref)

    a = lbraw_ref[...]
    e = jnp.exp(a - jnp.max(a, axis=0, keepdims=True))
    p = e / jnp.sum(e, axis=0, keepdims=True)
    run = p[0:1]
    for l in range(1, layer + 1):
        run = run + p[l:l + 1]
    lb = run - p[0:1]

    log_f, k = _hgrn_gates(f_ref[0], lb)
    q = _silu(q_ref[0])
    v = i_ref[0]
    cum = jnp.dot(scan_ref[...], log_f, precision=lax.Precision.HIGHEST, preferred_element_type=F32)
    b = cum[0:c]
    r = lax.broadcasted_iota(jnp.int32, (c, c), 0)
    s = lax.broadcasted_iota(jnp.int32, (c, c), 1)
    att = jnp.where(r == s, jnp.sum(q * k, axis=-1, keepdims=True), 0.0)
    for lvl, m in enumerate(HGRN_LEVELS):
        bref = cum[(lvl + 1) * c:(lvl + 2) * c]
        qs = q * jnp.exp(jnp.minimum(b - bref, 0.0))
        ks = k * jnp.exp(jnp.minimum(bref - b, 0.0))
        sh = int(math.log2(2 * m))
        pair = ((r >> sh) == (s >> sh)) & ((r & m) != 0) & ((s & m) == 0)
        att = att + jnp.where(pair, _dot_nt(qs.astype(BF16), ks.astype(BF16)), 0.0)
    st = st_ref[...]
    vb = v.astype(BF16)
    o = _dot(att.astype(BF16), vb) + _dot_nt((q * jnp.exp(b)).astype(BF16), st.astype(BF16))
    bl = b[c - 1:c]
    st_new = jnp.exp(bl) * st + _dot_tn(vb, (k * jnp.exp(bl - b)).astype(BF16))
    st_ref[...] = st_new
    o = o * lax.rsqrt(jnp.mean(o * o, axis=-1, keepdims=True) + NORM_EPS)
    o_ref[0] = o * gn_ref[...] * _silu(g_ref[0])

    @pl.when(cidx == pl.num_programs(2) - 1)
    def _():
        s_ref[0, 0] = st_new


def _hgrn_prompt(q, f, i, g, lbraw, gn, layer):
    n, t, _ = q.shape
    c = HGRN_CHUNK
    blk = pl.BlockSpec((1, c, HGRN_HEAD_DIM), lambda b, h, j: (b, j, h))
    scan = jnp.asarray(_hgrn_scan_matrix())
    return pl.pallas_call(
        functools.partial(_hgrn_prompt_kernel, layer=layer),
        grid=(n, HGRN_HEADS, t // c),
        in_specs=[blk, blk, blk, blk,
                  pl.BlockSpec((DEPTH, HGRN_HEAD_DIM), lambda b, h, j: (0, h)),
                  pl.BlockSpec((1, HGRN_HEAD_DIM), lambda b, h, j: (0, h)),
                  pl.BlockSpec(scan.shape, lambda b, h, j: (0, 0))],
        out_specs=[blk, pl.BlockSpec((1, 1, HGRN_HEAD_DIM, HGRN_HEAD_DIM), lambda b, h, j: (b, h, 0, 0))],
        out_shape=[jax.ShapeDtypeStruct((n, t, HGRN_WIDTH), F32),
                   jax.ShapeDtypeStruct((n, HGRN_HEADS, HGRN_HEAD_DIM, HGRN_HEAD_DIM), F32)],
        scratch_shapes=[pltpu.VMEM((HGRN_HEAD_DIM, HGRN_HEAD_DIM), F32)],
        compiler_params=_cparams(("parallel", "parallel", "arbitrary")),
    )(q, f, i, g, lbraw, gn, scan)


def _hgrn_sample_kernel(qt_ref, ft_ref, lbt_ref, v_ref, g_ref, gn_ref, s0_ref, o_ref, s_ref, *, layer, n):
    a = lbt_ref[0]
    e = jnp.exp(a - jnp.max(a, axis=1, keepdims=True))
    p = e / jnp.sum(e, axis=1, keepdims=True)
    run = p[:, 0:1]
    for l in range(1, layer + 1):
        run = run + p[:, l:l + 1]
    lb = run - p[:, 0:1]
    log_f, k = _hgrn_gates(ft_ref[0], lb)
    f = jnp.exp(log_f)
    q = _silu(qt_ref[0])
    v = v_ref[...]
    rows = []
    for b in range(n):
        s_new = f[:, b:b + 1] * s0_ref[b, 0] + k[:, b:b + 1] * v[b:b + 1, :]
        s_ref[b, 0] = s_new
        rows.append(jnp.sum(q[:, b:b + 1] * s_new, axis=0, keepdims=True))
    o = jnp.concatenate(rows, axis=0)
    o = o * lax.rsqrt(jnp.mean(o * o, axis=-1, keepdims=True) + NORM_EPS)
    o_ref[...] = o * gn_ref[...] * _silu(g_ref[...])


def _hgrn_sample(qt, ft, lbt, v, g, gn, s0, layer):
    n = v.shape[0]
    hd = HGRN_HEAD_DIM
    return pl.pallas_call(
        functools.partial(_hgrn_sample_kernel, layer=layer, n=n),
        grid=(HGRN_HEADS,),
        in_specs=[pl.BlockSpec((1, hd, n), lambda h: (h, 0, 0)),
                  pl.BlockSpec((1, hd, n), lambda h: (h, 0, 0)),
                  pl.BlockSpec((1, hd, DEPTH), lambda h: (h, 0, 0)),
                  pl.BlockSpec((n, hd), lambda h: (0, h)),
                  pl.BlockSpec((n, hd), lambda h: (0, h)),
                  pl.BlockSpec((1, hd), lambda h: (0, h)),
                  pl.BlockSpec((n, 1, hd, hd), lambda h: (0, h, 0, 0))],
        out_specs=[pl.BlockSpec((n, hd), lambda h: (0, h)),
                   pl.BlockSpec((n, 1, hd, hd), lambda h: (0, h, 0, 0))],
        out_shape=[jax.ShapeDtypeStruct((n, HGRN_WIDTH), F32),
                   jax.ShapeDtypeStruct((n, HGRN_HEADS, hd, hd), F32)],
        compiler_params=_cparams(("parallel",)),
    )(qt, ft, lbt, v, g, gn, s0)


def _gelu_tanh(x):
    return 0.5 * x * (1.0 + jnp.tanh(math.sqrt(2.0 / math.pi) * (x + 0.044715 * (x * x * x))))


def _mixout_kernel(h_ref, ya_ref, ob_ref, oc0_ref, oc1_ref, oc2_ref, l0_ref, l1_ref, l2_ref, od_ref,
                   wglu_ref, bglu_ref, ga_ref, gb_ref, gc_ref, woa_ref, wob_ref, woc_ref, wod_ref, gpost_ref, o_ref):
    z = _gelu_tanh(ya_ref[...])
    gate = _sigmoid(_dot(z.astype(BF16), wglu_ref[...]) + bglu_ref[...])
    out_a = _rms(z * gate, ga_ref[...])
    out_b = _rms(ob_ref[...], gb_ref[...])
    l0, l1, l2 = l0_ref[...], l1_ref[...], l2_ref[...]
    m = jnp.maximum(jnp.maximum(l0, l1), l2)
    e0, e1, e2 = jnp.exp(l0 - m), jnp.exp(l1 - m), jnp.exp(l2 - m)
    den = e0 + e1 + e2
    cs = [oc0_ref[...] * (e0 / den), oc1_ref[...] * (e1 / den), oc2_ref[...] * (e2 / den)]
    ss = sum(jnp.sum(c * c, axis=-1, keepdims=True) for c in cs)
    inv = lax.rsqrt(ss / DIL_WIDTH + NORM_EPS)
    gc = gc_ref[...]
    mix = _dot(out_a.astype(BF16), woa_ref[...]) + _dot(out_b.astype(BF16), wob_ref[...])
    for gi in range(3):
        c = cs[gi] * inv * gc[:, gi * DIL_GW:(gi + 1) * DIL_GW]
        mix = mix + _dot(c.astype(BF16), woc_ref[gi * DIL_GW:(gi + 1) * DIL_GW, :])
    mix = mix + _dot(od_ref[...].astype(BF16), wod_ref[...])
    o_ref[...] = h_ref[...] + _rms(mix, gpost_ref[...])


def _mixout(h, ya, ob, ocs, lses, od, wglu, bglu, ga, gb, gc, woa, wob, woc, wod, gpost, tm):
    m = h.shape[0]
    rowblk = lambda a: pl.BlockSpec((tm, a.shape[1]), lambda i: (i, 0))
    whole = lambda a: pl.BlockSpec(a.shape, lambda i: (0, 0))
    acts = [h, ya, ob, *ocs, *lses, od]
    params = [wglu, bglu, ga, gb, gc, woa, wob, woc, wod, gpost]
    return pl.pallas_call(
        _mixout_kernel,
        grid=(m // tm,),
        in_specs=[rowblk(a) for a in acts] + [whole(p) for p in params],
        out_specs=pl.BlockSpec((tm, D_MODEL), lambda i: (i, 0)),
        out_shape=jax.ShapeDtypeStruct((m, D_MODEL), F32),
        compiler_params=_cparams(("parallel",)),
    )(*acts, *params)


def _prep_layer(lp):
    pad = D_FF_PAD - D_FF
    out = {}
    for name in ('ffn1', 'ffn2'):
        out[name + '_wg'] = jnp.pad(lp[name + '_w_gate'].astype(BF16), ((0, 0), (0, pad)))
        out[name + '_wu'] = jnp.pad(lp[name + '_w_up'].astype(BF16), ((0, 0), (0, pad)))
        out[name + '_wd'] = jnp.pad(lp[name + '_w_down'].astype(BF16), ((0, pad), (0, 0)))
    w = lp['w_in']
    cuts = np.cumsum([0, SSM_WIDTH, SWA_WIDTH, SWA_KV_WIDTH, SWA_KV_WIDTH, DIL_WIDTH, DIL_KV_WIDTH, DIL_KV_WIDTH,
                      HGRN_WIDTH, HGRN_WIDTH, HGRN_WIDTH, HGRN_WIDTH])
    seg = lambda i: w[:, cuts[i]:cuts[i + 1]]
    order = [1, 2, 4, 5, 0, 3, 6, 7, 8, 9, 10]
    out['w_in'] = jnp.concatenate([seg(i) for i in order], axis=1).astype(BF16)
    wo = lp['w_out'].astype(BF16)
    o1, o2, o3 = SSM_WIDTH, SSM_WIDTH + SWA_WIDTH, SSM_WIDTH + SWA_WIDTH + DIL_WIDTH
    out['wo_a'], out['wo_b'], out['wo_c'], out['wo_d'] = wo[:o1], wo[o1:o2], wo[o2:o3], wo[o3:]
    out['w_glu'] = lp['ssm_w_glu'].astype(BF16)
    return out


def _row(x):
    return x.reshape(1, -1)


def _s5_toeplitz(kmat):
    g = kmat.shape[0]
    k4 = kmat.reshape(g, S5_CHUNK, SSM_GROUP, SSM_GROUP)
    k4 = jnp.concatenate([k4, jnp.zeros((g, 1, SSM_GROUP, SSM_GROUP), F32)], axis=1)
    s = np.arange(S5_CHUNK)[:, None]
    t = np.arange(S5_CHUNK)[None, :]
    idx = np.where(t >= s, t - s, S5_CHUNK)
    m = k4[:, idx]
    return m.transpose(0, 1, 4, 2, 3).reshape(g, S5_CHUNK * SSM_GROUP, S5_CHUNK * SSM_GROUP)


def _mixer_tail(h, ya, ob, ocs, lses, od, lp, wp, tm):
    return _mixout(h, ya, ob, ocs, lses, od, wp['w_glu'], _row(lp['ssm_b_glu']), _row(lp['out_norm_a']),
                   _row(lp['out_norm_b']), _row(lp['out_norm_c']), wp['wo_a'], wp['wo_b'], wp['wo_c'], wp['wo_d'],
                   _row(lp['mix_norm_post']), tm)


def _ffn_block(h, lp, wp, name, tm):
    return _ffn(h, _row(lp[name + '_norm_pre']), wp[name + '_wg'], wp[name + '_wu'], wp[name + '_wd'],
                _row(lp[name + '_norm_post']), tm)


def _prompt_mixer(h, lp, wp, s5p, lbraw, layer, n, t, rope, tm):
    m = n * t
    proj = _win(h, _row(lp['mix_norm_pre']), wp['w_in'], rope[0], rope[1], tm)
    col = lambda off, width: proj[:, off:off + width].reshape(n, t, width)

    lam_re, lam_im, cl_re, cl_im, r_re, r_im, kmat = s5p
    nchunk = t // S5_CHUNK
    u = col(OFF_U, SSM_WIDTH).reshape(n, nchunk, S5_CHUNK, SSM_GROUPS, SSM_GROUP)
    u = u.transpose(3, 0, 1, 2, 4).reshape(SSM_GROUPS, n * nchunk, S5_CHUNK * SSM_GROUP)
    d_t = jnp.tile(lp['ssm_d'], (1, S5_CHUNK)).reshape(SSM_GROUPS, 1, S5_CHUNK * SSM_GROUP)
    y, xre, xim = _s5_prompt(u, _s5_toeplitz(kmat), r_re, r_im, cl_re, cl_im, lam_re, lam_im, d_t, n, nchunk)
    ya = y.reshape(SSM_GROUPS, n, nchunk, S5_CHUNK, SSM_GROUP).transpose(1, 2, 3, 0, 4).reshape(m, SSM_WIDTH)
    ssm_new = jnp.stack([xre, xim], axis=-1).transpose(1, 0, 2, 3)

    qb = col(OFF_QB, SWA_WIDTH)
    kb = col(OFF_KB, SWA_KV_WIDTH)
    vb = col(OFF_VB, SWA_KV_WIDTH)
    gw = SWA_GROUP * HEAD_DIM
    heads = lambda a, w: a.reshape(n, t, SWA_KV_HEADS, w).transpose(0, 2, 1, 3).reshape(n * SWA_KV_HEADS, t, w)
    sink = jnp.repeat(lp['swa_sinks'], HEAD_DIM).reshape(SWA_KV_HEADS, 1, gw)
    ob, _ = _band_attn(heads(qb, gw), heads(kb, HEAD_DIM), heads(vb, HEAD_DIM), sink, SWA_GROUP, True)
    ob = ob.reshape(n, SWA_KV_HEADS, t, gw).transpose(0, 2, 1, 3).reshape(m, SWA_WIDTH)
    keep = min(WINDOW, t)
    swa_new = jnp.stack([kb[:, t - keep:].reshape(n, keep, SWA_KV_HEADS, HEAD_DIM),
                         vb[:, t - keep:].reshape(n, keep, SWA_KV_HEADS, HEAD_DIM)], axis=2)

    qc = col(OFF_QC, DIL_WIDTH)
    kc = col(OFF_KC, DIL_KV_WIDTH)
    vc = col(OFF_VC, DIL_KV_WIDTH)
    no_sink = jnp.zeros((1, 1, DIL_GW), F32)
    ocs, lses, dil_new = [], [], []
    for gi, (win, dil) in enumerate(DIL_PAIRS):
        def sub(a):
            w = a.shape[-1]
            return a.reshape(n, t // dil, dil, w).transpose(0, 2, 1, 3).reshape(n * dil, t // dil, w)

        def unsub(a):
            w = a.shape[-1]
            return a.reshape(n, dil, t // dil, w).transpose(0, 2, 1, 3).reshape(m, w)

        qg = qc[:, :, gi * DIL_GW:(gi + 1) * DIL_GW]
        kg = kc[:, :, gi * HEAD_DIM:(gi + 1) * HEAD_DIM]
        vg = vc[:, :, gi * HEAD_DIM:(gi + 1) * HEAD_DIM]
        o, lse = _band_attn(sub(qg), sub(kg), sub(vg), no_sink, DIL_HPG, False)
        ocs.append(unsub(o))
        lses.append(unsub(lse))
        keep = min(win, t)
        dil_new.append(jnp.stack([kg[:, t - keep:], vg[:, t - keep:]], axis=2))

    od, st = _hgrn_prompt(col(OFF_QD, HGRN_WIDTH), col(OFF_FD, HGRN_WIDTH), col(OFF_ID, HGRN_WIDTH),
                          col(OFF_GD, HGRN_WIDTH), lbraw, _row(lp['out_norm_d']), layer)
    hgrn_new = jnp.swapaxes(st, -1, -2)

    h = _mixer_tail(h, ya, ob, ocs, lses, od.reshape(m, HGRN_WIDTH), lp, wp, tm)
    return h, (ssm_new, swa_new, dil_new[0], dil_new[1], dil_new[2], hgrn_new)


def _prompt_layer(h, lp, wp, s5p, lbraw, layer, n, t, rope):
    tm = 512
    h = _ffn_block(h, lp, wp, 'ffn1', tm)
    h, state = _prompt_mixer(h, lp, wp, s5p, lbraw, layer, n, t, rope, tm)
    return _ffn_block(h, lp, wp, 'ffn2', tm), state


def _sample_mixer(h, lp, wp, s5p, lbraw, layer, cache, rope):
    n = h.shape[0]
    tm = n
    state_ssm, cache_swa, cache_d0, cache_d1, cache_d2, state_hgrn = cache
    proj = _win(h, _row(lp['mix_norm_pre']), wp['w_in'], rope[0], rope[1], tm)
    col = lambda off, width: proj[:, off:off + width]

    lam_re, lam_im, cl_re, cl_im, r_re, r_im, _ = s5p
    u = col(OFF_U, SSM_WIDTH).reshape(n, SSM_GROUPS, SSM_GROUP).transpose(1, 0, 2)
    x0 = state_ssm.transpose(1, 0, 2, 3)
    y, xre, xim = _s5_sample(u, x0[..., 0], x0[..., 1], r_re, r_im, cl_re, cl_im, lam_re, lam_im,
                             lp['ssm_d'].reshape(SSM_GROUPS, 1, SSM_GROUP))
    ya = y.transpose(1, 0, 2).reshape(n, SSM_WIDTH)
    ssm_new = jnp.stack([xre, xim], axis=-1).transpose(1, 0, 2, 3)

    qb = col(OFF_QB, SWA_WIDTH)
    kb = col(OFF_KB, SWA_KV_WIDTH)
    vb = col(OFF_VB, SWA_KV_WIDTH)
    lb_rows = cache_swa.shape[1]
    sink = jnp.repeat(lp['swa_sinks'], HEAD_DIM).reshape(1, SWA_WIDTH)
    cb = cache_swa.reshape(n, lb_rows, 2 * SWA_KV_WIDTH)
    ob, _ = _decode_attn(qb[:, None], kb[:, None], vb[:, None], cb, sink, SWA_KV_HEADS, SWA_GROUP,
                         (0, HEAD_DIM), (SWA_KV_WIDTH, SWA_KV_WIDTH + HEAD_DIM), True, 2 * SWA_KV_WIDTH)
    ob = ob.reshape(n, SWA_WIDTH)
    swa_new = jnp.stack([kb.reshape(n, 1, SWA_KV_HEADS, HEAD_DIM), vb.reshape(n, 1, SWA_KV_HEADS, HEAD_DIM)], axis=2)

    qc = col(OFF_QC, DIL_WIDTH)
    kc = col(OFF_KC, DIL_KV_WIDTH)
    vc = col(OFF_VC, DIL_KV_WIDTH)
    no_sink = jnp.zeros((1, DIL_GW), F32)
    ocs, lses, dil_new = [], [], []
    for gi, (buf, (win, dil)) in enumerate(zip((cache_d0, cache_d1, cache_d2), DIL_PAIRS)):
        qg = qc[:, gi * DIL_GW:(gi + 1) * DIL_GW]
        kg = kc[:, gi * HEAD_DIM:(gi + 1) * HEAD_DIM]
        vg = vc[:, gi * HEAD_DIM:(gi + 1) * HEAD_DIM]
        folded = buf.reshape(n, buf.shape[1] // dil, dil * 2 * HEAD_DIM)
        o, lse = _decode_attn(qg[:, None], kg[:, None], vg[:, None], folded, no_sink, 1, DIL_HPG,
                              (0,), (HEAD_DIM,), False, 2 * HEAD_DIM)
        ocs.append(o.reshape(n, DIL_GW))
        lses.append(lse.reshape(n, DIL_GW))
        dil_new.append(jnp.stack([kg[:, None], vg[:, None]], axis=2))

    cm = lambda a: a.reshape(n, HGRN_HEADS, HGRN_HEAD_DIM).transpose(1, 2, 0)
    lbt = lbraw.reshape(DEPTH, HGRN_HEADS, HGRN_HEAD_DIM).transpose(1, 2, 0)
    od, hgrn_new = _hgrn_sample(cm(col(OFF_QD, HGRN_WIDTH)), cm(col(OFF_FD, HGRN_WIDTH)), lbt,
                                col(OFF_ID, HGRN_WIDTH), col(OFF_GD, HGRN_WIDTH), _row(lp['out_norm_d']),
                                state_hgrn, layer)

    h = _mixer_tail(h, ya, ob, ocs, lses, od, lp, wp, tm)
    return h, (ssm_new, swa_new, dil_new[0], dil_new[1], dil_new[2], hgrn_new)


def _sample_layer(h, lp, wp, s5p, lbraw, layer, cache, rope):
    tm = h.shape[0]
    h = _ffn_block(h, lp, wp, 'ffn1', tm)
    h, state = _sample_mixer(h, lp, wp, s5p, lbraw, layer, cache, rope)
    return _ffn_block(h, lp, wp, 'ffn2', tm), state


_LAYER_KEYS = ('ffn1_norm_pre', 'ffn1_w_gate', 'ffn1_w_up', 'ffn1_w_down', 'ffn1_norm_post', 'mix_norm_pre', 'w_in',
               'ssm_a_re', 'ssm_a_im', 'ssm_log_dt', 'ssm_b_re', 'ssm_b_im', 'ssm_c_re', 'ssm_c_im', 'ssm_d',
               'ssm_w_glu', 'ssm_b_glu', 'swa_sinks', 'out_norm_a', 'out_norm_b', 'out_norm_c', 'out_norm_d',
               'w_out', 'mix_norm_post', 'ffn2_norm_pre', 'ffn2_w_gate', 'ffn2_w_up', 'ffn2_w_down', 'ffn2_norm_post')


def _forward(x_prompt, x_sample, caches, params, hgrn_lower_bounds):
    n, t, _ = x_prompt.shape
    ns, ts, _ = x_sample.shape
    assert ts == 1
    rope_p = _rope_tables(jnp.tile(jnp.arange(t, dtype=jnp.int32), n))
    rope_s = _rope_tables(jnp.full((ns,), PAST_LEN, dtype=jnp.int32))
    nsteps = int(math.log2(t // S5_CHUNK))
    h_p = x_prompt.reshape(n * t, D_MODEL)
    h_s = x_sample.reshape(ns, D_MODEL)
    new_p = [[] for _ in range(6)]
    new_s = [[] for _ in range(6)]
    for l in range(DEPTH):
        lp = {k: params[k][l] for k in _LAYER_KEYS}
        wp = _prep_layer(lp)
        s5p = _s5_params(lp['ssm_a_re'], lp['ssm_a_im'], lp['ssm_log_dt'], lp['ssm_b_re'], lp['ssm_b_im'],
                         lp['ssm_c_re'], lp['ssm_c_im'], nsteps)
        h_p, st_p = _prompt_layer(h_p, lp, wp, s5p, hgrn_lower_bounds, l, n, t, rope_p)
        h_s, st_s = _sample_layer(h_s, lp, wp, s5p, hgrn_lower_bounds, l, tuple(c[l] for c in caches), rope_s)
        for i in range(6):
            new_p[i].append(st_p[i])
            new_s[i].append(st_s[i])
    new_p = [jnp.stack(a, axis=0) for a in new_p]
    new_s = [jnp.stack(a, axis=0) for a in new_s]
    return (h_p.reshape(n, t, D_MODEL), h_s.reshape(ns, ts, D_MODEL), new_p[0], new_s[0], new_p[1], new_s[1],
            new_p[2], new_s[2], new_p[3], new_s[3], new_p[4], new_s[4], new_p[5], new_s[5])


def kernel(x_prompt, x_sample, state_ssm, cache_swa_kv, cache_dil0_kv, cache_dil1_kv, cache_dil2_kv, state_hgrn, ffn1_norm_pre, ffn1_w_gate, ffn1_w_up, ffn1_w_down, ffn1_norm_post, mix_norm_pre, w_in, ssm_a_re, ssm_a_im, ssm_log_dt, ssm_b_re, ssm_b_im, ssm_c_re, ssm_c_im, ssm_d, ssm_w_glu, ssm_b_glu, swa_sinks, hgrn_lower_bounds, out_norm_a, out_norm_b, out_norm_c, out_norm_d, w_out, mix_norm_post, ffn2_norm_pre, ffn2_w_gate, ffn2_w_up, ffn2_w_down, ffn2_norm_post):
    params = dict(ffn1_norm_pre=ffn1_norm_pre, ffn1_w_gate=ffn1_w_gate, ffn1_w_up=ffn1_w_up, ffn1_w_down=ffn1_w_down,
                  ffn1_norm_post=ffn1_norm_post, mix_norm_pre=mix_norm_pre, w_in=w_in, ssm_a_re=ssm_a_re,
                  ssm_a_im=ssm_a_im, ssm_log_dt=ssm_log_dt, ssm_b_re=ssm_b_re, ssm_b_im=ssm_b_im, ssm_c_re=ssm_c_re,
                  ssm_c_im=ssm_c_im, ssm_d=ssm_d, ssm_w_glu=ssm_w_glu, ssm_b_glu=ssm_b_glu, swa_sinks=swa_sinks,
                  out_norm_a=out_norm_a, out_norm_b=out_norm_b, out_norm_c=out_norm_c, out_norm_d=out_norm_d,
                  w_out=w_out, mix_norm_post=mix_norm_post, ffn2_norm_pre=ffn2_norm_pre, ffn2_w_gate=ffn2_w_gate,
                  ffn2_w_up=ffn2_w_up, ffn2_w_down=ffn2_w_down, ffn2_norm_post=ffn2_norm_post)
    caches = (state_ssm, cache_swa_kv, cache_dil0_kv, cache_dil1_kv, cache_dil2_kv, state_hgrn)
    return _forward(x_prompt, x_sample, caches, params, hgrn_lower_bounds)
```

```python
import functools
import math

import numpy as np
import jax
import jax.numpy as jnp
from jax import lax
from jax.experimental import pallas as pl
from jax.experimental.pallas import tpu as pltpu

F32 = jnp.float32
BF16 = jnp.bfloat16

D_MODEL = 2048
DEPTH = 2
PAST_LEN = 16384
HEAD_DIM = 64
ROPE_THETA = 10000.0
D_FF = 5504
NORM_EPS = 1e-6
NEG_BIG = -1e30
LB_FLOOR = 1e-30
SSM_GROUP = 16
SSM_STATE = 64
SSM_WIDTH = 448
SSM_GROUPS = 28
SWA_HEADS = 8
SWA_KV_HEADS = 2
SWA_GROUP = 4
SWA_WIDTH = 512
SWA_KV_WIDTH = 128
WINDOW = 128
DIL_PAIRS = ((128, 1), (512, 4), (2048, 16))
DIL_HPG = 3
DIL_GW = DIL_HPG * HEAD_DIM
DIL_WIDTH = 576
DIL_KV_WIDTH = 192
HGRN_HEAD_DIM = 128
HGRN_WIDTH = 512
HGRN_HEADS = 4
IN_COLS = 4224
ATTN_SCALE = HEAD_DIM ** -0.5

LANES = 128
FF_TILE = 256
N_FF_TILES = -(-D_FF // FF_TILE)
VMEM_LIMIT = 60 * 1024 * 1024

B_SLOT = SWA_WIDTH + 2 * SWA_KV_WIDTH
C_SLOT = 384
C_K = 256
OFF_B = 0
OFF_C = OFF_B + B_SLOT
OFF_D = 2048
OFF_U = OFF_D + 4 * HGRN_WIDTH
U_SLOT = 512
PROJ_COLS = 4608
WIN_TILE = 1536

S5_CHUNK = 16
S5_NPOW = S5_CHUNK + 1
HGRN_CHUNK = 128
HGRN_LEVELS = (64, 32, 16, 8, 4, 2, 1)


def _proj_segments():
    src = np.cumsum([0, SSM_WIDTH, SWA_WIDTH, SWA_KV_WIDTH, SWA_KV_WIDTH, DIL_WIDTH, DIL_KV_WIDTH, DIL_KV_WIDTH,
                     HGRN_WIDTH, HGRN_WIDTH, HGRN_WIDTH, HGRN_WIDTH])
    s_u, s_qb, s_kb, s_vb, s_qc, s_kc, s_vc, s_qd = src[:8]
    segs = [(OFF_B, s_qb, SWA_WIDTH), (OFF_B + SWA_WIDTH, s_kb, SWA_KV_WIDTH),
            (OFF_B + SWA_WIDTH + SWA_KV_WIDTH, s_vb, SWA_KV_WIDTH)]
    for g in range(3):
        base = OFF_C + g * C_SLOT
        segs += [(base, s_qc + g * DIL_GW, DIL_GW), (base + C_K, s_kc + g * HEAD_DIM, HEAD_DIM),
                 (base + C_K + HEAD_DIM, s_vc + g * HEAD_DIM, HEAD_DIM)]
    segs += [(OFF_D, s_qd, 4 * HGRN_WIDTH), (OFF_U, s_u, SSM_WIDTH)]
    return [(int(a), int(b), int(c)) for a, b, c in segs]


def _rotary_lane_mask():
    m = np.zeros((PROJ_COLS,), np.float32)
    m[OFF_B:OFF_B + SWA_WIDTH + SWA_KV_WIDTH] = 1.0
    for g in range(3):
        base = OFF_C + g * C_SLOT
        m[base:base + DIL_GW] = 1.0
        m[base + C_K:base + C_K + HEAD_DIM] = 1.0
    return m


def _cparams(sem):
    return pltpu.CompilerParams(dimension_semantics=sem, vmem_limit_bytes=VMEM_LIMIT)


def _v3(a):
    return a.reshape(a.shape[0], 1, a.shape[1])


def _rms(x, g):
    return x * lax.rsqrt(jnp.mean(x * x, axis=-1, keepdims=True) + NORM_EPS) * g


def _sigmoid(x):
    return 1.0 / (1.0 + jnp.exp(-x))


def _silu(x):
    return x * _sigmoid(x)


def _dot(a, b):
    return jnp.dot(a, b, preferred_element_type=F32)


def _dot_nt(a, b):
    return lax.dot_general(a, b, (((1,), (1,)), ((), ())), preferred_element_type=F32)


def _dot_tn(a, b):
    return lax.dot_general(a, b, (((0,), (0,)), ((), ())), preferred_element_type=F32)


def _ffn_kernel(x_ref, gpre_ref, wg_ref, wu_ref, wd_ref, gpost_ref, o_ref, xn_ref):
    f = pl.program_id(1)

    @pl.when(f == 0)
    def _():
        xn_ref[...] = _rms(x_ref[...], gpre_ref[...]).astype(BF16)

    xn = xn_ref[...]
    gate = _dot(xn, wg_ref[...].astype(BF16))
    up = _dot(xn, wu_ref[...].astype(BF16))
    valid = D_FF - f * FF_TILE
    col = lax.broadcasted_iota(jnp.int32, (1, FF_TILE), 1)
    row = lax.broadcasted_iota(jnp.int32, (FF_TILE, 1), 0)
    hid = jnp.where(col < valid, _silu(gate) * up, 0.0).astype(BF16)
    wd = jnp.where(row < valid, wd_ref[...], 0.0).astype(BF16)
    contrib = _dot(hid, wd)

    @pl.when(f == 0)
    def _():
        o_ref[...] = contrib

    @pl.when(f > 0)
    def _():
        o_ref[...] += contrib

    @pl.when(f == pl.num_programs(1) - 1)
    def _():
        o_ref[...] = x_ref[...] + 0.5 * _rms(o_ref[...], gpost_ref[...])


def _ffn(h, gpre, wg, wu, wd, gpost, layer, tm):
    m = h.shape[0]
    return pl.pallas_call(
        _ffn_kernel,
        grid=(m // tm, N_FF_TILES),
        in_specs=[
            pl.BlockSpec((tm, D_MODEL), lambda i, f: (i, 0), pipeline_mode=pl.Buffered(1)),
            pl.BlockSpec((None, 1, D_MODEL), lambda i, f: (layer, 0, 0)),
            pl.BlockSpec((None, D_MODEL, FF_TILE), lambda i, f: (layer, 0, f)),
            pl.BlockSpec((None, D_MODEL, FF_TILE), lambda i, f: (layer, 0, f)),
            pl.BlockSpec((None, FF_TILE, D_MODEL), lambda i, f: (layer, f, 0)),
            pl.BlockSpec((None, 1, D_MODEL), lambda i, f: (layer, 0, 0)),
        ],
        out_specs=pl.BlockSpec((tm, D_MODEL), lambda i, f: (i, 0)),
        out_shape=jax.ShapeDtypeStruct((m, D_MODEL), F32),
        scratch_shapes=[pltpu.VMEM((tm, D_MODEL), BF16)],
        compiler_params=_cparams(("parallel", "arbitrary")),
    )(h, _v3(gpre), wg, wu, wd, _v3(gpost))


def _win_prep_kernel(w_ref, o_ref):
    o_ref[...] = jnp.zeros_like(o_ref)
    for dst, src, width in _proj_segments():
        o_ref[:, dst:dst + width] = w_ref[:, src:src + width].astype(BF16)


def _win_prep(w_in, layer):
    rows = 256
    return pl.pallas_call(
        _win_prep_kernel,
        grid=(D_MODEL // rows,),
        in_specs=[pl.BlockSpec((None, rows, IN_COLS), lambda i: (layer, i, 0))],
        out_specs=pl.BlockSpec((rows, PROJ_COLS), lambda i: (i, 0)),
        out_shape=jax.ShapeDtypeStruct((D_MODEL, PROJ_COLS), BF16),
        compiler_params=_cparams(("parallel",)),
    )(w_in)


def _cast_kernel(w_ref, o_ref):
    o_ref[...] = w_ref[...].astype(BF16)


def _cast_bf16(w, layer):
    _, r, c = w.shape
    rows = 256
    return pl.pallas_call(
        _cast_kernel,
        grid=(r // rows,),
        in_specs=[pl.BlockSpec((None, rows, c), lambda i: (layer, i, 0))],
        out_specs=pl.BlockSpec((rows, c), lambda i: (i, 0)),
        out_shape=jax.ShapeDtypeStruct((r, c), BF16),
        compiler_params=_cparams(("parallel",)),
    )(w)


def _win_kernel(x_ref, g_ref, w_ref, cos_ref, sin_ref, o_ref, xn_ref):
    j = pl.program_id(1)

    @pl.when(j == 0)
    def _():
        xn_ref[...] = _rms(x_ref[...], g_ref[...]).astype(BF16)

    y = _dot(xn_ref[...], w_ref[...])
    rot = _rotary_lane_mask()
    for tile in range(PROJ_COLS // WIN_TILE):

        @pl.when(j == tile)
        def _(tile=tile):
            cos = cos_ref[...]
            sin = sin_ref[...]
            lane = lax.broadcasted_iota(jnp.int32, cos.shape, 1)
            first_half = (lane % HEAD_DIM) < (HEAD_DIM // 2)
            for c in range(WIN_TILE // LANES):
                sl = slice(c * LANES, (c + 1) * LANES)
                flags = rot[tile * WIN_TILE + c * LANES:tile * WIN_TILE + (c + 1) * LANES]
                x = y[:, sl]
                lo, hi = bool(flags[:HEAD_DIM].all()), bool(flags[HEAD_DIM:].all())
                if not (lo or hi):
                    o_ref[:, sl] = x
                    continue
                partner = jnp.where(first_half,
                                    pltpu.roll(x, LANES - HEAD_DIM // 2, axis=1),
                                    pltpu.roll(x, HEAD_DIM // 2, axis=1))
                rotated = x * cos + partner * sin
                if lo and hi:
                    o_ref[:, sl] = rotated
                else:
                    is_rot = (lane < HEAD_DIM) if lo else (lane >= HEAD_DIM)
                    o_ref[:, sl] = jnp.where(is_rot, rotated, x)


def _win(h, g, w, cos_t, sin_t, layer, tm):
    m = h.shape[0]
    return pl.pallas_call(
        _win_kernel,
        grid=(m // tm, PROJ_COLS // WIN_TILE),
        in_specs=[
            pl.BlockSpec((tm, D_MODEL), lambda i, j: (i, 0), pipeline_mode=pl.Buffered(1)),
            pl.BlockSpec((None, 1, D_MODEL), lambda i, j: (layer, 0, 0)),
            pl.BlockSpec((D_MODEL, WIN_TILE), lambda i, j: (0, j)),
            pl.BlockSpec((tm, LANES), lambda i, j: (i, 0)),
            pl.BlockSpec((tm, LANES), lambda i, j: (i, 0)),
        ],
        out_specs=pl.BlockSpec((tm, WIN_TILE), lambda i, j: (i, j)),
        out_shape=jax.ShapeDtypeStruct((m, PROJ_COLS), F32),
        scratch_shapes=[pltpu.VMEM((tm, D_MODEL), BF16)],
        compiler_params=_cparams(("parallel", "arbitrary")),
    )(h, _v3(g), w, cos_t, sin_t)


def _rope_tables(pos):
    half = HEAD_DIM // 2
    inv_freq = ROPE_THETA ** (-jnp.arange(half, dtype=F32) / half)
    ang = pos.astype(F32)[:, None] * inv_freq[None, :]
    cos, sin = jnp.cos(ang), jnp.sin(ang)
    reps = LANES // HEAD_DIM
    cos_t = jnp.tile(jnp.concatenate([cos, cos], axis=1), (1, reps))
    sin_t = jnp.tile(jnp.concatenate([-sin, sin], axis=1), (1, reps))
    return cos_t, sin_t


def _band_heads(q, kk, vv, valid, sinks):
    outs, lses = [], []
    for g in range(q.shape[1] // HEAD_DIM):
        qg = q[:, g * HEAD_DIM:(g + 1) * HEAD_DIM].astype(BF16)
        s = _dot_nt(qg, kk) * ATTN_SCALE
        s = jnp.where(valid, s, NEG_BIG)
        m = jnp.max(s, axis=-1, keepdims=True)
        if sinks is not None:
            m = jnp.maximum(m, sinks[g])
        p = jnp.exp(s - m)
        l = jnp.sum(p, axis=-1, keepdims=True)
        if sinks is not None:
            l = l + jnp.exp(sinks[g] - m)
        outs.append(_dot(p.astype(BF16), vv) / l)
        lses.append(m + jnp.log(l))
    return outs, lses


def _band_mask(i):
    w = WINDOW
    r = lax.broadcasted_iota(jnp.int32, (w, 2 * w), 0)
    c = lax.broadcasted_iota(jnp.int32, (w, 2 * w), 1)
    first_key = jnp.where(i == 0, w, 0)
    return (c >= r) & (c <= r + w) & (c >= first_key)


def _swa_prompt_kernel(cur_ref, prev_ref, sink_ref, o_ref):
    valid = _band_mask(pl.program_id(1))
    cur = cur_ref[0]
    prev = prev_ref[0]
    outs = []
    for h in range(SWA_KV_HEADS):
        ks = slice(SWA_WIDTH + h * HEAD_DIM, SWA_WIDTH + (h + 1) * HEAD_DIM)
        vs = slice(SWA_WIDTH + SWA_KV_WIDTH + h * HEAD_DIM, SWA_WIDTH + SWA_KV_WIDTH + (h + 1) * HEAD_DIM)
        kk = jnp.concatenate([prev[:, h * HEAD_DIM:(h + 1) * HEAD_DIM], cur[:, ks]], axis=0).astype(BF16)
        vv = jnp.concatenate([prev[:, SWA_KV_WIDTH + h * HEAD_DIM:SWA_KV_WIDTH + (h + 1) * HEAD_DIM], cur[:, vs]],
                             axis=0).astype(BF16)
        gw = SWA_GROUP * HEAD_DIM
        sinks = [sink_ref[:, h * gw + g * HEAD_DIM:h * gw + g * HEAD_DIM + 1] for g in range(SWA_GROUP)]
        o, _ = _band_heads(cur[:, h * gw:(h + 1) * gw], kk, vv, valid, sinks)
        outs += o
    o_ref[0] = jnp.concatenate(outs, axis=1)


def _swa_prompt(proj, sink):
    n, t, _ = proj.shape
    return pl.pallas_call(
        _swa_prompt_kernel,
        grid=(n, t // WINDOW),
        in_specs=[
            pl.BlockSpec((1, WINDOW, B_SLOT), lambda b, i: (b, i, OFF_B // B_SLOT)),
            pl.BlockSpec((1, WINDOW, 2 * SWA_KV_WIDTH),
                         lambda b, i: (b, jnp.maximum(i - 1, 0), (OFF_B + SWA_WIDTH) // (2 * SWA_KV_WIDTH))),
            pl.BlockSpec((1, SWA_WIDTH), lambda b, i: (0, 0)),
        ],
        out_specs=pl.BlockSpec((1, WINDOW, SWA_WIDTH), lambda b, i: (b, i, 0)),
        out_shape=jax.ShapeDtypeStruct((n, t, SWA_WIDTH), F32),
        compiler_params=_cparams(("parallel", "arbitrary")),
    )(proj, proj, sink)


def _dil_prompt_kernel(cur_ref, prev_ref, o_ref, cur3_ref, out3_ref, *, dil):
    valid = _band_mask(pl.program_id(1))
    nl = C_SLOT // LANES
    for c in range(nl):
        cur3_ref[c] = cur_ref[0, :, c * LANES:(c + 1) * LANES]

    def one(r):
        rows = pl.ds(r, WINDOW, stride=dil) if dil > 1 else slice(None)
        q = jnp.concatenate([cur3_ref[0, rows, :], cur3_ref[1, rows, :]], axis=1)[:, 0:DIL_GW]
        kv = cur3_ref[2, rows, :]
        pkv = prev_ref[0, rows, :]
        kk = jnp.concatenate([pkv[:, 0:HEAD_DIM], kv[:, 0:HEAD_DIM]], axis=0).astype(BF16)
        vv = jnp.concatenate([pkv[:, HEAD_DIM:], kv[:, HEAD_DIM:]], axis=0).astype(BF16)
        o, lse = _band_heads(q, kk, vv, valid, None)
        lse = [jnp.broadcast_to(x, (WINDOW, HEAD_DIM)) for x in lse]
        res = jnp.concatenate(o + lse, axis=1)
        for c in range(nl):
            out3_ref[c, rows, :] = res[:, c * LANES:(c + 1) * LANES]

    if dil <= 4:
        for r in range(dil):
            one(r)
    else:
        lax.fori_loop(0, dil, lambda r, c: (one(r), c)[1], 0)
    for c in range(nl):
        o_ref[0, :, c * LANES:(c + 1) * LANES] = out3_ref[c]


def _dil_prompt(proj, group, dil):
    n, t, _ = proj.shape
    rows = WINDOW * dil
    slot = OFF_C + group * C_SLOT
    return pl.pallas_call(
        functools.partial(_dil_prompt_kernel, dil=dil),
        grid=(n, t // rows),
        in_specs=[
            pl.BlockSpec((1, rows, C_SLOT), lambda b, i: (b, i, slot // C_SLOT)),
            pl.BlockSpec((1, rows, LANES), lambda b, i: (b, jnp.maximum(i - 1, 0), (slot + C_K) // LANES)),
        ],
        out_specs=pl.BlockSpec((1, rows, 2 * DIL_GW), lambda b, i: (b, i, 0)),
        out_shape=jax.ShapeDtypeStruct((n, t, 2 * DIL_GW), F32),
        scratch_shapes=[pltpu.VMEM((C_SLOT // LANES, rows, LANES), F32),
                        pltpu.VMEM((2 * DIL_GW // LANES, rows, LANES), F32)],
        compiler_params=_cparams(("parallel", "arbitrary")),
    )(proj, proj)


def _decode_heads(q, knew, vnew, kc, vc, sinks):
    s = _dot_nt(q.astype(BF16), kc.astype(BF16)) * ATTN_SCALE
    s_new = jnp.sum(q * knew, axis=-1, keepdims=True) * ATTN_SCALE
    m = jnp.maximum(jnp.max(s, axis=-1, keepdims=True), s_new)
    if sinks is not None:
        m = jnp.maximum(m, sinks)
    p = jnp.exp(s - m)
    p_new = jnp.exp(s_new - m)
    l = jnp.sum(p, axis=-1, keepdims=True) + p_new
    if sinks is not None:
        l = l + jnp.exp(sinks - m)
    o = (_dot(p.astype(BF16), vc.astype(BF16)) + p_new * vnew) / l
    return o, m + jnp.log(l)


def _rows_of(x, heads):
    return jnp.concatenate([x[:, g * HEAD_DIM:(g + 1) * HEAD_DIM] for g in range(heads)], axis=0)


def _swa_sample_kernel(p_ref, cache_ref, sink_ref, o_ref, *, bn):
    for b in range(bn):
        row = p_ref[b:b + 1, :]
        outs = []
        for h in range(SWA_KV_HEADS):
            gw = SWA_GROUP * HEAD_DIM
            q = _rows_of(row[:, h * gw:(h + 1) * gw], SWA_GROUP)
            knew = row[:, SWA_WIDTH + h * HEAD_DIM:SWA_WIDTH + (h + 1) * HEAD_DIM]
            vnew = row[:, SWA_WIDTH + SWA_KV_WIDTH + h * HEAD_DIM:SWA_WIDTH + SWA_KV_WIDTH + (h + 1) * HEAD_DIM]
            sinks = _rows_of(sink_ref[:, h * gw:(h + 1) * gw], SWA_GROUP)[:, 0:1]
            o, _ = _decode_heads(q, knew, vnew, cache_ref[b, :, 0, h, :], cache_ref[b, :, 1, h, :], sinks)
            outs += [o[g:g + 1] for g in range(SWA_GROUP)]
        o_ref[b:b + 1, :] = jnp.concatenate(outs, axis=1)


def _swa_sample(proj, cache, sink, layer):
    n = proj.shape[0]
    bn = 8
    return pl.pallas_call(
        functools.partial(_swa_sample_kernel, bn=bn),
        grid=(n // bn,),
        in_specs=[
            pl.BlockSpec((bn, B_SLOT), lambda i: (i, OFF_B // B_SLOT)),
            pl.BlockSpec((None, bn, WINDOW, 2, SWA_KV_HEADS, HEAD_DIM), lambda i: (layer, i, 0, 0, 0, 0)),
            pl.BlockSpec((1, SWA_WIDTH), lambda i: (0, 0)),
        ],
        out_specs=pl.BlockSpec((bn, SWA_WIDTH), lambda i: (i, 0)),
        out_shape=jax.ShapeDtypeStruct((n, SWA_WIDTH), F32),
        compiler_params=_cparams(("parallel",)),
    )(proj, cache, sink)


def _dil_sample_kernel(p_ref, cache_ref, o_ref, *, bn):
    for b in range(bn):
        row = p_ref[b:b + 1, :]
        q = _rows_of(row[:, 0:DIL_GW], DIL_HPG)
        o, lse = _decode_heads(q, row[:, C_K:C_K + HEAD_DIM], row[:, C_K + HEAD_DIM:C_K + 2 * HEAD_DIM],
                               cache_ref[b, :, 0, :], cache_ref[b, :, 1, :], None)
        lse = jnp.broadcast_to(lse, (DIL_HPG, HEAD_DIM))
        o_ref[b:b + 1, :] = jnp.concatenate([o[g:g + 1] for g in range(DIL_HPG)]
                                            + [lse[g:g + 1] for g in range(DIL_HPG)], axis=1)


def _dil_sample(proj, cache, group, dil, layer):
    n = proj.shape[0]
    bn = 8
    slot = OFF_C + group * C_SLOT
    return pl.pallas_call(
        functools.partial(_dil_sample_kernel, bn=bn),
        grid=(n // bn,),
        in_specs=[
            pl.BlockSpec((bn, C_SLOT), lambda i: (i, slot // C_SLOT)),
            pl.BlockSpec((None, bn, WINDOW, None, 2, HEAD_DIM), lambda i: (layer, i, 0, 0, 0, 0)),
        ],
        out_specs=pl.BlockSpec((bn, 2 * DIL_GW), lambda i: (i, 0)),
        out_shape=jax.ShapeDtypeStruct((n, 2 * DIL_GW), F32),
        compiler_params=_cparams(("parallel",)),
    )(proj, cache)


def _s5_param_kernel(are_ref, aim_ref, ldt_ref, bre_ref, bim_ref, cre_ref, cim_ref,
                     lam_re_ref, lam_im_ref, cl_re_ref, cl_im_ref, r_re_ref, r_im_ref, k_ref):
    ar = are_ref[0]
    ai = aim_ref[0]
    dt = jnp.exp(ldt_ref[0])
    nrow = lam_re_ref.shape[1]
    ri = lax.broadcasted_iota(jnp.int32, (nrow, 1), 0)
    pw = jnp.where(ri < S5_NPOW, ri, jnp.left_shift(S5_CHUNK, jnp.maximum(ri - S5_NPOW, 0))).astype(F32)
    mag = jnp.exp(pw * (ar * dt))
    ang = pw * (ai * dt)
    lam_re = mag * jnp.cos(ang)
    lam_im = mag * jnp.sin(ang)
    lam_re_ref[0] = lam_re
    lam_im_ref[0] = lam_im
    l1r, l1i = lam_re[1:2], lam_im[1:2]
    den = ar * ar + ai * ai
    z_re = ((l1r - 1.0) * ar + l1i * ai) / den
    z_im = (l1i * ar - (l1r - 1.0) * ai) / den
    b_re, b_im = bre_ref[0], bim_ref[0]
    bb_re = z_re * b_re - z_im * b_im
    bb_im = z_re * b_im + z_im * b_re
    c_re, c_im = cre_ref[0], cim_ref[0]
    cls_re, cls_im = [], []
    for d in range(S5_NPOW):
        lr, li = lam_re[d:d + 1], lam_im[d:d + 1]
        cr = c_re * lr - c_im * li
        ci = c_re * li + c_im * lr
        cl_re_ref[0, d * SSM_GROUP:(d + 1) * SSM_GROUP, :] = cr
        cl_im_ref[0, d * SSM_GROUP:(d + 1) * SSM_GROUP, :] = ci
        if d < S5_CHUNK:
            cls_re.append(cr)
            cls_im.append(ci)
    for s in range(S5_CHUNK):
        lr = lam_re[S5_CHUNK - 1 - s:S5_CHUNK - s]
        li = lam_im[S5_CHUNK - 1 - s:S5_CHUNK - s]
        r_re_ref[0, s * SSM_GROUP:(s + 1) * SSM_GROUP, :] = lr * bb_re - li * bb_im
        r_im_ref[0, s * SSM_GROUP:(s + 1) * SSM_GROUP, :] = lr * bb_im + li * bb_re
    call_re = jnp.concatenate(cls_re, axis=0)
    call_im = jnp.concatenate(cls_im, axis=0)
    hp = lax.Precision.HIGHEST
    nt = (((1,), (1,)), ((), ()))
    k_ref[0] = (lax.dot_general(call_re, bb_re, nt, precision=hp, preferred_element_type=F32)
                - lax.dot_general(call_im, bb_im, nt, precision=hp, preferred_element_type=F32))


def _s5_params(a_re, a_im, log_dt, b_re, b_im, c_re, c_im, nsteps):
    g = SSM_GROUPS
    nrow = ((S5_NPOW + nsteps + 7) // 8) * 8
    row = lambda i: (i, 0, 0)
    shp = lambda r, c: jax.ShapeDtypeStruct((g, r, c), F32)
    ncl = S5_NPOW * SSM_GROUP
    nr = S5_CHUNK * SSM_GROUP
    return pl.pallas_call(
        _s5_param_kernel,
        grid=(g,),
        in_specs=[
            pl.BlockSpec((1, 1, SSM_STATE), row),
            pl.BlockSpec((1, 1, SSM_STATE), row),
            pl.BlockSpec((1, 1, 1), row),
            pl.BlockSpec((1, SSM_GROUP, SSM_STATE), row),
            pl.BlockSpec((1, SSM_GROUP, SSM_STATE), row),
            pl.BlockSpec((1, SSM_GROUP, SSM_STATE), row),
            pl.BlockSpec((1, SSM_GROUP, SSM_STATE), row),
        ],
        out_specs=[
            pl.BlockSpec((1, nrow, SSM_STATE), row), pl.BlockSpec((1, nrow, SSM_STATE), row),
            pl.BlockSpec((1, ncl, SSM_STATE), row), pl.BlockSpec((1, ncl, SSM_STATE), row),
            pl.BlockSpec((1, nr, SSM_STATE), row), pl.BlockSpec((1, nr, SSM_STATE), row),
            pl.BlockSpec((1, nr, SSM_GROUP), row),
        ],
        out_shape=[shp(nrow, SSM_STATE), shp(nrow, SSM_STATE), shp(ncl, SSM_STATE), shp(ncl, SSM_STATE),
                   shp(nr, SSM_STATE), shp(nr, SSM_STATE), shp(nr, SSM_GROUP)],
        compiler_params=_cparams(("parallel",)),
    )(a_re.reshape(g, 1, SSM_STATE), a_im.reshape(g, 1, SSM_STATE), log_dt.reshape(g, 1, 1),
      jnp.swapaxes(b_re, 1, 2), jnp.swapaxes(b_im, 1, 2), c_re, c_im)


def _s5_prompt_kernel(u_ref, mt_ref, r_re_ref, r_im_ref, cl_re_ref, cl_im_ref, lam_re_ref, lam_im_ref, d_ref,
                      y_ref, xre_ref, xim_ref, *, nseq, nchunk):
    u = u_ref[0]
    ub = u.astype(BF16)
    y = _dot(ub, mt_ref[0].astype(BF16))
    xr = _dot(ub, r_re_ref[0].astype(BF16))
    xi = _dot(ub, r_im_ref[0].astype(BF16))
    rows = nseq * nchunk
    cidx = lax.broadcasted_iota(jnp.int32, (rows, SSM_STATE), 0) % nchunk
    lam_re, lam_im = lam_re_ref[0], lam_im_ref[0]
    for k in range(int(math.log2(nchunk))):
        sh = 1 << k
        lr = lam_re[S5_NPOW + k:S5_NPOW + k + 1]
        li = lam_im[S5_NPOW + k:S5_NPOW + k + 1]
        ok = cidx >= sh
        pr = jnp.where(ok, pltpu.roll(xr, sh, axis=0), 0.0)
        pi = jnp.where(ok, pltpu.roll(xi, sh, axis=0), 0.0)
        xr, xi = xr + lr * pr - li * pi, xi + lr * pi + li * pr
    ok = cidx >= 1
    er = jnp.where(ok, pltpu.roll(xr, 1, axis=0), 0.0)
    ei = jnp.where(ok, pltpu.roll(xi, 1, axis=0), 0.0)
    cl_re = cl_re_ref[0][SSM_GROUP:, :]
    cl_im = cl_im_ref[0][SSM_GROUP:, :]
    y = y + _dot_nt(er.astype(BF16), cl_re.astype(BF16)) - _dot_nt(ei.astype(BF16), cl_im.astype(BF16))
    y_ref[0] = y + d_ref[0] * u
    last = [s * nchunk + nchunk - 1 for s in range(nseq)]
    xre_ref[0] = jnp.concatenate([xr[i:i + 1] for i in last], axis=0)
    xim_ref[0] = jnp.concatenate([xi[i:i + 1] for i in last], axis=0)


def _s5_prompt(u, mt, r_re, r_im, cl_re, cl_im, lam_re, lam_im, d_t, nseq, nchunk):
    g = SSM_GROUPS
    rows = nseq * nchunk
    cw = S5_CHUNK * SSM_GROUP
    row = lambda i: (i, 0, 0)
    full = lambda a: pl.BlockSpec((1,) + a.shape[1:], row)
    return pl.pallas_call(
        functools.partial(_s5_prompt_kernel, nseq=nseq, nchunk=nchunk),
        grid=(g,),
        in_specs=[full(u), full(mt), full(r_re), full(r_im), full(cl_re), full(cl_im), full(lam_re), full(lam_im),
                  full(d_t)],
        out_specs=[pl.BlockSpec((1, rows, cw), row), pl.BlockSpec((1, nseq, SSM_STATE), row),
                   pl.BlockSpec((1, nseq, SSM_STATE), row)],
        out_shape=[jax.ShapeDtypeStruct((g, rows, cw), F32), jax.ShapeDtypeStruct((g, nseq, SSM_STATE), F32),
                   jax.ShapeDtypeStruct((g, nseq, SSM_STATE), F32)],
        compiler_params=_cparams(("parallel",)),
    )(u, mt, r_re, r_im, cl_re, cl_im, lam_re, lam_im, d_t)


def _s5_sample_kernel(u_ref, x0r_ref, x0i_ref, r_re_ref, r_im_ref, cl_re_ref, cl_im_ref, lam_re_ref, lam_im_ref,
                      d_ref, y_ref, xre_ref, xim_ref):
    u = u_ref[0]
    ub = u.astype(BF16)
    lo = (S5_CHUNK - 1) * SSM_GROUP
    bb_re = r_re_ref[0][lo:lo + SSM_GROUP, :]
    bb_im = r_im_ref[0][lo:lo + SSM_GROUP, :]
    lr, li = lam_re_ref[0][1:2], lam_im_ref[0][1:2]
    x0r, x0i = x0r_ref[0], x0i_ref[0]
    xr = _dot(ub, bb_re.astype(BF16)) + (lr * x0r - li * x0i)
    xi = _dot(ub, bb_im.astype(BF16)) + (lr * x0i + li * x0r)
    c_re = cl_re_ref[0][:SSM_GROUP, :]
    c_im = cl_im_ref[0][:SSM_GROUP, :]
    y_ref[0] = (_dot_nt(xr.astype(BF16), c_re.astype(BF16)) - _dot_nt(xi.astype(BF16), c_im.astype(BF16))
                + d_ref[0] * u)
    xre_ref[0] = xr
    xim_ref[0] = xi


def _s5_sample(u, x0r, x0i, r_re, r_im, cl_re, cl_im, lam_re, lam_im, d):
    g, n = u.shape[0], u.shape[1]
    row = lambda i: (i, 0, 0)
    full = lambda a: pl.BlockSpec((1,) + a.shape[1:], row)
    return pl.pallas_call(
        _s5_sample_kernel,
        grid=(g,),
        in_specs=[full(u), full(x0r), full(x0i), full(r_re), full(r_im), full(cl_re), full(cl_im), full(lam_re),
                  full(lam_im), full(d)],
        out_specs=[pl.BlockSpec((1, n, SSM_GROUP), row), pl.BlockSpec((1, n, SSM_STATE), row),
                   pl.BlockSpec((1, n, SSM_STATE), row)],
        out_shape=[jax.ShapeDtypeStruct((g, n, SSM_GROUP), F32), jax.ShapeDtypeStruct((g, n, SSM_STATE), F32),
                   jax.ShapeDtypeStruct((g, n, SSM_STATE), F32)],
        compiler_params=_cparams(("parallel",)),
    )(u, x0r, x0i, r_re, r_im, cl_re, cl_im, lam_re, lam_im, d)


def _log_sigmoid(x):
    return jnp.minimum(x, 0.0) - jnp.log1p(jnp.exp(-jnp.abs(x)))


def _logaddexp(a, b):
    return jnp.maximum(a, b) + jnp.log1p(jnp.exp(-jnp.abs(a - b)))


def _hgrn_gates(fpre, lb):
    log_f = _logaddexp(jnp.log(jnp.maximum(lb, LB_FLOOR)), jnp.log1p(-lb) + _log_sigmoid(fpre))
    k = (1.0 - lb) * _sigmoid(-fpre)
    return log_f, k


def _hgrn_scan_matrix():
    c = HGRN_CHUNK
    tri = np.tril(np.ones((c, c), np.float32))
    t = np.arange(c)
    mats = [tri]
    for m in HGRN_LEVELS:
        mats.append(tri[(t // (2 * m)) * (2 * m) + m - 1])
    return np.concatenate(mats, axis=0)


def _hgrn_prompt_kernel(p_ref, lbraw_ref, gn_ref, scan_ref, o_ref, s_ref, st_ref, *, layer):
    cidx = pl.program_id(1)
    c = HGRN_CHUNK
    hd = HGRN_HEAD_DIM

    @pl.when(cidx == 0)
    def _():
        st_ref[...] = jnp.zeros_like(st_ref)

    a = lbraw_ref[...]
    e = jnp.exp(a - jnp.max(a, axis=0, keepdims=True))
    p = e / jnp.sum(e, axis=0, keepdims=True)
    run = p[0:1]
    for l in range(1, layer + 1):
        run = run + p[l:l + 1]
    lb_all = run - p[0:1]

    r = lax.broadcasted_iota(jnp.int32, (c, c), 0)
    s = lax.broadcasted_iota(jnp.int32, (c, c), 1)
    pairs = []
    for m in HGRN_LEVELS:
        sh = int(math.log2(2 * m))
        pairs.append(((r >> sh) == (s >> sh)) & ((r & m) != 0) & ((s & m) == 0))
    scan = scan_ref[...]
    for h in range(HGRN_HEADS):
        col = lambda part: p_ref[0, :, part * HGRN_WIDTH + h * hd:part * HGRN_WIDTH + (h + 1) * hd]
        log_f, k = _hgrn_gates(col(1), lb_all[:, h * hd:(h + 1) * hd])
        q = _silu(col(0))
        v = col(2)
        cum = jnp.dot(scan, log_f, precision=lax.Precision.HIGHEST, preferred_element_type=F32)
        b = cum[0:c]
        att = jnp.where(r == s, jnp.sum(q * k, axis=-1, keepdims=True), 0.0)
        for lvl, m in enumerate(HGRN_LEVELS):
            bref = cum[(lvl + 1) * c:(lvl + 2) * c]
            qs = q * jnp.exp(jnp.minimum(b - bref, 0.0))
            ks = k * jnp.exp(jnp.minimum(bref - b, 0.0))
            att = att + jnp.where(pairs[lvl], _dot_nt(qs.astype(BF16), ks.astype(BF16)), 0.0)
        st = st_ref[h]
        vb = v.astype(BF16)
        o = _dot(att.astype(BF16), vb) + _dot_nt((q * jnp.exp(b)).astype(BF16), st.astype(BF16))
        bl = b[c - 1:c]
        st_new = jnp.exp(bl) * st + _dot_tn(vb, (k * jnp.exp(bl - b)).astype(BF16))
        st_ref[h] = st_new
        o = o * lax.rsqrt(jnp.mean(o * o, axis=-1, keepdims=True) + NORM_EPS)
        o_ref[0, :, h * hd:(h + 1) * hd] = o * gn_ref[:, h * hd:(h + 1) * hd] * _silu(col(3))

    @pl.when(cidx == pl.num_programs(1) - 1)
    def _():
        s_ref[0] = st_ref[...]


def _hgrn_prompt(proj, lbraw, gn, layer):
    n, t, _ = proj.shape
    c = HGRN_CHUNK
    hd = HGRN_HEAD_DIM
    scan = jnp.asarray(_hgrn_scan_matrix())
    return pl.pallas_call(
        functools.partial(_hgrn_prompt_kernel, layer=layer),
        grid=(n, t // c),
        in_specs=[pl.BlockSpec((1, c, 4 * HGRN_WIDTH), lambda b, j: (b, j, OFF_D // (4 * HGRN_WIDTH))),
                  pl.BlockSpec((DEPTH, HGRN_WIDTH), lambda b, j: (0, 0)),
                  pl.BlockSpec((None, 1, HGRN_WIDTH), lambda b, j: (layer, 0, 0)),
                  pl.BlockSpec(scan.shape, lambda b, j: (0, 0))],
        out_specs=[pl.BlockSpec((1, c, HGRN_WIDTH), lambda b, j: (b, j, 0)),
                   pl.BlockSpec((1, HGRN_HEADS, hd, hd), lambda b, j: (b, 0, 0, 0))],
        out_shape=[jax.ShapeDtypeStruct((n, t, HGRN_WIDTH), F32),
                   jax.ShapeDtypeStruct((n, HGRN_HEADS, hd, hd), F32)],
        scratch_shapes=[pltpu.VMEM((HGRN_HEADS, hd, hd), F32)],
        compiler_params=_cparams(("parallel", "arbitrary")),
    )(proj, lbraw, _v3(gn), scan)


def _hgrn_sample_kernel(qt_ref, ft_ref, lbt_ref, v_ref, g_ref, gn_ref, s0_ref, o_ref, s_ref, *, layer, n):
    hd = HGRN_HEAD_DIM
    a = lbt_ref[0]
    e = jnp.exp(a - jnp.max(a, axis=1, keepdims=True))
    p = e / jnp.sum(e, axis=1, keepdims=True)
    run = p[:, 0:1]
    for l in range(1, layer + 1):
        run = run + p[:, l:l + 1]
    lb = run - p[:, 0:1]
    log_f, k = _hgrn_gates(ft_ref[0], lb)
    f = jnp.exp(log_f)
    q = _silu(qt_ref[0])
    v = v_ref[...]
    gate = g_ref[...]
    rows = []
    for b in range(n):
        s_new = f[:, b:b + 1] * s0_ref[b, 0] + k[:, b:b + 1] * v[b:b + 1, :]
        s_ref[b, 0] = s_new
        rows.append(jnp.sum(q[:, b:b + 1] * s_new, axis=0, keepdims=True))
    o = jnp.concatenate(rows, axis=0)
    o = o * lax.rsqrt(jnp.mean(o * o, axis=-1, keepdims=True) + NORM_EPS)
    o_ref[...] = o * gn_ref[...] * _silu(gate)


def _hgrn_sample(qt, ft, lbt, proj, gn, s0, layer):
    n = proj.shape[0]
    hd = HGRN_HEAD_DIM
    v_blk = (OFF_D + 2 * HGRN_WIDTH) // hd
    g_blk = (OFF_D + 3 * HGRN_WIDTH) // hd
    return pl.pallas_call(
        functools.partial(_hgrn_sample_kernel, layer=layer, n=n),
        grid=(HGRN_HEADS,),
        in_specs=[pl.BlockSpec((1, hd, n), lambda h: (h, 0, 0)),
                  pl.BlockSpec((1, hd, n), lambda h: (h, 0, 0)),
                  pl.BlockSpec((1, hd, DEPTH), lambda h: (h, 0, 0)),
                  pl.BlockSpec((n, hd), lambda h: (0, v_blk + h)),
                  pl.BlockSpec((n, hd), lambda h: (0, g_blk + h)),
                  pl.BlockSpec((None, 1, hd), lambda h: (layer, 0, h)),
                  pl.BlockSpec((None, n, 1, hd, hd), lambda h: (layer, 0, h, 0, 0))],
        out_specs=[pl.BlockSpec((n, hd), lambda h: (0, h)),
                   pl.BlockSpec((n, 1, hd, hd), lambda h: (0, h, 0, 0))],
        out_shape=[jax.ShapeDtypeStruct((n, HGRN_WIDTH), F32),
                   jax.ShapeDtypeStruct((n, HGRN_HEADS, hd, hd), F32)],
        compiler_params=_cparams(("parallel",)),
    )(qt, ft, lbt, proj, proj, _v3(gn), s0)


def _gelu_tanh(x):
    return 0.5 * x * (1.0 + jnp.tanh(math.sqrt(2.0 / math.pi) * (x + 0.044715 * (x * x * x))))


def _mixout_kernel(h_ref, ya_ref, ob_ref, oc0_ref, oc1_ref, oc2_ref, od_ref,
                   wglu_ref, bglu_ref, ga_ref, gb_ref, gc_ref, wo_ref, gpost_ref, o_ref):
    z = _gelu_tanh(ya_ref[...])
    gate = _sigmoid(_dot(z.astype(BF16), wglu_ref[...].astype(BF16)) + bglu_ref[...])
    out_a = _rms(z * gate, ga_ref[...])
    out_b = _rms(ob_ref[...], gb_ref[...])
    ocs = [oc0_ref, oc1_ref, oc2_ref]
    ls = [r[:, DIL_GW:2 * DIL_GW] for r in ocs]
    m = jnp.maximum(jnp.maximum(ls[0], ls[1]), ls[2])
    es = [jnp.exp(l - m) for l in ls]
    den = es[0] + es[1] + es[2]
    cs = [ocs[gi][:, 0:DIL_GW] * (es[gi] / den) for gi in range(3)]
    ss = sum(jnp.sum(c * c, axis=-1, keepdims=True) for c in cs)
    inv = lax.rsqrt(ss / DIL_WIDTH + NORM_EPS)
    gc = gc_ref[...]
    o1, o2, o3 = SSM_WIDTH, SSM_WIDTH + SWA_WIDTH, SSM_WIDTH + SWA_WIDTH + DIL_WIDTH
    wo = lambda lo, hi: wo_ref[lo:hi, :]
    mix = _dot(out_a.astype(BF16), wo(0, o1)) + _dot(out_b.astype(BF16), wo(o1, o2))
    for gi in range(3):
        c = cs[gi] * inv * gc[:, gi * DIL_GW:(gi + 1) * DIL_GW]
        mix = mix + _dot(c.astype(BF16), wo(o2 + gi * DIL_GW, o2 + (gi + 1) * DIL_GW))
    mix = mix + _dot(od_ref[...].astype(BF16), wo(o3, D_MODEL))
    o_ref[...] = h_ref[...] + _rms(mix, gpost_ref[...])


def _mixout(h, ya, ob, ocs, od, lp_all, w_out_l, layer, tm):
    m = h.shape[0]
    rowblk = lambda a: pl.BlockSpec((tm, a.shape[1]), lambda i: (i, 0))
    vec = lambda a: pl.BlockSpec((None, 1, a.shape[1]), lambda i: (layer, 0, 0))
    mat = lambda a: pl.BlockSpec((None,) + a.shape[1:], lambda i: (layer, 0, 0))
    whole = lambda a: pl.BlockSpec(a.shape, lambda i: (0, 0), pipeline_mode=pl.Buffered(1))
    acts = [h, ya, ob, *ocs, od]
    names = ['ssm_w_glu', 'ssm_b_glu', 'out_norm_a', 'out_norm_b', 'out_norm_c', 'w_out', 'mix_norm_post']
    params = [w_out_l if k == 'w_out' else lp_all[k] for k in names]
    spec = lambda k, p: whole(p) if k == 'w_out' else (mat(p) if p.ndim == 3 else vec(p))
    return pl.pallas_call(
        _mixout_kernel,
        grid=(m // tm,),
        in_specs=[rowblk(a) for a in acts] + [spec(k, p) for k, p in zip(names, params)],
        out_specs=pl.BlockSpec((tm, D_MODEL), lambda i: (i, 0)),
        out_shape=jax.ShapeDtypeStruct((m, D_MODEL), F32),
        compiler_params=_cparams(("parallel",)),
    )(*acts, *[_v3(p) if p.ndim == 2 and k != 'w_out' else p for k, p in zip(names, params)])


def _s5_toeplitz(kmat):
    g = kmat.shape[0]
    k4 = kmat.reshape(g, S5_CHUNK, SSM_GROUP, SSM_GROUP)
    k4 = jnp.concatenate([k4, jnp.zeros((g, 1, SSM_GROUP, SSM_GROUP), F32)], axis=1)
    s = np.arange(S5_CHUNK)[:, None]
    t = np.arange(S5_CHUNK)[None, :]
    idx = np.where(t >= s, t - s, S5_CHUNK)
    m = k4[:, idx]
    return m.transpose(0, 1, 4, 2, 3).reshape(g, S5_CHUNK * SSM_GROUP, S5_CHUNK * SSM_GROUP)


def _ffn_block(h, P, name, layer, tm):
    return _ffn(h, P[name + '_norm_pre'], P[name + '_w_gate'], P[name + '_w_up'], P[name + '_w_down'],
                P[name + '_norm_post'], layer, tm)


def _prompt_mixer(h, P, w_in_l, w_out_l, s5p, layer, n, t, rope, tm):
    m = n * t
    proj2 = _win(h, P['mix_norm_pre'], w_in_l, rope[0], rope[1], layer, tm)
    proj = proj2.reshape(n, t, PROJ_COLS)

    lam_re, lam_im, cl_re, cl_im, r_re, r_im, kmat = s5p
    nchunk = t // S5_CHUNK
    u = proj2[:, OFF_U:OFF_U + SSM_WIDTH].reshape(n, nchunk, S5_CHUNK, SSM_GROUPS, SSM_GROUP)
    u = u.transpose(3, 0, 1, 2, 4).reshape(SSM_GROUPS, n * nchunk, S5_CHUNK * SSM_GROUP)
    d_t = jnp.tile(P['ssm_d'][layer], (1, S5_CHUNK)).reshape(SSM_GROUPS, 1, S5_CHUNK * SSM_GROUP)
    y, xre, xim = _s5_prompt(u, _s5_toeplitz(kmat), r_re, r_im, cl_re, cl_im, lam_re, lam_im, d_t, n, nchunk)
    ya = y.reshape(SSM_GROUPS, n, nchunk, S5_CHUNK, SSM_GROUP).transpose(1, 2, 3, 0, 4).reshape(m, SSM_WIDTH)
    ssm_new = jnp.stack([xre, xim], axis=-1).transpose(1, 0, 2, 3)

    sink = jnp.repeat(P['swa_sinks'][layer], HEAD_DIM).reshape(1, SWA_WIDTH)
    ob = _swa_prompt(proj, sink).reshape(m, SWA_WIDTH)
    keep = min(WINDOW, t)
    kv = proj[:, t - keep:, OFF_B + SWA_WIDTH:OFF_B + B_SLOT]
    swa_new = kv.reshape(n, keep, 2, SWA_KV_HEADS, HEAD_DIM)

    ocs, dil_new = [], []
    for gi, (win, dil) in enumerate(DIL_PAIRS):
        ocs.append(_dil_prompt(proj, gi, dil).reshape(m, 2 * DIL_GW))
        keep = min(win, t)
        lo = OFF_C + gi * C_SLOT + C_K
        dil_new.append(proj[:, t - keep:, lo:lo + 2 * HEAD_DIM].reshape(n, keep, 2, HEAD_DIM))

    od, st = _hgrn_prompt(proj, P['hgrn_lower_bounds'], P['out_norm_d'], layer)
    hgrn_new = jnp.swapaxes(st, -1, -2)

    h = _mixout(h, ya, ob, ocs, od.reshape(m, HGRN_WIDTH), P, w_out_l, layer, min(tm, 512))
    return h, (ssm_new, swa_new, dil_new[0], dil_new[1], dil_new[2], hgrn_new)


def _sample_mixer(h, P, w_in_l, w_out_l, s5p, layer, caches, rope):
    n = h.shape[0]
    state_ssm, cache_swa, cache_d0, cache_d1, cache_d2, state_hgrn = caches
    proj = _win(h, P['mix_norm_pre'], w_in_l, rope[0], rope[1], layer, n)

    lam_re, lam_im, cl_re, cl_im, r_re, r_im, _ = s5p
    u = proj[:, OFF_U:OFF_U + SSM_WIDTH].reshape(n, SSM_GROUPS, SSM_GROUP).transpose(1, 0, 2)
    x0 = state_ssm[layer].transpose(1, 0, 2, 3)
    y, xre, xim = _s5_sample(u, x0[..., 0], x0[..., 1], r_re, r_im, cl_re, cl_im, lam_re, lam_im,
                             P['ssm_d'][layer].reshape(SSM_GROUPS, 1, SSM_GROUP))
    ya = y.transpose(1, 0, 2).reshape(n, SSM_WIDTH)
    ssm_new = jnp.stack([xre, xim], axis=-1).transpose(1, 0, 2, 3)

    sink = jnp.repeat(P['swa_sinks'][layer], HEAD_DIM).reshape(1, SWA_WIDTH)
    ob = _swa_sample(proj, cache_swa, sink, layer)
    swa_new = proj[:, OFF_B + SWA_WIDTH:OFF_B + B_SLOT].reshape(n, 1, 2, SWA_KV_HEADS, HEAD_DIM)

    ocs, dil_new = [], []
    for gi, (buf, (win, dil)) in enumerate(zip((cache_d0, cache_d1, cache_d2), DIL_PAIRS)):
        folded = buf.reshape(DEPTH, n, buf.shape[2] // dil, dil, 2, HEAD_DIM)
        ocs.append(_dil_sample(proj, folded, gi, dil, layer))
        lo = OFF_C + gi * C_SLOT + C_K
        dil_new.append(proj[:, lo:lo + 2 * HEAD_DIM].reshape(n, 1, 2, HEAD_DIM))

    part = lambda i: proj[:, OFF_D + i * HGRN_WIDTH:OFF_D + (i + 1) * HGRN_WIDTH].reshape(n, HGRN_HEADS, HGRN_HEAD_DIM)
    cm = lambda a: a.transpose(1, 2, 0)
    lbt = P['hgrn_lower_bounds'].reshape(DEPTH, HGRN_HEADS, HGRN_HEAD_DIM).transpose(1, 2, 0)
    od, hgrn_new = _hgrn_sample(cm(part(0)), cm(part(1)), lbt, proj, P['out_norm_d'], state_hgrn, layer)

    h = _mixout(h, ya, ob, ocs, od, P, w_out_l, layer, n)
    return h, (ssm_new, swa_new, dil_new[0], dil_new[1], dil_new[2], hgrn_new)


def _forward(x_prompt, x_sample, caches, P):
    n, t, _ = x_prompt.shape
    ns, ts, _ = x_sample.shape
    assert ts == 1
    tm = 1024
    rope_p = _rope_tables(jnp.tile(jnp.arange(t, dtype=jnp.int32), n))
    rope_s = _rope_tables(jnp.full((ns,), PAST_LEN, dtype=jnp.int32))
    nsteps = int(math.log2(t // S5_CHUNK))
    h_p = x_prompt.reshape(n * t, D_MODEL)
    h_s = x_sample.reshape(ns, D_MODEL)
    new_p = [[] for _ in range(6)]
    new_s = [[] for _ in range(6)]
    for l in range(DEPTH):
        w_in_l = _win_prep(P['w_in'], l)
        w_out_l = _cast_bf16(P['w_out'], l)
        s5p = _s5_params(P['ssm_a_re'][l], P['ssm_a_im'][l], P['ssm_log_dt'][l], P['ssm_b_re'][l], P['ssm_b_im'][l],
                         P['ssm_c_re'][l], P['ssm_c_im'][l], nsteps)
        h_p = _ffn_block(h_p, P, 'ffn1', l, tm)
        h_p, st_p = _prompt_mixer(h_p, P, w_in_l, w_out_l, s5p, l, n, t, rope_p, tm)
        h_p = _ffn_block(h_p, P, 'ffn2', l, tm)
        h_s = _ffn_block(h_s, P, 'ffn1', l, ns)
        h_s, st_s = _sample_mixer(h_s, P, w_in_l, w_out_l, s5p, l, caches, rope_s)
        h_s = _ffn_block(h_s, P, 'ffn2', l, ns)
        for i in range(6):
            new_p[i].append(st_p[i])
            new_s[i].append(st_s[i])
    new_p = [jnp.stack(a, axis=0) for a in new_p]
    new_s = [jnp.stack(a, axis=0) for a in new_s]
    return (h_p.reshape(n, t, D_MODEL), h_s.reshape(ns, ts, D_MODEL), new_p[0], new_s[0], new_p[1], new_s[1],
            new_p[2], new_s[2], new_p[3], new_s[3], new_p[4], new_s[4], new_p[5], new_s[5])


def kernel(x_prompt, x_sample, state_ssm, cache_swa_kv, cache_dil0_kv, cache_dil1_kv, cache_dil2_kv, state_hgrn, ffn1_norm_pre, ffn1_w_gate, ffn1_w_up, ffn1_w_down, ffn1_norm_post, mix_norm_pre, w_in, ssm_a_re, ssm_a_im, ssm_log_dt, ssm_b_re, ssm_b_im, ssm_c_re, ssm_c_im, ssm_d, ssm_w_glu, ssm_b_glu, swa_sinks, hgrn_lower_bounds, out_norm_a, out_norm_b, out_norm_c, out_norm_d, w_out, mix_norm_post, ffn2_norm_pre, ffn2_w_gate, ffn2_w_up, ffn2_w_down, ffn2_norm_post):
    P = dict(ffn1_norm_pre=ffn1_norm_pre, ffn1_w_gate=ffn1_w_gate, ffn1_w_up=ffn1_w_up, ffn1_w_down=ffn1_w_down,
             ffn1_norm_post=ffn1_norm_post, mix_norm_pre=mix_norm_pre, w_in=w_in, ssm_a_re=ssm_a_re,
             ssm_a_im=ssm_a_im, ssm_log_dt=ssm_log_dt, ssm_b_re=ssm_b_re, ssm_b_im=ssm_b_im, ssm_c_re=ssm_c_re,
             ssm_c_im=ssm_c_im, ssm_d=ssm_d, ssm_w_glu=ssm_w_glu, ssm_b_glu=ssm_b_glu, swa_sinks=swa_sinks,
             hgrn_lower_bounds=hgrn_lower_bounds, out_norm_a=out_norm_a, out_norm_b=out_norm_b,
             out_norm_c=out_norm_c, out_norm_d=out_norm_d, w_out=w_out, mix_norm_post=mix_norm_post,
             ffn2_norm_pre=ffn2_norm_pre, ffn2_w_gate=ffn2_w_gate, ffn2_w_up=ffn2_w_up, ffn2_w_down=ffn2_w_down,
             ffn2_norm_post=ffn2_norm_post)
    caches = (state_ssm, cache_swa_kv, cache_dil0_kv, cache_dil1_kv, cache_dil2_kv, state_hgrn)
    return _forward(x_prompt, x_sample, caches, P)
```

```python
import functools
import math

import numpy as np
import jax
import jax.numpy as jnp
from jax import lax
from jax.experimental import pallas as pl
from jax.experimental.pallas import tpu as pltpu

F32 = jnp.float32
BF16 = jnp.bfloat16

D_MODEL = 2048
DEPTH = 2
PAST_LEN = 16384
HEAD_DIM = 64
ROPE_THETA = 10000.0
D_FF = 5504
NORM_EPS = 1e-6
NEG_BIG = -1e30
LB_FLOOR = 1e-30
SSM_GROUP = 16
SSM_STATE = 64
SSM_WIDTH = 448
SSM_GROUPS = 28
SWA_HEADS = 8
SWA_KV_HEADS = 2
SWA_GROUP = 4
SWA_WIDTH = 512
SWA_KV_WIDTH = 128
WINDOW = 128
DIL_PAIRS = ((128, 1), (512, 4), (2048, 16))
DIL_HPG = 3
DIL_GW = DIL_HPG * HEAD_DIM
DIL_WIDTH = 576
DIL_KV_WIDTH = 192
HGRN_HEAD_DIM = 128
HGRN_WIDTH = 512
HGRN_HEADS = 4
IN_COLS = 4224
ATTN_SCALE = HEAD_DIM ** -0.5

LANES = 128
FF_TILE = 256
N_FF_TILES = -(-D_FF // FF_TILE)
DOWN_TILE = 256
N_DOWN_TILES = D_MODEL // DOWN_TILE
VMEM_LIMIT = 60 * 1024 * 1024

B_SLOT = SWA_WIDTH + 2 * SWA_KV_WIDTH
C_SLOT = 384
C_K = 256
OFF_B = 0
OFF_C = OFF_B + B_SLOT
OFF_D = 2048
OFF_U = OFF_D + 4 * HGRN_WIDTH
U_SLOT = 512
PROJ_COLS = 4608
WIN_TILE = 1536

S5_CHUNK = 16
S5_NPOW = S5_CHUNK + 1
HGRN_CHUNK = 128
HGRN_LEVELS = (64, 32, 16, 8, 4, 2, 1)


def _proj_segments():
    src = np.cumsum([0, SSM_WIDTH, SWA_WIDTH, SWA_KV_WIDTH, SWA_KV_WIDTH, DIL_WIDTH, DIL_KV_WIDTH, DIL_KV_WIDTH,
                     HGRN_WIDTH, HGRN_WIDTH, HGRN_WIDTH, HGRN_WIDTH])
    s_u, s_qb, s_kb, s_vb, s_qc, s_kc, s_vc, s_qd = src[:8]
    segs = [(OFF_B, s_qb, SWA_WIDTH), (OFF_B + SWA_WIDTH, s_kb, SWA_KV_WIDTH),
            (OFF_B + SWA_WIDTH + SWA_KV_WIDTH, s_vb, SWA_KV_WIDTH)]
    for g in range(3):
        base = OFF_C + g * C_SLOT
        segs += [(base, s_qc + g * DIL_GW, DIL_GW), (base + C_K, s_kc + g * HEAD_DIM, HEAD_DIM),
                 (base + C_K + HEAD_DIM, s_vc + g * HEAD_DIM, HEAD_DIM)]
    segs += [(OFF_D, s_qd, 4 * HGRN_WIDTH), (OFF_U, s_u, SSM_WIDTH)]
    return [(int(a), int(b), int(c)) for a, b, c in segs]


def _rotary_lane_mask():
    m = np.zeros((PROJ_COLS,), np.float32)
    m[OFF_B:OFF_B + SWA_WIDTH + SWA_KV_WIDTH] = 1.0
    for g in range(3):
        base = OFF_C + g * C_SLOT
        m[base:base + DIL_GW] = 1.0
        m[base + C_K:base + C_K + HEAD_DIM] = 1.0
    return m


def _cparams(sem):
    return pltpu.CompilerParams(dimension_semantics=sem, vmem_limit_bytes=VMEM_LIMIT)


def _v3(a):
    return a.reshape(a.shape[0], 1, a.shape[1])


def _rms(x, g):
    return x * lax.rsqrt(jnp.mean(x * x, axis=-1, keepdims=True) + NORM_EPS) * g


def _sigmoid(x):
    return 1.0 / (1.0 + jnp.exp(-x))


def _silu(x):
    return x * _sigmoid(x)


def _dot(a, b):
    return jnp.dot(a, b, preferred_element_type=F32)


def _dot_nt(a, b):
    return lax.dot_general(a, b, (((1,), (1,)), ((), ())), preferred_element_type=F32)


def _dot_tn(a, b):
    return lax.dot_general(a, b, (((0,), (0,)), ((), ())), preferred_element_type=F32)


def _ffn_kernel(x_ref, gpre_ref, wg_ref, wu_ref, wd_ref, gpost_ref, o_ref, xn_ref, hid_ref):
    s = pl.program_id(1)

    @pl.when(s == 0)
    def _():
        xn_ref[...] = _rms(x_ref[...], gpre_ref[...]).astype(BF16)

    @pl.when(s < N_FF_TILES)
    def _():
        xn = xn_ref[...]
        gate = _dot(xn, wg_ref[...].astype(BF16))
        up = _dot(xn, wu_ref[...].astype(BF16))
        hid_ref[s] = (_silu(gate) * up).astype(BF16)

    for j in range(N_DOWN_TILES):

        @pl.when(s == N_FF_TILES + j)
        def _(j=j):
            hid = jnp.concatenate([hid_ref[f] for f in range(N_FF_TILES)], axis=1)[:, :D_FF]
            o_ref[:, j * DOWN_TILE:(j + 1) * DOWN_TILE] = _dot(hid, wd_ref[...].astype(BF16))

    @pl.when(s == pl.num_programs(1) - 1)
    def _():
        o_ref[...] = x_ref[...] + 0.5 * _rms(o_ref[...], gpost_ref[...])


def _ffn(h, gpre, wg, wu, wd, gpost, layer, tm):
    m = h.shape[0]
    last_ff = N_FF_TILES - 1
    return pl.pallas_call(
        _ffn_kernel,
        grid=(m // tm, N_FF_TILES + N_DOWN_TILES),
        in_specs=[
            pl.BlockSpec((tm, D_MODEL), lambda i, s: (i, 0), pipeline_mode=pl.Buffered(1)),
            pl.BlockSpec((None, 1, D_MODEL), lambda i, s: (layer, 0, 0)),
            pl.BlockSpec((None, D_MODEL, FF_TILE), lambda i, s: (layer, 0, jnp.minimum(s, last_ff))),
            pl.BlockSpec((None, D_MODEL, FF_TILE), lambda i, s: (layer, 0, jnp.minimum(s, last_ff))),
            pl.BlockSpec((None, D_FF, DOWN_TILE), lambda i, s: (layer, 0, jnp.maximum(s - N_FF_TILES, 0))),
            pl.BlockSpec((None, 1, D_MODEL), lambda i, s: (layer, 0, 0)),
        ],
        out_specs=pl.BlockSpec((tm, D_MODEL), lambda i, s: (i, 0), pipeline_mode=pl.Buffered(1)),
        out_shape=jax.ShapeDtypeStruct((m, D_MODEL), F32),
        scratch_shapes=[pltpu.VMEM((tm, D_MODEL), BF16), pltpu.VMEM((N_FF_TILES, tm, FF_TILE), BF16)],
        compiler_params=_cparams(("parallel", "arbitrary")),
    )(h, _v3(gpre), wg, wu, wd, _v3(gpost))


def _win_prep_kernel(w_ref, o_ref):
    o_ref[...] = jnp.zeros_like(o_ref)
    for dst, src, width in _proj_segments():
        o_ref[:, dst:dst + width] = w_ref[:, src:src + width].astype(BF16)


def _win_prep(w_in, layer):
    rows = 256
    return pl.pallas_call(
        _win_prep_kernel,
        grid=(D_MODEL // rows,),
        in_specs=[pl.BlockSpec((None, rows, IN_COLS), lambda i: (layer, i, 0))],
        out_specs=pl.BlockSpec((rows, PROJ_COLS), lambda i: (i, 0)),
        out_shape=jax.ShapeDtypeStruct((D_MODEL, PROJ_COLS), BF16),
        compiler_params=_cparams(("parallel",)),
    )(w_in)


def _cast_kernel(w_ref, o_ref):
    o_ref[...] = w_ref[...].astype(BF16)


def _cast_bf16(w, layer):
    _, r, c = w.shape
    rows = 256
    return pl.pallas_call(
        _cast_kernel,
        grid=(r // rows,),
        in_specs=[pl.BlockSpec((None, rows, c), lambda i: (layer, i, 0))],
        out_specs=pl.BlockSpec((rows, c), lambda i: (i, 0)),
        out_shape=jax.ShapeDtypeStruct((r, c), BF16),
        compiler_params=_cparams(("parallel",)),
    )(w)


def _win_kernel(x_ref, g_ref, w_ref, cos_ref, sin_ref, o_ref, xn_ref):
    j = pl.program_id(1)

    @pl.when(j == 0)
    def _():
        xn_ref[...] = _rms(x_ref[...], g_ref[...]).astype(BF16)

    rot = _rotary_lane_mask()
    for tile in range(PROJ_COLS // WIN_TILE):

        @pl.when(j == tile)
        def _(tile=tile):
            y = _dot(xn_ref[...], w_ref[...])
            cos = cos_ref[...]
            sin = sin_ref[...]
            lane = lax.broadcasted_iota(jnp.int32, cos.shape, 1)
            first_half = (lane % HEAD_DIM) < (HEAD_DIM // 2)
            for c in range(WIN_TILE // LANES):
                sl = slice(c * LANES, (c + 1) * LANES)
                flags = rot[tile * WIN_TILE + c * LANES:tile * WIN_TILE + (c + 1) * LANES]
                x = y[:, sl]
                lo, hi = bool(flags[:HEAD_DIM].all()), bool(flags[HEAD_DIM:].all())
                if not (lo or hi):
                    o_ref[:, sl] = x
                    continue
                partner = jnp.where(first_half,
                                    pltpu.roll(x, LANES - HEAD_DIM // 2, axis=1),
                                    pltpu.roll(x, HEAD_DIM // 2, axis=1))
                rotated = x * cos + partner * sin
                if lo and hi:
                    o_ref[:, sl] = rotated
                else:
                    is_rot = (lane < HEAD_DIM) if lo else (lane >= HEAD_DIM)
                    o_ref[:, sl] = jnp.where(is_rot, rotated, x)


def _win(h, g, w, cos_t, sin_t, layer, tm):
    m = h.shape[0]
    return pl.pallas_call(
        _win_kernel,
        grid=(m // tm, PROJ_COLS // WIN_TILE),
        in_specs=[
            pl.BlockSpec((tm, D_MODEL), lambda i, j: (i, 0)),
            pl.BlockSpec((None, 1, D_MODEL), lambda i, j: (layer, 0, 0)),
            pl.BlockSpec((D_MODEL, WIN_TILE), lambda i, j: (0, j)),
            pl.BlockSpec((tm, LANES), lambda i, j: (i, 0)),
            pl.BlockSpec((tm, LANES), lambda i, j: (i, 0)),
        ],
        out_specs=pl.BlockSpec((tm, WIN_TILE), lambda i, j: (i, j)),
        out_shape=jax.ShapeDtypeStruct((m, PROJ_COLS), F32),
        scratch_shapes=[pltpu.VMEM((tm, D_MODEL), BF16)],
        compiler_params=_cparams(("parallel", "arbitrary")),
    )(h, _v3(g), w, cos_t, sin_t)


def _rope_tables(pos):
    half = HEAD_DIM // 2
    inv_freq = ROPE_THETA ** (-jnp.arange(half, dtype=F32) / half)
    ang = pos.astype(F32)[:, None] * inv_freq[None, :]
    cos, sin = jnp.cos(ang), jnp.sin(ang)
    reps = LANES // HEAD_DIM
    cos_t = jnp.tile(jnp.concatenate([cos, cos], axis=1), (1, reps))
    sin_t = jnp.tile(jnp.concatenate([-sin, sin], axis=1), (1, reps))
    return cos_t, sin_t


def _band_heads(q, kk, vv, valid, sinks):
    w = WINDOW
    heads = q.shape[1] // HEAD_DIM
    qs = jnp.concatenate([q[:, g * HEAD_DIM:(g + 1) * HEAD_DIM] for g in range(heads)], axis=0).astype(BF16)
    s = _dot_nt(qs, kk) * ATTN_SCALE
    s = jnp.where(jnp.concatenate([valid] * heads, axis=0), s, NEG_BIG)
    m = jnp.max(s, axis=-1, keepdims=True)
    if sinks is not None:
        sink_col = jnp.concatenate([jnp.broadcast_to(sinks[g], (w, 1)) for g in range(heads)], axis=0)
        m = jnp.maximum(m, sink_col)
    p = jnp.exp(s - m)
    l = jnp.sum(p, axis=-1, keepdims=True)
    if sinks is not None:
        l = l + jnp.exp(sink_col - m)
    o = _dot(p.astype(BF16), vv) / l
    lse = m + jnp.log(l)
    return ([o[g * w:(g + 1) * w] for g in range(heads)], [lse[g * w:(g + 1) * w] for g in range(heads)])


def _band_mask(i):
    w = WINDOW
    r = lax.broadcasted_iota(jnp.int32, (w, 2 * w), 0)
    c = lax.broadcasted_iota(jnp.int32, (w, 2 * w), 1)
    first_key = jnp.where(i == 0, w, 0)
    return (c >= r) & (c <= r + w) & (c >= first_key)


def _swa_prompt_kernel(cur_ref, prev_ref, sink_ref, o_ref):
    valid = _band_mask(pl.program_id(1))
    cur = cur_ref[0]
    prev = prev_ref[0]
    outs = []
    for h in range(SWA_KV_HEADS):
        ks = slice(SWA_WIDTH + h * HEAD_DIM, SWA_WIDTH + (h + 1) * HEAD_DIM)
        vs = slice(SWA_WIDTH + SWA_KV_WIDTH + h * HEAD_DIM, SWA_WIDTH + SWA_KV_WIDTH + (h + 1) * HEAD_DIM)
        kk = jnp.concatenate([prev[:, h * HEAD_DIM:(h + 1) * HEAD_DIM], cur[:, ks]], axis=0).astype(BF16)
        vv = jnp.concatenate([prev[:, SWA_KV_WIDTH + h * HEAD_DIM:SWA_KV_WIDTH + (h + 1) * HEAD_DIM], cur[:, vs]],
                             axis=0).astype(BF16)
        gw = SWA_GROUP * HEAD_DIM
        sinks = [sink_ref[:, h * gw + g * HEAD_DIM:h * gw + g * HEAD_DIM + 1] for g in range(SWA_GROUP)]
        o, _ = _band_heads(cur[:, h * gw:(h + 1) * gw], kk, vv, valid, sinks)
        outs += o
    o_ref[0] = jnp.concatenate(outs, axis=1)


def _swa_prompt(proj, sink):
    n, t, _ = proj.shape
    return pl.pallas_call(
        _swa_prompt_kernel,
        grid=(n, t // WINDOW),
        in_specs=[
            pl.BlockSpec((1, WINDOW, B_SLOT), lambda b, i: (b, i, OFF_B // B_SLOT)),
            pl.BlockSpec((1, WINDOW, 2 * SWA_KV_WIDTH),
                         lambda b, i: (b, jnp.maximum(i - 1, 0), (OFF_B + SWA_WIDTH) // (2 * SWA_KV_WIDTH))),
            pl.BlockSpec((1, SWA_WIDTH), lambda b, i: (0, 0)),
        ],
        out_specs=pl.BlockSpec((1, WINDOW, SWA_WIDTH), lambda b, i: (b, i, 0)),
        out_shape=jax.ShapeDtypeStruct((n, t, SWA_WIDTH), F32),
        compiler_params=_cparams(("parallel", "arbitrary")),
    )(proj, proj, sink)


def _dil_prompt_kernel(cur_ref, prev_ref, o_ref, cur3_ref, out3_ref, *, dil):
    valid = _band_mask(pl.program_id(1))
    nl = C_SLOT // LANES
    for c in range(nl):
        cur3_ref[c] = cur_ref[0, :, c * LANES:(c + 1) * LANES]

    def one(r):
        rows = pl.ds(r, WINDOW, stride=dil) if dil > 1 else slice(None)
        q = jnp.concatenate([cur3_ref[0, rows, :], cur3_ref[1, rows, :]], axis=1)[:, 0:DIL_GW]
        kv = cur3_ref[2, rows, :]
        pkv = prev_ref[0, rows, :]
        kk = jnp.concatenate([pkv[:, 0:HEAD_DIM], kv[:, 0:HEAD_DIM]], axis=0).astype(BF16)
        vv = jnp.concatenate([pkv[:, HEAD_DIM:], kv[:, HEAD_DIM:]], axis=0).astype(BF16)
        o, lse = _band_heads(q, kk, vv, valid, None)
        lse = [jnp.broadcast_to(x, (WINDOW, HEAD_DIM)) for x in lse]
        res = jnp.concatenate(o + lse, axis=1)
        for c in range(nl):
            out3_ref[c, rows, :] = res[:, c * LANES:(c + 1) * LANES]

    if dil <= 4:
        for r in range(dil):
            one(r)
    else:
        lax.fori_loop(0, dil, lambda r, c: (one(r), c)[1], 0)
    for c in range(nl):
        o_ref[0, :, c * LANES:(c + 1) * LANES] = out3_ref[c]


def _dil_prompt(proj, group, dil):
    n, t, _ = proj.shape
    rows = WINDOW * dil
    slot = OFF_C + group * C_SLOT
    return pl.pallas_call(
        functools.partial(_dil_prompt_kernel, dil=dil),
        grid=(n, t // rows),
        in_specs=[
            pl.BlockSpec((1, rows, C_SLOT), lambda b, i: (b, i, slot // C_SLOT)),
            pl.BlockSpec((1, rows, LANES), lambda b, i: (b, jnp.maximum(i - 1, 0), (slot + C_K) // LANES)),
        ],
        out_specs=pl.BlockSpec((1, rows, 2 * DIL_GW), lambda b, i: (b, i, 0)),
        out_shape=jax.ShapeDtypeStruct((n, t, 2 * DIL_GW), F32),
        scratch_shapes=[pltpu.VMEM((C_SLOT // LANES, rows, LANES), F32),
                        pltpu.VMEM((2 * DIL_GW // LANES, rows, LANES), F32)],
        compiler_params=_cparams(("parallel", "arbitrary")),
    )(proj, proj)


def _decode_heads(q, knew, vnew, kc, vc, sinks):
    s = _dot_nt(q.astype(BF16), kc.astype(BF16)) * ATTN_SCALE
    s_new = jnp.sum(q * knew, axis=-1, keepdims=True) * ATTN_SCALE
    m = jnp.maximum(jnp.max(s, axis=-1, keepdims=True), s_new)
    if sinks is not None:
        m = jnp.maximum(m, sinks)
    p = jnp.exp(s - m)
    p_new = jnp.exp(s_new - m)
    l = jnp.sum(p, axis=-1, keepdims=True) + p_new
    if sinks is not None:
        l = l + jnp.exp(sinks - m)
    o = (_dot(p.astype(BF16), vc.astype(BF16)) + p_new * vnew) / l
    return o, m + jnp.log(l)


def _rows_of(x, heads):
    return jnp.concatenate([x[:, g * HEAD_DIM:(g + 1) * HEAD_DIM] for g in range(heads)], axis=0)


def _swa_sample_kernel(p_ref, cache_ref, sink_ref, o_ref, *, bn):
    for b in range(bn):
        row = p_ref[b:b + 1, :]
        outs = []
        for h in range(SWA_KV_HEADS):
            gw = SWA_GROUP * HEAD_DIM
            q = _rows_of(row[:, h * gw:(h + 1) * gw], SWA_GROUP)
            knew = row[:, SWA_WIDTH + h * HEAD_DIM:SWA_WIDTH + (h + 1) * HEAD_DIM]
            vnew = row[:, SWA_WIDTH + SWA_KV_WIDTH + h * HEAD_DIM:SWA_WIDTH + SWA_KV_WIDTH + (h + 1) * HEAD_DIM]
            sinks = _rows_of(sink_ref[:, h * gw:(h + 1) * gw], SWA_GROUP)[:, 0:1]
            o, _ = _decode_heads(q, knew, vnew, cache_ref[b, :, 0, h, :], cache_ref[b, :, 1, h, :], sinks)
            outs += [o[g:g + 1] for g in range(SWA_GROUP)]
        o_ref[b:b + 1, :] = jnp.concatenate(outs, axis=1)


def _swa_sample(proj, cache, sink, layer):
    n = proj.shape[0]
    bn = 8
    return pl.pallas_call(
        functools.partial(_swa_sample_kernel, bn=bn),
        grid=(n // bn,),
        in_specs=[
            pl.BlockSpec((bn, B_SLOT), lambda i: (i, OFF_B // B_SLOT)),
            pl.BlockSpec((None, bn, WINDOW, 2, SWA_KV_HEADS, HEAD_DIM), lambda i: (layer, i, 0, 0, 0, 0)),
            pl.BlockSpec((1, SWA_WIDTH), lambda i: (0, 0)),
        ],
        out_specs=pl.BlockSpec((bn, SWA_WIDTH), lambda i: (i, 0)),
        out_shape=jax.ShapeDtypeStruct((n, SWA_WIDTH), F32),
        compiler_params=_cparams(("parallel",)),
    )(proj, cache, sink)


def _dil_sample_kernel(p_ref, cache_ref, o_ref, *, bn):
    for b in range(bn):
        row = p_ref[b:b + 1, :]
        q = _rows_of(row[:, 0:DIL_GW], DIL_HPG)
        o, lse = _decode_heads(q, row[:, C_K:C_K + HEAD_DIM], row[:, C_K + HEAD_DIM:C_K + 2 * HEAD_DIM],
                               cache_ref[b, :, 0, :], cache_ref[b, :, 1, :], None)
        lse = jnp.broadcast_to(lse, (DIL_HPG, HEAD_DIM))
        o_ref[b:b + 1, :] = jnp.concatenate([o[g:g + 1] for g in range(DIL_HPG)]
                                            + [lse[g:g + 1] for g in range(DIL_HPG)], axis=1)


def _dil_sample(proj, cache, group, layer):
    n = proj.shape[0]
    bn = 8
    slot = OFF_C + group * C_SLOT
    return pl.pallas_call(
        functools.partial(_dil_sample_kernel, bn=bn),
        grid=(n // bn,),
        in_specs=[
            pl.BlockSpec((bn, C_SLOT), lambda i: (i, slot // C_SLOT)),
            pl.BlockSpec((None, bn, WINDOW, 2, HEAD_DIM), lambda i: (layer, i, 0, 0, 0)),
        ],
        out_specs=pl.BlockSpec((bn, 2 * DIL_GW), lambda i: (i, 0)),
        out_shape=jax.ShapeDtypeStruct((n, 2 * DIL_GW), F32),
        compiler_params=_cparams(("parallel",)),
    )(proj, cache)


def _s5_param_kernel(are_ref, aim_ref, ldt_ref, bre_ref, bim_ref, cre_ref, cim_ref,
                     lam_re_ref, lam_im_ref, cl_re_ref, cl_im_ref, r_re_ref, r_im_ref, k_ref):
    ar = are_ref[0]
    ai = aim_ref[0]
    dt = jnp.exp(ldt_ref[0])
    nrow = lam_re_ref.shape[1]
    ri = lax.broadcasted_iota(jnp.int32, (nrow, 1), 0)
    pw = jnp.where(ri < S5_NPOW, ri, jnp.left_shift(S5_CHUNK, jnp.maximum(ri - S5_NPOW, 0))).astype(F32)
    mag = jnp.exp(pw * (ar * dt))
    ang = pw * (ai * dt)
    lam_re = mag * jnp.cos(ang)
    lam_im = mag * jnp.sin(ang)
    lam_re_ref[0] = lam_re
    lam_im_ref[0] = lam_im
    l1r, l1i = lam_re[1:2], lam_im[1:2]
    den = ar * ar + ai * ai
    z_re = ((l1r - 1.0) * ar + l1i * ai) / den
    z_im = (l1i * ar - (l1r - 1.0) * ai) / den
    b_re, b_im = bre_ref[0], bim_ref[0]
    bb_re = z_re * b_re - z_im * b_im
    bb_im = z_re * b_im + z_im * b_re
    c_re, c_im = cre_ref[0], cim_ref[0]
    cls_re, cls_im = [], []
    for d in range(S5_NPOW):
        lr, li = lam_re[d:d + 1], lam_im[d:d + 1]
        cr = c_re * lr - c_im * li
        ci = c_re * li + c_im * lr
        cl_re_ref[0, d * SSM_GROUP:(d + 1) * SSM_GROUP, :] = cr
        cl_im_ref[0, d * SSM_GROUP:(d + 1) * SSM_GROUP, :] = ci
        if d < S5_CHUNK:
            cls_re.append(cr)
            cls_im.append(ci)
    for s in range(S5_CHUNK):
        lr = lam_re[S5_CHUNK - 1 - s:S5_CHUNK - s]
        li = lam_im[S5_CHUNK - 1 - s:S5_CHUNK - s]
        r_re_ref[0, s * SSM_GROUP:(s + 1) * SSM_GROUP, :] = lr * bb_re - li * bb_im
        r_im_ref[0, s * SSM_GROUP:(s + 1) * SSM_GROUP, :] = lr * bb_im + li * bb_re
    call_re = jnp.concatenate(cls_re, axis=0)
    call_im = jnp.concatenate(cls_im, axis=0)
    hp = lax.Precision.HIGHEST
    nt = (((1,), (1,)), ((), ()))
    k_ref[0] = (lax.dot_general(call_re, bb_re, nt, precision=hp, preferred_element_type=F32)
                - lax.dot_general(call_im, bb_im, nt, precision=hp, preferred_element_type=F32))


def _s5_params(a_re, a_im, log_dt, b_re, b_im, c_re, c_im, nsteps):
    g = SSM_GROUPS
    nrow = ((S5_NPOW + nsteps + 7) // 8) * 8
    row = lambda i: (i, 0, 0)
    shp = lambda r, c: jax.ShapeDtypeStruct((g, r, c), F32)
    ncl = S5_NPOW * SSM_GROUP
    nr = S5_CHUNK * SSM_GROUP
    return pl.pallas_call(
        _s5_param_kernel,
        grid=(g,),
        in_specs=[
            pl.BlockSpec((1, 1, SSM_STATE), row),
            pl.BlockSpec((1, 1, SSM_STATE), row),
            pl.BlockSpec((1, 1, 1), row),
            pl.BlockSpec((1, SSM_GROUP, SSM_STATE), row),
            pl.BlockSpec((1, SSM_GROUP, SSM_STATE), row),
            pl.BlockSpec((1, SSM_GROUP, SSM_STATE), row),
            pl.BlockSpec((1, SSM_GROUP, SSM_STATE), row),
        ],
        out_specs=[
            pl.BlockSpec((1, nrow, SSM_STATE), row), pl.BlockSpec((1, nrow, SSM_STATE), row),
            pl.BlockSpec((1, ncl, SSM_STATE), row), pl.BlockSpec((1, ncl, SSM_STATE), row),
            pl.BlockSpec((1, nr, SSM_STATE), row), pl.BlockSpec((1, nr, SSM_STATE), row),
            pl.BlockSpec((1, nr, SSM_GROUP), row),
        ],
        out_shape=[shp(nrow, SSM_STATE), shp(nrow, SSM_STATE), shp(ncl, SSM_STATE), shp(ncl, SSM_STATE),
                   shp(nr, SSM_STATE), shp(nr, SSM_STATE), shp(nr, SSM_GROUP)],
        compiler_params=_cparams(("parallel",)),
    )(a_re.reshape(g, 1, SSM_STATE), a_im.reshape(g, 1, SSM_STATE), log_dt.reshape(g, 1, 1),
      jnp.swapaxes(b_re, 1, 2), jnp.swapaxes(b_im, 1, 2), c_re, c_im)


def _s5_chunk_in_kernel(p_ref, u_ref, st_ref):
    tm = p_ref.shape[0]
    nc = tm // S5_CHUNK
    per_blk = LANES // SSM_GROUP
    for lb in range(U_SLOT // LANES):
        st_ref[lb] = p_ref[:, lb * LANES:(lb + 1) * LANES]
    slot = lax.broadcasted_iota(jnp.int32, (nc, LANES), 1) // SSM_GROUP
    for g in range(SSM_GROUPS):
        acc = [jnp.zeros((nc, LANES), F32) for _ in range(S5_CHUNK // per_blk)]
        for s in range(S5_CHUNK):
            v = st_ref[g // per_blk, pl.ds(s, nc, stride=S5_CHUNK), :]
            shift = (SSM_GROUP * (s % per_blk) - SSM_GROUP * (g % per_blk)) % LANES
            if shift:
                v = pltpu.roll(v, shift, axis=1)
            acc[s // per_blk] = jnp.where(slot == s % per_blk, v, acc[s // per_blk])
        for k, a in enumerate(acc):
            u_ref[g, :, k * LANES:(k + 1) * LANES] = a


def _s5_chunk_in(proj2, tm):
    m = proj2.shape[0]
    cw = S5_CHUNK * SSM_GROUP
    return pl.pallas_call(
        _s5_chunk_in_kernel,
        grid=(m // tm,),
        in_specs=[pl.BlockSpec((tm, U_SLOT), lambda i: (i, OFF_U // U_SLOT))],
        out_specs=pl.BlockSpec((SSM_GROUPS, tm // S5_CHUNK, cw), lambda i: (0, i, 0)),
        out_shape=jax.ShapeDtypeStruct((SSM_GROUPS, m // S5_CHUNK, cw), F32),
        scratch_shapes=[pltpu.VMEM((U_SLOT // LANES, tm, LANES), F32)],
        compiler_params=_cparams(("parallel",)),
    )(proj2)


def _s5_chunk_out_kernel(y_ref, o_ref, st_ref):
    nc = y_ref.shape[1]
    per_blk = LANES // SSM_GROUP
    slot = lax.broadcasted_iota(jnp.int32, (nc, LANES), 1) // SSM_GROUP
    nblk = -(-SSM_WIDTH // LANES)
    for lb in range(nblk):
        groups = range(lb * per_blk, min((lb + 1) * per_blk, SSM_GROUPS))
        for s in range(S5_CHUNK):
            acc = jnp.zeros((nc, LANES), F32)
            for g in groups:
                v = y_ref[g, :, (s // per_blk) * LANES:(s // per_blk + 1) * LANES]
                shift = (SSM_GROUP * (g % per_blk) - SSM_GROUP * (s % per_blk)) % LANES
                if shift:
                    v = pltpu.roll(v, shift, axis=1)
                acc = jnp.where(slot == g % per_blk, v, acc)
            st_ref[lb, pl.ds(s, nc, stride=S5_CHUNK), :] = acc
    for lb in range(nblk):
        width = min(LANES, SSM_WIDTH - lb * LANES)
        o_ref[:, lb * LANES:lb * LANES + width] = st_ref[lb][:, 0:width]


def _s5_chunk_out(y, tm):
    m = y.shape[1] * S5_CHUNK
    cw = S5_CHUNK * SSM_GROUP
    return pl.pallas_call(
        _s5_chunk_out_kernel,
        grid=(m // tm,),
        in_specs=[pl.BlockSpec((SSM_GROUPS, tm // S5_CHUNK, cw), lambda i: (0, i, 0))],
        out_specs=pl.BlockSpec((tm, SSM_WIDTH), lambda i: (i, 0)),
        out_shape=jax.ShapeDtypeStruct((m, SSM_WIDTH), F32),
        scratch_shapes=[pltpu.VMEM((-(-SSM_WIDTH // LANES), tm, LANES), F32)],
        compiler_params=_cparams(("parallel",)),
    )(y)


def _s5_prompt_kernel(u_ref, mt_ref, r_re_ref, r_im_ref, cl_re_ref, cl_im_ref, lam_re_ref, lam_im_ref, d_ref,
                      y_ref, xre_ref, xim_ref, *, nseq, nchunk):
    u = u_ref[0]
    ub = u.astype(BF16)
    y = _dot(ub, mt_ref[0].astype(BF16))
    xr = _dot(ub, r_re_ref[0].astype(BF16))
    xi = _dot(ub, r_im_ref[0].astype(BF16))
    rows = nseq * nchunk
    cidx = lax.broadcasted_iota(jnp.int32, (rows, SSM_STATE), 0) % nchunk
    lam_re, lam_im = lam_re_ref[0], lam_im_ref[0]
    for k in range(int(math.log2(nchunk))):
        sh = 1 << k
        lr = lam_re[S5_NPOW + k:S5_NPOW + k + 1]
        li = lam_im[S5_NPOW + k:S5_NPOW + k + 1]
        ok = cidx >= sh
        pr = jnp.where(ok, pltpu.roll(xr, sh, axis=0), 0.0)
        pi = jnp.where(ok, pltpu.roll(xi, sh, axis=0), 0.0)
        xr, xi = xr + lr * pr - li * pi, xi + lr * pi + li * pr
    ok = cidx >= 1
    er = jnp.where(ok, pltpu.roll(xr, 1, axis=0), 0.0)
    ei = jnp.where(ok, pltpu.roll(xi, 1, axis=0), 0.0)
    cl_re = cl_re_ref[0][SSM_GROUP:, :]
    cl_im = cl_im_ref[0][SSM_GROUP:, :]
    y = y + _dot_nt(er.astype(BF16), cl_re.astype(BF16)) - _dot_nt(ei.astype(BF16), cl_im.astype(BF16))
    y_ref[0] = y + d_ref[0] * u
    last = [s * nchunk + nchunk - 1 for s in range(nseq)]
    xre_ref[0] = jnp.concatenate([xr[i:i + 1] for i in last], axis=0)
    xim_ref[0] = jnp.concatenate([xi[i:i + 1] for i in last], axis=0)


def _s5_prompt(u, mt, r_re, r_im, cl_re, cl_im, lam_re, lam_im, d_t, nseq, nchunk):
    g = SSM_GROUPS
    rows = nseq * nchunk
    cw = S5_CHUNK * SSM_GROUP
    row = lambda i: (i, 0, 0)
    full = lambda a: pl.BlockSpec((1,) + a.shape[1:], row)
    return pl.pallas_call(
        functools.partial(_s5_prompt_kernel, nseq=nseq, nchunk=nchunk),
        grid=(g,),
        in_specs=[full(u), full(mt), full(r_re), full(r_im), full(cl_re), full(cl_im), full(lam_re), full(lam_im),
                  full(d_t)],
        out_specs=[pl.BlockSpec((1, rows, cw), row), pl.BlockSpec((1, nseq, SSM_STATE), row),
                   pl.BlockSpec((1, nseq, SSM_STATE), row)],
        out_shape=[jax.ShapeDtypeStruct((g, rows, cw), F32), jax.ShapeDtypeStruct((g, nseq, SSM_STATE), F32),
                   jax.ShapeDtypeStruct((g, nseq, SSM_STATE), F32)],
        compiler_params=_cparams(("parallel",)),
    )(u, mt, r_re, r_im, cl_re, cl_im, lam_re, lam_im, d_t)


def _s5_sample_kernel(u_ref, x0r_ref, x0i_ref, r_re_ref, r_im_ref, cl_re_ref, cl_im_ref, lam_re_ref, lam_im_ref,
                      d_ref, y_ref, xre_ref, xim_ref):
    u = u_ref[0]
    ub = u.astype(BF16)
    lo = (S5_CHUNK - 1) * SSM_GROUP
    bb_re = r_re_ref[0][lo:lo + SSM_GROUP, :]
    bb_im = r_im_ref[0][lo:lo + SSM_GROUP, :]
    lr, li = lam_re_ref[0][1:2], lam_im_ref[0][1:2]
    x0r, x0i = x0r_ref[0], x0i_ref[0]
    xr = _dot(ub, bb_re.astype(BF16)) + (lr * x0r - li * x0i)
    xi = _dot(ub, bb_im.astype(BF16)) + (lr * x0i + li * x0r)
    c_re = cl_re_ref[0][:SSM_GROUP, :]
    c_im = cl_im_ref[0][:SSM_GROUP, :]
    y_ref[0] = (_dot_nt(xr.astype(BF16), c_re.astype(BF16)) - _dot_nt(xi.astype(BF16), c_im.astype(BF16))
                + d_ref[0] * u)
    xre_ref[0] = xr
    xim_ref[0] = xi


def _s5_sample(u, x0r, x0i, r_re, r_im, cl_re, cl_im, lam_re, lam_im, d):
    g, n = u.shape[0], u.shape[1]
    row = lambda i: (i, 0, 0)
    full = lambda a: pl.BlockSpec((1,) + a.shape[1:], row)
    return pl.pallas_call(
        _s5_sample_kernel,
        grid=(g,),
        in_specs=[full(u), full(x0r), full(x0i), full(r_re), full(r_im), full(cl_re), full(cl_im), full(lam_re),
                  full(lam_im), full(d)],
        out_specs=[pl.BlockSpec((1, n, SSM_GROUP), row), pl.BlockSpec((1, n, SSM_STATE), row),
                   pl.BlockSpec((1, n, SSM_STATE), row)],
        out_shape=[jax.ShapeDtypeStruct((g, n, SSM_GROUP), F32), jax.ShapeDtypeStruct((g, n, SSM_STATE), F32),
                   jax.ShapeDtypeStruct((g, n, SSM_STATE), F32)],
        compiler_params=_cparams(("parallel",)),
    )(u, x0r, x0i, r_re, r_im, cl_re, cl_im, lam_re, lam_im, d)


def _log_sigmoid(x):
    return jnp.minimum(x, 0.0) - jnp.log1p(jnp.exp(-jnp.abs(x)))


def _logaddexp(a, b):
    return jnp.maximum(a, b) + jnp.log1p(jnp.exp(-jnp.abs(a - b)))


def _hgrn_gates(fpre, lb):
    log_f = _logaddexp(jnp.log(jnp.maximum(lb, LB_FLOOR)), jnp.log1p(-lb) + _log_sigmoid(fpre))
    k = (1.0 - lb) * _sigmoid(-fpre)
    return log_f, k


def _hgrn_scan_matrix():
    c = HGRN_CHUNK
    tri = np.tril(np.ones((c, c), np.float32))
    t = np.arange(c)
    mats = [tri]
    for m in HGRN_LEVELS:
        mats.append(tri[(t // (2 * m)) * (2 * m) + m - 1])
    return np.concatenate(mats, axis=0)


def _hgrn_prompt_kernel(p_ref, lbraw_ref, gn_ref, scan_ref, o_ref, s_ref, st_ref, *, layer):
    cidx = pl.program_id(1)
    c = HGRN_CHUNK
    hd = HGRN_HEAD_DIM

    @pl.when(cidx == 0)
    def _():
        st_ref[...] = jnp.zeros_like(st_ref)

    a = lbraw_ref[...]
    e = jnp.exp(a - jnp.max(a, axis=0, keepdims=True))
    p = e / jnp.sum(e, axis=0, keepdims=True)
    run = p[0:1]
    for l in range(1, layer + 1):
        run = run + p[l:l + 1]
    lb_all = run - p[0:1]

    r = lax.broadcasted_iota(jnp.int32, (c, c), 0)
    s = lax.broadcasted_iota(jnp.int32, (c, c), 1)
    pairs = []
    for m in HGRN_LEVELS:
        sh = int(math.log2(2 * m))
        pairs.append(((r >> sh) == (s >> sh)) & ((r & m) != 0) & ((s & m) == 0))
    scan = scan_ref[...]
    for h in range(HGRN_HEADS):
        col = lambda part: p_ref[0, :, part * HGRN_WIDTH + h * hd:part * HGRN_WIDTH + (h + 1) * hd]
        log_f, k = _hgrn_gates(col(1), lb_all[:, h * hd:(h + 1) * hd])
        q = _silu(col(0))
        v = col(2)
        hi = log_f.astype(BF16)
        rest = log_f - hi.astype(F32)
        mid = rest.astype(BF16)
        lo = (rest - mid.astype(F32)).astype(BF16)
        cum3 = _dot(scan, jnp.concatenate([hi, mid, lo], axis=1))
        cum = cum3[:, 0:hd] + cum3[:, hd:2 * hd] + cum3[:, 2 * hd:3 * hd]
        b = cum[0:c]
        att = jnp.where(r == s, jnp.sum(q * k, axis=-1, keepdims=True), 0.0)
        for lvl, m in enumerate(HGRN_LEVELS):
            bref = cum[(lvl + 1) * c:(lvl + 2) * c]
            qs = q * jnp.exp(jnp.minimum(b - bref, 0.0))
            ks = k * jnp.exp(jnp.minimum(bref - b, 0.0))
            att = att + jnp.where(pairs[lvl], _dot_nt(qs.astype(BF16), ks.astype(BF16)), 0.0)
        st = st_ref[h]
        vb = v.astype(BF16)
        o = _dot(att.astype(BF16), vb) + _dot_nt((q * jnp.exp(b)).astype(BF16), st.astype(BF16))
        bl = b[c - 1:c]
        st_new = jnp.exp(bl) * st + _dot_tn(vb, (k * jnp.exp(bl - b)).astype(BF16))
        st_ref[h] = st_new
        o = o * lax.rsqrt(jnp.mean(o * o, axis=-1, keepdims=True) + NORM_EPS)
        o_ref[0, :, h * hd:(h + 1) * hd] = o * gn_ref[:, h * hd:(h + 1) * hd] * _silu(col(3))

    @pl.when(cidx == pl.num_programs(1) - 1)
    def _():
        s_ref[0] = st_ref[...]


def _hgrn_prompt(proj, lbraw, gn, layer):
    n, t, _ = proj.shape
    c = HGRN_CHUNK
    hd = HGRN_HEAD_DIM
    scan = jnp.asarray(_hgrn_scan_matrix(), dtype=BF16)
    return pl.pallas_call(
        functools.partial(_hgrn_prompt_kernel, layer=layer),
        grid=(n, t // c),
        in_specs=[pl.BlockSpec((1, c, 4 * HGRN_WIDTH), lambda b, j: (b, j, OFF_D // (4 * HGRN_WIDTH))),
                  pl.BlockSpec((DEPTH, HGRN_WIDTH), lambda b, j: (0, 0)),
                  pl.BlockSpec((None, 1, HGRN_WIDTH), lambda b, j: (layer, 0, 0)),
                  pl.BlockSpec(scan.shape, lambda b, j: (0, 0))],
        out_specs=[pl.BlockSpec((1, c, HGRN_WIDTH), lambda b, j: (b, j, 0)),
                   pl.BlockSpec((1, HGRN_HEADS, hd, hd), lambda b, j: (b, 0, 0, 0))],
        out_shape=[jax.ShapeDtypeStruct((n, t, HGRN_WIDTH), F32),
                   jax.ShapeDtypeStruct((n, HGRN_HEADS, hd, hd), F32)],
        scratch_shapes=[pltpu.VMEM((HGRN_HEADS, hd, hd), F32)],
        compiler_params=_cparams(("parallel", "arbitrary")),
    )(proj, lbraw, _v3(gn), scan)


def _hgrn_sample_kernel(qt_ref, ft_ref, lbt_ref, v_ref, g_ref, gn_ref, s0_ref, o_ref, s_ref, *, layer, n):
    hd = HGRN_HEAD_DIM
    a = lbt_ref[0]
    e = jnp.exp(a - jnp.max(a, axis=1, keepdims=True))
    p = e / jnp.sum(e, axis=1, keepdims=True)
    run = p[:, 0:1]
    for l in range(1, layer + 1):
        run = run + p[:, l:l + 1]
    lb = run - p[:, 0:1]
    log_f, k = _hgrn_gates(ft_ref[0], lb)
    f = jnp.exp(log_f)
    q = _silu(qt_ref[0])
    v = v_ref[...]
    gate = g_ref[...]
    rows = []
    for b in range(n):
        s_new = f[:, b:b + 1] * s0_ref[b, 0] + k[:, b:b + 1] * v[b:b + 1, :]
        s_ref[b, 0] = s_new
        rows.append(jnp.sum(q[:, b:b + 1] * s_new, axis=0, keepdims=True))
    o = jnp.concatenate(rows, axis=0)
    o = o * lax.rsqrt(jnp.mean(o * o, axis=-1, keepdims=True) + NORM_EPS)
    o_ref[...] = o * gn_ref[...] * _silu(gate)


def _hgrn_sample(qt, ft, lbt, proj, gn, s0, layer):
    n = proj.shape[0]
    hd = HGRN_HEAD_DIM
    v_blk = (OFF_D + 2 * HGRN_WIDTH) // hd
    g_blk = (OFF_D + 3 * HGRN_WIDTH) // hd
    return pl.pallas_call(
        functools.partial(_hgrn_sample_kernel, layer=layer, n=n),
        grid=(HGRN_HEADS,),
        in_specs=[pl.BlockSpec((1, hd, n), lambda h: (h, 0, 0)),
                  pl.BlockSpec((1, hd, n), lambda h: (h, 0, 0)),
                  pl.BlockSpec((1, hd, DEPTH), lambda h: (h, 0, 0)),
                  pl.BlockSpec((n, hd), lambda h: (0, v_blk + h)),
                  pl.BlockSpec((n, hd), lambda h: (0, g_blk + h)),
                  pl.BlockSpec((None, 1, hd), lambda h: (layer, 0, h)),
                  pl.BlockSpec((None, n, 1, hd, hd), lambda h: (layer, 0, h, 0, 0))],
        out_specs=[pl.BlockSpec((n, hd), lambda h: (0, h)),
                   pl.BlockSpec((n, 1, hd, hd), lambda h: (0, h, 0, 0))],
        out_shape=[jax.ShapeDtypeStruct((n, HGRN_WIDTH), F32),
                   jax.ShapeDtypeStruct((n, HGRN_HEADS, hd, hd), F32)],
        compiler_params=_cparams(("parallel",)),
    )(qt, ft, lbt, proj, proj, _v3(gn), s0)


def _gelu_tanh(x):
    return 0.5 * x * (1.0 + jnp.tanh(math.sqrt(2.0 / math.pi) * (x + 0.044715 * (x * x * x))))


def _mixout_kernel(h_ref, ya_ref, ob_ref, oc0_ref, oc1_ref, oc2_ref, od_ref,
                   wglu_ref, bglu_ref, ga_ref, gb_ref, gc_ref, wo_ref, gpost_ref, o_ref):
    z = _gelu_tanh(ya_ref[...])
    gate = _sigmoid(_dot(z.astype(BF16), wglu_ref[...].astype(BF16)) + bglu_ref[...])
    out_a = _rms(z * gate, ga_ref[...])
    out_b = _rms(ob_ref[...], gb_ref[...])
    ocs = [oc0_ref, oc1_ref, oc2_ref]
    ls = [r[:, DIL_GW:2 * DIL_GW] for r in ocs]
    m = jnp.maximum(jnp.maximum(ls[0], ls[1]), ls[2])
    es = [jnp.exp(l - m) for l in ls]
    den = es[0] + es[1] + es[2]
    cs = [ocs[gi][:, 0:DIL_GW] * (es[gi] / den) for gi in range(3)]
    ss = sum(jnp.sum(c * c, axis=-1, keepdims=True) for c in cs)
    inv = lax.rsqrt(ss / DIL_WIDTH + NORM_EPS)
    gc = gc_ref[...]
    o1, o2, o3 = SSM_WIDTH, SSM_WIDTH + SWA_WIDTH, SSM_WIDTH + SWA_WIDTH + DIL_WIDTH
    wo = lambda lo, hi: wo_ref[lo:hi, :]
    mix = _dot(out_a.astype(BF16), wo(0, o1)) + _dot(out_b.astype(BF16), wo(o1, o2))
    for gi in range(3):
        c = cs[gi] * inv * gc[:, gi * DIL_GW:(gi + 1) * DIL_GW]
        mix = mix + _dot(c.astype(BF16), wo(o2 + gi * DIL_GW, o2 + (gi + 1) * DIL_GW))
    mix = mix + _dot(od_ref[...].astype(BF16), wo(o3, D_MODEL))
    o_ref[...] = h_ref[...] + _rms(mix, gpost_ref[...])


def _mixout(h, ya, ob, ocs, od, lp_all, w_out_l, layer, tm):
    m = h.shape[0]
    rowblk = lambda a: pl.BlockSpec((tm, a.shape[1]), lambda i: (i, 0))
    vec = lambda a: pl.BlockSpec((None, 1, a.shape[1]), lambda i: (layer, 0, 0))
    mat = lambda a: pl.BlockSpec((None,) + a.shape[1:], lambda i: (layer, 0, 0))
    whole = lambda a: pl.BlockSpec(a.shape, lambda i: (0, 0), pipeline_mode=pl.Buffered(1))
    acts = [h, ya, ob, *ocs, od]
    names = ['ssm_w_glu', 'ssm_b_glu', 'out_norm_a', 'out_norm_b', 'out_norm_c', 'w_out', 'mix_norm_post']
    params = [w_out_l if k == 'w_out' else lp_all[k] for k in names]
    spec = lambda k, p: whole(p) if k == 'w_out' else (mat(p) if p.ndim == 3 else vec(p))
    return pl.pallas_call(
        _mixout_kernel,
        grid=(m // tm,),
        in_specs=[rowblk(a) for a in acts] + [spec(k, p) for k, p in zip(names, params)],
        out_specs=pl.BlockSpec((tm, D_MODEL), lambda i: (i, 0)),
        out_shape=jax.ShapeDtypeStruct((m, D_MODEL), F32),
        compiler_params=_cparams(("parallel",)),
    )(*acts, *[_v3(p) if p.ndim == 2 and k != 'w_out' else p for k, p in zip(names, params)])


def _s5_toeplitz(kmat):
    g = kmat.shape[0]
    k4 = kmat.reshape(g, S5_CHUNK, SSM_GROUP, SSM_GROUP)
    k4 = jnp.concatenate([k4, jnp.zeros((g, 1, SSM_GROUP, SSM_GROUP), F32)], axis=1)
    s = np.arange(S5_CHUNK)[:, None]
    t = np.arange(S5_CHUNK)[None, :]
    idx = np.where(t >= s, t - s, S5_CHUNK)
    m = k4[:, idx]
    return m.transpose(0, 1, 4, 2, 3).reshape(g, S5_CHUNK * SSM_GROUP, S5_CHUNK * SSM_GROUP)


def _ffn_block(h, P, name, layer, tm):
    return _ffn(h, P[name + '_norm_pre'], P[name + '_w_gate'], P[name + '_w_up'], P[name + '_w_down'],
                P[name + '_norm_post'], layer, tm)


def _prompt_mixer(h, P, w_in_l, w_out_l, s5p, layer, n, t, rope, tm):
    m = n * t
    proj2 = _win(h, P['mix_norm_pre'], w_in_l, rope[0], rope[1], layer, tm)
    proj = proj2.reshape(n, t, PROJ_COLS)

    lam_re, lam_im, cl_re, cl_im, r_re, r_im, kmat = s5p
    nchunk = t // S5_CHUNK
    u = _s5_chunk_in(proj2, tm)
    d_t = jnp.tile(P['ssm_d'][layer], (1, S5_CHUNK)).reshape(SSM_GROUPS, 1, S5_CHUNK * SSM_GROUP)
    y, xre, xim = _s5_prompt(u, _s5_toeplitz(kmat), r_re, r_im, cl_re, cl_im, lam_re, lam_im, d_t, n, nchunk)
    ya = _s5_chunk_out(y, tm)
    ssm_new = jnp.stack([xre, xim], axis=-1).transpose(1, 0, 2, 3)

    sink = jnp.repeat(P['swa_sinks'][layer], HEAD_DIM).reshape(1, SWA_WIDTH)
    ob = _swa_prompt(proj, sink).reshape(m, SWA_WIDTH)
    keep = min(WINDOW, t)
    kv = proj[:, t - keep:, OFF_B + SWA_WIDTH:OFF_B + B_SLOT]
    swa_new = kv.reshape(n, keep, 2, SWA_KV_HEADS, HEAD_DIM)

    ocs, dil_new = [], []
    for gi, (win, dil) in enumerate(DIL_PAIRS):
        ocs.append(_dil_prompt(proj, gi, dil).reshape(m, 2 * DIL_GW))
        keep = min(win, t)
        lo = OFF_C + gi * C_SLOT + C_K
        dil_new.append(proj[:, t - keep:, lo:lo + 2 * HEAD_DIM].reshape(n, keep, 2, HEAD_DIM))

    od, st = _hgrn_prompt(proj, P['hgrn_lower_bounds'], P['out_norm_d'], layer)
    hgrn_new = jnp.swapaxes(st, -1, -2)

    h = _mixout(h, ya, ob, ocs, od.reshape(m, HGRN_WIDTH), P, w_out_l, layer, min(tm, 512))
    return h, (ssm_new, swa_new, dil_new[0], dil_new[1], dil_new[2], hgrn_new)


def _sample_mixer(h, P, w_in_l, w_out_l, s5p, layer, caches, rope):
    n = h.shape[0]
    state_ssm, cache_swa, cache_d0, cache_d1, cache_d2, state_hgrn = caches
    proj = _win(h, P['mix_norm_pre'], w_in_l, rope[0], rope[1], layer, n)

    lam_re, lam_im, cl_re, cl_im, r_re, r_im, _ = s5p
    u = proj[:, OFF_U:OFF_U + SSM_WIDTH].reshape(n, SSM_GROUPS, SSM_GROUP).transpose(1, 0, 2)
    x0 = state_ssm[layer].transpose(1, 0, 2, 3)
    y, xre, xim = _s5_sample(u, x0[..., 0], x0[..., 1], r_re, r_im, cl_re, cl_im, lam_re, lam_im,
                             P['ssm_d'][layer].reshape(SSM_GROUPS, 1, SSM_GROUP))
    ya = y.transpose(1, 0, 2).reshape(n, SSM_WIDTH)
    ssm_new = jnp.stack([xre, xim], axis=-1).transpose(1, 0, 2, 3)

    sink = jnp.repeat(P['swa_sinks'][layer], HEAD_DIM).reshape(1, SWA_WIDTH)
    ob = _swa_sample(proj, cache_swa, sink, layer)
    swa_new = proj[:, OFF_B + SWA_WIDTH:OFF_B + B_SLOT].reshape(n, 1, 2, SWA_KV_HEADS, HEAD_DIM)

    ocs, dil_new = [], []
    for gi, (buf, (win, dil)) in enumerate(zip((cache_d0, cache_d1, cache_d2), DIL_PAIRS)):
        ocs.append(_dil_sample(proj, buf[:, :, ::dil], gi, layer))
        lo = OFF_C + gi * C_SLOT + C_K
        dil_new.append(proj[:, lo:lo + 2 * HEAD_DIM].reshape(n, 1, 2, HEAD_DIM))

    part = lambda i: proj[:, OFF_D + i * HGRN_WIDTH:OFF_D + (i + 1) * HGRN_WIDTH].reshape(n, HGRN_HEADS, HGRN_HEAD_DIM)
    cm = lambda a: a.transpose(1, 2, 0)
    lbt = P['hgrn_lower_bounds'].reshape(DEPTH, HGRN_HEADS, HGRN_HEAD_DIM).transpose(1, 2, 0)
    od, hgrn_new = _hgrn_sample(cm(part(0)), cm(part(1)), lbt, proj, P['out_norm_d'], state_hgrn, layer)

    h = _mixout(h, ya, ob, ocs, od, P, w_out_l, layer, n)
    return h, (ssm_new, swa_new, dil_new[0], dil_new[1], dil_new[2], hgrn_new)


def _forward(x_prompt, x_sample, caches, P):
    n, t, _ = x_prompt.shape
    ns, ts, _ = x_sample.shape
    assert ts == 1
    tm = 1024
    rope_p = _rope_tables(jnp.tile(jnp.arange(t, dtype=jnp.int32), n))
    rope_s = _rope_tables(jnp.full((ns,), PAST_LEN, dtype=jnp.int32))
    nsteps = int(math.log2(t // S5_CHUNK))
    h_p = x_prompt.reshape(n * t, D_MODEL)
    h_s = x_sample.reshape(ns, D_MODEL)
    new_p = [[] for _ in range(6)]
    new_s = [[] for _ in range(6)]
    for l in range(DEPTH):
        w_in_l = _win_prep(P['w_in'], l)
        w_out_l = _cast_bf16(P['w_out'], l)
        s5p = _s5_params(P['ssm_a_re'][l], P['ssm_a_im'][l], P['ssm_log_dt'][l], P['ssm_b_re'][l], P['ssm_b_im'][l],
                         P['ssm_c_re'][l], P['ssm_c_im'][l], nsteps)
        h_p = _ffn_block(h_p, P, 'ffn1', l, tm)
        h_p, st_p = _prompt_mixer(h_p, P, w_in_l, w_out_l, s5p, l, n, t, rope_p, tm)
        h_p = _ffn_block(h_p, P, 'ffn2', l, tm)
        h_s = _ffn_block(h_s, P, 'ffn1', l, ns)
        h_s, st_s = _sample_mixer(h_s, P, w_in_l, w_out_l, s5p, l, caches, rope_s)
        h_s = _ffn_block(h_s, P, 'ffn2', l, ns)
        for i in range(6):
            new_p[i].append(st_p[i])
            new_s[i].append(st_s[i])
    new_p = [jnp.stack(a, axis=0) for a in new_p]
    new_s = [jnp.stack(a, axis=0) for a in new_s]
    return (h_p.reshape(n, t, D_MODEL), h_s.reshape(ns, ts, D_MODEL), new_p[0], new_s[0], new_p[1], new_s[1],
            new_p[2], new_s[2], new_p[3], new_s[3], new_p[4], new_s[4], new_p[5], new_s[5])


def kernel(x_prompt, x_sample, state_ssm, cache_swa_kv, cache_dil0_kv, cache_dil1_kv, cache_dil2_kv, state_hgrn, ffn1_norm_pre, ffn1_w_gate, ffn1_w_up, ffn1_w_down, ffn1_norm_post, mix_norm_pre, w_in, ssm_a_re, ssm_a_im, ssm_log_dt, ssm_b_re, ssm_b_im, ssm_c_re, ssm_c_im, ssm_d, ssm_w_glu, ssm_b_glu, swa_sinks, hgrn_lower_bounds, out_norm_a, out_norm_b, out_norm_c, out_norm_d, w_out, mix_norm_post, ffn2_norm_pre, ffn2_w_gate, ffn2_w_up, ffn2_w_down, ffn2_norm_post):
    P = dict(ffn1_norm_pre=ffn1_norm_pre, ffn1_w_gate=ffn1_w_gate, ffn1_w_up=ffn1_w_up, ffn1_w_down=ffn1_w_down,
             ffn1_norm_post=ffn1_norm_post, mix_norm_pre=mix_norm_pre, w_in=w_in, ssm_a_re=ssm_a_re,
             ssm_a_im=ssm_a_im, ssm_log_dt=ssm_log_dt, ssm_b_re=ssm_b_re, ssm_b_im=ssm_b_im, ssm_c_re=ssm_c_re,
             ssm_c_im=ssm_c_im, ssm_d=ssm_d, ssm_w_glu=ssm_w_glu, ssm_b_glu=ssm_b_glu, swa_sinks=swa_sinks,
             hgrn_lower_bounds=hgrn_lower_bounds, out_norm_a=out_norm_a, out_norm_b=out_norm_b,
             out_norm_c=out_norm_c, out_norm_d=out_norm_d, w_out=w_out, mix_norm_post=mix_norm_post,
             ffn2_norm_pre=ffn2_norm_pre, ffn2_w_gate=ffn2_w_gate, ffn2_w_up=ffn2_w_up, ffn2_w_down=ffn2_w_down,
             ffn2_norm_post=ffn2_norm_post)
    caches = (state_ssm, cache_swa_kv, cache_dil0_kv, cache_dil1_kv, cache_dil2_kv, state_hgrn)
    return _forward(x_prompt, x_sample, caches, P)
```

```python
import functools
import math

import numpy as np
import jax
import jax.numpy as jnp
from jax import lax
from jax.experimental import pallas as pl
from jax.experimental.pallas import tpu as pltpu

F32 = jnp.float32
BF16 = jnp.bfloat16

D_MODEL = 2048
DEPTH = 2
PAST_LEN = 16384
HEAD_DIM = 64
ROPE_THETA = 10000.0
D_FF = 5504
NORM_EPS = 1e-6
NEG_BIG = -1e30
LB_FLOOR = 1e-30
SSM_GROUP = 16
SSM_STATE = 64
SSM_WIDTH = 448
SSM_GROUPS = 28
SWA_HEADS = 8
SWA_KV_HEADS = 2
SWA_GROUP = 4
SWA_WIDTH = 512
SWA_KV_WIDTH = 128
WINDOW = 128
DIL_PAIRS = ((128, 1), (512, 4), (2048, 16))
DIL_HPG = 3
DIL_GW = DIL_HPG * HEAD_DIM
DIL_WIDTH = 576
DIL_KV_WIDTH = 192
HGRN_HEAD_DIM = 128
HGRN_WIDTH = 512
HGRN_HEADS = 4
IN_COLS = 4224
ATTN_SCALE = HEAD_DIM ** -0.5

LANES = 128
FF_TILE = 256
N_FF_TILES = -(-D_FF // FF_TILE)
DOWN_TILE = 256
N_DOWN_TILES = D_MODEL // DOWN_TILE
VMEM_LIMIT = 60 * 1024 * 1024

B_SLOT = SWA_WIDTH + 2 * SWA_KV_WIDTH
C_SLOT = 384
C_K = 256
OFF_B = 0
OFF_C = OFF_B + B_SLOT
OFF_D = 2048
OFF_U = OFF_D + 4 * HGRN_WIDTH
U_SLOT = 512
PROJ_COLS = 4608
WIN_TILE = 1536

S5_CHUNK = 16
S5_NPOW = S5_CHUNK + 1
HGRN_CHUNK = 128
HGRN_LEVELS = (64, 32, 16, 8, 4, 2, 1)


def _proj_segments():
    src = np.cumsum([0, SSM_WIDTH, SWA_WIDTH, SWA_KV_WIDTH, SWA_KV_WIDTH, DIL_WIDTH, DIL_KV_WIDTH, DIL_KV_WIDTH,
                     HGRN_WIDTH, HGRN_WIDTH, HGRN_WIDTH, HGRN_WIDTH])
    s_u, s_qb, s_kb, s_vb, s_qc, s_kc, s_vc, s_qd = src[:8]
    segs = [(OFF_B, s_qb, SWA_WIDTH), (OFF_B + SWA_WIDTH, s_kb, SWA_KV_WIDTH),
            (OFF_B + SWA_WIDTH + SWA_KV_WIDTH, s_vb, SWA_KV_WIDTH)]
    for g in range(3):
        base = OFF_C + g * C_SLOT
        segs += [(base, s_qc + g * DIL_GW, DIL_GW), (base + C_K, s_kc + g * HEAD_DIM, HEAD_DIM),
                 (base + C_K + HEAD_DIM, s_vc + g * HEAD_DIM, HEAD_DIM)]
    segs += [(OFF_D, s_qd, 4 * HGRN_WIDTH), (OFF_U, s_u, SSM_WIDTH)]
    return [(int(a), int(b), int(c)) for a, b, c in segs]


def _rotary_lane_mask():
    m = np.zeros((PROJ_COLS,), np.float32)
    m[OFF_B:OFF_B + SWA_WIDTH + SWA_KV_WIDTH] = 1.0
    for g in range(3):
        base = OFF_C + g * C_SLOT
        m[base:base + DIL_GW] = 1.0
        m[base + C_K:base + C_K + HEAD_DIM] = 1.0
    return m


def _cparams(sem):
    return pltpu.CompilerParams(dimension_semantics=sem, vmem_limit_bytes=VMEM_LIMIT)


def _v3(a):
    return a.reshape(a.shape[0], 1, a.shape[1])


def _rms(x, g):
    return x * lax.rsqrt(jnp.mean(x * x, axis=-1, keepdims=True) + NORM_EPS) * g


def _sigmoid(x):
    return 1.0 / (1.0 + jnp.exp(-x))


def _silu(x):
    return x * _sigmoid(x)


def _dot(a, b):
    return jnp.dot(a, b, preferred_element_type=F32)


def _dot_nt(a, b):
    return lax.dot_general(a, b, (((1,), (1,)), ((), ())), preferred_element_type=F32)


def _dot_tn(a, b):
    return lax.dot_general(a, b, (((0,), (0,)), ((), ())), preferred_element_type=F32)


def _ffn_kernel(x_ref, gpre_ref, wg_ref, wu_ref, wd_ref, gpost_ref, o_ref, xn_ref, hid_ref):
    s = pl.program_id(1)

    @pl.when(s == 0)
    def _():
        xn_ref[...] = _rms(x_ref[...], gpre_ref[...]).astype(BF16)

    @pl.when(s < N_FF_TILES)
    def _():
        xn = xn_ref[...]
        gate = _dot(xn, wg_ref[...].astype(BF16))
        up = _dot(xn, wu_ref[...].astype(BF16))
        hid_ref[s] = (_silu(gate) * up).astype(BF16)

    for j in range(N_DOWN_TILES):

        @pl.when(s == N_FF_TILES + j)
        def _(j=j):
            hid = jnp.concatenate([hid_ref[f] for f in range(N_FF_TILES)], axis=1)[:, :D_FF]
            o_ref[:, j * DOWN_TILE:(j + 1) * DOWN_TILE] = _dot(hid, wd_ref[...].astype(BF16))

    @pl.when(s == pl.num_programs(1) - 1)
    def _():
        o_ref[...] = x_ref[...] + 0.5 * _rms(o_ref[...], gpost_ref[...])


def _ffn(h, gpre, wg, wu, wd, gpost, layer, tm):
    m = h.shape[0]
    last_ff = N_FF_TILES - 1
    return pl.pallas_call(
        _ffn_kernel,
        grid=(m // tm, N_FF_TILES + N_DOWN_TILES),
        in_specs=[
            pl.BlockSpec((tm, D_MODEL), lambda i, s: (i, 0), pipeline_mode=pl.Buffered(1)),
            pl.BlockSpec((None, 1, D_MODEL), lambda i, s: (layer, 0, 0)),
            pl.BlockSpec((None, D_MODEL, FF_TILE), lambda i, s: (layer, 0, jnp.minimum(s, last_ff))),
            pl.BlockSpec((None, D_MODEL, FF_TILE), lambda i, s: (layer, 0, jnp.minimum(s, last_ff))),
            pl.BlockSpec((None, D_FF, DOWN_TILE), lambda i, s: (layer, 0, jnp.maximum(s - N_FF_TILES, 0))),
            pl.BlockSpec((None, 1, D_MODEL), lambda i, s: (layer, 0, 0)),
        ],
        out_specs=pl.BlockSpec((tm, D_MODEL), lambda i, s: (i, 0), pipeline_mode=pl.Buffered(1)),
        out_shape=jax.ShapeDtypeStruct((m, D_MODEL), F32),
        scratch_shapes=[pltpu.VMEM((tm, D_MODEL), BF16), pltpu.VMEM((N_FF_TILES, tm, FF_TILE), BF16)],
        compiler_params=_cparams(("parallel", "arbitrary")),
    )(h, _v3(gpre), wg, wu, wd, _v3(gpost))


def _win_prep_kernel(w_ref, o_ref):
    o_ref[...] = jnp.zeros_like(o_ref)
    for dst, src, width in _proj_segments():
        o_ref[:, dst:dst + width] = w_ref[:, src:src + width].astype(BF16)


def _win_prep(w_in, layer):
    rows = 256
    return pl.pallas_call(
        _win_prep_kernel,
        grid=(D_MODEL // rows,),
        in_specs=[pl.BlockSpec((None, rows, IN_COLS), lambda i: (layer, i, 0))],
        out_specs=pl.BlockSpec((rows, PROJ_COLS), lambda i: (i, 0)),
        out_shape=jax.ShapeDtypeStruct((D_MODEL, PROJ_COLS), BF16),
        compiler_params=_cparams(("parallel",)),
    )(w_in)


def _cast_kernel(w_ref, o_ref):
    o_ref[...] = w_ref[...].astype(BF16)


def _cast_bf16(w, layer):
    _, r, c = w.shape
    rows = 256
    return pl.pallas_call(
        _cast_kernel,
        grid=(r // rows,),
        in_specs=[pl.BlockSpec((None, rows, c), lambda i: (layer, i, 0))],
        out_specs=pl.BlockSpec((rows, c), lambda i: (i, 0)),
        out_shape=jax.ShapeDtypeStruct((r, c), BF16),
        compiler_params=_cparams(("parallel",)),
    )(w)


def _win_kernel(x_ref, g_ref, w_ref, cos_ref, sin_ref, o_ref, xn_ref):
    j = pl.program_id(1)

    @pl.when(j == 0)
    def _():
        xn_ref[...] = _rms(x_ref[...], g_ref[...]).astype(BF16)

    rot = _rotary_lane_mask()
    for tile in range(PROJ_COLS // WIN_TILE):

        @pl.when(j == tile)
        def _(tile=tile):
            y = _dot(xn_ref[...], w_ref[...])
            cos = cos_ref[...]
            sin = sin_ref[...]
            lane = lax.broadcasted_iota(jnp.int32, cos.shape, 1)
            first_half = (lane % HEAD_DIM) < (HEAD_DIM // 2)
            for c in range(WIN_TILE // LANES):
                sl = slice(c * LANES, (c + 1) * LANES)
                flags = rot[tile * WIN_TILE + c * LANES:tile * WIN_TILE + (c + 1) * LANES]
                x = y[:, sl]
                lo, hi = bool(flags[:HEAD_DIM].all()), bool(flags[HEAD_DIM:].all())
                if not (lo or hi):
                    o_ref[:, sl] = x
                    continue
                partner = jnp.where(first_half,
                                    pltpu.roll(x, LANES - HEAD_DIM // 2, axis=1),
                                    pltpu.roll(x, HEAD_DIM // 2, axis=1))
                rotated = x * cos + partner * sin
                if lo and hi:
                    o_ref[:, sl] = rotated
                else:
                    is_rot = (lane < HEAD_DIM) if lo else (lane >= HEAD_DIM)
                    o_ref[:, sl] = jnp.where(is_rot, rotated, x)


def _win(h, g, w, cos_t, sin_t, layer, tm):
    m = h.shape[0]
    return pl.pallas_call(
        _win_kernel,
        grid=(m // tm, PROJ_COLS // WIN_TILE),
        in_specs=[
            pl.BlockSpec((tm, D_MODEL), lambda i, j: (i, 0)),
            pl.BlockSpec((None, 1, D_MODEL), lambda i, j: (layer, 0, 0)),
            pl.BlockSpec((D_MODEL, WIN_TILE), lambda i, j: (0, j)),
            pl.BlockSpec((tm, LANES), lambda i, j: (i, 0)),
            pl.BlockSpec((tm, LANES), lambda i, j: (i, 0)),
        ],
        out_specs=pl.BlockSpec((tm, WIN_TILE), lambda i, j: (i, j)),
        out_shape=jax.ShapeDtypeStruct((m, PROJ_COLS), F32),
        scratch_shapes=[pltpu.VMEM((tm, D_MODEL), BF16)],
        compiler_params=_cparams(("parallel", "arbitrary")),
    )(h, _v3(g), w, cos_t, sin_t)


def _rope_tables(pos):
    half = HEAD_DIM // 2
    inv_freq = ROPE_THETA ** (-jnp.arange(half, dtype=F32) / half)
    ang = pos.astype(F32)[:, None] * inv_freq[None, :]
    cos, sin = jnp.cos(ang), jnp.sin(ang)
    reps = LANES // HEAD_DIM
    cos_t = jnp.tile(jnp.concatenate([cos, cos], axis=1), (1, reps))
    sin_t = jnp.tile(jnp.concatenate([-sin, sin], axis=1), (1, reps))
    return cos_t, sin_t


def _band_heads(q, kk, vv, valid, sinks):
    w = WINDOW
    heads = q.shape[1] // HEAD_DIM
    qs = jnp.concatenate([q[:, g * HEAD_DIM:(g + 1) * HEAD_DIM] for g in range(heads)], axis=0).astype(BF16)
    s = _dot_nt(kk, qs) * ATTN_SCALE
    s = jnp.where(jnp.concatenate([valid] * heads, axis=1), s, NEG_BIG)
    m = jnp.max(s, axis=0, keepdims=True)
    if sinks is not None:
        sink_row = jnp.concatenate([jnp.broadcast_to(sinks[g], (1, w)) for g in range(heads)], axis=1)
        m = jnp.maximum(m, sink_row)
    p = jnp.exp(s - m)
    l = jnp.sum(p, axis=0, keepdims=True)
    if sinks is not None:
        l = l + jnp.exp(sink_row - m)
    ot = _dot_tn(vv, p.astype(BF16)) / l
    lse = m + jnp.log(l)
    return ([ot[:, g * w:(g + 1) * w] for g in range(heads)], [lse[:, g * w:(g + 1) * w] for g in range(heads)])


def _untranspose(blocks):
    pairs = [jnp.concatenate(blocks[i:i + 2], axis=0).T for i in range(0, len(blocks), 2)]
    return jnp.concatenate(pairs, axis=1)


def _band_mask(i):
    w = WINDOW
    k = lax.broadcasted_iota(jnp.int32, (2 * w, w), 0)
    q = lax.broadcasted_iota(jnp.int32, (2 * w, w), 1)
    first_key = jnp.where(i == 0, w, 0)
    return (k >= q) & (k <= q + w) & (k >= first_key)


def _swa_prompt_kernel(cur_ref, prev_ref, sink_ref, o_ref):
    valid = _band_mask(pl.program_id(1))
    cur = cur_ref[0]
    prev = prev_ref[0]
    outs = []
    for h in range(SWA_KV_HEADS):
        ks = slice(SWA_WIDTH + h * HEAD_DIM, SWA_WIDTH + (h + 1) * HEAD_DIM)
        vs = slice(SWA_WIDTH + SWA_KV_WIDTH + h * HEAD_DIM, SWA_WIDTH + SWA_KV_WIDTH + (h + 1) * HEAD_DIM)
        kk = jnp.concatenate([prev[:, h * HEAD_DIM:(h + 1) * HEAD_DIM], cur[:, ks]], axis=0).astype(BF16)
        vv = jnp.concatenate([prev[:, SWA_KV_WIDTH + h * HEAD_DIM:SWA_KV_WIDTH + (h + 1) * HEAD_DIM], cur[:, vs]],
                             axis=0).astype(BF16)
        gw = SWA_GROUP * HEAD_DIM
        sinks = [sink_ref[:, h * gw + g * HEAD_DIM:h * gw + g * HEAD_DIM + 1] for g in range(SWA_GROUP)]
        o, _ = _band_heads(cur[:, h * gw:(h + 1) * gw], kk, vv, valid, sinks)
        outs += o
    o_ref[0] = _untranspose(outs)


def _swa_prompt(proj, sink):
    n, t, _ = proj.shape
    return pl.pallas_call(
        _swa_prompt_kernel,
        grid=(n, t // WINDOW),
        in_specs=[
            pl.BlockSpec((1, WINDOW, B_SLOT), lambda b, i: (b, i, OFF_B // B_SLOT)),
            pl.BlockSpec((1, WINDOW, 2 * SWA_KV_WIDTH),
                         lambda b, i: (b, jnp.maximum(i - 1, 0), (OFF_B + SWA_WIDTH) // (2 * SWA_KV_WIDTH))),
            pl.BlockSpec((1, SWA_WIDTH), lambda b, i: (0, 0)),
        ],
        out_specs=pl.BlockSpec((1, WINDOW, SWA_WIDTH), lambda b, i: (b, i, 0)),
        out_shape=jax.ShapeDtypeStruct((n, t, SWA_WIDTH), F32),
        compiler_params=_cparams(("parallel", "arbitrary")),
    )(proj, proj, sink)


def _dil_prompt_kernel(cur_ref, prev_ref, o_ref, cur3_ref, out3_ref, *, dil):
    valid = _band_mask(pl.program_id(1))
    nl = C_SLOT // LANES
    for c in range(nl):
        cur3_ref[c] = cur_ref[0, :, c * LANES:(c + 1) * LANES]

    def one(r):
        rows = pl.ds(r, WINDOW, stride=dil) if dil > 1 else slice(None)
        q = jnp.concatenate([cur3_ref[0, rows, :], cur3_ref[1, rows, :]], axis=1)[:, 0:DIL_GW]
        kv = cur3_ref[2, rows, :]
        pkv = prev_ref[0, rows, :]
        kk = jnp.concatenate([pkv[:, 0:HEAD_DIM], kv[:, 0:HEAD_DIM]], axis=0).astype(BF16)
        vv = jnp.concatenate([pkv[:, HEAD_DIM:], kv[:, HEAD_DIM:]], axis=0).astype(BF16)
        o, lse = _band_heads(q, kk, vv, valid, None)
        res = _untranspose(o + [jnp.broadcast_to(x, (HEAD_DIM, WINDOW)) for x in lse])
        for c in range(nl):
            out3_ref[c, rows, :] = res[:, c * LANES:(c + 1) * LANES]

    if dil <= 4:
        for r in range(dil):
            one(r)
    else:
        lax.fori_loop(0, dil, lambda r, c: (one(r), c)[1], 0)
    for c in range(nl):
        o_ref[0, :, c * LANES:(c + 1) * LANES] = out3_ref[c]


def _dil_prompt(proj, group, dil):
    n, t, _ = proj.shape
    rows = WINDOW * dil
    slot = OFF_C + group * C_SLOT
    return pl.pallas_call(
        functools.partial(_dil_prompt_kernel, dil=dil),
        grid=(n, t // rows),
        in_specs=[
            pl.BlockSpec((1, rows, C_SLOT), lambda b, i: (b, i, slot // C_SLOT)),
            pl.BlockSpec((1, rows, LANES), lambda b, i: (b, jnp.maximum(i - 1, 0), (slot + C_K) // LANES)),
        ],
        out_specs=pl.BlockSpec((1, rows, 2 * DIL_GW), lambda b, i: (b, i, 0)),
        out_shape=jax.ShapeDtypeStruct((n, t, 2 * DIL_GW), F32),
        scratch_shapes=[pltpu.VMEM((C_SLOT // LANES, rows, LANES), F32),
                        pltpu.VMEM((2 * DIL_GW // LANES, rows, LANES), F32)],
        compiler_params=_cparams(("parallel", "arbitrary")),
    )(proj, proj)


def _decode_heads(q, knew, vnew, kt, vt, dil, sinks):
    s = _dot(q.astype(BF16), kt.astype(BF16)) * ATTN_SCALE
    if dil > 1:
        row = lax.broadcasted_iota(jnp.int32, s.shape, 1)
        s = jnp.where(row % dil == 0, s, NEG_BIG)
    s_new = jnp.sum(q * knew, axis=-1, keepdims=True) * ATTN_SCALE
    m = jnp.maximum(jnp.max(s, axis=-1, keepdims=True), s_new)
    if sinks is not None:
        m = jnp.maximum(m, sinks)
    p = jnp.exp(s - m)
    p_new = jnp.exp(s_new - m)
    l = jnp.sum(p, axis=-1, keepdims=True) + p_new
    if sinks is not None:
        l = l + jnp.exp(sinks - m)
    o = (_dot_nt(p.astype(BF16), vt.astype(BF16)) + p_new * vnew) / l
    return o, m + jnp.log(l)


def _rows_of(x, heads):
    return jnp.concatenate([x[:, g * HEAD_DIM:(g + 1) * HEAD_DIM] for g in range(heads)], axis=0)


def _swa_sample_kernel(p_ref, cache_ref, sink_ref, o_ref, *, bn):
    for b in range(bn):
        row = p_ref[b:b + 1, :]
        outs = []
        for h in range(SWA_KV_HEADS):
            gw = SWA_GROUP * HEAD_DIM
            q = _rows_of(row[:, h * gw:(h + 1) * gw], SWA_GROUP)
            knew = row[:, SWA_WIDTH + h * HEAD_DIM:SWA_WIDTH + (h + 1) * HEAD_DIM]
            vnew = row[:, SWA_WIDTH + SWA_KV_WIDTH + h * HEAD_DIM:SWA_WIDTH + SWA_KV_WIDTH + (h + 1) * HEAD_DIM]
            sinks = _rows_of(sink_ref[:, h * gw:(h + 1) * gw], SWA_GROUP)[:, 0:1]
            o, _ = _decode_heads(q, knew, vnew, cache_ref[b, 0, h], cache_ref[b, 1, h], 1, sinks)
            outs += [o[g:g + 1] for g in range(SWA_GROUP)]
        o_ref[b:b + 1, :] = jnp.concatenate(outs, axis=1)


def _swa_sample(proj, cache, sink, layer):
    n = proj.shape[0]
    bn = 8
    return pl.pallas_call(
        functools.partial(_swa_sample_kernel, bn=bn),
        grid=(n // bn,),
        in_specs=[
            pl.BlockSpec((bn, B_SLOT), lambda i: (i, OFF_B // B_SLOT)),
            pl.BlockSpec((None, bn, 2, SWA_KV_HEADS, HEAD_DIM, WINDOW), lambda i: (layer, i, 0, 0, 0, 0)),
            pl.BlockSpec((1, SWA_WIDTH), lambda i: (0, 0)),
        ],
        out_specs=pl.BlockSpec((bn, SWA_WIDTH), lambda i: (i, 0)),
        out_shape=jax.ShapeDtypeStruct((n, SWA_WIDTH), F32),
        compiler_params=_cparams(("parallel",)),
    )(proj, cache, sink)


def _dil_sample_kernel(p_ref, cache_ref, o_ref, *, bn, dil):
    for b in range(bn):
        row = p_ref[b:b + 1, :]
        q = _rows_of(row[:, 0:DIL_GW], DIL_HPG)
        o, lse = _decode_heads(q, row[:, C_K:C_K + HEAD_DIM], row[:, C_K + HEAD_DIM:C_K + 2 * HEAD_DIM],
                               cache_ref[b, 0], cache_ref[b, 1], dil, None)
        lse = jnp.broadcast_to(lse, (DIL_HPG, HEAD_DIM))
        o_ref[b:b + 1, :] = jnp.concatenate([o[g:g + 1] for g in range(DIL_HPG)]
                                            + [lse[g:g + 1] for g in range(DIL_HPG)], axis=1)


def _dil_sample(proj, cache, group, dil, layer):
    n = proj.shape[0]
    bn = 8
    slot = OFF_C + group * C_SLOT
    return pl.pallas_call(
        functools.partial(_dil_sample_kernel, bn=bn, dil=dil),
        grid=(n // bn,),
        in_specs=[
            pl.BlockSpec((bn, C_SLOT), lambda i: (i, slot // C_SLOT)),
            pl.BlockSpec((None, bn, 2, HEAD_DIM, WINDOW * dil), lambda i: (layer, i, 0, 0, 0)),
        ],
        out_specs=pl.BlockSpec((bn, 2 * DIL_GW), lambda i: (i, 0)),
        out_shape=jax.ShapeDtypeStruct((n, 2 * DIL_GW), F32),
        compiler_params=_cparams(("parallel",)),
    )(proj, cache)


def _s5_param_kernel(are_ref, aim_ref, ldt_ref, bre_ref, bim_ref, cre_ref, cim_ref,
                     lam_re_ref, lam_im_ref, cl_re_ref, cl_im_ref, r_re_ref, r_im_ref, k_ref):
    ar = are_ref[0]
    ai = aim_ref[0]
    dt = jnp.exp(ldt_ref[0])
    nrow = lam_re_ref.shape[1]
    ri = lax.broadcasted_iota(jnp.int32, (nrow, 1), 0)
    pw = jnp.where(ri < S5_NPOW, ri, jnp.left_shift(S5_CHUNK, jnp.maximum(ri - S5_NPOW, 0))).astype(F32)
    mag = jnp.exp(pw * (ar * dt))
    ang = pw * (ai * dt)
    lam_re = mag * jnp.cos(ang)
    lam_im = mag * jnp.sin(ang)
    lam_re_ref[0] = lam_re
    lam_im_ref[0] = lam_im
    l1r, l1i = lam_re[1:2], lam_im[1:2]
    den = ar * ar + ai * ai
    z_re = ((l1r - 1.0) * ar + l1i * ai) / den
    z_im = (l1i * ar - (l1r - 1.0) * ai) / den
    b_re, b_im = bre_ref[0], bim_ref[0]
    bb_re = z_re * b_re - z_im * b_im
    bb_im = z_re * b_im + z_im * b_re
    c_re, c_im = cre_ref[0], cim_ref[0]
    cls_re, cls_im = [], []
    for d in range(S5_NPOW):
        lr, li = lam_re[d:d + 1], lam_im[d:d + 1]
        cr = c_re * lr - c_im * li
        ci = c_re * li + c_im * lr
        cl_re_ref[0, d * SSM_GROUP:(d + 1) * SSM_GROUP, :] = cr
        cl_im_ref[0, d * SSM_GROUP:(d + 1) * SSM_GROUP, :] = ci
        if d < S5_CHUNK:
            cls_re.append(cr)
            cls_im.append(ci)
    for s in range(S5_CHUNK):
        lr = lam_re[S5_CHUNK - 1 - s:S5_CHUNK - s]
        li = lam_im[S5_CHUNK - 1 - s:S5_CHUNK - s]
        r_re_ref[0, s * SSM_GROUP:(s + 1) * SSM_GROUP, :] = lr * bb_re - li * bb_im
        r_im_ref[0, s * SSM_GROUP:(s + 1) * SSM_GROUP, :] = lr * bb_im + li * bb_re
    call_re = jnp.concatenate(cls_re, axis=0)
    call_im = jnp.concatenate(cls_im, axis=0)
    hp = lax.Precision.HIGHEST
    nt = (((1,), (1,)), ((), ()))
    k_ref[0] = (lax.dot_general(call_re, bb_re, nt, precision=hp, preferred_element_type=F32)
                - lax.dot_general(call_im, bb_im, nt, precision=hp, preferred_element_type=F32))


def _s5_params(a_re, a_im, log_dt, b_re, b_im, c_re, c_im, nsteps):
    g = SSM_GROUPS
    nrow = ((S5_NPOW + nsteps + 7) // 8) * 8
    row = lambda i: (i, 0, 0)
    shp = lambda r, c: jax.ShapeDtypeStruct((g, r, c), F32)
    ncl = S5_NPOW * SSM_GROUP
    nr = S5_CHUNK * SSM_GROUP
    return pl.pallas_call(
        _s5_param_kernel,
        grid=(g,),
        in_specs=[
            pl.BlockSpec((1, 1, SSM_STATE), row),
            pl.BlockSpec((1, 1, SSM_STATE), row),
            pl.BlockSpec((1, 1, 1), row),
            pl.BlockSpec((1, SSM_GROUP, SSM_STATE), row),
            pl.BlockSpec((1, SSM_GROUP, SSM_STATE), row),
            pl.BlockSpec((1, SSM_GROUP, SSM_STATE), row),
            pl.BlockSpec((1, SSM_GROUP, SSM_STATE), row),
        ],
        out_specs=[
            pl.BlockSpec((1, nrow, SSM_STATE), row), pl.BlockSpec((1, nrow, SSM_STATE), row),
            pl.BlockSpec((1, ncl, SSM_STATE), row), pl.BlockSpec((1, ncl, SSM_STATE), row),
            pl.BlockSpec((1, nr, SSM_STATE), row), pl.BlockSpec((1, nr, SSM_STATE), row),
            pl.BlockSpec((1, nr, SSM_GROUP), row),
        ],
        out_shape=[shp(nrow, SSM_STATE), shp(nrow, SSM_STATE), shp(ncl, SSM_STATE), shp(ncl, SSM_STATE),
                   shp(nr, SSM_STATE), shp(nr, SSM_STATE), shp(nr, SSM_GROUP)],
        compiler_params=_cparams(("parallel",)),
    )(a_re.reshape(g, 1, SSM_STATE), a_im.reshape(g, 1, SSM_STATE), log_dt.reshape(g, 1, 1),
      jnp.swapaxes(b_re, 1, 2), jnp.swapaxes(b_im, 1, 2), c_re, c_im)


def _s5_chunk_in_kernel(p_ref, u_ref, st_ref):
    tm = p_ref.shape[0]
    nc = tm // S5_CHUNK
    per_blk = LANES // SSM_GROUP
    for lb in range(U_SLOT // LANES):
        st_ref[lb] = p_ref[:, lb * LANES:(lb + 1) * LANES]
    slot = lax.broadcasted_iota(jnp.int32, (nc, LANES), 1) // SSM_GROUP
    for g in range(SSM_GROUPS):
        acc = [jnp.zeros((nc, LANES), F32) for _ in range(S5_CHUNK // per_blk)]
        for s in range(S5_CHUNK):
            v = st_ref[g // per_blk, pl.ds(s, nc, stride=S5_CHUNK), :]
            shift = (SSM_GROUP * (s % per_blk) - SSM_GROUP * (g % per_blk)) % LANES
            if shift:
                v = pltpu.roll(v, shift, axis=1)
            acc[s // per_blk] = jnp.where(slot == s % per_blk, v, acc[s // per_blk])
        for k, a in enumerate(acc):
            u_ref[g, :, k * LANES:(k + 1) * LANES] = a


def _s5_chunk_in(proj2, tm):
    m = proj2.shape[0]
    cw = S5_CHUNK * SSM_GROUP
    return pl.pallas_call(
        _s5_chunk_in_kernel,
        grid=(m // tm,),
        in_specs=[pl.BlockSpec((tm, U_SLOT), lambda i: (i, OFF_U // U_SLOT))],
        out_specs=pl.BlockSpec((SSM_GROUPS, tm // S5_CHUNK, cw), lambda i: (0, i, 0)),
        out_shape=jax.ShapeDtypeStruct((SSM_GROUPS, m // S5_CHUNK, cw), F32),
        scratch_shapes=[pltpu.VMEM((U_SLOT // LANES, tm, LANES), F32)],
        compiler_params=_cparams(("parallel",)),
    )(proj2)


def _s5_chunk_out_kernel(y_ref, o_ref, st_ref):
    nc = y_ref.shape[1]
    per_blk = LANES // SSM_GROUP
    slot = lax.broadcasted_iota(jnp.int32, (nc, LANES), 1) // SSM_GROUP
    nblk = -(-SSM_WIDTH // LANES)
    for lb in range(nblk):
        groups = range(lb * per_blk, min((lb + 1) * per_blk, SSM_GROUPS))
        for s in range(S5_CHUNK):
            acc = jnp.zeros((nc, LANES), F32)
            for g in groups:
                v = y_ref[g, :, (s // per_blk) * LANES:(s // per_blk + 1) * LANES]
                shift = (SSM_GROUP * (g % per_blk) - SSM_GROUP * (s % per_blk)) % LANES
                if shift:
                    v = pltpu.roll(v, shift, axis=1)
                acc = jnp.where(slot == g % per_blk, v, acc)
            st_ref[lb, pl.ds(s, nc, stride=S5_CHUNK), :] = acc
    for lb in range(nblk):
        width = min(LANES, SSM_WIDTH - lb * LANES)
        o_ref[:, lb * LANES:lb * LANES + width] = st_ref[lb][:, 0:width]


def _s5_chunk_out(y, tm):
    m = y.shape[1] * S5_CHUNK
    cw = S5_CHUNK * SSM_GROUP
    return pl.pallas_call(
        _s5_chunk_out_kernel,
        grid=(m // tm,),
        in_specs=[pl.BlockSpec((SSM_GROUPS, tm // S5_CHUNK, cw), lambda i: (0, i, 0))],
        out_specs=pl.BlockSpec((tm, SSM_WIDTH), lambda i: (i, 0)),
        out_shape=jax.ShapeDtypeStruct((m, SSM_WIDTH), F32),
        scratch_shapes=[pltpu.VMEM((-(-SSM_WIDTH // LANES), tm, LANES), F32)],
        compiler_params=_cparams(("parallel",)),
    )(y)


def _s5_prompt_kernel(u_ref, mt_ref, r_re_ref, r_im_ref, cl_re_ref, cl_im_ref, lam_re_ref, lam_im_ref, d_ref,
                      y_ref, xre_ref, xim_ref, *, nseq, nchunk):
    u = u_ref[0]
    ub = u.astype(BF16)
    y = _dot(ub, mt_ref[0].astype(BF16))
    xr = _dot(ub, r_re_ref[0].astype(BF16))
    xi = _dot(ub, r_im_ref[0].astype(BF16))
    rows = nseq * nchunk
    cidx = lax.broadcasted_iota(jnp.int32, (rows, SSM_STATE), 0) % nchunk
    lam_re, lam_im = lam_re_ref[0], lam_im_ref[0]
    for k in range(int(math.log2(nchunk))):
        sh = 1 << k
        lr = lam_re[S5_NPOW + k:S5_NPOW + k + 1]
        li = lam_im[S5_NPOW + k:S5_NPOW + k + 1]
        ok = cidx >= sh
        pr = jnp.where(ok, pltpu.roll(xr, sh, axis=0), 0.0)
        pi = jnp.where(ok, pltpu.roll(xi, sh, axis=0), 0.0)
        xr, xi = xr + lr * pr - li * pi, xi + lr * pi + li * pr
    ok = cidx >= 1
    er = jnp.where(ok, pltpu.roll(xr, 1, axis=0), 0.0)
    ei = jnp.where(ok, pltpu.roll(xi, 1, axis=0), 0.0)
    cl_re = cl_re_ref[0][SSM_GROUP:, :]
    cl_im = cl_im_ref[0][SSM_GROUP:, :]
    y = y + _dot_nt(er.astype(BF16), cl_re.astype(BF16)) - _dot_nt(ei.astype(BF16), cl_im.astype(BF16))
    y_ref[0] = y + d_ref[0] * u
    last = [s * nchunk + nchunk - 1 for s in range(nseq)]
    xre_ref[0] = jnp.concatenate([xr[i:i + 1] for i in last], axis=0)
    xim_ref[0] = jnp.concatenate([xi[i:i + 1] for i in last], axis=0)


def _s5_prompt(u, mt, r_re, r_im, cl_re, cl_im, lam_re, lam_im, d_t, nseq, nchunk):
    g = SSM_GROUPS
    rows = nseq * nchunk
    cw = S5_CHUNK * SSM_GROUP
    row = lambda i: (i, 0, 0)
    full = lambda a: pl.BlockSpec((1,) + a.shape[1:], row)
    return pl.pallas_call(
        functools.partial(_s5_prompt_kernel, nseq=nseq, nchunk=nchunk),
        grid=(g,),
        in_specs=[full(u), full(mt), full(r_re), full(r_im), full(cl_re), full(cl_im), full(lam_re), full(lam_im),
                  full(d_t)],
        out_specs=[pl.BlockSpec((1, rows, cw), row), pl.BlockSpec((1, nseq, SSM_STATE), row),
                   pl.BlockSpec((1, nseq, SSM_STATE), row)],
        out_shape=[jax.ShapeDtypeStruct((g, rows, cw), F32), jax.ShapeDtypeStruct((g, nseq, SSM_STATE), F32),
                   jax.ShapeDtypeStruct((g, nseq, SSM_STATE), F32)],
        compiler_params=_cparams(("parallel",)),
    )(u, mt, r_re, r_im, cl_re, cl_im, lam_re, lam_im, d_t)


def _s5_sample_kernel(u_ref, x0r_ref, x0i_ref, r_re_ref, r_im_ref, cl_re_ref, cl_im_ref, lam_re_ref, lam_im_ref,
                      d_ref, y_ref, xre_ref, xim_ref):
    u = u_ref[0]
    ub = u.astype(BF16)
    lo = (S5_CHUNK - 1) * SSM_GROUP
    bb_re = r_re_ref[0][lo:lo + SSM_GROUP, :]
    bb_im = r_im_ref[0][lo:lo + SSM_GROUP, :]
    lr, li = lam_re_ref[0][1:2], lam_im_ref[0][1:2]
    x0r, x0i = x0r_ref[0], x0i_ref[0]
    xr = _dot(ub, bb_re.astype(BF16)) + (lr * x0r - li * x0i)
    xi = _dot(ub, bb_im.astype(BF16)) + (lr * x0i + li * x0r)
    c_re = cl_re_ref[0][:SSM_GROUP, :]
    c_im = cl_im_ref[0][:SSM_GROUP, :]
    y_ref[0] = (_dot_nt(xr.astype(BF16), c_re.astype(BF16)) - _dot_nt(xi.astype(BF16), c_im.astype(BF16))
                + d_ref[0] * u)
    xre_ref[0] = xr
    xim_ref[0] = xi


def _s5_sample(u, x0r, x0i, r_re, r_im, cl_re, cl_im, lam_re, lam_im, d):
    g, n = u.shape[0], u.shape[1]
    row = lambda i: (i, 0, 0)
    full = lambda a: pl.BlockSpec((1,) + a.shape[1:], row)
    return pl.pallas_call(
        _s5_sample_kernel,
        grid=(g,),
        in_specs=[full(u), full(x0r), full(x0i), full(r_re), full(r_im), full(cl_re), full(cl_im), full(lam_re),
                  full(lam_im), full(d)],
        out_specs=[pl.BlockSpec((1, n, SSM_GROUP), row), pl.BlockSpec((1, n, SSM_STATE), row),
                   pl.BlockSpec((1, n, SSM_STATE), row)],
        out_shape=[jax.ShapeDtypeStruct((g, n, SSM_GROUP), F32), jax.ShapeDtypeStruct((g, n, SSM_STATE), F32),
                   jax.ShapeDtypeStruct((g, n, SSM_STATE), F32)],
        compiler_params=_cparams(("parallel",)),
    )(u, x0r, x0i, r_re, r_im, cl_re, cl_im, lam_re, lam_im, d)


def _log_sigmoid(x):
    return jnp.minimum(x, 0.0) - jnp.log1p(jnp.exp(-jnp.abs(x)))


def _logaddexp(a, b):
    return jnp.maximum(a, b) + jnp.log1p(jnp.exp(-jnp.abs(a - b)))


def _hgrn_gates(fpre, lb):
    log_f = _logaddexp(jnp.log(jnp.maximum(lb, LB_FLOOR)), jnp.log1p(-lb) + _log_sigmoid(fpre))
    k = (1.0 - lb) * _sigmoid(-fpre)
    return log_f, k


def _hgrn_scan_matrix():
    c = HGRN_CHUNK
    tri = np.tril(np.ones((c, c), np.float32))
    t = np.arange(c)
    mats = [tri]
    for m in HGRN_LEVELS:
        mats.append(tri[(t // (2 * m)) * (2 * m) + m - 1])
    return np.concatenate(mats, axis=0)


def _hgrn_prompt_kernel(p_ref, lbraw_ref, gn_ref, scan_ref, o_ref, s_ref, st_ref, *, layer):
    cidx = pl.program_id(1)
    c = HGRN_CHUNK
    hd = HGRN_HEAD_DIM

    @pl.when(cidx == 0)
    def _():
        st_ref[...] = jnp.zeros_like(st_ref)

    a = lbraw_ref[...]
    e = jnp.exp(a - jnp.max(a, axis=0, keepdims=True))
    p = e / jnp.sum(e, axis=0, keepdims=True)
    run = p[0:1]
    for l in range(1, layer + 1):
        run = run + p[l:l + 1]
    lb_all = run - p[0:1]

    r = lax.broadcasted_iota(jnp.int32, (c, c), 0)
    s = lax.broadcasted_iota(jnp.int32, (c, c), 1)
    pairs = []
    for m in HGRN_LEVELS:
        sh = int(math.log2(2 * m))
        pairs.append(((r >> sh) == (s >> sh)) & ((r & m) != 0) & ((s & m) == 0))
    scan = scan_ref[...]
    for h in range(HGRN_HEADS):
        col = lambda part: p_ref[0, :, part * HGRN_WIDTH + h * hd:part * HGRN_WIDTH + (h + 1) * hd]
        log_f, k = _hgrn_gates(col(1), lb_all[:, h * hd:(h + 1) * hd])
        q = _silu(col(0))
        v = col(2)
        hi = log_f.astype(BF16)
        rest = log_f - hi.astype(F32)
        mid = rest.astype(BF16)
        lo = (rest - mid.astype(F32)).astype(BF16)
        cum3 = _dot(scan, jnp.concatenate([hi, mid, lo], axis=1))
        cum = cum3[:, 0:hd] + cum3[:, hd:2 * hd] + cum3[:, 2 * hd:3 * hd]
        b = cum[0:c]
        att = jnp.where(r == s, jnp.sum(q * k, axis=-1, keepdims=True), 0.0)
        for lvl, m in enumerate(HGRN_LEVELS):
            bref = cum[(lvl + 1) * c:(lvl + 2) * c]
            qs = q * jnp.exp(jnp.minimum(b - bref, 0.0))
            ks = k * jnp.exp(jnp.minimum(bref - b, 0.0))
            att = att + jnp.where(pairs[lvl], _dot_nt(qs.astype(BF16), ks.astype(BF16)), 0.0)
        st = st_ref[h]
        vb = v.astype(BF16)
        o = _dot(att.astype(BF16), vb) + _dot_nt((q * jnp.exp(b)).astype(BF16), st.astype(BF16))
        bl = b[c - 1:c]
        st_new = jnp.exp(bl) * st + _dot_tn(vb, (k * jnp.exp(bl - b)).astype(BF16))
        st_ref[h] = st_new
        o = o * lax.rsqrt(jnp.mean(o * o, axis=-1, keepdims=True) + NORM_EPS)
        o_ref[0, :, h * hd:(h + 1) * hd] = o * gn_ref[:, h * hd:(h + 1) * hd] * _silu(col(3))

    @pl.when(cidx == pl.num_programs(1) - 1)
    def _():
        s_ref[0] = st_ref[...]


def _hgrn_prompt(proj, lbraw, gn, layer):
    n, t, _ = proj.shape
    c = HGRN_CHUNK
    hd = HGRN_HEAD_DIM
    scan = jnp.asarray(_hgrn_scan_matrix(), dtype=BF16)
    return pl.pallas_call(
        functools.partial(_hgrn_prompt_kernel, layer=layer),
        grid=(n, t // c),
        in_specs=[pl.BlockSpec((1, c, 4 * HGRN_WIDTH), lambda b, j: (b, j, OFF_D // (4 * HGRN_WIDTH))),
                  pl.BlockSpec((DEPTH, HGRN_WIDTH), lambda b, j: (0, 0)),
                  pl.BlockSpec((None, 1, HGRN_WIDTH), lambda b, j: (layer, 0, 0)),
                  pl.BlockSpec(scan.shape, lambda b, j: (0, 0))],
        out_specs=[pl.BlockSpec((1, c, HGRN_WIDTH), lambda b, j: (b, j, 0)),
                   pl.BlockSpec((1, HGRN_HEADS, hd, hd), lambda b, j: (b, 0, 0, 0))],
        out_shape=[jax.ShapeDtypeStruct((n, t, HGRN_WIDTH), F32),
                   jax.ShapeDtypeStruct((n, HGRN_HEADS, hd, hd), F32)],
        scratch_shapes=[pltpu.VMEM((HGRN_HEADS, hd, hd), F32)],
        compiler_params=_cparams(("parallel", "arbitrary")),
    )(proj, lbraw, _v3(gn), scan)


def _hgrn_sample_kernel(qt_ref, ft_ref, lbt_ref, v_ref, g_ref, gn_ref, s0_ref, o_ref, s_ref, *, layer, n):
    hd = HGRN_HEAD_DIM
    a = lbt_ref[0]
    e = jnp.exp(a - jnp.max(a, axis=1, keepdims=True))
    p = e / jnp.sum(e, axis=1, keepdims=True)
    run = p[:, 0:1]
    for l in range(1, layer + 1):
        run = run + p[:, l:l + 1]
    lb = run - p[:, 0:1]
    log_f, k = _hgrn_gates(ft_ref[0], lb)
    f = jnp.exp(log_f)
    q = _silu(qt_ref[0])
    v = v_ref[...]
    gate = g_ref[...]
    rows = []
    for b in range(n):
        s_new = f[:, b:b + 1] * s0_ref[b, 0] + k[:, b:b + 1] * v[b:b + 1, :]
        s_ref[b, 0] = s_new
        rows.append(jnp.sum(q[:, b:b + 1] * s_new, axis=0, keepdims=True))
    o = jnp.concatenate(rows, axis=0)
    o = o * lax.rsqrt(jnp.mean(o * o, axis=-1, keepdims=True) + NORM_EPS)
    o_ref[...] = o * gn_ref[...] * _silu(gate)


def _hgrn_sample(qt, ft, lbt, proj, gn, s0, layer):
    n = proj.shape[0]
    hd = HGRN_HEAD_DIM
    v_blk = (OFF_D + 2 * HGRN_WIDTH) // hd
    g_blk = (OFF_D + 3 * HGRN_WIDTH) // hd
    return pl.pallas_call(
        functools.partial(_hgrn_sample_kernel, layer=layer, n=n),
        grid=(HGRN_HEADS,),
        in_specs=[pl.BlockSpec((1, hd, n), lambda h: (h, 0, 0)),
                  pl.BlockSpec((1, hd, n), lambda h: (h, 0, 0)),
                  pl.BlockSpec((1, hd, DEPTH), lambda h: (h, 0, 0)),
                  pl.BlockSpec((n, hd), lambda h: (0, v_blk + h)),
                  pl.BlockSpec((n, hd), lambda h: (0, g_blk + h)),
                  pl.BlockSpec((None, 1, hd), lambda h: (layer, 0, h)),
                  pl.BlockSpec((None, n, 1, hd, hd), lambda h: (layer, 0, h, 0, 0))],
        out_specs=[pl.BlockSpec((n, hd), lambda h: (0, h)),
                   pl.BlockSpec((n, 1, hd, hd), lambda h: (0, h, 0, 0))],
        out_shape=[jax.ShapeDtypeStruct((n, HGRN_WIDTH), F32),
                   jax.ShapeDtypeStruct((n, HGRN_HEADS, hd, hd), F32)],
        compiler_params=_cparams(("parallel",)),
    )(qt, ft, lbt, proj, proj, _v3(gn), s0)


def _gelu_tanh(x):
    return 0.5 * x * (1.0 + jnp.tanh(math.sqrt(2.0 / math.pi) * (x + 0.044715 * (x * x * x))))


def _mixout_kernel(h_ref, ya_ref, ob_ref, oc0_ref, oc1_ref, oc2_ref, od_ref,
                   wglu_ref, bglu_ref, ga_ref, gb_ref, gc_ref, wo_ref, gpost_ref, o_ref):
    z = _gelu_tanh(ya_ref[...])
    gate = _sigmoid(_dot(z.astype(BF16), wglu_ref[...].astype(BF16)) + bglu_ref[...])
    out_a = _rms(z * gate, ga_ref[...])
    out_b = _rms(ob_ref[...], gb_ref[...])
    ocs = [oc0_ref, oc1_ref, oc2_ref]
    ls = [r[:, DIL_GW:2 * DIL_GW] for r in ocs]
    m = jnp.maximum(jnp.maximum(ls[0], ls[1]), ls[2])
    es = [jnp.exp(l - m) for l in ls]
    den = es[0] + es[1] + es[2]
    cs = [ocs[gi][:, 0:DIL_GW] * (es[gi] / den) for gi in range(3)]
    ss = sum(jnp.sum(c * c, axis=-1, keepdims=True) for c in cs)
    inv = lax.rsqrt(ss / DIL_WIDTH + NORM_EPS)
    gc = gc_ref[...]
    o1, o2, o3 = SSM_WIDTH, SSM_WIDTH + SWA_WIDTH, SSM_WIDTH + SWA_WIDTH + DIL_WIDTH
    wo = lambda lo, hi: wo_ref[lo:hi, :]
    mix = _dot(out_a.astype(BF16), wo(0, o1)) + _dot(out_b.astype(BF16), wo(o1, o2))
    for gi in range(3):
        c = cs[gi] * inv * gc[:, gi * DIL_GW:(gi + 1) * DIL_GW]
        mix = mix + _dot(c.astype(BF16), wo(o2 + gi * DIL_GW, o2 + (gi + 1) * DIL_GW))
    mix = mix + _dot(od_ref[...].astype(BF16), wo(o3, D_MODEL))
    o_ref[...] = h_ref[...] + _rms(mix, gpost_ref[...])


def _mixout(h, ya, ob, ocs, od, lp_all, w_out_l, layer, tm):
    m = h.shape[0]
    rowblk = lambda a: pl.BlockSpec((tm, a.shape[1]), lambda i: (i, 0))
    vec = lambda a: pl.BlockSpec((None, 1, a.shape[1]), lambda i: (layer, 0, 0))
    mat = lambda a: pl.BlockSpec((None,) + a.shape[1:], lambda i: (layer, 0, 0))
    whole = lambda a: pl.BlockSpec(a.shape, lambda i: (0, 0), pipeline_mode=pl.Buffered(1))
    acts = [h, ya, ob, *ocs, od]
    names = ['ssm_w_glu', 'ssm_b_glu', 'out_norm_a', 'out_norm_b', 'out_norm_c', 'w_out', 'mix_norm_post']
    params = [w_out_l if k == 'w_out' else lp_all[k] for k in names]
    spec = lambda k, p: whole(p) if k == 'w_out' else (mat(p) if p.ndim == 3 else vec(p))
    return pl.pallas_call(
        _mixout_kernel,
        grid=(m // tm,),
        in_specs=[rowblk(a) for a in acts] + [spec(k, p) for k, p in zip(names, params)],
        out_specs=pl.BlockSpec((tm, D_MODEL), lambda i: (i, 0)),
        out_shape=jax.ShapeDtypeStruct((m, D_MODEL), F32),
        compiler_params=_cparams(("parallel",)),
    )(*acts, *[_v3(p) if p.ndim == 2 and k != 'w_out' else p for k, p in zip(names, params)])


def _s5_toeplitz(kmat):
    g = kmat.shape[0]
    k4 = kmat.reshape(g, S5_CHUNK, SSM_GROUP, SSM_GROUP)
    k4 = jnp.concatenate([k4, jnp.zeros((g, 1, SSM_GROUP, SSM_GROUP), F32)], axis=1)
    s = np.arange(S5_CHUNK)[:, None]
    t = np.arange(S5_CHUNK)[None, :]
    idx = np.where(t >= s, t - s, S5_CHUNK)
    m = k4[:, idx]
    return m.transpose(0, 1, 4, 2, 3).reshape(g, S5_CHUNK * SSM_GROUP, S5_CHUNK * SSM_GROUP)


def _ffn_block(h, P, name, layer, tm):
    return _ffn(h, P[name + '_norm_pre'], P[name + '_w_gate'], P[name + '_w_up'], P[name + '_w_down'],
                P[name + '_norm_post'], layer, tm)


def _prompt_mixer(h, P, w_in_l, w_out_l, s5p, layer, n, t, rope, tm):
    m = n * t
    proj2 = _win(h, P['mix_norm_pre'], w_in_l, rope[0], rope[1], layer, tm)
    proj = proj2.reshape(n, t, PROJ_COLS)

    lam_re, lam_im, cl_re, cl_im, r_re, r_im, kmat = s5p
    nchunk = t // S5_CHUNK
    u = _s5_chunk_in(proj2, tm)
    d_t = jnp.tile(P['ssm_d'][layer], (1, S5_CHUNK)).reshape(SSM_GROUPS, 1, S5_CHUNK * SSM_GROUP)
    y, xre, xim = _s5_prompt(u, _s5_toeplitz(kmat), r_re, r_im, cl_re, cl_im, lam_re, lam_im, d_t, n, nchunk)
    ya = _s5_chunk_out(y, tm)
    ssm_new = jnp.stack([xre, xim], axis=-1).transpose(1, 0, 2, 3)

    sink = jnp.repeat(P['swa_sinks'][layer], HEAD_DIM).reshape(1, SWA_WIDTH)
    ob = _swa_prompt(proj, sink).reshape(m, SWA_WIDTH)
    keep = min(WINDOW, t)
    kv = proj[:, t - keep:, OFF_B + SWA_WIDTH:OFF_B + B_SLOT]
    swa_new = kv.reshape(n, keep, 2, SWA_KV_HEADS, HEAD_DIM)

    ocs, dil_new = [], []
    for gi, (win, dil) in enumerate(DIL_PAIRS):
        ocs.append(_dil_prompt(proj, gi, dil).reshape(m, 2 * DIL_GW))
        keep = min(win, t)
        lo = OFF_C + gi * C_SLOT + C_K
        dil_new.append(proj[:, t - keep:, lo:lo + 2 * HEAD_DIM].reshape(n, keep, 2, HEAD_DIM))

    od, st = _hgrn_prompt(proj, P['hgrn_lower_bounds'], P['out_norm_d'], layer)
    hgrn_new = jnp.swapaxes(st, -1, -2)

    h = _mixout(h, ya, ob, ocs, od.reshape(m, HGRN_WIDTH), P, w_out_l, layer, min(tm, 512))
    return h, (ssm_new, swa_new, dil_new[0], dil_new[1], dil_new[2], hgrn_new)


def _sample_mixer(h, P, w_in_l, w_out_l, s5p, layer, caches, rope):
    n = h.shape[0]
    state_ssm, cache_swa, cache_d0, cache_d1, cache_d2, state_hgrn = caches
    proj = _win(h, P['mix_norm_pre'], w_in_l, rope[0], rope[1], layer, n)

    lam_re, lam_im, cl_re, cl_im, r_re, r_im, _ = s5p
    u = proj[:, OFF_U:OFF_U + SSM_WIDTH].reshape(n, SSM_GROUPS, SSM_GROUP).transpose(1, 0, 2)
    x0 = state_ssm[layer].transpose(1, 0, 2, 3)
    y, xre, xim = _s5_sample(u, x0[..., 0], x0[..., 1], r_re, r_im, cl_re, cl_im, lam_re, lam_im,
                             P['ssm_d'][layer].reshape(SSM_GROUPS, 1, SSM_GROUP))
    ya = y.transpose(1, 0, 2).reshape(n, SSM_WIDTH)
    ssm_new = jnp.stack([xre, xim], axis=-1).transpose(1, 0, 2, 3)

    sink = jnp.repeat(P['swa_sinks'][layer], HEAD_DIM).reshape(1, SWA_WIDTH)
    ob = _swa_sample(proj, cache_swa.transpose(0, 1, 3, 4, 5, 2), sink, layer)
    swa_new = proj[:, OFF_B + SWA_WIDTH:OFF_B + B_SLOT].reshape(n, 1, 2, SWA_KV_HEADS, HEAD_DIM)

    ocs, dil_new = [], []
    for gi, (buf, (win, dil)) in enumerate(zip((cache_d0, cache_d1, cache_d2), DIL_PAIRS)):
        ocs.append(_dil_sample(proj, buf.transpose(0, 1, 3, 4, 2), gi, dil, layer))
        lo = OFF_C + gi * C_SLOT + C_K
        dil_new.append(proj[:, lo:lo + 2 * HEAD_DIM].reshape(n, 1, 2, HEAD_DIM))

    part = lambda i: proj[:, OFF_D + i * HGRN_WIDTH:OFF_D + (i + 1) * HGRN_WIDTH].reshape(n, HGRN_HEADS, HGRN_HEAD_DIM)
    cm = lambda a: a.transpose(1, 2, 0)
    lbt = P['hgrn_lower_bounds'].reshape(DEPTH, HGRN_HEADS, HGRN_HEAD_DIM).transpose(1, 2, 0)
    od, hgrn_new = _hgrn_sample(cm(part(0)), cm(part(1)), lbt, proj, P['out_norm_d'], state_hgrn, layer)

    h = _mixout(h, ya, ob, ocs, od, P, w_out_l, layer, n)
    return h, (ssm_new, swa_new, dil_new[0], dil_new[1], dil_new[2], hgrn_new)


def _forward(x_prompt, x_sample, caches, P):
    n, t, _ = x_prompt.shape
    ns, ts, _ = x_sample.shape
    assert ts == 1
    tm = 1024
    rope_p = _rope_tables(jnp.tile(jnp.arange(t, dtype=jnp.int32), n))
    rope_s = _rope_tables(jnp.full((ns,), PAST_LEN, dtype=jnp.int32))
    nsteps = int(math.log2(t // S5_CHUNK))
    h_p = x_prompt.reshape(n * t, D_MODEL)
    h_s = x_sample.reshape(ns, D_MODEL)
    new_p = [[] for _ in range(6)]
    new_s = [[] for _ in range(6)]
    for l in range(DEPTH):
        w_in_l = _win_prep(P['w_in'], l)
        w_out_l = _cast_bf16(P['w_out'], l)
        s5p = _s5_params(P['ssm_a_re'][l], P['ssm_a_im'][l], P['ssm_log_dt'][l], P['ssm_b_re'][l], P['ssm_b_im'][l],
                         P['ssm_c_re'][l], P['ssm_c_im'][l], nsteps)
        h_p = _ffn_block(h_p, P, 'ffn1', l, tm)
        h_p, st_p = _prompt_mixer(h_p, P, w_in_l, w_out_l, s5p, l, n, t, rope_p, tm)
        h_p = _ffn_block(h_p, P, 'ffn2', l, tm)
        h_s = _ffn_block(h_s, P, 'ffn1', l, ns)
        h_s, st_s = _sample_mixer(h_s, P, w_in_l, w_out_l, s5p, l, caches, rope_s)
        h_s = _ffn_block(h_s, P, 'ffn2', l, ns)
        for i in range(6):
            new_p[i].append(st_p[i])
            new_s[i].append(st_s[i])
    new_p = [jnp.stack(a, axis=0) for a in new_p]
    new_s = [jnp.stack(a, axis=0) for a in new_s]
    return (h_p.reshape(n, t, D_MODEL), h_s.reshape(ns, ts, D_MODEL), new_p[0], new_s[0], new_p[1], new_s[1],
            new_p[2], new_s[2], new_p[3], new_s[3], new_p[4], new_s[4], new_p[5], new_s[5])


def kernel(x_prompt, x_sample, state_ssm, cache_swa_kv, cache_dil0_kv, cache_dil1_kv, cache_dil2_kv, state_hgrn, ffn1_norm_pre, ffn1_w_gate, ffn1_w_up, ffn1_w_down, ffn1_norm_post, mix_norm_pre, w_in, ssm_a_re, ssm_a_im, ssm_log_dt, ssm_b_re, ssm_b_im, ssm_c_re, ssm_c_im, ssm_d, ssm_w_glu, ssm_b_glu, swa_sinks, hgrn_lower_bounds, out_norm_a, out_norm_b, out_norm_c, out_norm_d, w_out, mix_norm_post, ffn2_norm_pre, ffn2_w_gate, ffn2_w_up, ffn2_w_down, ffn2_norm_post):
    P = dict(ffn1_norm_pre=ffn1_norm_pre, ffn1_w_gate=ffn1_w_gate, ffn1_w_up=ffn1_w_up, ffn1_w_down=ffn1_w_down,
             ffn1_norm_post=ffn1_norm_post, mix_norm_pre=mix_norm_pre, w_in=w_in, ssm_a_re=ssm_a_re,
             ssm_a_im=ssm_a_im, ssm_log_dt=ssm_log_dt, ssm_b_re=ssm_b_re, ssm_b_im=ssm_b_im, ssm_c_re=ssm_c_re,
             ssm_c_im=ssm_c_im, ssm_d=ssm_d, ssm_w_glu=ssm_w_glu, ssm_b_glu=ssm_b_glu, swa_sinks=swa_sinks,
             hgrn_lower_bounds=hgrn_lower_bounds, out_norm_a=out_norm_a, out_norm_b=out_norm_b,
             out_norm_c=out_norm_c, out_norm_d=out_norm_d, w_out=w_out, mix_norm_post=mix_norm_post,
             ffn2_norm_pre=ffn2_norm_pre, ffn2_w_gate=ffn2_w_gate, ffn2_w_up=ffn2_w_up, ffn2_w_down=ffn2_w_down,
             ffn2_norm_post=ffn2_norm_post)
    caches = (state_ssm, cache_swa_kv, cache_dil0_kv, cache_dil1_kv, cache_dil2_kv, state_hgrn)
    return _forward(x_prompt, x_sample, caches, P)
```

```python
import functools
import math

import numpy as np
import jax
import jax.numpy as jnp
from jax import lax
from jax.experimental import pallas as pl
from jax.experimental.pallas import tpu as pltpu

F32 = jnp.float32
BF16 = jnp.bfloat16

D_MODEL = 2048
DEPTH = 2
PAST_LEN = 16384
HEAD_DIM = 64
ROPE_THETA = 10000.0
D_FF = 5504
NORM_EPS = 1e-6
NEG_BIG = -1e30
LB_FLOOR = 1e-30
SSM_GROUP = 16
SSM_STATE = 64
SSM_WIDTH = 448
SSM_GROUPS = 28
SWA_HEADS = 8
SWA_KV_HEADS = 2
SWA_GROUP = 4
SWA_WIDTH = 512
SWA_KV_WIDTH = 128
WINDOW = 128
DIL_PAIRS = ((128, 1), (512, 4), (2048, 16))
DIL_HPG = 3
DIL_GW = DIL_HPG * HEAD_DIM
DIL_WIDTH = 576
DIL_KV_WIDTH = 192
HGRN_HEAD_DIM = 128
HGRN_WIDTH = 512
HGRN_HEADS = 4
IN_COLS = 4224
ATTN_SCALE = HEAD_DIM ** -0.5

LANES = 128
FF_TILE = 256
N_FF_TILES = -(-D_FF // FF_TILE)
DOWN_TILE = 256
N_DOWN_TILES = D_MODEL // DOWN_TILE
VMEM_LIMIT = 60 * 1024 * 1024

B_SLOT = SWA_WIDTH + 2 * SWA_KV_WIDTH
C_SLOT = 384
C_K = 256
OFF_B = 0
OFF_C = OFF_B + B_SLOT
OFF_D = 2048
OFF_U = OFF_D + 4 * HGRN_WIDTH
U_SLOT = 512
PROJ_COLS = 4608
WIN_TILE = 1536

S5_CHUNK = 16
S5_NPOW = S5_CHUNK + 1
HGRN_CHUNK = 128
HGRN_LEVELS = (64, 32, 16, 8, 4, 2, 1)


def _proj_segments():
    src = np.cumsum([0, SSM_WIDTH, SWA_WIDTH, SWA_KV_WIDTH, SWA_KV_WIDTH, DIL_WIDTH, DIL_KV_WIDTH, DIL_KV_WIDTH,
                     HGRN_WIDTH, HGRN_WIDTH, HGRN_WIDTH, HGRN_WIDTH])
    s_u, s_qb, s_kb, s_vb, s_qc, s_kc, s_vc, s_qd = src[:8]
    segs = [(OFF_B, s_qb, SWA_WIDTH), (OFF_B + SWA_WIDTH, s_kb, SWA_KV_WIDTH),
            (OFF_B + SWA_WIDTH + SWA_KV_WIDTH, s_vb, SWA_KV_WIDTH)]
    for g in range(3):
        base = OFF_C + g * C_SLOT
        segs += [(base, s_qc + g * DIL_GW, DIL_GW), (base + C_K, s_kc + g * HEAD_DIM, HEAD_DIM),
                 (base + C_K + HEAD_DIM, s_vc + g * HEAD_DIM, HEAD_DIM)]
    segs += [(OFF_D, s_qd, 4 * HGRN_WIDTH), (OFF_U, s_u, SSM_WIDTH)]
    return [(int(a), int(b), int(c)) for a, b, c in segs]


def _rotary_lane_mask():
    m = np.zeros((PROJ_COLS,), np.float32)
    m[OFF_B:OFF_B + SWA_WIDTH + SWA_KV_WIDTH] = 1.0
    for g in range(3):
        base = OFF_C + g * C_SLOT
        m[base:base + DIL_GW] = 1.0
        m[base + C_K:base + C_K + HEAD_DIM] = 1.0
    return m


def _cparams(sem):
    return pltpu.CompilerParams(dimension_semantics=sem, vmem_limit_bytes=VMEM_LIMIT)


def _v3(a):
    return a.reshape(a.shape[0], 1, a.shape[1])


def _rms(x, g):
    return x * lax.rsqrt(jnp.mean(x * x, axis=-1, keepdims=True) + NORM_EPS) * g


def _sigmoid(x):
    return 1.0 / (1.0 + jnp.exp(-x))


def _silu(x):
    return x * _sigmoid(x)


def _dot(a, b):
    return jnp.dot(a, b, preferred_element_type=F32)


def _dot_nt(a, b):
    return lax.dot_general(a, b, (((1,), (1,)), ((), ())), preferred_element_type=F32)


def _dot_tn(a, b):
    return lax.dot_general(a, b, (((0,), (0,)), ((), ())), preferred_element_type=F32)


def _ffn_kernel(x_ref, gpre_ref, wg_ref, wu_ref, wd_ref, gpost_ref, o_ref, xn_ref, hid_ref):
    s = pl.program_id(1)

    @pl.when(s == 0)
    def _():
        xn_ref[...] = _rms(x_ref[...], gpre_ref[...]).astype(BF16)

    @pl.when(s < N_FF_TILES)
    def _():
        xn = xn_ref[...]
        gate = _dot(xn, wg_ref[...].astype(BF16))
        up = _dot(xn, wu_ref[...].astype(BF16))
        hid_ref[s] = (_silu(gate) * up).astype(BF16)

    for j in range(N_DOWN_TILES):

        @pl.when(s == N_FF_TILES + j)
        def _(j=j):
            hid = jnp.concatenate([hid_ref[f] for f in range(N_FF_TILES)], axis=1)[:, :D_FF]
            o_ref[:, j * DOWN_TILE:(j + 1) * DOWN_TILE] = _dot(hid, wd_ref[...].astype(BF16))

    @pl.when(s == pl.num_programs(1) - 1)
    def _():
        o_ref[...] = x_ref[...] + 0.5 * _rms(o_ref[...], gpost_ref[...])


def _ffn(h, gpre, wg, wu, wd, gpost, layer, tm):
    m = h.shape[0]
    last_ff = N_FF_TILES - 1
    return pl.pallas_call(
        _ffn_kernel,
        grid=(m // tm, N_FF_TILES + N_DOWN_TILES),
        in_specs=[
            pl.BlockSpec((tm, D_MODEL), lambda i, s: (i, 0), pipeline_mode=pl.Buffered(1)),
            pl.BlockSpec((None, 1, D_MODEL), lambda i, s: (layer, 0, 0)),
            pl.BlockSpec((None, D_MODEL, FF_TILE), lambda i, s: (layer, 0, jnp.minimum(s, last_ff))),
            pl.BlockSpec((None, D_MODEL, FF_TILE), lambda i, s: (layer, 0, jnp.minimum(s, last_ff))),
            pl.BlockSpec((None, D_FF, DOWN_TILE), lambda i, s: (layer, 0, jnp.maximum(s - N_FF_TILES, 0))),
            pl.BlockSpec((None, 1, D_MODEL), lambda i, s: (layer, 0, 0)),
        ],
        out_specs=pl.BlockSpec((tm, D_MODEL), lambda i, s: (i, 0), pipeline_mode=pl.Buffered(1)),
        out_shape=jax.ShapeDtypeStruct((m, D_MODEL), F32),
        scratch_shapes=[pltpu.VMEM((tm, D_MODEL), BF16), pltpu.VMEM((N_FF_TILES, tm, FF_TILE), BF16)],
        compiler_params=_cparams(("parallel", "arbitrary")),
    )(h, _v3(gpre), wg, wu, wd, _v3(gpost))


def _win_prep_kernel(w_ref, o_ref):
    o_ref[...] = jnp.zeros_like(o_ref)
    for dst, src, width in _proj_segments():
        o_ref[:, dst:dst + width] = w_ref[:, src:src + width].astype(BF16)


def _win_prep(w_in, layer):
    rows = 256
    return pl.pallas_call(
        _win_prep_kernel,
        grid=(D_MODEL // rows,),
        in_specs=[pl.BlockSpec((None, rows, IN_COLS), lambda i: (layer, i, 0))],
        out_specs=pl.BlockSpec((rows, PROJ_COLS), lambda i: (i, 0)),
        out_shape=jax.ShapeDtypeStruct((D_MODEL, PROJ_COLS), BF16),
        compiler_params=_cparams(("parallel",)),
    )(w_in)


def _cast_kernel(w_ref, o_ref):
    o_ref[...] = w_ref[...].astype(BF16)


def _cast_bf16(w, layer):
    _, r, c = w.shape
    rows = 256
    return pl.pallas_call(
        _cast_kernel,
        grid=(r // rows,),
        in_specs=[pl.BlockSpec((None, rows, c), lambda i: (layer, i, 0))],
        out_specs=pl.BlockSpec((rows, c), lambda i: (i, 0)),
        out_shape=jax.ShapeDtypeStruct((r, c), BF16),
        compiler_params=_cparams(("parallel",)),
    )(w)


def _win_kernel(x_ref, g_ref, w_ref, cos_ref, sin_ref, o_ref, xn_ref):
    j = pl.program_id(1)

    @pl.when(j == 0)
    def _():
        xn_ref[...] = _rms(x_ref[...], g_ref[...]).astype(BF16)

    rot = _rotary_lane_mask()
    for tile in range(PROJ_COLS // WIN_TILE):

        @pl.when(j == tile)
        def _(tile=tile):
            y = _dot(xn_ref[...], w_ref[...])
            cos = cos_ref[...]
            sin = sin_ref[...]
            lane = lax.broadcasted_iota(jnp.int32, cos.shape, 1)
            first_half = (lane % HEAD_DIM) < (HEAD_DIM // 2)
            for c in range(WIN_TILE // LANES):
                sl = slice(c * LANES, (c + 1) * LANES)
                flags = rot[tile * WIN_TILE + c * LANES:tile * WIN_TILE + (c + 1) * LANES]
                x = y[:, sl]
                lo, hi = bool(flags[:HEAD_DIM].all()), bool(flags[HEAD_DIM:].all())
                if not (lo or hi):
                    o_ref[:, sl] = x
                    continue
                partner = jnp.where(first_half,
                                    pltpu.roll(x, LANES - HEAD_DIM // 2, axis=1),
                                    pltpu.roll(x, HEAD_DIM // 2, axis=1))
                rotated = x * cos + partner * sin
                if lo and hi:
                    o_ref[:, sl] = rotated
                else:
                    is_rot = (lane < HEAD_DIM) if lo else (lane >= HEAD_DIM)
                    o_ref[:, sl] = jnp.where(is_rot, rotated, x)


def _win(h, g, w, cos_t, sin_t, layer, tm):
    m = h.shape[0]
    return pl.pallas_call(
        _win_kernel,
        grid=(m // tm, PROJ_COLS // WIN_TILE),
        in_specs=[
            pl.BlockSpec((tm, D_MODEL), lambda i, j: (i, 0)),
            pl.BlockSpec((None, 1, D_MODEL), lambda i, j: (layer, 0, 0)),
            pl.BlockSpec((D_MODEL, WIN_TILE), lambda i, j: (0, j)),
            pl.BlockSpec((tm, LANES), lambda i, j: (i, 0)),
            pl.BlockSpec((tm, LANES), lambda i, j: (i, 0)),
        ],
        out_specs=pl.BlockSpec((tm, WIN_TILE), lambda i, j: (i, j)),
        out_shape=jax.ShapeDtypeStruct((m, PROJ_COLS), F32),
        scratch_shapes=[pltpu.VMEM((tm, D_MODEL), BF16)],
        compiler_params=_cparams(("parallel", "arbitrary")),
    )(h, _v3(g), w, cos_t, sin_t)


def _rope_tables(pos):
    half = HEAD_DIM // 2
    inv_freq = ROPE_THETA ** (-jnp.arange(half, dtype=F32) / half)
    ang = pos.astype(F32)[:, None] * inv_freq[None, :]
    cos, sin = jnp.cos(ang), jnp.sin(ang)
    reps = LANES // HEAD_DIM
    cos_t = jnp.tile(jnp.concatenate([cos, cos], axis=1), (1, reps))
    sin_t = jnp.tile(jnp.concatenate([-sin, sin], axis=1), (1, reps))
    return cos_t, sin_t


def _band_heads(q, kk, vv, valid, sinks):
    w = WINDOW
    heads = q.shape[1] // HEAD_DIM
    qs = jnp.concatenate([q[:, g * HEAD_DIM:(g + 1) * HEAD_DIM] for g in range(heads)], axis=0).astype(BF16)
    s = _dot_nt(kk, qs) * ATTN_SCALE
    s = jnp.where(jnp.concatenate([valid] * heads, axis=1), s, NEG_BIG)
    m = jnp.max(s, axis=0, keepdims=True)
    if sinks is not None:
        sink_row = jnp.concatenate([jnp.broadcast_to(sinks[g], (1, w)) for g in range(heads)], axis=1)
        m = jnp.maximum(m, sink_row)
    p = jnp.exp(s - m)
    l = jnp.sum(p, axis=0, keepdims=True)
    if sinks is not None:
        l = l + jnp.exp(sink_row - m)
    ot = _dot_tn(vv, p.astype(BF16)) / l
    lse = m + jnp.log(l)
    return ([ot[:, g * w:(g + 1) * w] for g in range(heads)], [lse[:, g * w:(g + 1) * w] for g in range(heads)])


def _untranspose(blocks):
    pairs = [jnp.concatenate(blocks[i:i + 2], axis=0).T for i in range(0, len(blocks), 2)]
    return jnp.concatenate(pairs, axis=1)


def _band_mask(no_prev):
    w = WINDOW
    k = lax.broadcasted_iota(jnp.int32, (2 * w, w), 0)
    q = lax.broadcasted_iota(jnp.int32, (2 * w, w), 1)
    valid = (k >= q) & (k <= q + w)
    if no_prev is False:
        return valid
    return valid & (k >= jnp.where(no_prev, w, 0))


def _swa_prompt_kernel(cur_ref, prev_ref, sink_ref, o_ref, *, nsub):
    w = WINDOW
    gw = SWA_GROUP * HEAD_DIM
    first = pl.program_id(1) == 0
    for j in range(nsub):
        valid = _band_mask(first if j == 0 else False)
        cur = cur_ref[0, j * w:(j + 1) * w, :]
        prev = prev_ref[0] if j == 0 else cur_ref[0, (j - 1) * w:j * w, SWA_WIDTH:B_SLOT]
        outs = []
        for h in range(SWA_KV_HEADS):
            ks = slice(SWA_WIDTH + h * HEAD_DIM, SWA_WIDTH + (h + 1) * HEAD_DIM)
            vs = slice(SWA_WIDTH + SWA_KV_WIDTH + h * HEAD_DIM, SWA_WIDTH + SWA_KV_WIDTH + (h + 1) * HEAD_DIM)
            kk = jnp.concatenate([prev[:, h * HEAD_DIM:(h + 1) * HEAD_DIM], cur[:, ks]], axis=0).astype(BF16)
            vv = jnp.concatenate([prev[:, SWA_KV_WIDTH + h * HEAD_DIM:SWA_KV_WIDTH + (h + 1) * HEAD_DIM],
                                  cur[:, vs]], axis=0).astype(BF16)
            sinks = [sink_ref[:, h * gw + g * HEAD_DIM:h * gw + g * HEAD_DIM + 1] for g in range(SWA_GROUP)]
            o, _ = _band_heads(cur[:, h * gw:(h + 1) * gw], kk, vv, valid, sinks)
            outs += o
        o_ref[0, j * w:(j + 1) * w, :] = _untranspose(outs)


def _swa_prompt(proj, sink):
    n, t, _ = proj.shape
    nsub = 2
    rows = WINDOW * nsub
    return pl.pallas_call(
        functools.partial(_swa_prompt_kernel, nsub=nsub),
        grid=(n, t // rows),
        in_specs=[
            pl.BlockSpec((1, rows, B_SLOT), lambda b, i: (b, i, OFF_B // B_SLOT)),
            pl.BlockSpec((1, WINDOW, 2 * SWA_KV_WIDTH),
                         lambda b, i: (b, jnp.maximum(i * nsub - 1, 0), (OFF_B + SWA_WIDTH) // (2 * SWA_KV_WIDTH))),
            pl.BlockSpec((1, SWA_WIDTH), lambda b, i: (0, 0)),
        ],
        out_specs=pl.BlockSpec((1, rows, SWA_WIDTH), lambda b, i: (b, i, 0)),
        out_shape=jax.ShapeDtypeStruct((n, t, SWA_WIDTH), F32),
        compiler_params=_cparams(("parallel", "arbitrary")),
    )(proj, proj, sink)


def _dil_prompt_kernel(cur_ref, prev_ref, o_ref, cur3_ref, out3_ref, *, dil, nsub):
    nl = C_SLOT // LANES
    span = WINDOW * dil
    first = pl.program_id(1) == 0
    for c in range(nl):
        cur3_ref[c] = cur_ref[0, :, c * LANES:(c + 1) * LANES]

    def one(j, r):
        def rows(base):
            return pl.ds(base + r, WINDOW, stride=dil) if dil > 1 else pl.ds(base, WINDOW)

        valid = _band_mask(first if j == 0 else False)
        cur = rows(j * span)
        q = jnp.concatenate([cur3_ref[0, cur, :], cur3_ref[1, cur, :]], axis=1)[:, 0:DIL_GW]
        kv = cur3_ref[2, cur, :]
        pkv = prev_ref[0, rows(0), :] if j == 0 else cur3_ref[2, rows((j - 1) * span), :]
        kk = jnp.concatenate([pkv[:, 0:HEAD_DIM], kv[:, 0:HEAD_DIM]], axis=0).astype(BF16)
        vv = jnp.concatenate([pkv[:, HEAD_DIM:], kv[:, HEAD_DIM:]], axis=0).astype(BF16)
        o, lse = _band_heads(q, kk, vv, valid, None)
        res = _untranspose(o + [jnp.broadcast_to(x, (HEAD_DIM, WINDOW)) for x in lse])
        for c in range(nl):
            out3_ref[c, cur, :] = res[:, c * LANES:(c + 1) * LANES]

    for j in range(nsub):
        if dil <= 4:
            for r in range(dil):
                one(j, r)
        else:
            lax.fori_loop(0, dil, lambda r, c, j=j: (one(j, r), c)[1], 0, unroll=2)
    for c in range(nl):
        o_ref[0, :, c * LANES:(c + 1) * LANES] = out3_ref[c]


def _dil_prompt(proj, group, dil):
    n, t, _ = proj.shape
    span = WINDOW * dil
    nsub = max(1, 4 // dil)
    rows = span * nsub
    slot = OFF_C + group * C_SLOT
    return pl.pallas_call(
        functools.partial(_dil_prompt_kernel, dil=dil, nsub=nsub),
        grid=(n, t // rows),
        in_specs=[
            pl.BlockSpec((1, rows, C_SLOT), lambda b, i: (b, i, slot // C_SLOT)),
            pl.BlockSpec((1, span, LANES), lambda b, i: (b, jnp.maximum(i * nsub - 1, 0), (slot + C_K) // LANES)),
        ],
        out_specs=pl.BlockSpec((1, rows, 2 * DIL_GW), lambda b, i: (b, i, 0)),
        out_shape=jax.ShapeDtypeStruct((n, t, 2 * DIL_GW), F32),
        scratch_shapes=[pltpu.VMEM((C_SLOT // LANES, rows, LANES), F32),
                        pltpu.VMEM((2 * DIL_GW // LANES, rows, LANES), F32)],
        compiler_params=_cparams(("parallel", "arbitrary")),
    )(proj, proj)


def _decode_heads(q, knew, vnew, kt, vt, dil, sinks):
    s = _dot(q.astype(BF16), kt.astype(BF16)) * ATTN_SCALE
    if dil > 1:
        row = lax.broadcasted_iota(jnp.int32, s.shape, 1)
        s = jnp.where(row % dil == 0, s, NEG_BIG)
    s_new = jnp.sum(q * knew, axis=-1, keepdims=True) * ATTN_SCALE
    m = jnp.maximum(jnp.max(s, axis=-1, keepdims=True), s_new)
    if sinks is not None:
        m = jnp.maximum(m, sinks)
    p = jnp.exp(s - m)
    p_new = jnp.exp(s_new - m)
    l = jnp.sum(p, axis=-1, keepdims=True) + p_new
    if sinks is not None:
        l = l + jnp.exp(sinks - m)
    o = (_dot_nt(p.astype(BF16), vt.astype(BF16)) + p_new * vnew) / l
    return o, m + jnp.log(l)


def _rows_of(x, heads):
    return jnp.concatenate([x[:, g * HEAD_DIM:(g + 1) * HEAD_DIM] for g in range(heads)], axis=0)


def _swa_sample_kernel(p_ref, cache_ref, sink_ref, o_ref, *, bn):
    for b in range(bn):
        row = p_ref[b:b + 1, :]
        outs = []
        for h in range(SWA_KV_HEADS):
            gw = SWA_GROUP * HEAD_DIM
            q = _rows_of(row[:, h * gw:(h + 1) * gw], SWA_GROUP)
            knew = row[:, SWA_WIDTH + h * HEAD_DIM:SWA_WIDTH + (h + 1) * HEAD_DIM]
            vnew = row[:, SWA_WIDTH + SWA_KV_WIDTH + h * HEAD_DIM:SWA_WIDTH + SWA_KV_WIDTH + (h + 1) * HEAD_DIM]
            sinks = _rows_of(sink_ref[:, h * gw:(h + 1) * gw], SWA_GROUP)[:, 0:1]
            o, _ = _decode_heads(q, knew, vnew, cache_ref[b, 0, h], cache_ref[b, 1, h], 1, sinks)
            outs += [o[g:g + 1] for g in range(SWA_GROUP)]
        o_ref[b:b + 1, :] = jnp.concatenate(outs, axis=1)


def _swa_sample(proj, cache, sink, layer):
    n = proj.shape[0]
    bn = 8
    return pl.pallas_call(
        functools.partial(_swa_sample_kernel, bn=bn),
        grid=(n // bn,),
        in_specs=[
            pl.BlockSpec((bn, B_SLOT), lambda i: (i, OFF_B // B_SLOT)),
            pl.BlockSpec((None, bn, 2, SWA_KV_HEADS, HEAD_DIM, WINDOW), lambda i: (layer, i, 0, 0, 0, 0)),
            pl.BlockSpec((1, SWA_WIDTH), lambda i: (0, 0)),
        ],
        out_specs=pl.BlockSpec((bn, SWA_WIDTH), lambda i: (i, 0)),
        out_shape=jax.ShapeDtypeStruct((n, SWA_WIDTH), F32),
        compiler_params=_cparams(("parallel",)),
    )(proj, cache, sink)


def _dil_sample_kernel(p_ref, cache_ref, o_ref, *, bn, dil):
    for b in range(bn):
        row = p_ref[b:b + 1, :]
        q = _rows_of(row[:, 0:DIL_GW], DIL_HPG)
        o, lse = _decode_heads(q, row[:, C_K:C_K + HEAD_DIM], row[:, C_K + HEAD_DIM:C_K + 2 * HEAD_DIM],
                               cache_ref[b, 0], cache_ref[b, 1], dil, None)
        lse = jnp.broadcast_to(lse, (DIL_HPG, HEAD_DIM))
        o_ref[b:b + 1, :] = jnp.concatenate([o[g:g + 1] for g in range(DIL_HPG)]
                                            + [lse[g:g + 1] for g in range(DIL_HPG)], axis=1)


def _dil_sample(proj, cache, group, dil, layer):
    n = proj.shape[0]
    bn = 8
    slot = OFF_C + group * C_SLOT
    return pl.pallas_call(
        functools.partial(_dil_sample_kernel, bn=bn, dil=dil),
        grid=(n // bn,),
        in_specs=[
            pl.BlockSpec((bn, C_SLOT), lambda i: (i, slot // C_SLOT)),
            pl.BlockSpec((None, bn, 2, HEAD_DIM, WINDOW * dil), lambda i: (layer, i, 0, 0, 0)),
        ],
        out_specs=pl.BlockSpec((bn, 2 * DIL_GW), lambda i: (i, 0)),
        out_shape=jax.ShapeDtypeStruct((n, 2 * DIL_GW), F32),
        compiler_params=_cparams(("parallel",)),
    )(proj, cache)


def _s5_param_kernel(are_ref, aim_ref, ldt_ref, bre_ref, bim_ref, cre_ref, cim_ref,
                     lam_re_ref, lam_im_ref, cl_re_ref, cl_im_ref, r_re_ref, r_im_ref, k_ref):
    ar = are_ref[0]
    ai = aim_ref[0]
    dt = jnp.exp(ldt_ref[0])
    nrow = lam_re_ref.shape[1]
    ri = lax.broadcasted_iota(jnp.int32, (nrow, 1), 0)
    pw = jnp.where(ri < S5_NPOW, ri, jnp.left_shift(S5_CHUNK, jnp.maximum(ri - S5_NPOW, 0))).astype(F32)
    mag = jnp.exp(pw * (ar * dt))
    ang = pw * (ai * dt)
    lam_re = mag * jnp.cos(ang)
    lam_im = mag * jnp.sin(ang)
    lam_re_ref[0] = lam_re
    lam_im_ref[0] = lam_im
    l1r, l1i = lam_re[1:2], lam_im[1:2]
    den = ar * ar + ai * ai
    z_re = ((l1r - 1.0) * ar + l1i * ai) / den
    z_im = (l1i * ar - (l1r - 1.0) * ai) / den
    b_re, b_im = bre_ref[0], bim_ref[0]
    bb_re = z_re * b_re - z_im * b_im
    bb_im = z_re * b_im + z_im * b_re
    c_re, c_im = cre_ref[0], cim_ref[0]
    cls_re, cls_im = [], []
    for d in range(S5_NPOW):
        lr, li = lam_re[d:d + 1], lam_im[d:d + 1]
        cr = c_re * lr - c_im * li
        ci = c_re * li + c_im * lr
        cl_re_ref[0, d * SSM_GROUP:(d + 1) * SSM_GROUP, :] = cr
        cl_im_ref[0, d * SSM_GROUP:(d + 1) * SSM_GROUP, :] = ci
        if d < S5_CHUNK:
            cls_re.append(cr)
            cls_im.append(ci)
    for s in range(S5_CHUNK):
        lr = lam_re[S5_CHUNK - 1 - s:S5_CHUNK - s]
        li = lam_im[S5_CHUNK - 1 - s:S5_CHUNK - s]
        r_re_ref[0, s * SSM_GROUP:(s + 1) * SSM_GROUP, :] = lr * bb_re - li * bb_im
        r_im_ref[0, s * SSM_GROUP:(s + 1) * SSM_GROUP, :] = lr * bb_im + li * bb_re
    call_re = jnp.concatenate(cls_re, axis=0)
    call_im = jnp.concatenate(cls_im, axis=0)
    hp = lax.Precision.HIGHEST
    nt = (((1,), (1,)), ((), ()))
    kt = (lax.dot_general(bb_re, call_re, nt, precision=hp, preferred_element_type=F32)
          - lax.dot_general(bb_im, call_im, nt, precision=hp, preferred_element_type=F32))
    cw = S5_CHUNK * SSM_GROUP
    for s in range(S5_CHUNK):
        if s == 0:
            blk = kt
        else:
            blk = jnp.concatenate([jnp.zeros((SSM_GROUP, s * SSM_GROUP), F32), kt[:, :cw - s * SSM_GROUP]], axis=1)
        k_ref[0, s * SSM_GROUP:(s + 1) * SSM_GROUP, :] = blk


def _s5_params(a_re, a_im, log_dt, b_re, b_im, c_re, c_im, nsteps):
    g = SSM_GROUPS
    nrow = ((S5_NPOW + nsteps + 7) // 8) * 8
    row = lambda i: (i, 0, 0)
    shp = lambda r, c: jax.ShapeDtypeStruct((g, r, c), F32)
    ncl = S5_NPOW * SSM_GROUP
    nr = S5_CHUNK * SSM_GROUP
    return pl.pallas_call(
        _s5_param_kernel,
        grid=(g,),
        in_specs=[
            pl.BlockSpec((1, 1, SSM_STATE), row),
            pl.BlockSpec((1, 1, SSM_STATE), row),
            pl.BlockSpec((1, 1, 1), row),
            pl.BlockSpec((1, SSM_GROUP, SSM_STATE), row),
            pl.BlockSpec((1, SSM_GROUP, SSM_STATE), row),
            pl.BlockSpec((1, SSM_GROUP, SSM_STATE), row),
            pl.BlockSpec((1, SSM_GROUP, SSM_STATE), row),
        ],
        out_specs=[
            pl.BlockSpec((1, nrow, SSM_STATE), row), pl.BlockSpec((1, nrow, SSM_STATE), row),
            pl.BlockSpec((1, ncl, SSM_STATE), row), pl.BlockSpec((1, ncl, SSM_STATE), row),
            pl.BlockSpec((1, nr, SSM_STATE), row), pl.BlockSpec((1, nr, SSM_STATE), row),
            pl.BlockSpec((1, nr, nr), row),
        ],
        out_shape=[shp(nrow, SSM_STATE), shp(nrow, SSM_STATE), shp(ncl, SSM_STATE), shp(ncl, SSM_STATE),
                   shp(nr, SSM_STATE), shp(nr, SSM_STATE), shp(nr, nr)],
        compiler_params=_cparams(("parallel",)),
    )(a_re.reshape(g, 1, SSM_STATE), a_im.reshape(g, 1, SSM_STATE), log_dt.reshape(g, 1, 1),
      jnp.swapaxes(b_re, 1, 2), jnp.swapaxes(b_im, 1, 2), c_re, c_im)


def _s5_chunk_in_kernel(p_ref, u_ref, st_ref):
    tm = p_ref.shape[0]
    nc = tm // S5_CHUNK
    per_blk = LANES // SSM_GROUP
    for lb in range(U_SLOT // LANES):
        st_ref[lb] = p_ref[:, lb * LANES:(lb + 1) * LANES]
    slot = lax.broadcasted_iota(jnp.int32, (nc, LANES), 1) // SSM_GROUP
    for g in range(SSM_GROUPS):
        acc = [jnp.zeros((nc, LANES), F32) for _ in range(S5_CHUNK // per_blk)]
        for s in range(S5_CHUNK):
            v = st_ref[g // per_blk, pl.ds(s, nc, stride=S5_CHUNK), :]
            shift = (SSM_GROUP * (s % per_blk) - SSM_GROUP * (g % per_blk)) % LANES
            if shift:
                v = pltpu.roll(v, shift, axis=1)
            acc[s // per_blk] = jnp.where(slot == s % per_blk, v, acc[s // per_blk])
        for k, a in enumerate(acc):
            u_ref[g, :, k * LANES:(k + 1) * LANES] = a


def _s5_chunk_in(proj2, tm):
    m = proj2.shape[0]
    cw = S5_CHUNK * SSM_GROUP
    return pl.pallas_call(
        _s5_chunk_in_kernel,
        grid=(m // tm,),
        in_specs=[pl.BlockSpec((tm, U_SLOT), lambda i: (i, OFF_U // U_SLOT))],
        out_specs=pl.BlockSpec((SSM_GROUPS, tm // S5_CHUNK, cw), lambda i: (0, i, 0)),
        out_shape=jax.ShapeDtypeStruct((SSM_GROUPS, m // S5_CHUNK, cw), F32),
        scratch_shapes=[pltpu.VMEM((U_SLOT // LANES, tm, LANES), F32)],
        compiler_params=_cparams(("parallel",)),
    )(proj2)


def _s5_chunk_out_kernel(y_ref, o_ref, st_ref):
    nc = y_ref.shape[1]
    per_blk = LANES // SSM_GROUP
    slot = lax.broadcasted_iota(jnp.int32, (nc, LANES), 1) // SSM_GROUP
    nblk = -(-SSM_WIDTH // LANES)
    for lb in range(nblk):
        groups = range(lb * per_blk, min((lb + 1) * per_blk, SSM_GROUPS))
        for s in range(S5_CHUNK):
            acc = jnp.zeros((nc, LANES), F32)
            for g in groups:
                v = y_ref[g, :, (s // per_blk) * LANES:(s // per_blk + 1) * LANES]
                shift = (SSM_GROUP * (g % per_blk) - SSM_GROUP * (s % per_blk)) % LANES
                if shift:
                    v = pltpu.roll(v, shift, axis=1)
                acc = jnp.where(slot == g % per_blk, v, acc)
            st_ref[lb, pl.ds(s, nc, stride=S5_CHUNK), :] = acc
    for lb in range(nblk):
        width = min(LANES, SSM_WIDTH - lb * LANES)
        o_ref[:, lb * LANES:lb * LANES + width] = st_ref[lb][:, 0:width]


def _s5_chunk_out(y, tm):
    m = y.shape[1] * S5_CHUNK
    cw = S5_CHUNK * SSM_GROUP
    return pl.pallas_call(
        _s5_chunk_out_kernel,
        grid=(m // tm,),
        in_specs=[pl.BlockSpec((SSM_GROUPS, tm // S5_CHUNK, cw), lambda i: (0, i, 0))],
        out_specs=pl.BlockSpec((tm, SSM_WIDTH), lambda i: (i, 0)),
        out_shape=jax.ShapeDtypeStruct((m, SSM_WIDTH), F32),
        scratch_shapes=[pltpu.VMEM((-(-SSM_WIDTH // LANES), tm, LANES), F32)],
        compiler_params=_cparams(("parallel",)),
    )(y)


def _s5_prompt_kernel(u_ref, mt_ref, r_re_ref, r_im_ref, cl_re_ref, cl_im_ref, lam_re_ref, lam_im_ref, d_ref,
                      y_ref, xre_ref, xim_ref, *, nseq, nchunk):
    u = u_ref[0]
    ub = u.astype(BF16)
    y = _dot(ub, mt_ref[0].astype(BF16))
    xr = _dot(ub, r_re_ref[0].astype(BF16))
    xi = _dot(ub, r_im_ref[0].astype(BF16))
    rows = nseq * nchunk
    cidx = lax.broadcasted_iota(jnp.int32, (rows, SSM_STATE), 0) % nchunk
    lam_re, lam_im = lam_re_ref[0], lam_im_ref[0]
    for k in range(int(math.log2(nchunk))):
        sh = 1 << k
        lr = lam_re[S5_NPOW + k:S5_NPOW + k + 1]
        li = lam_im[S5_NPOW + k:S5_NPOW + k + 1]
        ok = cidx >= sh
        pr = jnp.where(ok, pltpu.roll(xr, sh, axis=0), 0.0)
        pi = jnp.where(ok, pltpu.roll(xi, sh, axis=0), 0.0)
        xr, xi = xr + lr * pr - li * pi, xi + lr * pi + li * pr
    ok = cidx >= 1
    er = jnp.where(ok, pltpu.roll(xr, 1, axis=0), 0.0)
    ei = jnp.where(ok, pltpu.roll(xi, 1, axis=0), 0.0)
    cl_re = cl_re_ref[0][SSM_GROUP:, :]
    cl_im = cl_im_ref[0][SSM_GROUP:, :]
    y = y + _dot_nt(er.astype(BF16), cl_re.astype(BF16)) - _dot_nt(ei.astype(BF16), cl_im.astype(BF16))
    y_ref[0] = y + d_ref[0] * u
    last = [s * nchunk + nchunk - 1 for s in range(nseq)]
    xre_ref[0] = jnp.concatenate([xr[i:i + 1] for i in last], axis=0)
    xim_ref[0] = jnp.concatenate([xi[i:i + 1] for i in last], axis=0)


def _s5_prompt(u, mt, r_re, r_im, cl_re, cl_im, lam_re, lam_im, d_t, nseq, nchunk):
    g = SSM_GROUPS
    rows = nseq * nchunk
    cw = S5_CHUNK * SSM_GROUP
    row = lambda i: (i, 0, 0)
    full = lambda a: pl.BlockSpec((1,) + a.shape[1:], row)
    return pl.pallas_call(
        functools.partial(_s5_prompt_kernel, nseq=nseq, nchunk=nchunk),
        grid=(g,),
        in_specs=[full(u), full(mt), full(r_re), full(r_im), full(cl_re), full(cl_im), full(lam_re), full(lam_im),
                  full(d_t)],
        out_specs=[pl.BlockSpec((1, rows, cw), row), pl.BlockSpec((1, nseq, SSM_STATE), row),
                   pl.BlockSpec((1, nseq, SSM_STATE), row)],
        out_shape=[jax.ShapeDtypeStruct((g, rows, cw), F32), jax.ShapeDtypeStruct((g, nseq, SSM_STATE), F32),
                   jax.ShapeDtypeStruct((g, nseq, SSM_STATE), F32)],
        compiler_params=_cparams(("parallel",)),
    )(u, mt, r_re, r_im, cl_re, cl_im, lam_re, lam_im, d_t)


def _s5_sample_kernel(u_ref, x0r_ref, x0i_ref, r_re_ref, r_im_ref, cl_re_ref, cl_im_ref, lam_re_ref, lam_im_ref,
                      d_ref, y_ref, xre_ref, xim_ref):
    u = u_ref[0]
    ub = u.astype(BF16)
    lo = (S5_CHUNK - 1) * SSM_GROUP
    bb_re = r_re_ref[0][lo:lo + SSM_GROUP, :]
    bb_im = r_im_ref[0][lo:lo + SSM_GROUP, :]
    lr, li = lam_re_ref[0][1:2], lam_im_ref[0][1:2]
    x0r, x0i = x0r_ref[0], x0i_ref[0]
    xr = _dot(ub, bb_re.astype(BF16)) + (lr * x0r - li * x0i)
    xi = _dot(ub, bb_im.astype(BF16)) + (lr * x0i + li * x0r)
    c_re = cl_re_ref[0][:SSM_GROUP, :]
    c_im = cl_im_ref[0][:SSM_GROUP, :]
    y_ref[0] = (_dot_nt(xr.astype(BF16), c_re.astype(BF16)) - _dot_nt(xi.astype(BF16), c_im.astype(BF16))
                + d_ref[0] * u)
    xre_ref[0] = xr
    xim_ref[0] = xi


def _s5_sample(u, x0r, x0i, r_re, r_im, cl_re, cl_im, lam_re, lam_im, d):
    g, n = u.shape[0], u.shape[1]
    row = lambda i: (i, 0, 0)
    full = lambda a: pl.BlockSpec((1,) + a.shape[1:], row)
    return pl.pallas_call(
        _s5_sample_kernel,
        grid=(g,),
        in_specs=[full(u), full(x0r), full(x0i), full(r_re), full(r_im), full(cl_re), full(cl_im), full(lam_re),
                  full(lam_im), full(d)],
        out_specs=[pl.BlockSpec((1, n, SSM_GROUP), row), pl.BlockSpec((1, n, SSM_STATE), row),
                   pl.BlockSpec((1, n, SSM_STATE), row)],
        out_shape=[jax.ShapeDtypeStruct((g, n, SSM_GROUP), F32), jax.ShapeDtypeStruct((g, n, SSM_STATE), F32),
                   jax.ShapeDtypeStruct((g, n, SSM_STATE), F32)],
        compiler_params=_cparams(("parallel",)),
    )(u, x0r, x0i, r_re, r_im, cl_re, cl_im, lam_re, lam_im, d)


def _log_sigmoid(x):
    return jnp.minimum(x, 0.0) - jnp.log1p(jnp.exp(-jnp.abs(x)))


def _logaddexp(a, b):
    return jnp.maximum(a, b) + jnp.log1p(jnp.exp(-jnp.abs(a - b)))


def _hgrn_gates(fpre, lb):
    log_f = _logaddexp(jnp.log(jnp.maximum(lb, LB_FLOOR)), jnp.log1p(-lb) + _log_sigmoid(fpre))
    k = (1.0 - lb) * _sigmoid(-fpre)
    return log_f, k


def _hgrn_scan_matrix():
    c = HGRN_CHUNK
    tri = np.tril(np.ones((c, c), np.float32))
    t = np.arange(c)
    mats = [tri]
    for m in HGRN_LEVELS:
        mats.append(tri[(t // (2 * m)) * (2 * m) + m - 1])
    return np.concatenate(mats, axis=0)


def _hgrn_prompt_kernel(p_ref, lbraw_ref, gn_ref, scan_ref, o_ref, s_ref, st_ref, *, layer):
    cidx = pl.program_id(1)
    c = HGRN_CHUNK
    hd = HGRN_HEAD_DIM

    @pl.when(cidx == 0)
    def _():
        st_ref[...] = jnp.zeros_like(st_ref)

    a = lbraw_ref[...]
    e = jnp.exp(a - jnp.max(a, axis=0, keepdims=True))
    p = e / jnp.sum(e, axis=0, keepdims=True)
    run = p[0:1]
    for l in range(1, layer + 1):
        run = run + p[l:l + 1]
    lb_all = run - p[0:1]

    r = lax.broadcasted_iota(jnp.int32, (c, c), 0)
    s = lax.broadcasted_iota(jnp.int32, (c, c), 1)
    pairs = []
    for m in HGRN_LEVELS:
        sh = int(math.log2(2 * m))
        pairs.append(((r >> sh) == (s >> sh)) & ((r & m) != 0) & ((s & m) == 0))
    scan = scan_ref[...]
    for h in range(HGRN_HEADS):
        col = lambda part: p_ref[0, :, part * HGRN_WIDTH + h * hd:part * HGRN_WIDTH + (h + 1) * hd]
        log_f, k = _hgrn_gates(col(1), lb_all[:, h * hd:(h + 1) * hd])
        q = _silu(col(0))
        v = col(2)
        hi = log_f.astype(BF16)
        rest = log_f - hi.astype(F32)
        mid = rest.astype(BF16)
        lo = (rest - mid.astype(F32)).astype(BF16)
        cum3 = _dot(scan, jnp.concatenate([hi, mid, lo], axis=1))
        cum = cum3[:, 0:hd] + cum3[:, hd:2 * hd] + cum3[:, 2 * hd:3 * hd]
        b = cum[0:c]
        att = jnp.where(r == s, jnp.sum(q * k, axis=-1, keepdims=True), 0.0)
        for lvl, m in enumerate(HGRN_LEVELS):
            bref = cum[(lvl + 1) * c:(lvl + 2) * c]
            qs = q * jnp.exp(jnp.minimum(b - bref, 0.0))
            ks = k * jnp.exp(jnp.minimum(bref - b, 0.0))
            att = att + jnp.where(pairs[lvl], _dot_nt(qs.astype(BF16), ks.astype(BF16)), 0.0)
        st = st_ref[h]
        vb = v.astype(BF16)
        o = _dot(att.astype(BF16), vb) + _dot_nt((q * jnp.exp(b)).astype(BF16), st.astype(BF16))
        bl = b[c - 1:c]
        st_new = jnp.exp(bl) * st + _dot_tn(vb, (k * jnp.exp(bl - b)).astype(BF16))
        st_ref[h] = st_new
        o = o * lax.rsqrt(jnp.mean(o * o, axis=-1, keepdims=True) + NORM_EPS)
        o_ref[0, :, h * hd:(h + 1) * hd] = o * gn_ref[:, h * hd:(h + 1) * hd] * _silu(col(3))

    @pl.when(cidx == pl.num_programs(1) - 1)
    def _():
        s_ref[0] = st_ref[...]


def _hgrn_prompt(proj, lbraw, gn, layer):
    n, t, _ = proj.shape
    c = HGRN_CHUNK
    hd = HGRN_HEAD_DIM
    scan = jnp.asarray(_hgrn_scan_matrix(), dtype=BF16)
    return pl.pallas_call(
        functools.partial(_hgrn_prompt_kernel, layer=layer),
        grid=(n, t // c),
        in_specs=[pl.BlockSpec((1, c, 4 * HGRN_WIDTH), lambda b, j: (b, j, OFF_D // (4 * HGRN_WIDTH))),
                  pl.BlockSpec((DEPTH, HGRN_WIDTH), lambda b, j: (0, 0)),
                  pl.BlockSpec((None, 1, HGRN_WIDTH), lambda b, j: (layer, 0, 0)),
                  pl.BlockSpec(scan.shape, lambda b, j: (0, 0))],
        out_specs=[pl.BlockSpec((1, c, HGRN_WIDTH), lambda b, j: (b, j, 0)),
                   pl.BlockSpec((1, HGRN_HEADS, hd, hd), lambda b, j: (b, 0, 0, 0))],
        out_shape=[jax.ShapeDtypeStruct((n, t, HGRN_WIDTH), F32),
                   jax.ShapeDtypeStruct((n, HGRN_HEADS, hd, hd), F32)],
        scratch_shapes=[pltpu.VMEM((HGRN_HEADS, hd, hd), F32)],
        compiler_params=_cparams(("parallel", "arbitrary")),
    )(proj, lbraw, _v3(gn), scan)


def _hgrn_sample_kernel(qt_ref, ft_ref, lbt_ref, v_ref, g_ref, gn_ref, s0_ref, o_ref, s_ref, *, layer, n):
    hd = HGRN_HEAD_DIM
    a = lbt_ref[0]
    e = jnp.exp(a - jnp.max(a, axis=1, keepdims=True))
    p = e / jnp.sum(e, axis=1, keepdims=True)
    run = p[:, 0:1]
    for l in range(1, layer + 1):
        run = run + p[:, l:l + 1]
    lb = run - p[:, 0:1]
    log_f, k = _hgrn_gates(ft_ref[0], lb)
    f = jnp.exp(log_f)
    q = _silu(qt_ref[0])
    v = v_ref[...]
    gate = g_ref[...]
    rows = []
    for b in range(n):
        s_new = f[:, b:b + 1] * s0_ref[b, 0] + k[:, b:b + 1] * v[b:b + 1, :]
        s_ref[b, 0] = s_new
        rows.append(jnp.sum(q[:, b:b + 1] * s_new, axis=0, keepdims=True))
    o = jnp.concatenate(rows, axis=0)
    o = o * lax.rsqrt(jnp.mean(o * o, axis=-1, keepdims=True) + NORM_EPS)
    o_ref[...] = o * gn_ref[...] * _silu(gate)


def _hgrn_sample(qt, ft, lbt, proj, gn, s0, layer):
    n = proj.shape[0]
    hd = HGRN_HEAD_DIM
    v_blk = (OFF_D + 2 * HGRN_WIDTH) // hd
    g_blk = (OFF_D + 3 * HGRN_WIDTH) // hd
    return pl.pallas_call(
        functools.partial(_hgrn_sample_kernel, layer=layer, n=n),
        grid=(HGRN_HEADS,),
        in_specs=[pl.BlockSpec((1, hd, n), lambda h: (h, 0, 0)),
                  pl.BlockSpec((1, hd, n), lambda h: (h, 0, 0)),
                  pl.BlockSpec((1, hd, DEPTH), lambda h: (h, 0, 0)),
                  pl.BlockSpec((n, hd), lambda h: (0, v_blk + h)),
                  pl.BlockSpec((n, hd), lambda h: (0, g_blk + h)),
                  pl.BlockSpec((None, 1, hd), lambda h: (layer, 0, h)),
                  pl.BlockSpec((None, n, 1, hd, hd), lambda h: (layer, 0, h, 0, 0))],
        out_specs=[pl.BlockSpec((n, hd), lambda h: (0, h)),
                   pl.BlockSpec((n, 1, hd, hd), lambda h: (0, h, 0, 0))],
        out_shape=[jax.ShapeDtypeStruct((n, HGRN_WIDTH), F32),
                   jax.ShapeDtypeStruct((n, HGRN_HEADS, hd, hd), F32)],
        compiler_params=_cparams(("parallel",)),
    )(qt, ft, lbt, proj, proj, _v3(gn), s0)


def _gelu_tanh(x):
    return 0.5 * x * (1.0 + jnp.tanh(math.sqrt(2.0 / math.pi) * (x + 0.044715 * (x * x * x))))


def _mixout_kernel(h_ref, ya_ref, ob_ref, oc0_ref, oc1_ref, oc2_ref, od_ref,
                   wglu_ref, bglu_ref, ga_ref, gb_ref, gc_ref, wo_ref, gpost_ref, o_ref):
    z = _gelu_tanh(ya_ref[...])
    gate = _sigmoid(_dot(z.astype(BF16), wglu_ref[...].astype(BF16)) + bglu_ref[...])
    out_a = _rms(z * gate, ga_ref[...])
    out_b = _rms(ob_ref[...], gb_ref[...])
    ocs = [oc0_ref, oc1_ref, oc2_ref]
    ls = [r[:, DIL_GW:2 * DIL_GW] for r in ocs]
    m = jnp.maximum(jnp.maximum(ls[0], ls[1]), ls[2])
    es = [jnp.exp(l - m) for l in ls]
    den = es[0] + es[1] + es[2]
    cs = [ocs[gi][:, 0:DIL_GW] * (es[gi] / den) for gi in range(3)]
    ss = sum(jnp.sum(c * c, axis=-1, keepdims=True) for c in cs)
    inv = lax.rsqrt(ss / DIL_WIDTH + NORM_EPS)
    gc = gc_ref[...]
    o1, o2, o3 = SSM_WIDTH, SSM_WIDTH + SWA_WIDTH, SSM_WIDTH + SWA_WIDTH + DIL_WIDTH
    wo = lambda lo, hi: wo_ref[lo:hi, :]
    mix = _dot(out_a.astype(BF16), wo(0, o1)) + _dot(out_b.astype(BF16), wo(o1, o2))
    for gi in range(3):
        c = cs[gi] * inv * gc[:, gi * DIL_GW:(gi + 1) * DIL_GW]
        mix = mix + _dot(c.astype(BF16), wo(o2 + gi * DIL_GW, o2 + (gi + 1) * DIL_GW))
    mix = mix + _dot(od_ref[...].astype(BF16), wo(o3, D_MODEL))
    o_ref[...] = h_ref[...] + _rms(mix, gpost_ref[...])


def _mixout(h, ya, ob, ocs, od, lp_all, w_out_l, layer, tm):
    m = h.shape[0]
    rowblk = lambda a: pl.BlockSpec((tm, a.shape[1]), lambda i: (i, 0))
    vec = lambda a: pl.BlockSpec((None, 1, a.shape[1]), lambda i: (layer, 0, 0))
    mat = lambda a: pl.BlockSpec((None,) + a.shape[1:], lambda i: (layer, 0, 0))
    whole = lambda a: pl.BlockSpec(a.shape, lambda i: (0, 0), pipeline_mode=pl.Buffered(1))
    acts = [h, ya, ob, *ocs, od]
    names = ['ssm_w_glu', 'ssm_b_glu', 'out_norm_a', 'out_norm_b', 'out_norm_c', 'w_out', 'mix_norm_post']
    params = [w_out_l if k == 'w_out' else lp_all[k] for k in names]
    spec = lambda k, p: whole(p) if k == 'w_out' else (mat(p) if p.ndim == 3 else vec(p))
    return pl.pallas_call(
        _mixout_kernel,
        grid=(m // tm,),
        in_specs=[rowblk(a) for a in acts] + [spec(k, p) for k, p in zip(names, params)],
        out_specs=pl.BlockSpec((tm, D_MODEL), lambda i: (i, 0)),
        out_shape=jax.ShapeDtypeStruct((m, D_MODEL), F32),
        compiler_params=_cparams(("parallel",)),
    )(*acts, *[_v3(p) if p.ndim == 2 and k != 'w_out' else p for k, p in zip(names, params)])


def _ffn_block(h, P, name, layer, tm):
    return _ffn(h, P[name + '_norm_pre'], P[name + '_w_gate'], P[name + '_w_up'], P[name + '_w_down'],
                P[name + '_norm_post'], layer, tm)


def _prompt_mixer(h, P, w_in_l, w_out_l, s5p, layer, n, t, rope, tm):
    m = n * t
    proj2 = _win(h, P['mix_norm_pre'], w_in_l, rope[0], rope[1], layer, tm)
    proj = proj2.reshape(n, t, PROJ_COLS)

    lam_re, lam_im, cl_re, cl_im, r_re, r_im, kmat = s5p
    nchunk = t // S5_CHUNK
    u = _s5_chunk_in(proj2, tm)
    d_t = jnp.tile(P['ssm_d'][layer], (1, S5_CHUNK)).reshape(SSM_GROUPS, 1, S5_CHUNK * SSM_GROUP)
    y, xre, xim = _s5_prompt(u, kmat, r_re, r_im, cl_re, cl_im, lam_re, lam_im, d_t, n, nchunk)
    ya = _s5_chunk_out(y, tm)
    ssm_new = jnp.stack([xre, xim], axis=-1).transpose(1, 0, 2, 3)

    sink = jnp.repeat(P['swa_sinks'][layer], HEAD_DIM).reshape(1, SWA_WIDTH)
    ob = _swa_prompt(proj, sink).reshape(m, SWA_WIDTH)
    keep = min(WINDOW, t)
    kv = proj[:, t - keep:, OFF_B + SWA_WIDTH:OFF_B + B_SLOT]
    swa_new = kv.reshape(n, keep, 2, SWA_KV_HEADS, HEAD_DIM)

    ocs, dil_new = [], []
    for gi, (win, dil) in enumerate(DIL_PAIRS):
        ocs.append(_dil_prompt(proj, gi, dil).reshape(m, 2 * DIL_GW))
        keep = min(win, t)
        lo = OFF_C + gi * C_SLOT + C_K
        dil_new.append(proj[:, t - keep:, lo:lo + 2 * HEAD_DIM].reshape(n, keep, 2, HEAD_DIM))

    od, st = _hgrn_prompt(proj, P['hgrn_lower_bounds'], P['out_norm_d'], layer)
    hgrn_new = jnp.swapaxes(st, -1, -2)

    h = _mixout(h, ya, ob, ocs, od.reshape(m, HGRN_WIDTH), P, w_out_l, layer, min(tm, 512))
    return h, (ssm_new, swa_new, dil_new[0], dil_new[1], dil_new[2], hgrn_new)


def _sample_mixer(h, P, w_in_l, w_out_l, s5p, layer, caches, rope):
    n = h.shape[0]
    state_ssm, cache_swa, cache_d0, cache_d1, cache_d2, state_hgrn = caches
    proj = _win(h, P['mix_norm_pre'], w_in_l, rope[0], rope[1], layer, n)

    lam_re, lam_im, cl_re, cl_im, r_re, r_im, _ = s5p
    u = proj[:, OFF_U:OFF_U + SSM_WIDTH].reshape(n, SSM_GROUPS, SSM_GROUP).transpose(1, 0, 2)
    x0 = state_ssm[layer].transpose(1, 0, 2, 3)
    y, xre, xim = _s5_sample(u, x0[..., 0], x0[..., 1], r_re, r_im, cl_re, cl_im, lam_re, lam_im,
                             P['ssm_d'][layer].reshape(SSM_GROUPS, 1, SSM_GROUP))
    ya = y.transpose(1, 0, 2).reshape(n, SSM_WIDTH)
    ssm_new = jnp.stack([xre, xim], axis=-1).transpose(1, 0, 2, 3)

    sink = jnp.repeat(P['swa_sinks'][layer], HEAD_DIM).reshape(1, SWA_WIDTH)
    ob = _swa_sample(proj, cache_swa.transpose(0, 1, 3, 4, 5, 2), sink, layer)
    swa_new = proj[:, OFF_B + SWA_WIDTH:OFF_B + B_SLOT].reshape(n, 1, 2, SWA_KV_HEADS, HEAD_DIM)

    ocs, dil_new = [], []
    for gi, (buf, (win, dil)) in enumerate(zip((cache_d0, cache_d1, cache_d2), DIL_PAIRS)):
        ocs.append(_dil_sample(proj, buf.transpose(0, 1, 3, 4, 2), gi, dil, layer))
        lo = OFF_C + gi * C_SLOT + C_K
        dil_new.append(proj[:, lo:lo + 2 * HEAD_DIM].reshape(n, 1, 2, HEAD_DIM))

    part = lambda i: proj[:, OFF_D + i * HGRN_WIDTH:OFF_D + (i + 1) * HGRN_WIDTH].reshape(n, HGRN_HEADS, HGRN_HEAD_DIM)
    cm = lambda a: a.transpose(1, 2, 0)
    lbt = P['hgrn_lower_bounds'].reshape(DEPTH, HGRN_HEADS, HGRN_HEAD_DIM).transpose(1, 2, 0)
    od, hgrn_new = _hgrn_sample(cm(part(0)), cm(part(1)), lbt, proj, P['out_norm_d'], state_hgrn, layer)

    h = _mixout(h, ya, ob, ocs, od, P, w_out_l, layer, n)
    return h, (ssm_new, swa_new, dil_new[0], dil_new[1], dil_new[2], hgrn_new)


def _forward(x_prompt, x_sample, caches, P):
    n, t, _ = x_prompt.shape
    ns, ts, _ = x_sample.shape
    assert ts == 1
    tm = 1024
    rope_p = _rope_tables(jnp.tile(jnp.arange(t, dtype=jnp.int32), n))
    rope_s = _rope_tables(jnp.full((ns,), PAST_LEN, dtype=jnp.int32))
    nsteps = int(math.log2(t // S5_CHUNK))
    h_p = x_prompt.reshape(n * t, D_MODEL)
    h_s = x_sample.reshape(ns, D_MODEL)
    new_p = [[] for _ in range(6)]
    new_s = [[] for _ in range(6)]
    for l in range(DEPTH):
        w_in_l = _win_prep(P['w_in'], l)
        w_out_l = _cast_bf16(P['w_out'], l)
        s5p = _s5_params(P['ssm_a_re'][l], P['ssm_a_im'][l], P['ssm_log_dt'][l], P['ssm_b_re'][l], P['ssm_b_im'][l],
                         P['ssm_c_re'][l], P['ssm_c_im'][l], nsteps)
        h_p = _ffn_block(h_p, P, 'ffn1', l, tm)
        h_p, st_p = _prompt_mixer(h_p, P, w_in_l, w_out_l, s5p, l, n, t, rope_p, tm)
        h_p = _ffn_block(h_p, P, 'ffn2', l, tm)
        h_s = _ffn_block(h_s, P, 'ffn1', l, ns)
        h_s, st_s = _sample_mixer(h_s, P, w_in_l, w_out_l, s5p, l, caches, rope_s)
        h_s = _ffn_block(h_s, P, 'ffn2', l, ns)
        for i in range(6):
            new_p[i].append(st_p[i])
            new_s[i].append(st_s[i])
    new_p = [jnp.stack(a, axis=0) for a in new_p]
    new_s = [jnp.stack(a, axis=0) for a in new_s]
    return (h_p.reshape(n, t, D_MODEL), h_s.reshape(ns, ts, D_MODEL), new_p[0], new_s[0], new_p[1], new_s[1],
            new_p[2], new_s[2], new_p[3], new_s[3], new_p[4], new_s[4], new_p[5], new_s[5])


def kernel(x_prompt, x_sample, state_ssm, cache_swa_kv, cache_dil0_kv, cache_dil1_kv, cache_dil2_kv, state_hgrn, ffn1_norm_pre, ffn1_w_gate, ffn1_w_up, ffn1_w_down, ffn1_norm_post, mix_norm_pre, w_in, ssm_a_re, ssm_a_im, ssm_log_dt, ssm_b_re, ssm_b_im, ssm_c_re, ssm_c_im, ssm_d, ssm_w_glu, ssm_b_glu, swa_sinks, hgrn_lower_bounds, out_norm_a, out_norm_b, out_norm_c, out_norm_d, w_out, mix_norm_post, ffn2_norm_pre, ffn2_w_gate, ffn2_w_up, ffn2_w_down, ffn2_norm_post):
    P = dict(ffn1_norm_pre=ffn1_norm_pre, ffn1_w_gate=ffn1_w_gate, ffn1_w_up=ffn1_w_up, ffn1_w_down=ffn1_w_down,
             ffn1_norm_post=ffn1_norm_post, mix_norm_pre=mix_norm_pre, w_in=w_in, ssm_a_re=ssm_a_re,
             ssm_a_im=ssm_a_im, ssm_log_dt=ssm_log_dt, ssm_b_re=ssm_b_re, ssm_b_im=ssm_b_im, ssm_c_re=ssm_c_re,
             ssm_c_im=ssm_c_im, ssm_d=ssm_d, ssm_w_glu=ssm_w_glu, ssm_b_glu=ssm_b_glu, swa_sinks=swa_sinks,
             hgrn_lower_bounds=hgrn_lower_bounds, out_norm_a=out_norm_a, out_norm_b=out_norm_b,
             out_norm_c=out_norm_c, out_norm_d=out_norm_d, w_out=w_out, mix_norm_post=mix_norm_post,
             ffn2_norm_pre=ffn2_norm_pre, ffn2_w_gate=ffn2_w_gate, ffn2_w_up=ffn2_w_up, ffn2_w_down=ffn2_w_down,
             ffn2_norm_post=ffn2_norm_post)
    caches = (state_ssm, cache_swa_kv, cache_dil0_kv, cache_dil1_kv, cache_dil2_kv, state_hgrn)
    return _forward(x_prompt, x_sample, caches, P)
```

```python
import functools
import math

import numpy as np
import jax
import jax.numpy as jnp
from jax import lax
from jax.experimental import pallas as pl
from jax.experimental.pallas import tpu as pltpu

F32 = jnp.float32
BF16 = jnp.bfloat16

D_MODEL = 2048
DEPTH = 2
PAST_LEN = 16384
HEAD_DIM = 64
ROPE_THETA = 10000.0
D_FF = 5504
NORM_EPS = 1e-6
NEG_BIG = -1e30
LB_FLOOR = 1e-30
SSM_GROUP = 16
SSM_STATE = 64
SSM_WIDTH = 448
SSM_GROUPS = 28
SWA_HEADS = 8
SWA_KV_HEADS = 2
SWA_GROUP = 4
SWA_WIDTH = 512
SWA_KV_WIDTH = 128
WINDOW = 128
DIL_PAIRS = ((128, 1), (512, 4), (2048, 16))
DIL_HPG = 3
DIL_GW = DIL_HPG * HEAD_DIM
DIL_WIDTH = 576
DIL_KV_WIDTH = 192
HGRN_HEAD_DIM = 128
HGRN_WIDTH = 512
HGRN_HEADS = 4
IN_COLS = 4224
ATTN_SCALE = HEAD_DIM ** -0.5

LANES = 128
FF_TILE = 256
N_FF_TILES = -(-D_FF // FF_TILE)
DOWN_TILE = 256
N_DOWN_TILES = D_MODEL // DOWN_TILE
VMEM_LIMIT = 60 * 1024 * 1024

B_SLOT = SWA_WIDTH + 2 * SWA_KV_WIDTH
C_SLOT = 384
C_K = 256
OFF_B = 0
OFF_C = OFF_B + B_SLOT
OFF_D = 2048
OFF_U = OFF_D + 4 * HGRN_WIDTH
U_SLOT = 512
PROJ_COLS = 4608
WIN_TILE = 1536

S5_CHUNK = 16
S5_NPOW = S5_CHUNK + 1
HGRN_CHUNK = 128
HGRN_LEVELS = (64, 32, 16, 8, 4, 2, 1)


def _proj_segments():
    src = np.cumsum([0, SSM_WIDTH, SWA_WIDTH, SWA_KV_WIDTH, SWA_KV_WIDTH, DIL_WIDTH, DIL_KV_WIDTH, DIL_KV_WIDTH,
                     HGRN_WIDTH, HGRN_WIDTH, HGRN_WIDTH, HGRN_WIDTH])
    s_u, s_qb, s_kb, s_vb, s_qc, s_kc, s_vc, s_qd = src[:8]
    segs = [(OFF_B, s_qb, SWA_WIDTH), (OFF_B + SWA_WIDTH, s_kb, SWA_KV_WIDTH),
            (OFF_B + SWA_WIDTH + SWA_KV_WIDTH, s_vb, SWA_KV_WIDTH)]
    for g in range(3):
        base = OFF_C + g * C_SLOT
        segs += [(base, s_qc + g * DIL_GW, DIL_GW), (base + C_K, s_kc + g * HEAD_DIM, HEAD_DIM),
                 (base + C_K + HEAD_DIM, s_vc + g * HEAD_DIM, HEAD_DIM)]
    segs += [(OFF_D, s_qd, 4 * HGRN_WIDTH), (OFF_U, s_u, SSM_WIDTH)]
    return [(int(a), int(b), int(c)) for a, b, c in segs]


def _rotary_lane_mask():
    m = np.zeros((PROJ_COLS,), np.float32)
    m[OFF_B:OFF_B + SWA_WIDTH + SWA_KV_WIDTH] = 1.0
    for g in range(3):
        base = OFF_C + g * C_SLOT
        m[base:base + DIL_GW] = 1.0
        m[base + C_K:base + C_K + HEAD_DIM] = 1.0
    return m


def _cparams(sem):
    return pltpu.CompilerParams(dimension_semantics=sem, vmem_limit_bytes=VMEM_LIMIT)


def _v3(a):
    return a.reshape(a.shape[0], 1, a.shape[1])


def _rms(x, g):
    return x * lax.rsqrt(jnp.mean(x * x, axis=-1, keepdims=True) + NORM_EPS) * g


def _sigmoid(x):
    return 1.0 / (1.0 + jnp.exp(-x))


def _silu(x):
    return x * _sigmoid(x)


def _dot(a, b):
    return jnp.dot(a, b, preferred_element_type=F32)


def _dot_nt(a, b):
    return lax.dot_general(a, b, (((1,), (1,)), ((), ())), preferred_element_type=F32)


def _dot_tn(a, b):
    return lax.dot_general(a, b, (((0,), (0,)), ((), ())), preferred_element_type=F32)


def _ffn_kernel(x_ref, gpre_ref, wg_ref, wu_ref, wd_ref, gpost_ref, o_ref, xn_ref, hid_ref):
    s = pl.program_id(1)

    @pl.when(s == 0)
    def _():
        xn_ref[...] = _rms(x_ref[...], gpre_ref[...]).astype(BF16)

    @pl.when(s < N_FF_TILES)
    def _():
        xn = xn_ref[...]
        gate = _dot(xn, wg_ref[...].astype(BF16))
        up = _dot(xn, wu_ref[...].astype(BF16))
        hid_ref[s] = (_silu(gate) * up).astype(BF16)

    for j in range(N_DOWN_TILES):

        @pl.when(s == N_FF_TILES + j)
        def _(j=j):
            hid = jnp.concatenate([hid_ref[f] for f in range(N_FF_TILES)], axis=1)[:, :D_FF]
            o_ref[:, j * DOWN_TILE:(j + 1) * DOWN_TILE] = _dot(hid, wd_ref[...].astype(BF16))

    @pl.when(s == pl.num_programs(1) - 1)
    def _():
        o_ref[...] = x_ref[...] + 0.5 * _rms(o_ref[...], gpost_ref[...])


def _ffn(h, gpre, wg, wu, wd, gpost, layer, tm):
    m = h.shape[0]
    last_ff = N_FF_TILES - 1
    return pl.pallas_call(
        _ffn_kernel,
        grid=(m // tm, N_FF_TILES + N_DOWN_TILES),
        in_specs=[
            pl.BlockSpec((tm, D_MODEL), lambda i, s: (i, 0)),
            pl.BlockSpec((None, 1, D_MODEL), lambda i, s: (layer, 0, 0)),
            pl.BlockSpec((None, D_MODEL, FF_TILE), lambda i, s: (layer, 0, jnp.minimum(s, last_ff))),
            pl.BlockSpec((None, D_MODEL, FF_TILE), lambda i, s: (layer, 0, jnp.minimum(s, last_ff))),
            pl.BlockSpec((None, D_FF, DOWN_TILE), lambda i, s: (layer, 0, jnp.maximum(s - N_FF_TILES, 0))),
            pl.BlockSpec((None, 1, D_MODEL), lambda i, s: (layer, 0, 0)),
        ],
        out_specs=pl.BlockSpec((tm, D_MODEL), lambda i, s: (i, 0), pipeline_mode=pl.Buffered(1)),
        out_shape=jax.ShapeDtypeStruct((m, D_MODEL), F32),
        scratch_shapes=[pltpu.VMEM((tm, D_MODEL), BF16), pltpu.VMEM((N_FF_TILES, tm, FF_TILE), BF16)],
        compiler_params=_cparams(("parallel", "arbitrary")),
    )(h, _v3(gpre), wg, wu, wd, _v3(gpost))


def _win_prep_kernel(w_ref, o_ref):
    o_ref[...] = jnp.zeros_like(o_ref)
    for dst, src, width in _proj_segments():
        o_ref[:, dst:dst + width] = w_ref[:, src:src + width].astype(BF16)


def _win_prep(w_in, layer):
    rows = 256
    return pl.pallas_call(
        _win_prep_kernel,
        grid=(D_MODEL // rows,),
        in_specs=[pl.BlockSpec((None, rows, IN_COLS), lambda i: (layer, i, 0))],
        out_specs=pl.BlockSpec((rows, PROJ_COLS), lambda i: (i, 0)),
        out_shape=jax.ShapeDtypeStruct((D_MODEL, PROJ_COLS), BF16),
        compiler_params=_cparams(("parallel",)),
    )(w_in)


def _cast_kernel(w_ref, o_ref):
    o_ref[...] = w_ref[...].astype(BF16)


def _cast_bf16(w, layer):
    _, r, c = w.shape
    rows = 256
    return pl.pallas_call(
        _cast_kernel,
        grid=(r // rows,),
        in_specs=[pl.BlockSpec((None, rows, c), lambda i: (layer, i, 0))],
        out_specs=pl.BlockSpec((rows, c), lambda i: (i, 0)),
        out_shape=jax.ShapeDtypeStruct((r, c), BF16),
        compiler_params=_cparams(("parallel",)),
    )(w)


def _win_kernel(x_ref, g_ref, w_ref, cos_ref, sin_ref, o_ref, *rest, chunks):
    if chunks:
        u_ref, xn_ref, st_ref = rest
    else:
        (xn_ref,) = rest
    j = pl.program_id(1)

    @pl.when(j == 0)
    def _():
        xn_ref[...] = _rms(x_ref[...], g_ref[...]).astype(BF16)

    rot = _rotary_lane_mask()
    for tile in range(PROJ_COLS // WIN_TILE):

        @pl.when(j == tile)
        def _(tile=tile):
            y = _dot(xn_ref[...], w_ref[...])
            cos = cos_ref[...]
            sin = sin_ref[...]
            lane = lax.broadcasted_iota(jnp.int32, cos.shape, 1)
            first_half = (lane % HEAD_DIM) < (HEAD_DIM // 2)
            for c in range(WIN_TILE // LANES):
                sl = slice(c * LANES, (c + 1) * LANES)
                flags = rot[tile * WIN_TILE + c * LANES:tile * WIN_TILE + (c + 1) * LANES]
                x = y[:, sl]
                lo, hi = bool(flags[:HEAD_DIM].all()), bool(flags[HEAD_DIM:].all())
                if not (lo or hi):
                    o_ref[:, sl] = x
                    continue
                partner = jnp.where(first_half,
                                    pltpu.roll(x, LANES - HEAD_DIM // 2, axis=1),
                                    pltpu.roll(x, HEAD_DIM // 2, axis=1))
                rotated = x * cos + partner * sin
                if lo and hi:
                    o_ref[:, sl] = rotated
                else:
                    is_rot = (lane < HEAD_DIM) if lo else (lane >= HEAD_DIM)
                    o_ref[:, sl] = jnp.where(is_rot, rotated, x)
            if chunks and tile == OFF_U // WIN_TILE:
                base = OFF_U % WIN_TILE
                for lb in range(U_SLOT // LANES):
                    st_ref[lb] = y[:, base + lb * LANES:base + (lb + 1) * LANES]
                _s5_rows_to_chunks(st_ref, u_ref)


def _win(h, g, w, cos_t, sin_t, layer, tm, chunks):
    m = h.shape[0]
    cw = S5_CHUNK * SSM_GROUP
    out_specs = [pl.BlockSpec((tm, WIN_TILE), lambda i, j: (i, j))]
    out_shape = [jax.ShapeDtypeStruct((m, PROJ_COLS), F32)]
    scratch = [pltpu.VMEM((tm, D_MODEL), BF16)]
    if chunks:
        out_specs.append(pl.BlockSpec((SSM_GROUPS, tm // S5_CHUNK, cw), lambda i, j: (0, i, 0)))
        out_shape.append(jax.ShapeDtypeStruct((SSM_GROUPS, m // S5_CHUNK, cw), F32))
        scratch.append(pltpu.VMEM((U_SLOT // LANES, tm, LANES), F32))
    return pl.pallas_call(
        functools.partial(_win_kernel, chunks=chunks),
        grid=(m // tm, PROJ_COLS // WIN_TILE),
        in_specs=[
            pl.BlockSpec((tm, D_MODEL), lambda i, j: (i, 0)),
            pl.BlockSpec((None, 1, D_MODEL), lambda i, j: (layer, 0, 0)),
            pl.BlockSpec((D_MODEL, WIN_TILE), lambda i, j: (0, j)),
            pl.BlockSpec((tm, LANES), lambda i, j: (i, 0)),
            pl.BlockSpec((tm, LANES), lambda i, j: (i, 0)),
        ],
        out_specs=out_specs,
        out_shape=out_shape,
        scratch_shapes=scratch,
        compiler_params=_cparams(("parallel", "arbitrary")),
    )(h, _v3(g), w, cos_t, sin_t)


def _rope_tables(pos):
    half = HEAD_DIM // 2
    inv_freq = ROPE_THETA ** (-jnp.arange(half, dtype=F32) / half)
    ang = pos.astype(F32)[:, None] * inv_freq[None, :]
    cos, sin = jnp.cos(ang), jnp.sin(ang)
    reps = LANES // HEAD_DIM
    cos_t = jnp.tile(jnp.concatenate([cos, cos], axis=1), (1, reps))
    sin_t = jnp.tile(jnp.concatenate([-sin, sin], axis=1), (1, reps))
    return cos_t, sin_t


def _band_heads(q, kk, vv, valid, sinks):
    w = WINDOW
    heads = q.shape[1] // HEAD_DIM
    qs = jnp.concatenate([q[:, g * HEAD_DIM:(g + 1) * HEAD_DIM] for g in range(heads)], axis=0).astype(BF16)
    s = _dot_nt(kk, qs) * ATTN_SCALE
    s = jnp.where(jnp.concatenate([valid] * heads, axis=1), s, NEG_BIG)
    m = jnp.max(s, axis=0, keepdims=True)
    if sinks is not None:
        sink_row = jnp.concatenate([jnp.broadcast_to(sinks[g], (1, w)) for g in range(heads)], axis=1)
        m = jnp.maximum(m, sink_row)
    p = jnp.exp(s - m)
    l = jnp.sum(p, axis=0, keepdims=True)
    if sinks is not None:
        l = l + jnp.exp(sink_row - m)
    ot = _dot_tn(vv, p.astype(BF16)) / l
    lse = m + jnp.log(l)
    return ([ot[:, g * w:(g + 1) * w] for g in range(heads)], [lse[:, g * w:(g + 1) * w] for g in range(heads)])


def _untranspose(blocks):
    pairs = [jnp.concatenate(blocks[i:i + 2], axis=0).T for i in range(0, len(blocks), 2)]
    return jnp.concatenate(pairs, axis=1)


def _band_mask(no_prev):
    w = WINDOW
    k = lax.broadcasted_iota(jnp.int32, (2 * w, w), 0)
    q = lax.broadcasted_iota(jnp.int32, (2 * w, w), 1)
    valid = (k >= q) & (k <= q + w)
    if no_prev is False:
        return valid
    return valid & (k >= jnp.where(no_prev, w, 0))


def _swa_prompt_kernel(cur_ref, prev_ref, sink_ref, o_ref, *, nsub):
    w = WINDOW
    gw = SWA_GROUP * HEAD_DIM
    first = pl.program_id(1) == 0
    for j in range(nsub):
        valid = _band_mask(first if j == 0 else False)
        cur = cur_ref[0, j * w:(j + 1) * w, :]
        prev = prev_ref[0] if j == 0 else cur_ref[0, (j - 1) * w:j * w, SWA_WIDTH:B_SLOT]
        outs = []
        for h in range(SWA_KV_HEADS):
            ks = slice(SWA_WIDTH + h * HEAD_DIM, SWA_WIDTH + (h + 1) * HEAD_DIM)
            vs = slice(SWA_WIDTH + SWA_KV_WIDTH + h * HEAD_DIM, SWA_WIDTH + SWA_KV_WIDTH + (h + 1) * HEAD_DIM)
            kk = jnp.concatenate([prev[:, h * HEAD_DIM:(h + 1) * HEAD_DIM], cur[:, ks]], axis=0).astype(BF16)
            vv = jnp.concatenate([prev[:, SWA_KV_WIDTH + h * HEAD_DIM:SWA_KV_WIDTH + (h + 1) * HEAD_DIM],
                                  cur[:, vs]], axis=0).astype(BF16)
            sinks = [sink_ref[:, h * gw + g * HEAD_DIM:h * gw + g * HEAD_DIM + 1] for g in range(SWA_GROUP)]
            o, _ = _band_heads(cur[:, h * gw:(h + 1) * gw], kk, vv, valid, sinks)
            outs += o
        o_ref[0, j * w:(j + 1) * w, :] = _untranspose(outs)


def _swa_prompt(proj, sink):
    n, t, _ = proj.shape
    nsub = 2
    rows = WINDOW * nsub
    return pl.pallas_call(
        functools.partial(_swa_prompt_kernel, nsub=nsub),
        grid=(n, t // rows),
        in_specs=[
            pl.BlockSpec((1, rows, B_SLOT), lambda b, i: (b, i, OFF_B // B_SLOT)),
            pl.BlockSpec((1, WINDOW, 2 * SWA_KV_WIDTH),
                         lambda b, i: (b, jnp.maximum(i * nsub - 1, 0), (OFF_B + SWA_WIDTH) // (2 * SWA_KV_WIDTH))),
            pl.BlockSpec((1, SWA_WIDTH), lambda b, i: (0, 0)),
        ],
        out_specs=pl.BlockSpec((1, rows, SWA_WIDTH), lambda b, i: (b, i, 0)),
        out_shape=jax.ShapeDtypeStruct((n, t, SWA_WIDTH), F32),
        compiler_params=_cparams(("parallel", "arbitrary")),
    )(proj, proj, sink)


def _dil_prompt_kernel(cur_ref, prev_ref, o_ref, cur3_ref, out3_ref, *, dil, nsub):
    nl = C_SLOT // LANES
    span = WINDOW * dil
    first = pl.program_id(1) == 0
    for c in range(nl):
        cur3_ref[c] = cur_ref[0, :, c * LANES:(c + 1) * LANES]

    def one(j, r):
        def rows(base):
            return pl.ds(base + r, WINDOW, stride=dil) if dil > 1 else pl.ds(base, WINDOW)

        valid = _band_mask(first if j == 0 else False)
        cur = rows(j * span)
        q = jnp.concatenate([cur3_ref[0, cur, :], cur3_ref[1, cur, :]], axis=1)[:, 0:DIL_GW]
        kv = cur3_ref[2, cur, :]
        pkv = prev_ref[0, rows(0), :] if j == 0 else cur3_ref[2, rows((j - 1) * span), :]
        kk = jnp.concatenate([pkv[:, 0:HEAD_DIM], kv[:, 0:HEAD_DIM]], axis=0).astype(BF16)
        vv = jnp.concatenate([pkv[:, HEAD_DIM:], kv[:, HEAD_DIM:]], axis=0).astype(BF16)
        o, lse = _band_heads(q, kk, vv, valid, None)
        res = _untranspose(o + [jnp.broadcast_to(x, (HEAD_DIM, WINDOW)) for x in lse])
        for c in range(nl):
            out3_ref[c, cur, :] = res[:, c * LANES:(c + 1) * LANES]

    for j in range(nsub):
        if dil <= 4:
            for r in range(dil):
                one(j, r)
        else:
            lax.fori_loop(0, dil, lambda r, c, j=j: (one(j, r), c)[1], 0, unroll=2)
    for c in range(nl):
        o_ref[0, :, c * LANES:(c + 1) * LANES] = out3_ref[c]


def _dil_prompt(proj, group, dil):
    n, t, _ = proj.shape
    span = WINDOW * dil
    nsub = max(1, 4 // dil)
    rows = span * nsub
    slot = OFF_C + group * C_SLOT
    return pl.pallas_call(
        functools.partial(_dil_prompt_kernel, dil=dil, nsub=nsub),
        grid=(n, t // rows),
        in_specs=[
            pl.BlockSpec((1, rows, C_SLOT), lambda b, i: (b, i, slot // C_SLOT)),
            pl.BlockSpec((1, span, LANES), lambda b, i: (b, jnp.maximum(i * nsub - 1, 0), (slot + C_K) // LANES)),
        ],
        out_specs=pl.BlockSpec((1, rows, 2 * DIL_GW), lambda b, i: (b, i, 0)),
        out_shape=jax.ShapeDtypeStruct((n, t, 2 * DIL_GW), F32),
        scratch_shapes=[pltpu.VMEM((C_SLOT // LANES, rows, LANES), F32),
                        pltpu.VMEM((2 * DIL_GW // LANES, rows, LANES), F32)],
        compiler_params=_cparams(("parallel", "arbitrary")),
    )(proj, proj)


def _decode_heads(q, knew, vnew, kt, vt, dil, sinks):
    s = _dot(q.astype(BF16), kt.astype(BF16)) * ATTN_SCALE
    if dil > 1:
        row = lax.broadcasted_iota(jnp.int32, s.shape, 1)
        s = jnp.where(row % dil == 0, s, NEG_BIG)
    s_new = jnp.sum(q * knew, axis=-1, keepdims=True) * ATTN_SCALE
    m = jnp.maximum(jnp.max(s, axis=-1, keepdims=True), s_new)
    if sinks is not None:
        m = jnp.maximum(m, sinks)
    p = jnp.exp(s - m)
    p_new = jnp.exp(s_new - m)
    l = jnp.sum(p, axis=-1, keepdims=True) + p_new
    if sinks is not None:
        l = l + jnp.exp(sinks - m)
    o = (_dot_nt(p.astype(BF16), vt.astype(BF16)) + p_new * vnew) / l
    return o, m + jnp.log(l)


def _rows_of(x, heads):
    return jnp.concatenate([x[:, g * HEAD_DIM:(g + 1) * HEAD_DIM] for g in range(heads)], axis=0)


def _swa_sample_kernel(p_ref, cache_ref, sink_ref, o_ref, *, bn):
    for b in range(bn):
        row = p_ref[b:b + 1, :]
        outs = []
        for h in range(SWA_KV_HEADS):
            gw = SWA_GROUP * HEAD_DIM
            q = _rows_of(row[:, h * gw:(h + 1) * gw], SWA_GROUP)
            knew = row[:, SWA_WIDTH + h * HEAD_DIM:SWA_WIDTH + (h + 1) * HEAD_DIM]
            vnew = row[:, SWA_WIDTH + SWA_KV_WIDTH + h * HEAD_DIM:SWA_WIDTH + SWA_KV_WIDTH + (h + 1) * HEAD_DIM]
            sinks = _rows_of(sink_ref[:, h * gw:(h + 1) * gw], SWA_GROUP)[:, 0:1]
            o, _ = _decode_heads(q, knew, vnew, cache_ref[b, 0, h], cache_ref[b, 1, h], 1, sinks)
            outs += [o[g:g + 1] for g in range(SWA_GROUP)]
        o_ref[b:b + 1, :] = jnp.concatenate(outs, axis=1)


def _swa_sample(proj, cache, sink, layer):
    n = proj.shape[0]
    bn = 8
    return pl.pallas_call(
        functools.partial(_swa_sample_kernel, bn=bn),
        grid=(n // bn,),
        in_specs=[
            pl.BlockSpec((bn, B_SLOT), lambda i: (i, OFF_B // B_SLOT)),
            pl.BlockSpec((None, bn, 2, SWA_KV_HEADS, HEAD_DIM, WINDOW), lambda i: (layer, i, 0, 0, 0, 0)),
            pl.BlockSpec((1, SWA_WIDTH), lambda i: (0, 0)),
        ],
        out_specs=pl.BlockSpec((bn, SWA_WIDTH), lambda i: (i, 0)),
        out_shape=jax.ShapeDtypeStruct((n, SWA_WIDTH), F32),
        compiler_params=_cparams(("parallel",)),
    )(proj, cache, sink)


def _dil_sample_kernel(p_ref, cache_ref, o_ref, *, bn, dil):
    for b in range(bn):
        row = p_ref[b:b + 1, :]
        q = _rows_of(row[:, 0:DIL_GW], DIL_HPG)
        o, lse = _decode_heads(q, row[:, C_K:C_K + HEAD_DIM], row[:, C_K + HEAD_DIM:C_K + 2 * HEAD_DIM],
                               cache_ref[b, 0], cache_ref[b, 1], dil, None)
        lse = jnp.broadcast_to(lse, (DIL_HPG, HEAD_DIM))
        o_ref[b:b + 1, :] = jnp.concatenate([o[g:g + 1] for g in range(DIL_HPG)]
                                            + [lse[g:g + 1] for g in range(DIL_HPG)], axis=1)


def _dil_sample(proj, cache, group, dil, layer):
    n = proj.shape[0]
    bn = 8
    slot = OFF_C + group * C_SLOT
    return pl.pallas_call(
        functools.partial(_dil_sample_kernel, bn=bn, dil=dil),
        grid=(n // bn,),
        in_specs=[
            pl.BlockSpec((bn, C_SLOT), lambda i: (i, slot // C_SLOT)),
            pl.BlockSpec((None, bn, 2, HEAD_DIM, WINDOW * dil), lambda i: (layer, i, 0, 0, 0)),
        ],
        out_specs=pl.BlockSpec((bn, 2 * DIL_GW), lambda i: (i, 0)),
        out_shape=jax.ShapeDtypeStruct((n, 2 * DIL_GW), F32),
        compiler_params=_cparams(("parallel",)),
    )(proj, cache)


def _s5_param_kernel(are_ref, aim_ref, ldt_ref, bre_ref, bim_ref, cre_ref, cim_ref,
                     lam_re_ref, lam_im_ref, cl_re_ref, cl_im_ref, r_re_ref, r_im_ref, k_ref):
    ar = are_ref[0]
    ai = aim_ref[0]
    dt = jnp.exp(ldt_ref[0])
    nrow = lam_re_ref.shape[1]
    ri = lax.broadcasted_iota(jnp.int32, (nrow, 1), 0)
    pw = jnp.where(ri < S5_NPOW, ri, jnp.left_shift(S5_CHUNK, jnp.maximum(ri - S5_NPOW, 0))).astype(F32)
    mag = jnp.exp(pw * (ar * dt))
    ang = pw * (ai * dt)
    lam_re = mag * jnp.cos(ang)
    lam_im = mag * jnp.sin(ang)
    lam_re_ref[0] = lam_re
    lam_im_ref[0] = lam_im
    l1r, l1i = lam_re[1:2], lam_im[1:2]
    den = ar * ar + ai * ai
    z_re = ((l1r - 1.0) * ar + l1i * ai) / den
    z_im = (l1i * ar - (l1r - 1.0) * ai) / den
    b_re, b_im = bre_ref[0], bim_ref[0]
    bb_re = z_re * b_re - z_im * b_im
    bb_im = z_re * b_im + z_im * b_re
    c_re, c_im = cre_ref[0], cim_ref[0]
    cls_re, cls_im = [], []
    for d in range(S5_NPOW):
        lr, li = lam_re[d:d + 1], lam_im[d:d + 1]
        cr = c_re * lr - c_im * li
        ci = c_re * li + c_im * lr
        cl_re_ref[0, d * SSM_GROUP:(d + 1) * SSM_GROUP, :] = cr
        cl_im_ref[0, d * SSM_GROUP:(d + 1) * SSM_GROUP, :] = ci
        if d < S5_CHUNK:
            cls_re.append(cr)
            cls_im.append(ci)
    for s in range(S5_CHUNK):
        lr = lam_re[S5_CHUNK - 1 - s:S5_CHUNK - s]
        li = lam_im[S5_CHUNK - 1 - s:S5_CHUNK - s]
        r_re_ref[0, s * SSM_GROUP:(s + 1) * SSM_GROUP, :] = lr * bb_re - li * bb_im
        r_im_ref[0, s * SSM_GROUP:(s + 1) * SSM_GROUP, :] = lr * bb_im + li * bb_re
    call_re = jnp.concatenate(cls_re, axis=0)
    call_im = jnp.concatenate(cls_im, axis=0)
    hp = lax.Precision.HIGHEST
    nt = (((1,), (1,)), ((), ()))
    kt = (lax.dot_general(bb_re, call_re, nt, precision=hp, preferred_element_type=F32)
          - lax.dot_general(bb_im, call_im, nt, precision=hp, preferred_element_type=F32))
    cw = S5_CHUNK * SSM_GROUP
    for s in range(S5_CHUNK):
        if s == 0:
            blk = kt
        else:
            blk = jnp.concatenate([jnp.zeros((SSM_GROUP, s * SSM_GROUP), F32), kt[:, :cw - s * SSM_GROUP]], axis=1)
        k_ref[0, s * SSM_GROUP:(s + 1) * SSM_GROUP, :] = blk


def _s5_params(a_re, a_im, log_dt, b_re, b_im, c_re, c_im, nsteps):
    g = SSM_GROUPS
    nrow = ((S5_NPOW + nsteps + 7) // 8) * 8
    row = lambda i: (i, 0, 0)
    shp = lambda r, c: jax.ShapeDtypeStruct((g, r, c), F32)
    ncl = S5_NPOW * SSM_GROUP
    nr = S5_CHUNK * SSM_GROUP
    return pl.pallas_call(
        _s5_param_kernel,
        grid=(g,),
        in_specs=[
            pl.BlockSpec((1, 1, SSM_STATE), row),
            pl.BlockSpec((1, 1, SSM_STATE), row),
            pl.BlockSpec((1, 1, 1), row),
            pl.BlockSpec((1, SSM_GROUP, SSM_STATE), row),
            pl.BlockSpec((1, SSM_GROUP, SSM_STATE), row),
            pl.BlockSpec((1, SSM_GROUP, SSM_STATE), row),
            pl.BlockSpec((1, SSM_GROUP, SSM_STATE), row),
        ],
        out_specs=[
            pl.BlockSpec((1, nrow, SSM_STATE), row), pl.BlockSpec((1, nrow, SSM_STATE), row),
            pl.BlockSpec((1, ncl, SSM_STATE), row), pl.BlockSpec((1, ncl, SSM_STATE), row),
            pl.BlockSpec((1, nr, SSM_STATE), row), pl.BlockSpec((1, nr, SSM_STATE), row),
            pl.BlockSpec((1, nr, nr), row),
        ],
        out_shape=[shp(nrow, SSM_STATE), shp(nrow, SSM_STATE), shp(ncl, SSM_STATE), shp(ncl, SSM_STATE),
                   shp(nr, SSM_STATE), shp(nr, SSM_STATE), shp(nr, nr)],
        compiler_params=_cparams(("parallel",)),
    )(a_re.reshape(g, 1, SSM_STATE), a_im.reshape(g, 1, SSM_STATE), log_dt.reshape(g, 1, 1),
      jnp.swapaxes(b_re, 1, 2), jnp.swapaxes(b_im, 1, 2), c_re, c_im)


def _s5_rows_to_chunks(st_ref, u_ref):
    nc = st_ref.shape[1] // S5_CHUNK
    per_blk = LANES // SSM_GROUP
    slot = lax.broadcasted_iota(jnp.int32, (nc, LANES), 1) // SSM_GROUP
    for g in range(SSM_GROUPS):
        acc = [jnp.zeros((nc, LANES), F32) for _ in range(S5_CHUNK // per_blk)]
        for s in range(S5_CHUNK):
            v = st_ref[g // per_blk, pl.ds(s, nc, stride=S5_CHUNK), :]
            shift = (SSM_GROUP * (s % per_blk) - SSM_GROUP * (g % per_blk)) % LANES
            if shift:
                v = pltpu.roll(v, shift, axis=1)
            acc[s // per_blk] = jnp.where(slot == s % per_blk, v, acc[s // per_blk])
        for k, a in enumerate(acc):
            u_ref[g, :, k * LANES:(k + 1) * LANES] = a


def _s5_chunks_to_rows(y_ref, st_ref):
    nc = y_ref.shape[1]
    per_blk = LANES // SSM_GROUP
    slot = lax.broadcasted_iota(jnp.int32, (nc, LANES), 1) // SSM_GROUP
    nblk = -(-SSM_WIDTH // LANES)
    for lb in range(nblk):
        groups = range(lb * per_blk, min((lb + 1) * per_blk, SSM_GROUPS))
        for s in range(S5_CHUNK):
            acc = jnp.zeros((nc, LANES), F32)
            for g in groups:
                v = y_ref[g, :, (s // per_blk) * LANES:(s // per_blk + 1) * LANES]
                shift = (SSM_GROUP * (g % per_blk) - SSM_GROUP * (s % per_blk)) % LANES
                if shift:
                    v = pltpu.roll(v, shift, axis=1)
                acc = jnp.where(slot == g % per_blk, v, acc)
            st_ref[lb, pl.ds(s, nc, stride=S5_CHUNK), :] = acc
    return jnp.concatenate([st_ref[lb][:, 0:min(LANES, SSM_WIDTH - lb * LANES)] for lb in range(nblk)], axis=1)


def _s5_prompt_kernel(u_ref, mt_ref, r_re_ref, r_im_ref, cl_re_ref, cl_im_ref, lam_re_ref, lam_im_ref, d_ref,
                      y_ref, xre_ref, xim_ref, *, nseq, nchunk):
    u = u_ref[0]
    ub = u.astype(BF16)
    y = _dot(ub, mt_ref[0].astype(BF16))
    xr = _dot(ub, r_re_ref[0].astype(BF16))
    xi = _dot(ub, r_im_ref[0].astype(BF16))
    rows = nseq * nchunk
    cidx = lax.broadcasted_iota(jnp.int32, (rows, SSM_STATE), 0) % nchunk
    lam_re, lam_im = lam_re_ref[0], lam_im_ref[0]
    for k in range(int(math.log2(nchunk))):
        sh = 1 << k
        lr = lam_re[S5_NPOW + k:S5_NPOW + k + 1]
        li = lam_im[S5_NPOW + k:S5_NPOW + k + 1]
        ok = cidx >= sh
        pr = jnp.where(ok, pltpu.roll(xr, sh, axis=0), 0.0)
        pi = jnp.where(ok, pltpu.roll(xi, sh, axis=0), 0.0)
        xr, xi = xr + lr * pr - li * pi, xi + lr * pi + li * pr
    ok = cidx >= 1
    er = jnp.where(ok, pltpu.roll(xr, 1, axis=0), 0.0)
    ei = jnp.where(ok, pltpu.roll(xi, 1, axis=0), 0.0)
    cl_re = cl_re_ref[0][SSM_GROUP:, :]
    cl_im = cl_im_ref[0][SSM_GROUP:, :]
    y = y + _dot_nt(er.astype(BF16), cl_re.astype(BF16)) - _dot_nt(ei.astype(BF16), cl_im.astype(BF16))
    y_ref[0] = y + d_ref[0] * u
    last = [s * nchunk + nchunk - 1 for s in range(nseq)]
    xre_ref[0] = jnp.concatenate([xr[i:i + 1] for i in last], axis=0)
    xim_ref[0] = jnp.concatenate([xi[i:i + 1] for i in last], axis=0)


def _s5_prompt(u, mt, r_re, r_im, cl_re, cl_im, lam_re, lam_im, d_t, nseq, nchunk):
    g = SSM_GROUPS
    rows = nseq * nchunk
    cw = S5_CHUNK * SSM_GROUP
    row = lambda i: (i, 0, 0)
    full = lambda a: pl.BlockSpec((1,) + a.shape[1:], row)
    return pl.pallas_call(
        functools.partial(_s5_prompt_kernel, nseq=nseq, nchunk=nchunk),
        grid=(g,),
        in_specs=[full(u), full(mt), full(r_re), full(r_im), full(cl_re), full(cl_im), full(lam_re), full(lam_im),
                  full(d_t)],
        out_specs=[pl.BlockSpec((1, rows, cw), row), pl.BlockSpec((1, nseq, SSM_STATE), row),
                   pl.BlockSpec((1, nseq, SSM_STATE), row)],
        out_shape=[jax.ShapeDtypeStruct((g, rows, cw), F32), jax.ShapeDtypeStruct((g, nseq, SSM_STATE), F32),
                   jax.ShapeDtypeStruct((g, nseq, SSM_STATE), F32)],
        compiler_params=_cparams(("parallel",)),
    )(u, mt, r_re, r_im, cl_re, cl_im, lam_re, lam_im, d_t)


def _s5_sample_kernel(u_ref, x0r_ref, x0i_ref, r_re_ref, r_im_ref, cl_re_ref, cl_im_ref, lam_re_ref, lam_im_ref,
                      d_ref, y_ref, xre_ref, xim_ref):
    u = u_ref[0]
    ub = u.astype(BF16)
    lo = (S5_CHUNK - 1) * SSM_GROUP
    bb_re = r_re_ref[0][lo:lo + SSM_GROUP, :]
    bb_im = r_im_ref[0][lo:lo + SSM_GROUP, :]
    lr, li = lam_re_ref[0][1:2], lam_im_ref[0][1:2]
    x0r, x0i = x0r_ref[0], x0i_ref[0]
    xr = _dot(ub, bb_re.astype(BF16)) + (lr * x0r - li * x0i)
    xi = _dot(ub, bb_im.astype(BF16)) + (lr * x0i + li * x0r)
    c_re = cl_re_ref[0][:SSM_GROUP, :]
    c_im = cl_im_ref[0][:SSM_GROUP, :]
    y_ref[0] = (_dot_nt(xr.astype(BF16), c_re.astype(BF16)) - _dot_nt(xi.astype(BF16), c_im.astype(BF16))
                + d_ref[0] * u)
    xre_ref[0] = xr
    xim_ref[0] = xi


def _s5_sample(u, x0r, x0i, r_re, r_im, cl_re, cl_im, lam_re, lam_im, d):
    g, n = u.shape[0], u.shape[1]
    row = lambda i: (i, 0, 0)
    full = lambda a: pl.BlockSpec((1,) + a.shape[1:], row)
    return pl.pallas_call(
        _s5_sample_kernel,
        grid=(g,),
        in_specs=[full(u), full(x0r), full(x0i), full(r_re), full(r_im), full(cl_re), full(cl_im), full(lam_re),
                  full(lam_im), full(d)],
        out_specs=[pl.BlockSpec((1, n, SSM_GROUP), row), pl.BlockSpec((1, n, SSM_STATE), row),
                   pl.BlockSpec((1, n, SSM_STATE), row)],
        out_shape=[jax.ShapeDtypeStruct((g, n, SSM_GROUP), F32), jax.ShapeDtypeStruct((g, n, SSM_STATE), F32),
                   jax.ShapeDtypeStruct((g, n, SSM_STATE), F32)],
        compiler_params=_cparams(("parallel",)),
    )(u, x0r, x0i, r_re, r_im, cl_re, cl_im, lam_re, lam_im, d)


def _log_sigmoid(x):
    return jnp.minimum(x, 0.0) - jnp.log1p(jnp.exp(-jnp.abs(x)))


def _logaddexp(a, b):
    return jnp.maximum(a, b) + jnp.log1p(jnp.exp(-jnp.abs(a - b)))


def _hgrn_gates(fpre, lb):
    log_f = _logaddexp(jnp.log(jnp.maximum(lb, LB_FLOOR)), jnp.log1p(-lb) + _log_sigmoid(fpre))
    k = (1.0 - lb) * _sigmoid(-fpre)
    return log_f, k


def _hgrn_scan_matrix():
    c = HGRN_CHUNK
    tri = np.tril(np.ones((c, c), np.float32))
    t = np.arange(c)
    mats = [tri]
    for m in HGRN_LEVELS:
        mats.append(tri[(t // (2 * m)) * (2 * m) + m - 1])
    return np.concatenate(mats, axis=0)


def _hgrn_prompt_kernel(p_ref, lbraw_ref, gn_ref, scan_ref, o_ref, s_ref, st_ref, *, layer):
    cidx = pl.program_id(1)
    c = HGRN_CHUNK
    hd = HGRN_HEAD_DIM

    @pl.when(cidx == 0)
    def _():
        st_ref[...] = jnp.zeros_like(st_ref)

    a = lbraw_ref[...]
    e = jnp.exp(a - jnp.max(a, axis=0, keepdims=True))
    p = e / jnp.sum(e, axis=0, keepdims=True)
    run = p[0:1]
    for l in range(1, layer + 1):
        run = run + p[l:l + 1]
    lb_all = run - p[0:1]

    r = lax.broadcasted_iota(jnp.int32, (c, c), 0)
    s = lax.broadcasted_iota(jnp.int32, (c, c), 1)
    pairs = []
    for m in HGRN_LEVELS:
        sh = int(math.log2(2 * m))
        pairs.append(((r >> sh) == (s >> sh)) & ((r & m) != 0) & ((s & m) == 0))
    scan = scan_ref[...]
    for h in range(HGRN_HEADS):
        col = lambda part: p_ref[0, :, part * HGRN_WIDTH + h * hd:part * HGRN_WIDTH + (h + 1) * hd]
        log_f, k = _hgrn_gates(col(1), lb_all[:, h * hd:(h + 1) * hd])
        q = _silu(col(0))
        v = col(2)
        hi = log_f.astype(BF16)
        rest = log_f - hi.astype(F32)
        mid = rest.astype(BF16)
        lo = (rest - mid.astype(F32)).astype(BF16)
        cum3 = _dot(scan, jnp.concatenate([hi, mid, lo], axis=1))
        cum = cum3[:, 0:hd] + cum3[:, hd:2 * hd] + cum3[:, 2 * hd:3 * hd]
        b = cum[0:c]
        att = jnp.where(r == s, jnp.sum(q * k, axis=-1, keepdims=True), 0.0)
        for lvl, m in enumerate(HGRN_LEVELS):
            bref = cum[(lvl + 1) * c:(lvl + 2) * c]
            qs = q * jnp.exp(jnp.minimum(b - bref, 0.0))
            ks = k * jnp.exp(jnp.minimum(bref - b, 0.0))
            att = att + jnp.where(pairs[lvl], _dot_nt(qs.astype(BF16), ks.astype(BF16)), 0.0)
        st = st_ref[h]
        vb = v.astype(BF16)
        o = _dot(att.astype(BF16), vb) + _dot_nt((q * jnp.exp(b)).astype(BF16), st.astype(BF16))
        bl = b[c - 1:c]
        st_new = jnp.exp(bl) * st + _dot_tn(vb, (k * jnp.exp(bl - b)).astype(BF16))
        st_ref[h] = st_new
        o = o * lax.rsqrt(jnp.mean(o * o, axis=-1, keepdims=True) + NORM_EPS)
        o_ref[0, :, h * hd:(h + 1) * hd] = o * gn_ref[:, h * hd:(h + 1) * hd] * _silu(col(3))

    @pl.when(cidx == pl.num_programs(1) - 1)
    def _():
        s_ref[0] = st_ref[...]


def _hgrn_prompt(proj, lbraw, gn, layer):
    n, t, _ = proj.shape
    c = HGRN_CHUNK
    hd = HGRN_HEAD_DIM
    scan = jnp.asarray(_hgrn_scan_matrix(), dtype=BF16)
    return pl.pallas_call(
        functools.partial(_hgrn_prompt_kernel, layer=layer),
        grid=(n, t // c),
        in_specs=[pl.BlockSpec((1, c, 4 * HGRN_WIDTH), lambda b, j: (b, j, OFF_D // (4 * HGRN_WIDTH))),
                  pl.BlockSpec((DEPTH, HGRN_WIDTH), lambda b, j: (0, 0)),
                  pl.BlockSpec((None, 1, HGRN_WIDTH), lambda b, j: (layer, 0, 0)),
                  pl.BlockSpec(scan.shape, lambda b, j: (0, 0))],
        out_specs=[pl.BlockSpec((1, c, HGRN_WIDTH), lambda b, j: (b, j, 0)),
                   pl.BlockSpec((1, HGRN_HEADS, hd, hd), lambda b, j: (b, 0, 0, 0))],
        out_shape=[jax.ShapeDtypeStruct((n, t, HGRN_WIDTH), F32),
                   jax.ShapeDtypeStruct((n, HGRN_HEADS, hd, hd), F32)],
        scratch_shapes=[pltpu.VMEM((HGRN_HEADS, hd, hd), F32)],
        compiler_params=_cparams(("parallel", "arbitrary")),
    )(proj, lbraw, _v3(gn), scan)


def _hgrn_sample_kernel(qt_ref, ft_ref, lbt_ref, v_ref, g_ref, gn_ref, s0_ref, o_ref, s_ref, *, layer, n):
    hd = HGRN_HEAD_DIM
    a = lbt_ref[0]
    e = jnp.exp(a - jnp.max(a, axis=1, keepdims=True))
    p = e / jnp.sum(e, axis=1, keepdims=True)
    run = p[:, 0:1]
    for l in range(1, layer + 1):
        run = run + p[:, l:l + 1]
    lb = run - p[:, 0:1]
    log_f, k = _hgrn_gates(ft_ref[0], lb)
    f = jnp.exp(log_f)
    q = _silu(qt_ref[0])
    v = v_ref[...]
    gate = g_ref[...]
    rows = []
    for b in range(n):
        s_new = f[:, b:b + 1] * s0_ref[b, 0] + k[:, b:b + 1] * v[b:b + 1, :]
        s_ref[b, 0] = s_new
        rows.append(jnp.sum(q[:, b:b + 1] * s_new, axis=0, keepdims=True))
    o = jnp.concatenate(rows, axis=0)
    o = o * lax.rsqrt(jnp.mean(o * o, axis=-1, keepdims=True) + NORM_EPS)
    o_ref[...] = o * gn_ref[...] * _silu(gate)


def _hgrn_sample(qt, ft, lbt, proj, gn, s0, layer):
    n = proj.shape[0]
    hd = HGRN_HEAD_DIM
    v_blk = (OFF_D + 2 * HGRN_WIDTH) // hd
    g_blk = (OFF_D + 3 * HGRN_WIDTH) // hd
    return pl.pallas_call(
        functools.partial(_hgrn_sample_kernel, layer=layer, n=n),
        grid=(HGRN_HEADS,),
        in_specs=[pl.BlockSpec((1, hd, n), lambda h: (h, 0, 0)),
                  pl.BlockSpec((1, hd, n), lambda h: (h, 0, 0)),
                  pl.BlockSpec((1, hd, DEPTH), lambda h: (h, 0, 0)),
                  pl.BlockSpec((n, hd), lambda h: (0, v_blk + h)),
                  pl.BlockSpec((n, hd), lambda h: (0, g_blk + h)),
                  pl.BlockSpec((None, 1, hd), lambda h: (layer, 0, h)),
                  pl.BlockSpec((None, n, 1, hd, hd), lambda h: (layer, 0, h, 0, 0))],
        out_specs=[pl.BlockSpec((n, hd), lambda h: (0, h)),
                   pl.BlockSpec((n, 1, hd, hd), lambda h: (0, h, 0, 0))],
        out_shape=[jax.ShapeDtypeStruct((n, HGRN_WIDTH), F32),
                   jax.ShapeDtypeStruct((n, HGRN_HEADS, hd, hd), F32)],
        compiler_params=_cparams(("parallel",)),
    )(qt, ft, lbt, proj, proj, _v3(gn), s0)


def _gelu_tanh(x):
    return 0.5 * x * (1.0 + jnp.tanh(math.sqrt(2.0 / math.pi) * (x + 0.044715 * (x * x * x))))


def _mixout_kernel(h_ref, ya_ref, ob_ref, oc0_ref, oc1_ref, oc2_ref, od_ref,
                   wglu_ref, bglu_ref, ga_ref, gb_ref, gc_ref, wo_ref, gpost_ref, o_ref, *scratch):
    ya = _s5_chunks_to_rows(ya_ref, scratch[0]) if scratch else ya_ref[...]
    z = _gelu_tanh(ya)
    gate = _sigmoid(_dot(z.astype(BF16), wglu_ref[...].astype(BF16)) + bglu_ref[...])
    out_a = _rms(z * gate, ga_ref[...])
    out_b = _rms(ob_ref[...], gb_ref[...])
    ocs = [oc0_ref, oc1_ref, oc2_ref]
    ls = [r[:, DIL_GW:2 * DIL_GW] for r in ocs]
    m = jnp.maximum(jnp.maximum(ls[0], ls[1]), ls[2])
    es = [jnp.exp(l - m) for l in ls]
    den = es[0] + es[1] + es[2]
    cs = [ocs[gi][:, 0:DIL_GW] * (es[gi] / den) for gi in range(3)]
    ss = sum(jnp.sum(c * c, axis=-1, keepdims=True) for c in cs)
    inv = lax.rsqrt(ss / DIL_WIDTH + NORM_EPS)
    gc = gc_ref[...]
    o1, o2, o3 = SSM_WIDTH, SSM_WIDTH + SWA_WIDTH, SSM_WIDTH + SWA_WIDTH + DIL_WIDTH
    wo = lambda lo, hi: wo_ref[lo:hi, :]
    mix = _dot(out_a.astype(BF16), wo(0, o1)) + _dot(out_b.astype(BF16), wo(o1, o2))
    for gi in range(3):
        c = cs[gi] * inv * gc[:, gi * DIL_GW:(gi + 1) * DIL_GW]
        mix = mix + _dot(c.astype(BF16), wo(o2 + gi * DIL_GW, o2 + (gi + 1) * DIL_GW))
    mix = mix + _dot(od_ref[...].astype(BF16), wo(o3, D_MODEL))
    o_ref[...] = h_ref[...] + _rms(mix, gpost_ref[...])


def _mixout(h, ya, ob, ocs, od, lp_all, w_out_l, layer, tm):
    m = h.shape[0]
    rowblk = lambda a: pl.BlockSpec((tm, a.shape[1]), lambda i: (i, 0))
    vec = lambda a: pl.BlockSpec((None, 1, a.shape[1]), lambda i: (layer, 0, 0))
    mat = lambda a: pl.BlockSpec((None,) + a.shape[1:], lambda i: (layer, 0, 0))
    whole = lambda a: pl.BlockSpec(a.shape, lambda i: (0, 0), pipeline_mode=pl.Buffered(1))
    acts = [h, ya, ob, *ocs, od]
    names = ['ssm_w_glu', 'ssm_b_glu', 'out_norm_a', 'out_norm_b', 'out_norm_c', 'w_out', 'mix_norm_post']
    params = [w_out_l if k == 'w_out' else lp_all[k] for k in names]
    spec = lambda k, p: whole(p) if k == 'w_out' else (mat(p) if p.ndim == 3 else vec(p))
    act_specs = [rowblk(a) for a in acts]
    scratch = []
    if ya.ndim == 3:
        act_specs[1] = pl.BlockSpec((SSM_GROUPS, tm // S5_CHUNK, ya.shape[2]), lambda i: (0, i, 0))
        scratch = [pltpu.VMEM((-(-SSM_WIDTH // LANES), tm, LANES), F32)]
    return pl.pallas_call(
        _mixout_kernel,
        grid=(m // tm,),
        in_specs=act_specs + [spec(k, p) for k, p in zip(names, params)],
        out_specs=pl.BlockSpec((tm, D_MODEL), lambda i: (i, 0)),
        out_shape=jax.ShapeDtypeStruct((m, D_MODEL), F32),
        scratch_shapes=scratch,
        compiler_params=_cparams(("parallel",)),
    )(*acts, *[_v3(p) if p.ndim == 2 and k != 'w_out' else p for k, p in zip(names, params)])


def _ffn_block(h, P, name, layer, tm):
    return _ffn(h, P[name + '_norm_pre'], P[name + '_w_gate'], P[name + '_w_up'], P[name + '_w_down'],
                P[name + '_norm_post'], layer, tm)


def _prompt_mixer(h, P, w_in_l, w_out_l, s5p, layer, n, t, rope, tm):
    m = n * t
    proj2, u = _win(h, P['mix_norm_pre'], w_in_l, rope[0], rope[1], layer, tm, True)
    proj = proj2.reshape(n, t, PROJ_COLS)

    lam_re, lam_im, cl_re, cl_im, r_re, r_im, kmat = s5p
    nchunk = t // S5_CHUNK
    d_t = jnp.tile(P['ssm_d'][layer], (1, S5_CHUNK)).reshape(SSM_GROUPS, 1, S5_CHUNK * SSM_GROUP)
    y, xre, xim = _s5_prompt(u, kmat, r_re, r_im, cl_re, cl_im, lam_re, lam_im, d_t, n, nchunk)
    ssm_new = jnp.stack([xre, xim], axis=-1).transpose(1, 0, 2, 3)

    sink = jnp.repeat(P['swa_sinks'][layer], HEAD_DIM).reshape(1, SWA_WIDTH)
    ob = _swa_prompt(proj, sink).reshape(m, SWA_WIDTH)
    keep = min(WINDOW, t)
    kv = proj[:, t - keep:, OFF_B + SWA_WIDTH:OFF_B + B_SLOT]
    swa_new = kv.reshape(n, keep, 2, SWA_KV_HEADS, HEAD_DIM)

    ocs, dil_new = [], []
    for gi, (win, dil) in enumerate(DIL_PAIRS):
        ocs.append(_dil_prompt(proj, gi, dil).reshape(m, 2 * DIL_GW))
        keep = min(win, t)
        lo = OFF_C + gi * C_SLOT + C_K
        dil_new.append(proj[:, t - keep:, lo:lo + 2 * HEAD_DIM].reshape(n, keep, 2, HEAD_DIM))

    od, st = _hgrn_prompt(proj, P['hgrn_lower_bounds'], P['out_norm_d'], layer)
    hgrn_new = jnp.swapaxes(st, -1, -2)

    h = _mixout(h, y, ob, ocs, od.reshape(m, HGRN_WIDTH), P, w_out_l, layer, min(tm, 512))
    return h, (ssm_new, swa_new, dil_new[0], dil_new[1], dil_new[2], hgrn_new)


def _sample_mixer(h, P, w_in_l, w_out_l, s5p, layer, caches, rope):
    n = h.shape[0]
    state_ssm, cache_swa, cache_d0, cache_d1, cache_d2, state_hgrn = caches
    (proj,) = _win(h, P['mix_norm_pre'], w_in_l, rope[0], rope[1], layer, n, False)

    lam_re, lam_im, cl_re, cl_im, r_re, r_im, _ = s5p
    u = proj[:, OFF_U:OFF_U + SSM_WIDTH].reshape(n, SSM_GROUPS, SSM_GROUP).transpose(1, 0, 2)
    x0 = state_ssm[layer].transpose(1, 0, 2, 3)
    y, xre, xim = _s5_sample(u, x0[..., 0], x0[..., 1], r_re, r_im, cl_re, cl_im, lam_re, lam_im,
                             P['ssm_d'][layer].reshape(SSM_GROUPS, 1, SSM_GROUP))
    ya = y.transpose(1, 0, 2).reshape(n, SSM_WIDTH)
    ssm_new = jnp.stack([xre, xim], axis=-1).transpose(1, 0, 2, 3)

    sink = jnp.repeat(P['swa_sinks'][layer], HEAD_DIM).reshape(1, SWA_WIDTH)
    ob = _swa_sample(proj, cache_swa.transpose(0, 1, 3, 4, 5, 2), sink, layer)
    swa_new = proj[:, OFF_B + SWA_WIDTH:OFF_B + B_SLOT].reshape(n, 1, 2, SWA_KV_HEADS, HEAD_DIM)

    ocs, dil_new = [], []
    for gi, (buf, (win, dil)) in enumerate(zip((cache_d0, cache_d1, cache_d2), DIL_PAIRS)):
        ocs.append(_dil_sample(proj, buf.transpose(0, 1, 3, 4, 2), gi, dil, layer))
        lo = OFF_C + gi * C_SLOT + C_K
        dil_new.append(proj[:, lo:lo + 2 * HEAD_DIM].reshape(n, 1, 2, HEAD_DIM))

    part = lambda i: proj[:, OFF_D + i * HGRN_WIDTH:OFF_D + (i + 1) * HGRN_WIDTH].reshape(n, HGRN_HEADS, HGRN_HEAD_DIM)
    cm = lambda a: a.transpose(1, 2, 0)
    lbt = P['hgrn_lower_bounds'].reshape(DEPTH, HGRN_HEADS, HGRN_HEAD_DIM).transpose(1, 2, 0)
    od, hgrn_new = _hgrn_sample(cm(part(0)), cm(part(1)), lbt, proj, P['out_norm_d'], state_hgrn, layer)

    h = _mixout(h, ya, ob, ocs, od, P, w_out_l, layer, n)
    return h, (ssm_new, swa_new, dil_new[0], dil_new[1], dil_new[2], hgrn_new)


def _forward(x_prompt, x_sample, caches, P):
    n, t, _ = x_prompt.shape
    ns, ts, _ = x_sample.shape
    assert ts == 1
    tm = 1024
    rope_p = _rope_tables(jnp.tile(jnp.arange(t, dtype=jnp.int32), n))
    rope_s = _rope_tables(jnp.full((ns,), PAST_LEN, dtype=jnp.int32))
    nsteps = int(math.log2(t // S5_CHUNK))
    h_p = x_prompt.reshape(n * t, D_MODEL)
    h_s = x_sample.reshape(ns, D_MODEL)
    new_p = [[] for _ in range(6)]
    new_s = [[] for _ in range(6)]
    for l in range(DEPTH):
        w_in_l = _win_prep(P['w_in'], l)
        w_out_l = _cast_bf16(P['w_out'], l)
        s5p = _s5_params(P['ssm_a_re'][l], P['ssm_a_im'][l], P['ssm_log_dt'][l], P['ssm_b_re'][l], P['ssm_b_im'][l],
                         P['ssm_c_re'][l], P['ssm_c_im'][l], nsteps)
        h_p = _ffn_block(h_p, P, 'ffn1', l, tm)
        h_p, st_p = _prompt_mixer(h_p, P, w_in_l, w_out_l, s5p, l, n, t, rope_p, tm)
        h_p = _ffn_block(h_p, P, 'ffn2', l, tm)
        h_s = _ffn_block(h_s, P, 'ffn1', l, ns)
        h_s, st_s = _sample_mixer(h_s, P, w_in_l, w_out_l, s5p, l, caches, rope_s)
        h_s = _ffn_block(h_s, P, 'ffn2', l, ns)
        for i in range(6):
            new_p[i].append(st_p[i])
            new_s[i].append(st_s[i])
    new_p = [jnp.stack(a, axis=0) for a in new_p]
    new_s = [jnp.stack(a, axis=0) for a in new_s]
    return (h_p.reshape(n, t, D_MODEL), h_s.reshape(ns, ts, D_MODEL), new_p[0], new_s[0], new_p[1], new_s[1],
            new_p[2], new_s[2], new_p[3], new_s[3], new_p[4], new_s[4], new_p[5], new_s[5])


def kernel(x_prompt, x_sample, state_ssm, cache_swa_kv, cache_dil0_kv, cache_dil1_kv, cache_dil2_kv, state_hgrn, ffn1_norm_pre, ffn1_w_gate, ffn1_w_up, ffn1_w_down, ffn1_norm_post, mix_norm_pre, w_in, ssm_a_re, ssm_a_im, ssm_log_dt, ssm_b_re, ssm_b_im, ssm_c_re, ssm_c_im, ssm_d, ssm_w_glu, ssm_b_glu, swa_sinks, hgrn_lower_bounds, out_norm_a, out_norm_b, out_norm_c, out_norm_d, w_out, mix_norm_post, ffn2_norm_pre, ffn2_w_gate, ffn2_w_up, ffn2_w_down, ffn2_norm_post):
    P = dict(ffn1_norm_pre=ffn1_norm_pre, ffn1_w_gate=ffn1_w_gate, ffn1_w_up=ffn1_w_up, ffn1_w_down=ffn1_w_down,
             ffn1_norm_post=ffn1_norm_post, mix_norm_pre=mix_norm_pre, w_in=w_in, ssm_a_re=ssm_a_re,
             ssm_a_im=ssm_a_im, ssm_log_dt=ssm_log_dt, ssm_b_re=ssm_b_re, ssm_b_im=ssm_b_im, ssm_c_re=ssm_c_re,
             ssm_c_im=ssm_c_im, ssm_d=ssm_d, ssm_w_glu=ssm_w_glu, ssm_b_glu=ssm_b_glu, swa_sinks=swa_sinks,
             hgrn_lower_bounds=hgrn_lower_bounds, out_norm_a=out_norm_a, out_norm_b=out_norm_b,
             out_norm_c=out_norm_c, out_norm_d=out_norm_d, w_out=w_out, mix_norm_post=mix_norm_post,
             ffn2_norm_pre=ffn2_norm_pre, ffn2_w_gate=ffn2_w_gate, ffn2_w_up=ffn2_w_up, ffn2_w_down=ffn2_w_down,
             ffn2_norm_post=ffn2_norm_post)
    caches = (state_ssm, cache_swa_kv, cache_dil0_kv, cache_dil1_kv, cache_dil2_kv, state_hgrn)
    return _forward(x_prompt, x_sample, caches, P)
```

```python
import functools
import math

import numpy as np
import jax
import jax.numpy as jnp
from jax import lax
from jax.experimental import pallas as pl
from jax.experimental.pallas import tpu as pltpu

F32 = jnp.float32
BF16 = jnp.bfloat16

D_MODEL = 2048
DEPTH = 2
PAST_LEN = 16384
HEAD_DIM = 64
ROPE_THETA = 10000.0
D_FF = 5504
NORM_EPS = 1e-6
NEG_BIG = -1e30
LB_FLOOR = 1e-30
SSM_GROUP = 16
SSM_STATE = 64
SSM_WIDTH = 448
SSM_GROUPS = 28
SWA_HEADS = 8
SWA_KV_HEADS = 2
SWA_GROUP = 4
SWA_WIDTH = 512
SWA_KV_WIDTH = 128
WINDOW = 128
DIL_PAIRS = ((128, 1), (512, 4), (2048, 16))
DIL_HPG = 3
DIL_GW = DIL_HPG * HEAD_DIM
DIL_WIDTH = 576
DIL_KV_WIDTH = 192
HGRN_HEAD_DIM = 128
HGRN_WIDTH = 512
HGRN_HEADS = 4
IN_COLS = 4224
ATTN_SCALE = HEAD_DIM ** -0.5

LANES = 128
FF_TILE = 256
N_FF_TILES = -(-D_FF // FF_TILE)
DOWN_TILE = 256
N_DOWN_TILES = D_MODEL // DOWN_TILE
FFN_ROWS = 1040
VMEM_LIMIT = 60 * 1024 * 1024

B_SLOT = SWA_WIDTH + 2 * SWA_KV_WIDTH
C_SLOT = 384
C_K = 256
OFF_B = 0
OFF_C = OFF_B + B_SLOT
OFF_D = 2048
OFF_U = OFF_D + 4 * HGRN_WIDTH
U_SLOT = 512
PROJ_COLS = 4608
WIN_TILE = 1536

S5_CHUNK = 16
S5_NPOW = S5_CHUNK + 1
HGRN_CHUNK = 128
HGRN_LEVELS = (64, 32, 16, 8, 4, 2, 1)


def _proj_segments():
    src = np.cumsum([0, SSM_WIDTH, SWA_WIDTH, SWA_KV_WIDTH, SWA_KV_WIDTH, DIL_WIDTH, DIL_KV_WIDTH, DIL_KV_WIDTH,
                     HGRN_WIDTH, HGRN_WIDTH, HGRN_WIDTH, HGRN_WIDTH])
    s_u, s_qb, s_kb, s_vb, s_qc, s_kc, s_vc, s_qd = src[:8]
    segs = [(OFF_B, s_qb, SWA_WIDTH), (OFF_B + SWA_WIDTH, s_kb, SWA_KV_WIDTH),
            (OFF_B + SWA_WIDTH + SWA_KV_WIDTH, s_vb, SWA_KV_WIDTH)]
    for g in range(3):
        base = OFF_C + g * C_SLOT
        segs += [(base, s_qc + g * DIL_GW, DIL_GW), (base + C_K, s_kc + g * HEAD_DIM, HEAD_DIM),
                 (base + C_K + HEAD_DIM, s_vc + g * HEAD_DIM, HEAD_DIM)]
    segs += [(OFF_D, s_qd, 4 * HGRN_WIDTH), (OFF_U, s_u, SSM_WIDTH)]
    return [(int(a), int(b), int(c)) for a, b, c in segs]


def _rotary_lane_mask():
    m = np.zeros((PROJ_COLS,), np.float32)
    m[OFF_B:OFF_B + SWA_WIDTH + SWA_KV_WIDTH] = 1.0
    for g in range(3):
        base = OFF_C + g * C_SLOT
        m[base:base + DIL_GW] = 1.0
        m[base + C_K:base + C_K + HEAD_DIM] = 1.0
    return m


def _cparams(sem):
    return pltpu.CompilerParams(dimension_semantics=sem, vmem_limit_bytes=VMEM_LIMIT)


def _v3(a):
    return a.reshape(a.shape[0], 1, a.shape[1])


def _rms(x, g):
    return x * lax.rsqrt(jnp.mean(x * x, axis=-1, keepdims=True) + NORM_EPS) * g


def _sigmoid(x):
    return 1.0 / (1.0 + jnp.exp(-x))


def _silu(x):
    return x * _sigmoid(x)


def _dot(a, b):
    return jnp.dot(a, b, preferred_element_type=F32)


def _dot_nt(a, b):
    return lax.dot_general(a, b, (((1,), (1,)), ((), ())), preferred_element_type=F32)


def _dot_tn(a, b):
    return lax.dot_general(a, b, (((0,), (0,)), ((), ())), preferred_element_type=F32)


def _ffn_kernel(*refs, tail, n_tail, tail_in, tail_out):
    refs = list(refs)
    x_ref = refs.pop(0)
    xs_ref = refs.pop(0) if tail_in else None
    gpre_ref, wg_ref, wu_ref, wd_ref, gpost_ref, o_ref = refs[:6]
    os_ref = refs[6] if tail_out else None
    xn_ref, hid_ref = refs[-2:]
    s = pl.program_id(1)
    in_last_block = pl.program_id(0) == pl.num_programs(0) - 1
    tm = x_ref.shape[0]

    def load_x():
        x = x_ref[...]
        if tail_in:
            placed = jnp.concatenate([jnp.zeros((tail, D_MODEL), F32), xs_ref[...],
                                      jnp.zeros((tm - tail - n_tail, D_MODEL), F32)], axis=0)
            row = lax.broadcasted_iota(jnp.int32, (tm, 1), 0)
            first = jnp.where(in_last_block, tail, tm)
            x = jnp.where((row >= first) & (row < tail + n_tail), placed, x)
        return x

    @pl.when(s == 0)
    def _():
        xn_ref[...] = _rms(load_x(), gpre_ref[...]).astype(BF16)

    @pl.when(s < N_FF_TILES)
    def _():
        xn = xn_ref[...]
        gate = _dot(xn, wg_ref[...].astype(BF16))
        up = _dot(xn, wu_ref[...].astype(BF16))
        hid_ref[s] = (_silu(gate) * up).astype(BF16)

    for j in range(N_DOWN_TILES):

        @pl.when(s == N_FF_TILES + j)
        def _(j=j):
            hid = jnp.concatenate([hid_ref[f] for f in range(N_FF_TILES)], axis=1)[:, :D_FF]
            o_ref[:, j * DOWN_TILE:(j + 1) * DOWN_TILE] = _dot(hid, wd_ref[...].astype(BF16))

    @pl.when(s == pl.num_programs(1) - 1)
    def _():
        res = load_x() + 0.5 * _rms(o_ref[...], gpost_ref[...])
        o_ref[...] = res
        if tail_out:

            @pl.when(in_last_block)
            def _():
                os_ref[...] = res[tail:tail + n_tail]


def _ffn(x, x_tail, gpre, wg, wu, wd, gpost, layer, tm, n_main, n_tail, tail_out):
    m = n_main + n_tail
    nblk = pl.cdiv(m, tm)
    tail = n_main - (nblk - 1) * tm
    assert 0 <= tail and tail + n_tail <= tm and tail % 8 == 0
    last_ff = N_FF_TILES - 1
    x_buffers = {} if x_tail is None else dict(pipeline_mode=pl.Buffered(1))
    rowblk = pl.BlockSpec((tm, D_MODEL), lambda i, s: (i, 0), **x_buffers)
    in_specs = [rowblk]
    args = [x]
    if x_tail is not None:
        in_specs.append(pl.BlockSpec((n_tail, D_MODEL), lambda i, s: (0, 0)))
        args.append(x_tail)
    in_specs += [
        pl.BlockSpec((None, 1, D_MODEL), lambda i, s: (layer, 0, 0)),
        pl.BlockSpec((None, D_MODEL, FF_TILE), lambda i, s: (layer, 0, jnp.minimum(s, last_ff))),
        pl.BlockSpec((None, D_MODEL, FF_TILE), lambda i, s: (layer, 0, jnp.minimum(s, last_ff))),
        pl.BlockSpec((None, D_FF, DOWN_TILE), lambda i, s: (layer, 0, jnp.maximum(s - N_FF_TILES, 0))),
        pl.BlockSpec((None, 1, D_MODEL), lambda i, s: (layer, 0, 0)),
    ]
    args += [_v3(gpre), wg, wu, wd, _v3(gpost)]
    out_main = pl.BlockSpec((tm, D_MODEL), lambda i, s: (i, 0), pipeline_mode=pl.Buffered(1))
    if tail_out:
        out_specs = [out_main, pl.BlockSpec((n_tail, D_MODEL), lambda i, s: (0, 0))]
        out_shape = [jax.ShapeDtypeStruct((n_main, D_MODEL), F32), jax.ShapeDtypeStruct((n_tail, D_MODEL), F32)]
    else:
        out_specs = out_main
        out_shape = jax.ShapeDtypeStruct((m, D_MODEL), F32)
    return pl.pallas_call(
        functools.partial(_ffn_kernel, tail=tail, n_tail=n_tail, tail_in=x_tail is not None, tail_out=tail_out),
        grid=(nblk, N_FF_TILES + N_DOWN_TILES),
        in_specs=in_specs,
        out_specs=out_specs,
        out_shape=out_shape,
        scratch_shapes=[pltpu.VMEM((tm, D_MODEL), BF16), pltpu.VMEM((N_FF_TILES, tm, FF_TILE), BF16)],
        compiler_params=_cparams(("arbitrary", "arbitrary")),
    )(*args)


def _win_prep_kernel(w_ref, o_ref):
    o_ref[...] = jnp.zeros_like(o_ref)
    for dst, src, width in _proj_segments():
        o_ref[:, dst:dst + width] = w_ref[:, src:src + width].astype(BF16)


def _win_prep(w_in, layer):
    rows = 256
    return pl.pallas_call(
        _win_prep_kernel,
        grid=(D_MODEL // rows,),
        in_specs=[pl.BlockSpec((None, rows, IN_COLS), lambda i: (layer, i, 0))],
        out_specs=pl.BlockSpec((rows, PROJ_COLS), lambda i: (i, 0)),
        out_shape=jax.ShapeDtypeStruct((D_MODEL, PROJ_COLS), BF16),
        compiler_params=_cparams(("parallel",)),
    )(w_in)


def _cast_kernel(w_ref, o_ref):
    o_ref[...] = w_ref[...].astype(BF16)


def _cast_bf16(w, layer):
    _, r, c = w.shape
    rows = 256
    return pl.pallas_call(
        _cast_kernel,
        grid=(r // rows,),
        in_specs=[pl.BlockSpec((None, rows, c), lambda i: (layer, i, 0))],
        out_specs=pl.BlockSpec((rows, c), lambda i: (i, 0)),
        out_shape=jax.ShapeDtypeStruct((r, c), BF16),
        compiler_params=_cparams(("parallel",)),
    )(w)


def _win_kernel(x_ref, g_ref, w_ref, cos_ref, sin_ref, o_ref, *rest, chunks):
    if chunks:
        u_ref, xn_ref, st_ref = rest
    else:
        (xn_ref,) = rest
    j = pl.program_id(1)

    @pl.when(j == 0)
    def _():
        xn_ref[...] = _rms(x_ref[...], g_ref[...]).astype(BF16)

    rot = _rotary_lane_mask()
    for tile in range(PROJ_COLS // WIN_TILE):

        @pl.when(j == tile)
        def _(tile=tile):
            y = _dot(xn_ref[...], w_ref[...])
            cos = cos_ref[...]
            sin = sin_ref[...]
            lane = lax.broadcasted_iota(jnp.int32, cos.shape, 1)
            first_half = (lane % HEAD_DIM) < (HEAD_DIM // 2)
            for c in range(WIN_TILE // LANES):
                sl = slice(c * LANES, (c + 1) * LANES)
                flags = rot[tile * WIN_TILE + c * LANES:tile * WIN_TILE + (c + 1) * LANES]
                x = y[:, sl]
                lo, hi = bool(flags[:HEAD_DIM].all()), bool(flags[HEAD_DIM:].all())
                if not (lo or hi):
                    o_ref[:, sl] = x
                    continue
                partner = jnp.where(first_half,
                                    pltpu.roll(x, LANES - HEAD_DIM // 2, axis=1),
                                    pltpu.roll(x, HEAD_DIM // 2, axis=1))
                rotated = x * cos + partner * sin
                if lo and hi:
                    o_ref[:, sl] = rotated
                else:
                    is_rot = (lane < HEAD_DIM) if lo else (lane >= HEAD_DIM)
                    o_ref[:, sl] = jnp.where(is_rot, rotated, x)
            if chunks and tile == OFF_U // WIN_TILE:
                base = OFF_U % WIN_TILE
                for lb in range(U_SLOT // LANES):
                    st_ref[lb] = y[:, base + lb * LANES:base + (lb + 1) * LANES]
                _s5_rows_to_chunks(st_ref, u_ref)


def _win(h, g, w, cos_t, sin_t, layer, tm, chunks, row0=0):
    m = cos_t.shape[0]
    blk0 = row0 // tm
    cw = S5_CHUNK * SSM_GROUP
    out_specs = [pl.BlockSpec((tm, WIN_TILE), lambda i, j: (i, j))]
    out_shape = [jax.ShapeDtypeStruct((m, PROJ_COLS), F32)]
    scratch = [pltpu.VMEM((tm, D_MODEL), BF16)]
    if chunks:
        out_specs.append(pl.BlockSpec((SSM_GROUPS, tm // S5_CHUNK, cw), lambda i, j: (0, i, 0)))
        out_shape.append(jax.ShapeDtypeStruct((SSM_GROUPS, m // S5_CHUNK, cw), F32))
        scratch.append(pltpu.VMEM((U_SLOT // LANES, tm, LANES), F32))
    return pl.pallas_call(
        functools.partial(_win_kernel, chunks=chunks),
        grid=(m // tm, PROJ_COLS // WIN_TILE),
        in_specs=[
            pl.BlockSpec((tm, D_MODEL), lambda i, j: (i + blk0, 0)),
            pl.BlockSpec((None, 1, D_MODEL), lambda i, j: (layer, 0, 0)),
            pl.BlockSpec((D_MODEL, WIN_TILE), lambda i, j: (0, j)),
            pl.BlockSpec((tm, LANES), lambda i, j: (i, 0)),
            pl.BlockSpec((tm, LANES), lambda i, j: (i, 0)),
        ],
        out_specs=out_specs,
        out_shape=out_shape,
        scratch_shapes=scratch,
        compiler_params=_cparams(("parallel", "arbitrary")),
    )(h, _v3(g), w, cos_t, sin_t)


def _rope_tables(pos):
    half = HEAD_DIM // 2
    inv_freq = ROPE_THETA ** (-jnp.arange(half, dtype=F32) / half)
    ang = pos.astype(F32)[:, None] * inv_freq[None, :]
    cos, sin = jnp.cos(ang), jnp.sin(ang)
    reps = LANES // HEAD_DIM
    cos_t = jnp.tile(jnp.concatenate([cos, cos], axis=1), (1, reps))
    sin_t = jnp.tile(jnp.concatenate([-sin, sin], axis=1), (1, reps))
    return cos_t, sin_t


def _band_heads(q, kk, vv, valid, sinks):
    w = WINDOW
    heads = q.shape[1] // HEAD_DIM
    qs = jnp.concatenate([q[:, g * HEAD_DIM:(g + 1) * HEAD_DIM] for g in range(heads)], axis=0).astype(BF16)
    s = _dot_nt(kk, qs) * ATTN_SCALE
    s = jnp.where(jnp.concatenate([valid] * heads, axis=1), s, NEG_BIG)
    m = jnp.max(s, axis=0, keepdims=True)
    if sinks is not None:
        sink_row = jnp.concatenate([jnp.broadcast_to(sinks[g], (1, w)) for g in range(heads)], axis=1)
        m = jnp.maximum(m, sink_row)
    p = jnp.exp(s - m)
    l = jnp.sum(p, axis=0, keepdims=True)
    if sinks is not None:
        l = l + jnp.exp(sink_row - m)
    ot = _dot_tn(vv, p.astype(BF16)) / l
    lse = m + jnp.log(l)
    return ([ot[:, g * w:(g + 1) * w] for g in range(heads)], [lse[:, g * w:(g + 1) * w] for g in range(heads)])


def _untranspose(blocks):
    pairs = [jnp.concatenate(blocks[i:i + 2], axis=0).T for i in range(0, len(blocks), 2)]
    return jnp.concatenate(pairs, axis=1)


def _band_mask(no_prev):
    w = WINDOW
    k = lax.broadcasted_iota(jnp.int32, (2 * w, w), 0)
    q = lax.broadcasted_iota(jnp.int32, (2 * w, w), 1)
    valid = (k >= q) & (k <= q + w)
    if no_prev is False:
        return valid
    return valid & (k >= jnp.where(no_prev, w, 0))


def _swa_prompt_kernel(cur_ref, prev_ref, sink_ref, o_ref, *, nsub):
    w = WINDOW
    gw = SWA_GROUP * HEAD_DIM
    first = pl.program_id(1) == 0
    for j in range(nsub):
        valid = _band_mask(first if j == 0 else False)
        cur = cur_ref[0, j * w:(j + 1) * w, :]
        prev = prev_ref[0] if j == 0 else cur_ref[0, (j - 1) * w:j * w, SWA_WIDTH:B_SLOT]
        outs = []
        for h in range(SWA_KV_HEADS):
            ks = slice(SWA_WIDTH + h * HEAD_DIM, SWA_WIDTH + (h + 1) * HEAD_DIM)
            vs = slice(SWA_WIDTH + SWA_KV_WIDTH + h * HEAD_DIM, SWA_WIDTH + SWA_KV_WIDTH + (h + 1) * HEAD_DIM)
            kk = jnp.concatenate([prev[:, h * HEAD_DIM:(h + 1) * HEAD_DIM], cur[:, ks]], axis=0).astype(BF16)
            vv = jnp.concatenate([prev[:, SWA_KV_WIDTH + h * HEAD_DIM:SWA_KV_WIDTH + (h + 1) * HEAD_DIM],
                                  cur[:, vs]], axis=0).astype(BF16)
            sinks = [sink_ref[:, h * gw + g * HEAD_DIM:h * gw + g * HEAD_DIM + 1] for g in range(SWA_GROUP)]
            o, _ = _band_heads(cur[:, h * gw:(h + 1) * gw], kk, vv, valid, sinks)
            outs += o
        o_ref[0, j * w:(j + 1) * w, :] = _untranspose(outs)


def _swa_prompt(proj, sink):
    n, t, _ = proj.shape
    nsub = 2
    rows = WINDOW * nsub
    return pl.pallas_call(
        functools.partial(_swa_prompt_kernel, nsub=nsub),
        grid=(n, t // rows),
        in_specs=[
            pl.BlockSpec((1, rows, B_SLOT), lambda b, i: (b, i, OFF_B // B_SLOT)),
            pl.BlockSpec((1, WINDOW, 2 * SWA_KV_WIDTH),
                         lambda b, i: (b, jnp.maximum(i * nsub - 1, 0), (OFF_B + SWA_WIDTH) // (2 * SWA_KV_WIDTH))),
            pl.BlockSpec((1, SWA_WIDTH), lambda b, i: (0, 0)),
        ],
        out_specs=pl.BlockSpec((1, rows, SWA_WIDTH), lambda b, i: (b, i, 0)),
        out_shape=jax.ShapeDtypeStruct((n, t, SWA_WIDTH), F32),
        compiler_params=_cparams(("parallel", "arbitrary")),
    )(proj, proj, sink)


def _dil_prompt_kernel(cur_ref, prev_ref, o_ref, cur3_ref, out3_ref, *, dil, nsub):
    nl = C_SLOT // LANES
    span = WINDOW * dil
    first = pl.program_id(1) == 0
    for c in range(nl):
        cur3_ref[c] = cur_ref[0, :, c * LANES:(c + 1) * LANES]

    def one(j, r):
        def rows(base):
            return pl.ds(base + r, WINDOW, stride=dil) if dil > 1 else pl.ds(base, WINDOW)

        valid = _band_mask(first if j == 0 else False)
        cur = rows(j * span)
        q = jnp.concatenate([cur3_ref[0, cur, :], cur3_ref[1, cur, :]], axis=1)[:, 0:DIL_GW]
        kv = cur3_ref[2, cur, :]
        pkv = prev_ref[0, rows(0), :] if j == 0 else cur3_ref[2, rows((j - 1) * span), :]
        kk = jnp.concatenate([pkv[:, 0:HEAD_DIM], kv[:, 0:HEAD_DIM]], axis=0).astype(BF16)
        vv = jnp.concatenate([pkv[:, HEAD_DIM:], kv[:, HEAD_DIM:]], axis=0).astype(BF16)
        o, lse = _band_heads(q, kk, vv, valid, None)
        res = _untranspose(o + [jnp.broadcast_to(x, (HEAD_DIM, WINDOW)) for x in lse])
        for c in range(nl):
            out3_ref[c, cur, :] = res[:, c * LANES:(c + 1) * LANES]

    for j in range(nsub):
        if dil <= 4:
            for r in range(dil):
                one(j, r)
        else:
            lax.fori_loop(0, dil, lambda r, c, j=j: (one(j, r), c)[1], 0, unroll=2)
    for c in range(nl):
        o_ref[0, :, c * LANES:(c + 1) * LANES] = out3_ref[c]


def _dil_prompt(proj, group, dil):
    n, t, _ = proj.shape
    span = WINDOW * dil
    nsub = max(1, 4 // dil)
    rows = span * nsub
    slot = OFF_C + group * C_SLOT
    return pl.pallas_call(
        functools.partial(_dil_prompt_kernel, dil=dil, nsub=nsub),
        grid=(n, t // rows),
        in_specs=[
            pl.BlockSpec((1, rows, C_SLOT), lambda b, i: (b, i, slot // C_SLOT)),
            pl.BlockSpec((1, span, LANES), lambda b, i: (b, jnp.maximum(i * nsub - 1, 0), (slot + C_K) // LANES)),
        ],
        out_specs=pl.BlockSpec((1, rows, 2 * DIL_GW), lambda b, i: (b, i, 0)),
        out_shape=jax.ShapeDtypeStruct((n, t, 2 * DIL_GW), F32),
        scratch_shapes=[pltpu.VMEM((C_SLOT // LANES, rows, LANES), F32),
                        pltpu.VMEM((2 * DIL_GW // LANES, rows, LANES), F32)],
        compiler_params=_cparams(("parallel", "arbitrary")),
    )(proj, proj)


def _decode_heads(q, knew, vnew, kt, vt, dil, sinks):
    s = _dot(q.astype(BF16), kt.astype(BF16)) * ATTN_SCALE
    if dil > 1:
        row = lax.broadcasted_iota(jnp.int32, s.shape, 1)
        s = jnp.where(row % dil == 0, s, NEG_BIG)
    s_new = jnp.sum(q * knew, axis=-1, keepdims=True) * ATTN_SCALE
    m = jnp.maximum(jnp.max(s, axis=-1, keepdims=True), s_new)
    if sinks is not None:
        m = jnp.maximum(m, sinks)
    p = jnp.exp(s - m)
    p_new = jnp.exp(s_new - m)
    l = jnp.sum(p, axis=-1, keepdims=True) + p_new
    if sinks is not None:
        l = l + jnp.exp(sinks - m)
    o = (_dot_nt(p.astype(BF16), vt.astype(BF16)) + p_new * vnew) / l
    return o, m + jnp.log(l)


def _rows_of(x, heads):
    return jnp.concatenate([x[:, g * HEAD_DIM:(g + 1) * HEAD_DIM] for g in range(heads)], axis=0)


def _swa_sample_kernel(p_ref, cache_ref, sink_ref, o_ref, *, bn):
    for b in range(bn):
        row = p_ref[b:b + 1, :]
        outs = []
        for h in range(SWA_KV_HEADS):
            gw = SWA_GROUP * HEAD_DIM
            q = _rows_of(row[:, h * gw:(h + 1) * gw], SWA_GROUP)
            knew = row[:, SWA_WIDTH + h * HEAD_DIM:SWA_WIDTH + (h + 1) * HEAD_DIM]
            vnew = row[:, SWA_WIDTH + SWA_KV_WIDTH + h * HEAD_DIM:SWA_WIDTH + SWA_KV_WIDTH + (h + 1) * HEAD_DIM]
            sinks = _rows_of(sink_ref[:, h * gw:(h + 1) * gw], SWA_GROUP)[:, 0:1]
            o, _ = _decode_heads(q, knew, vnew, cache_ref[b, 0, h], cache_ref[b, 1, h], 1, sinks)
            outs += [o[g:g + 1] for g in range(SWA_GROUP)]
        o_ref[b:b + 1, :] = jnp.concatenate(outs, axis=1)


def _swa_sample(proj, cache, sink, layer):
    n = proj.shape[0]
    bn = 8
    return pl.pallas_call(
        functools.partial(_swa_sample_kernel, bn=bn),
        grid=(n // bn,),
        in_specs=[
            pl.BlockSpec((bn, B_SLOT), lambda i: (i, OFF_B // B_SLOT)),
            pl.BlockSpec((None, bn, 2, SWA_KV_HEADS, HEAD_DIM, WINDOW), lambda i: (layer, i, 0, 0, 0, 0)),
            pl.BlockSpec((1, SWA_WIDTH), lambda i: (0, 0)),
        ],
        out_specs=pl.BlockSpec((bn, SWA_WIDTH), lambda i: (i, 0)),
        out_shape=jax.ShapeDtypeStruct((n, SWA_WIDTH), F32),
        compiler_params=_cparams(("parallel",)),
    )(proj, cache, sink)


def _dil_sample_kernel(p_ref, cache_ref, o_ref, *, bn, dil):
    for b in range(bn):
        row = p_ref[b:b + 1, :]
        q = _rows_of(row[:, 0:DIL_GW], DIL_HPG)
        o, lse = _decode_heads(q, row[:, C_K:C_K + HEAD_DIM], row[:, C_K + HEAD_DIM:C_K + 2 * HEAD_DIM],
                               cache_ref[b, 0], cache_ref[b, 1], dil, None)
        lse = jnp.broadcast_to(lse, (DIL_HPG, HEAD_DIM))
        o_ref[b:b + 1, :] = jnp.concatenate([o[g:g + 1] for g in range(DIL_HPG)]
                                            + [lse[g:g + 1] for g in range(DIL_HPG)], axis=1)


def _dil_sample(proj, cache, group, dil, layer):
    n = proj.shape[0]
    bn = 8
    slot = OFF_C + group * C_SLOT
    return pl.pallas_call(
        functools.partial(_dil_sample_kernel, bn=bn, dil=dil),
        grid=(n // bn,),
        in_specs=[
            pl.BlockSpec((bn, C_SLOT), lambda i: (i, slot // C_SLOT)),
            pl.BlockSpec((None, bn, 2, HEAD_DIM, WINDOW * dil), lambda i: (layer, i, 0, 0, 0)),
        ],
        out_specs=pl.BlockSpec((bn, 2 * DIL_GW), lambda i: (i, 0)),
        out_shape=jax.ShapeDtypeStruct((n, 2 * DIL_GW), F32),
        compiler_params=_cparams(("parallel",)),
    )(proj, cache)


def _s5_param_kernel(are_ref, aim_ref, ldt_ref, bre_ref, bim_ref, cre_ref, cim_ref,
                     lam_re_ref, lam_im_ref, cl_re_ref, cl_im_ref, r_re_ref, r_im_ref, k_ref):
    ar = are_ref[0]
    ai = aim_ref[0]
    dt = jnp.exp(ldt_ref[0])
    nrow = lam_re_ref.shape[1]
    ri = lax.broadcasted_iota(jnp.int32, (nrow, 1), 0)
    pw = jnp.where(ri < S5_NPOW, ri, jnp.left_shift(S5_CHUNK, jnp.maximum(ri - S5_NPOW, 0))).astype(F32)
    mag = jnp.exp(pw * (ar * dt))
    ang = pw * (ai * dt)
    lam_re = mag * jnp.cos(ang)
    lam_im = mag * jnp.sin(ang)
    lam_re_ref[0] = lam_re
    lam_im_ref[0] = lam_im
    l1r, l1i = lam_re[1:2], lam_im[1:2]
    den = ar * ar + ai * ai
    z_re = ((l1r - 1.0) * ar + l1i * ai) / den
    z_im = (l1i * ar - (l1r - 1.0) * ai) / den
    b_re, b_im = bre_ref[0], bim_ref[0]
    bb_re = z_re * b_re - z_im * b_im
    bb_im = z_re * b_im + z_im * b_re
    c_re, c_im = cre_ref[0], cim_ref[0]
    cls_re, cls_im = [], []
    for d in range(S5_NPOW):
        lr, li = lam_re[d:d + 1], lam_im[d:d + 1]
        cr = c_re * lr - c_im * li
        ci = c_re * li + c_im * lr
        cl_re_ref[0, d * SSM_GROUP:(d + 1) * SSM_GROUP, :] = cr
        cl_im_ref[0, d * SSM_GROUP:(d + 1) * SSM_GROUP, :] = ci
        if d < S5_CHUNK:
            cls_re.append(cr)
            cls_im.append(ci)
    for s in range(S5_CHUNK):
        lr = lam_re[S5_CHUNK - 1 - s:S5_CHUNK - s]
        li = lam_im[S5_CHUNK - 1 - s:S5_CHUNK - s]
        r_re_ref[0, s * SSM_GROUP:(s + 1) * SSM_GROUP, :] = lr * bb_re - li * bb_im
        r_im_ref[0, s * SSM_GROUP:(s + 1) * SSM_GROUP, :] = lr * bb_im + li * bb_re
    call_re = jnp.concatenate(cls_re, axis=0)
    call_im = jnp.concatenate(cls_im, axis=0)
    hp = lax.Precision.HIGHEST
    nt = (((1,), (1,)), ((), ()))
    kt = (lax.dot_general(bb_re, call_re, nt, precision=hp, preferred_element_type=F32)
          - lax.dot_general(bb_im, call_im, nt, precision=hp, preferred_element_type=F32))
    cw = S5_CHUNK * SSM_GROUP
    for s in range(S5_CHUNK):
        if s == 0:
            blk = kt
        else:
            blk = jnp.concatenate([jnp.zeros((SSM_GROUP, s * SSM_GROUP), F32), kt[:, :cw - s * SSM_GROUP]], axis=1)
        k_ref[0, s * SSM_GROUP:(s + 1) * SSM_GROUP, :] = blk


def _s5_params(a_re, a_im, log_dt, b_re, b_im, c_re, c_im, nsteps):
    g = SSM_GROUPS
    nrow = ((S5_NPOW + nsteps + 7) // 8) * 8
    row = lambda i: (i, 0, 0)
    shp = lambda r, c: jax.ShapeDtypeStruct((g, r, c), F32)
    ncl = S5_NPOW * SSM_GROUP
    nr = S5_CHUNK * SSM_GROUP
    return pl.pallas_call(
        _s5_param_kernel,
        grid=(g,),
        in_specs=[
            pl.BlockSpec((1, 1, SSM_STATE), row),
            pl.BlockSpec((1, 1, SSM_STATE), row),
            pl.BlockSpec((1, 1, 1), row),
            pl.BlockSpec((1, SSM_GROUP, SSM_STATE), row),
            pl.BlockSpec((1, SSM_GROUP, SSM_STATE), row),
            pl.BlockSpec((1, SSM_GROUP, SSM_STATE), row),
            pl.BlockSpec((1, SSM_GROUP, SSM_STATE), row),
        ],
        out_specs=[
            pl.BlockSpec((1, nrow, SSM_STATE), row), pl.BlockSpec((1, nrow, SSM_STATE), row),
            pl.BlockSpec((1, ncl, SSM_STATE), row), pl.BlockSpec((1, ncl, SSM_STATE), row),
            pl.BlockSpec((1, nr, SSM_STATE), row), pl.BlockSpec((1, nr, SSM_STATE), row),
            pl.BlockSpec((1, nr, nr), row),
        ],
        out_shape=[shp(nrow, SSM_STATE), shp(nrow, SSM_STATE), shp(ncl, SSM_STATE), shp(ncl, SSM_STATE),
                   shp(nr, SSM_STATE), shp(nr, SSM_STATE), shp(nr, nr)],
        compiler_params=_cparams(("parallel",)),
    )(a_re.reshape(g, 1, SSM_STATE), a_im.reshape(g, 1, SSM_STATE), log_dt.reshape(g, 1, 1),
      jnp.swapaxes(b_re, 1, 2), jnp.swapaxes(b_im, 1, 2), c_re, c_im)


def _s5_rows_to_chunks(st_ref, u_ref):
    nc = st_ref.shape[1] // S5_CHUNK
    per_blk = LANES // SSM_GROUP
    slot = lax.broadcasted_iota(jnp.int32, (nc, LANES), 1) // SSM_GROUP
    for g in range(SSM_GROUPS):
        acc = [jnp.zeros((nc, LANES), F32) for _ in range(S5_CHUNK // per_blk)]
        for s in range(S5_CHUNK):
            v = st_ref[g // per_blk, pl.ds(s, nc, stride=S5_CHUNK), :]
            shift = (SSM_GROUP * (s % per_blk) - SSM_GROUP * (g % per_blk)) % LANES
            if shift:
                v = pltpu.roll(v, shift, axis=1)
            acc[s // per_blk] = jnp.where(slot == s % per_blk, v, acc[s // per_blk])
        for k, a in enumerate(acc):
            u_ref[g, :, k * LANES:(k + 1) * LANES] = a


def _s5_chunks_to_rows(y_ref, st_ref):
    nc = y_ref.shape[1]
    per_blk = LANES // SSM_GROUP
    slot = lax.broadcasted_iota(jnp.int32, (nc, LANES), 1) // SSM_GROUP
    nblk = -(-SSM_WIDTH // LANES)
    for lb in range(nblk):
        groups = range(lb * per_blk, min((lb + 1) * per_blk, SSM_GROUPS))
        for s in range(S5_CHUNK):
            acc = jnp.zeros((nc, LANES), F32)
            for g in groups:
                v = y_ref[g, :, (s // per_blk) * LANES:(s // per_blk + 1) * LANES]
                shift = (SSM_GROUP * (g % per_blk) - SSM_GROUP * (s % per_blk)) % LANES
                if shift:
                    v = pltpu.roll(v, shift, axis=1)
                acc = jnp.where(slot == g % per_blk, v, acc)
            st_ref[lb, pl.ds(s, nc, stride=S5_CHUNK), :] = acc
    return jnp.concatenate([st_ref[lb][:, 0:min(LANES, SSM_WIDTH - lb * LANES)] for lb in range(nblk)], axis=1)


def _s5_prompt_kernel(u_ref, mt_ref, r_re_ref, r_im_ref, cl_re_ref, cl_im_ref, lam_re_ref, lam_im_ref, d_ref,
                      y_ref, xre_ref, xim_ref, *, nseq, nchunk):
    u = u_ref[0]
    ub = u.astype(BF16)
    y = _dot(ub, mt_ref[0].astype(BF16))
    xr = _dot(ub, r_re_ref[0].astype(BF16))
    xi = _dot(ub, r_im_ref[0].astype(BF16))
    rows = nseq * nchunk
    cidx = lax.broadcasted_iota(jnp.int32, (rows, SSM_STATE), 0) % nchunk
    lam_re, lam_im = lam_re_ref[0], lam_im_ref[0]
    for k in range(int(math.log2(nchunk))):
        sh = 1 << k
        lr = lam_re[S5_NPOW + k:S5_NPOW + k + 1]
        li = lam_im[S5_NPOW + k:S5_NPOW + k + 1]
        ok = cidx >= sh
        pr = jnp.where(ok, pltpu.roll(xr, sh, axis=0), 0.0)
        pi = jnp.where(ok, pltpu.roll(xi, sh, axis=0), 0.0)
        xr, xi = xr + lr * pr - li * pi, xi + lr * pi + li * pr
    ok = cidx >= 1
    er = jnp.where(ok, pltpu.roll(xr, 1, axis=0), 0.0)
    ei = jnp.where(ok, pltpu.roll(xi, 1, axis=0), 0.0)
    cl_re = cl_re_ref[0][SSM_GROUP:, :]
    cl_im = cl_im_ref[0][SSM_GROUP:, :]
    y = y + _dot_nt(er.astype(BF16), cl_re.astype(BF16)) - _dot_nt(ei.astype(BF16), cl_im.astype(BF16))
    y_ref[0] = y + d_ref[0] * u
    last = [s * nchunk + nchunk - 1 for s in range(nseq)]
    xre_ref[0] = jnp.concatenate([xr[i:i + 1] for i in last], axis=0)
    xim_ref[0] = jnp.concatenate([xi[i:i + 1] for i in last], axis=0)


def _s5_prompt(u, mt, r_re, r_im, cl_re, cl_im, lam_re, lam_im, d_t, nseq, nchunk):
    g = SSM_GROUPS
    rows = nseq * nchunk
    cw = S5_CHUNK * SSM_GROUP
    row = lambda i: (i, 0, 0)
    full = lambda a: pl.BlockSpec((1,) + a.shape[1:], row)
    return pl.pallas_call(
        functools.partial(_s5_prompt_kernel, nseq=nseq, nchunk=nchunk),
        grid=(g,),
        in_specs=[full(u), full(mt), full(r_re), full(r_im), full(cl_re), full(cl_im), full(lam_re), full(lam_im),
                  full(d_t)],
        out_specs=[pl.BlockSpec((1, rows, cw), row), pl.BlockSpec((1, nseq, SSM_STATE), row),
                   pl.BlockSpec((1, nseq, SSM_STATE), row)],
        out_shape=[jax.ShapeDtypeStruct((g, rows, cw), F32), jax.ShapeDtypeStruct((g, nseq, SSM_STATE), F32),
                   jax.ShapeDtypeStruct((g, nseq, SSM_STATE), F32)],
        compiler_params=_cparams(("parallel",)),
    )(u, mt, r_re, r_im, cl_re, cl_im, lam_re, lam_im, d_t)


def _s5_sample_kernel(u_ref, x0r_ref, x0i_ref, r_re_ref, r_im_ref, cl_re_ref, cl_im_ref, lam_re_ref, lam_im_ref,
                      d_ref, y_ref, xre_ref, xim_ref):
    u = u_ref[0]
    ub = u.astype(BF16)
    lo = (S5_CHUNK - 1) * SSM_GROUP
    bb_re = r_re_ref[0][lo:lo + SSM_GROUP, :]
    bb_im = r_im_ref[0][lo:lo + SSM_GROUP, :]
    lr, li = lam_re_ref[0][1:2], lam_im_ref[0][1:2]
    x0r, x0i = x0r_ref[0], x0i_ref[0]
    xr = _dot(ub, bb_re.astype(BF16)) + (lr * x0r - li * x0i)
    xi = _dot(ub, bb_im.astype(BF16)) + (lr * x0i + li * x0r)
    c_re = cl_re_ref[0][:SSM_GROUP, :]
    c_im = cl_im_ref[0][:SSM_GROUP, :]
    y_ref[0] = (_dot_nt(xr.astype(BF16), c_re.astype(BF16)) - _dot_nt(xi.astype(BF16), c_im.astype(BF16))
                + d_ref[0] * u)
    xre_ref[0] = xr
    xim_ref[0] = xi


def _s5_sample(u, x0r, x0i, r_re, r_im, cl_re, cl_im, lam_re, lam_im, d):
    g, n = u.shape[0], u.shape[1]
    row = lambda i: (i, 0, 0)
    full = lambda a: pl.BlockSpec((1,) + a.shape[1:], row)
    return pl.pallas_call(
        _s5_sample_kernel,
        grid=(g,),
        in_specs=[full(u), full(x0r), full(x0i), full(r_re), full(r_im), full(cl_re), full(cl_im), full(lam_re),
                  full(lam_im), full(d)],
        out_specs=[pl.BlockSpec((1, n, SSM_GROUP), row), pl.BlockSpec((1, n, SSM_STATE), row),
                   pl.BlockSpec((1, n, SSM_STATE), row)],
        out_shape=[jax.ShapeDtypeStruct((g, n, SSM_GROUP), F32), jax.ShapeDtypeStruct((g, n, SSM_STATE), F32),
                   jax.ShapeDtypeStruct((g, n, SSM_STATE), F32)],
        compiler_params=_cparams(("parallel",)),
    )(u, x0r, x0i, r_re, r_im, cl_re, cl_im, lam_re, lam_im, d)


def _log_sigmoid(x):
    return jnp.minimum(x, 0.0) - jnp.log1p(jnp.exp(-jnp.abs(x)))


def _logaddexp(a, b):
    return jnp.maximum(a, b) + jnp.log1p(jnp.exp(-jnp.abs(a - b)))


def _hgrn_gates(fpre, lb):
    log_f = _logaddexp(jnp.log(jnp.maximum(lb, LB_FLOOR)), jnp.log1p(-lb) + _log_sigmoid(fpre))
    k = (1.0 - lb) * _sigmoid(-fpre)
    return log_f, k


def _hgrn_scan_matrix():
    c = HGRN_CHUNK
    tri = np.tril(np.ones((c, c), np.float32))
    t = np.arange(c)
    mats = [tri]
    for m in HGRN_LEVELS:
        mats.append(tri[(t // (2 * m)) * (2 * m) + m - 1])
    return np.concatenate(mats, axis=0)


def _hgrn_prompt_kernel(p_ref, lbraw_ref, gn_ref, scan_ref, o_ref, s_ref, st_ref, *, layer):
    cidx = pl.program_id(1)
    c = HGRN_CHUNK
    hd = HGRN_HEAD_DIM

    @pl.when(cidx == 0)
    def _():
        st_ref[...] = jnp.zeros_like(st_ref)

    a = lbraw_ref[...]
    e = jnp.exp(a - jnp.max(a, axis=0, keepdims=True))
    p = e / jnp.sum(e, axis=0, keepdims=True)
    run = p[0:1]
    for l in range(1, layer + 1):
        run = run + p[l:l + 1]
    lb_all = run - p[0:1]

    r = lax.broadcasted_iota(jnp.int32, (c, c), 0)
    s = lax.broadcasted_iota(jnp.int32, (c, c), 1)
    pairs = []
    for m in HGRN_LEVELS:
        sh = int(math.log2(2 * m))
        pairs.append(((r >> sh) == (s >> sh)) & ((r & m) != 0) & ((s & m) == 0))
    scan = scan_ref[...]
    for h in range(HGRN_HEADS):
        col = lambda part: p_ref[0, :, part * HGRN_WIDTH + h * hd:part * HGRN_WIDTH + (h + 1) * hd]
        log_f, k = _hgrn_gates(col(1), lb_all[:, h * hd:(h + 1) * hd])
        q = _silu(col(0))
        v = col(2)
        hi = log_f.astype(BF16)
        rest = log_f - hi.astype(F32)
        mid = rest.astype(BF16)
        lo = (rest - mid.astype(F32)).astype(BF16)
        cum3 = _dot(scan, jnp.concatenate([hi, mid, lo], axis=1))
        cum = cum3[:, 0:hd] + cum3[:, hd:2 * hd] + cum3[:, 2 * hd:3 * hd]
        b = cum[0:c]
        att = jnp.where(r == s, jnp.sum(q * k, axis=-1, keepdims=True), 0.0)
        for lvl, m in enumerate(HGRN_LEVELS):
            bref = cum[(lvl + 1) * c:(lvl + 2) * c]
            qs = q * jnp.exp(jnp.minimum(b - bref, 0.0))
            ks = k * jnp.exp(jnp.minimum(bref - b, 0.0))
            att = att + jnp.where(pairs[lvl], _dot_nt(qs.astype(BF16), ks.astype(BF16)), 0.0)
        st = st_ref[h]
        vb = v.astype(BF16)
        o = _dot(att.astype(BF16), vb) + _dot_nt((q * jnp.exp(b)).astype(BF16), st.astype(BF16))
        bl = b[c - 1:c]
        st_new = jnp.exp(bl) * st + _dot_tn(vb, (k * jnp.exp(bl - b)).astype(BF16))
        st_ref[h] = st_new
        o = o * lax.rsqrt(jnp.mean(o * o, axis=-1, keepdims=True) + NORM_EPS)
        o_ref[0, :, h * hd:(h + 1) * hd] = o * gn_ref[:, h * hd:(h + 1) * hd] * _silu(col(3))

    @pl.when(cidx == pl.num_programs(1) - 1)
    def _():
        s_ref[0] = st_ref[...]


def _hgrn_prompt(proj, lbraw, gn, layer):
    n, t, _ = proj.shape
    c = HGRN_CHUNK
    hd = HGRN_HEAD_DIM
    scan = jnp.asarray(_hgrn_scan_matrix(), dtype=BF16)
    return pl.pallas_call(
        functools.partial(_hgrn_prompt_kernel, layer=layer),
        grid=(n, t // c),
        in_specs=[pl.BlockSpec((1, c, 4 * HGRN_WIDTH), lambda b, j: (b, j, OFF_D // (4 * HGRN_WIDTH))),
                  pl.BlockSpec((DEPTH, HGRN_WIDTH), lambda b, j: (0, 0)),
                  pl.BlockSpec((None, 1, HGRN_WIDTH), lambda b, j: (layer, 0, 0)),
                  pl.BlockSpec(scan.shape, lambda b, j: (0, 0))],
        out_specs=[pl.BlockSpec((1, c, HGRN_WIDTH), lambda b, j: (b, j, 0)),
                   pl.BlockSpec((1, HGRN_HEADS, hd, hd), lambda b, j: (b, 0, 0, 0))],
        out_shape=[jax.ShapeDtypeStruct((n, t, HGRN_WIDTH), F32),
                   jax.ShapeDtypeStruct((n, HGRN_HEADS, hd, hd), F32)],
        scratch_shapes=[pltpu.VMEM((HGRN_HEADS, hd, hd), F32)],
        compiler_params=_cparams(("parallel", "arbitrary")),
    )(proj, lbraw, _v3(gn), scan)


def _hgrn_sample_kernel(qt_ref, ft_ref, lbt_ref, v_ref, g_ref, gn_ref, s0_ref, o_ref, s_ref, *, layer, n):
    hd = HGRN_HEAD_DIM
    a = lbt_ref[0]
    e = jnp.exp(a - jnp.max(a, axis=1, keepdims=True))
    p = e / jnp.sum(e, axis=1, keepdims=True)
    run = p[:, 0:1]
    for l in range(1, layer + 1):
        run = run + p[:, l:l + 1]
    lb = run - p[:, 0:1]
    log_f, k = _hgrn_gates(ft_ref[0], lb)
    f = jnp.exp(log_f)
    q = _silu(qt_ref[0])
    v = v_ref[...]
    gate = g_ref[...]
    rows = []
    for b in range(n):
        s_new = f[:, b:b + 1] * s0_ref[b, 0] + k[:, b:b + 1] * v[b:b + 1, :]
        s_ref[b, 0] = s_new
        rows.append(jnp.sum(q[:, b:b + 1] * s_new, axis=0, keepdims=True))
    o = jnp.concatenate(rows, axis=0)
    o = o * lax.rsqrt(jnp.mean(o * o, axis=-1, keepdims=True) + NORM_EPS)
    o_ref[...] = o * gn_ref[...] * _silu(gate)


def _hgrn_sample(qt, ft, lbt, proj, gn, s0, layer):
    n = proj.shape[0]
    hd = HGRN_HEAD_DIM
    v_blk = (OFF_D + 2 * HGRN_WIDTH) // hd
    g_blk = (OFF_D + 3 * HGRN_WIDTH) // hd
    return pl.pallas_call(
        functools.partial(_hgrn_sample_kernel, layer=layer, n=n),
        grid=(HGRN_HEADS,),
        in_specs=[pl.BlockSpec((1, hd, n), lambda h: (h, 0, 0)),
                  pl.BlockSpec((1, hd, n), lambda h: (h, 0, 0)),
                  pl.BlockSpec((1, hd, DEPTH), lambda h: (h, 0, 0)),
                  pl.BlockSpec((n, hd), lambda h: (0, v_blk + h)),
                  pl.BlockSpec((n, hd), lambda h: (0, g_blk + h)),
                  pl.BlockSpec((None, 1, hd), lambda h: (layer, 0, h)),
                  pl.BlockSpec((None, n, 1, hd, hd), lambda h: (layer, 0, h, 0, 0))],
        out_specs=[pl.BlockSpec((n, hd), lambda h: (0, h)),
                   pl.BlockSpec((n, 1, hd, hd), lambda h: (0, h, 0, 0))],
        out_shape=[jax.ShapeDtypeStruct((n, HGRN_WIDTH), F32),
                   jax.ShapeDtypeStruct((n, HGRN_HEADS, hd, hd), F32)],
        compiler_params=_cparams(("parallel",)),
    )(qt, ft, lbt, proj, proj, _v3(gn), s0)


def _gelu_tanh(x):
    return 0.5 * x * (1.0 + jnp.tanh(math.sqrt(2.0 / math.pi) * (x + 0.044715 * (x * x * x))))


def _mixout_kernel(h_ref, ya_ref, ob_ref, oc0_ref, oc1_ref, oc2_ref, od_ref,
                   wglu_ref, bglu_ref, ga_ref, gb_ref, gc_ref, wo_ref, gpost_ref, *rest, chunked, into):
    rest = list(rest)
    if into:
        rest.pop(0)
    o_ref = rest[0]
    ya = _s5_chunks_to_rows(ya_ref, rest[1]) if chunked else ya_ref[...]
    z = _gelu_tanh(ya)
    gate = _sigmoid(_dot(z.astype(BF16), wglu_ref[...].astype(BF16)) + bglu_ref[...])
    out_a = _rms(z * gate, ga_ref[...])
    out_b = _rms(ob_ref[...], gb_ref[...])
    ocs = [oc0_ref, oc1_ref, oc2_ref]
    ls = [r[:, DIL_GW:2 * DIL_GW] for r in ocs]
    m = jnp.maximum(jnp.maximum(ls[0], ls[1]), ls[2])
    es = [jnp.exp(l - m) for l in ls]
    den = es[0] + es[1] + es[2]
    cs = [ocs[gi][:, 0:DIL_GW] * (es[gi] / den) for gi in range(3)]
    ss = sum(jnp.sum(c * c, axis=-1, keepdims=True) for c in cs)
    inv = lax.rsqrt(ss / DIL_WIDTH + NORM_EPS)
    gc = gc_ref[...]
    o1, o2, o3 = SSM_WIDTH, SSM_WIDTH + SWA_WIDTH, SSM_WIDTH + SWA_WIDTH + DIL_WIDTH
    wo = lambda lo, hi: wo_ref[lo:hi, :]
    mix = _dot(out_a.astype(BF16), wo(0, o1)) + _dot(out_b.astype(BF16), wo(o1, o2))
    for gi in range(3):
        c = cs[gi] * inv * gc[:, gi * DIL_GW:(gi + 1) * DIL_GW]
        mix = mix + _dot(c.astype(BF16), wo(o2 + gi * DIL_GW, o2 + (gi + 1) * DIL_GW))
    mix = mix + _dot(od_ref[...].astype(BF16), wo(o3, D_MODEL))
    o_ref[...] = h_ref[...] + _rms(mix, gpost_ref[...])


def _mixout(h, ya, ob, ocs, od, lp_all, w_out_l, layer, tm, row0=0, into=None):
    m = ob.shape[0]
    blk0 = row0 // tm
    rowblk = lambda a: pl.BlockSpec((tm, a.shape[1]), lambda i: (i, 0))
    vec = lambda a: pl.BlockSpec((None, 1, a.shape[1]), lambda i: (layer, 0, 0))
    mat = lambda a: pl.BlockSpec((None,) + a.shape[1:], lambda i: (layer, 0, 0))
    whole = lambda a: pl.BlockSpec(a.shape, lambda i: (0, 0), pipeline_mode=pl.Buffered(1))
    acts = [h, ya, ob, *ocs, od]
    names = ['ssm_w_glu', 'ssm_b_glu', 'out_norm_a', 'out_norm_b', 'out_norm_c', 'w_out', 'mix_norm_post']
    params = [w_out_l if k == 'w_out' else lp_all[k] for k in names]
    spec = lambda k, p: whole(p) if k == 'w_out' else (mat(p) if p.ndim == 3 else vec(p))
    act_specs = [rowblk(a) for a in acts]
    act_specs[0] = pl.BlockSpec((tm, D_MODEL), lambda i: (i + blk0, 0))
    scratch = []
    chunked = ya.ndim == 3
    if chunked:
        act_specs[1] = pl.BlockSpec((SSM_GROUPS, tm // S5_CHUNK, ya.shape[2]), lambda i: (0, i, 0))
        scratch = [pltpu.VMEM((-(-SSM_WIDTH // LANES), tm, LANES), F32)]
    args = [*acts, *[_v3(p) if p.ndim == 2 and k != 'w_out' else p for k, p in zip(names, params)]]
    in_specs = act_specs + [spec(k, p) for k, p in zip(names, params)]
    aliases = {}
    if into is not None:
        aliases = {len(args): 0}
        args.append(into)
        in_specs.append(pl.BlockSpec(memory_space=pl.ANY))
    return pl.pallas_call(
        functools.partial(_mixout_kernel, chunked=chunked, into=into is not None),
        grid=(m // tm,),
        in_specs=in_specs,
        out_specs=pl.BlockSpec((tm, D_MODEL), lambda i: (i + blk0, 0)),
        out_shape=jax.ShapeDtypeStruct((h.shape[0], D_MODEL), F32),
        input_output_aliases=aliases,
        scratch_shapes=scratch,
        compiler_params=_cparams(("parallel",)),
    )(*args)


def _ffn_block(x, x_tail, P, name, layer, tm, n_main, n_tail, tail_out):
    return _ffn(x, x_tail, P[name + '_norm_pre'], P[name + '_w_gate'], P[name + '_w_up'], P[name + '_w_down'],
                P[name + '_norm_post'], layer, tm, n_main, n_tail, tail_out)


def _prompt_mixer(h, P, w_in_l, w_out_l, s5p, layer, n, t, rope, tm):
    m = n * t
    proj2, u = _win(h, P['mix_norm_pre'], w_in_l, rope[0], rope[1], layer, tm, True)
    proj = proj2.reshape(n, t, PROJ_COLS)

    lam_re, lam_im, cl_re, cl_im, r_re, r_im, kmat = s5p
    nchunk = t // S5_CHUNK
    d_t = jnp.tile(P['ssm_d'][layer], (1, S5_CHUNK)).reshape(SSM_GROUPS, 1, S5_CHUNK * SSM_GROUP)
    y, xre, xim = _s5_prompt(u, kmat, r_re, r_im, cl_re, cl_im, lam_re, lam_im, d_t, n, nchunk)
    ssm_new = jnp.stack([xre, xim], axis=-1).transpose(1, 0, 2, 3)

    sink = jnp.repeat(P['swa_sinks'][layer], HEAD_DIM).reshape(1, SWA_WIDTH)
    ob = _swa_prompt(proj, sink).reshape(m, SWA_WIDTH)
    keep = min(WINDOW, t)
    kv = proj[:, t - keep:, OFF_B + SWA_WIDTH:OFF_B + B_SLOT]
    swa_new = kv.reshape(n, keep, 2, SWA_KV_HEADS, HEAD_DIM)

    ocs, dil_new = [], []
    for gi, (win, dil) in enumerate(DIL_PAIRS):
        ocs.append(_dil_prompt(proj, gi, dil).reshape(m, 2 * DIL_GW))
        keep = min(win, t)
        lo = OFF_C + gi * C_SLOT + C_K
        dil_new.append(proj[:, t - keep:, lo:lo + 2 * HEAD_DIM].reshape(n, keep, 2, HEAD_DIM))

    od, st = _hgrn_prompt(proj, P['hgrn_lower_bounds'], P['out_norm_d'], layer)
    hgrn_new = jnp.swapaxes(st, -1, -2)

    h = _mixout(h, y, ob, ocs, od.reshape(m, HGRN_WIDTH), P, w_out_l, layer, min(tm, 512))
    return h, (ssm_new, swa_new, dil_new[0], dil_new[1], dil_new[2], hgrn_new)


def _sample_mixer(h, row0, into, P, w_in_l, w_out_l, s5p, layer, caches, rope):
    n = rope[0].shape[0]
    state_ssm, cache_swa, cache_d0, cache_d1, cache_d2, state_hgrn = caches
    (proj,) = _win(h, P['mix_norm_pre'], w_in_l, rope[0], rope[1], layer, n, False, row0)

    lam_re, lam_im, cl_re, cl_im, r_re, r_im, _ = s5p
    u = proj[:, OFF_U:OFF_U + SSM_WIDTH].reshape(n, SSM_GROUPS, SSM_GROUP).transpose(1, 0, 2)
    x0 = state_ssm[layer].transpose(1, 0, 2, 3)
    y, xre, xim = _s5_sample(u, x0[..., 0], x0[..., 1], r_re, r_im, cl_re, cl_im, lam_re, lam_im,
                             P['ssm_d'][layer].reshape(SSM_GROUPS, 1, SSM_GROUP))
    ya = y.transpose(1, 0, 2).reshape(n, SSM_WIDTH)
    ssm_new = jnp.stack([xre, xim], axis=-1).transpose(1, 0, 2, 3)

    sink = jnp.repeat(P['swa_sinks'][layer], HEAD_DIM).reshape(1, SWA_WIDTH)
    ob = _swa_sample(proj, cache_swa.transpose(0, 1, 3, 4, 5, 2), sink, layer)
    swa_new = proj[:, OFF_B + SWA_WIDTH:OFF_B + B_SLOT].reshape(n, 1, 2, SWA_KV_HEADS, HEAD_DIM)

    ocs, dil_new = [], []
    for gi, (buf, (win, dil)) in enumerate(zip((cache_d0, cache_d1, cache_d2), DIL_PAIRS)):
        ocs.append(_dil_sample(proj, buf.transpose(0, 1, 3, 4, 2), gi, dil, layer))
        lo = OFF_C + gi * C_SLOT + C_K
        dil_new.append(proj[:, lo:lo + 2 * HEAD_DIM].reshape(n, 1, 2, HEAD_DIM))

    part = lambda i: proj[:, OFF_D + i * HGRN_WIDTH:OFF_D + (i + 1) * HGRN_WIDTH].reshape(n, HGRN_HEADS, HGRN_HEAD_DIM)
    cm = lambda a: a.transpose(1, 2, 0)
    lbt = P['hgrn_lower_bounds'].reshape(DEPTH, HGRN_HEADS, HGRN_HEAD_DIM).transpose(1, 2, 0)
    od, hgrn_new = _hgrn_sample(cm(part(0)), cm(part(1)), lbt, proj, P['out_norm_d'], state_hgrn, layer)

    h = _mixout(h, ya, ob, ocs, od, P, w_out_l, layer, n, row0, into)
    return h, (ssm_new, swa_new, dil_new[0], dil_new[1], dil_new[2], hgrn_new)


def _forward(x_prompt, x_sample, caches, P):
    n, t, _ = x_prompt.shape
    ns, ts, _ = x_sample.shape
    assert ts == 1
    tm = 1024
    m_p = n * t
    rope_p = _rope_tables(jnp.tile(jnp.arange(t, dtype=jnp.int32), n))
    rope_s = _rope_tables(jnp.full((ns,), PAST_LEN, dtype=jnp.int32))
    nsteps = int(math.log2(t // S5_CHUNK))
    h = x_prompt.reshape(m_p, D_MODEL)
    h_tail = x_sample.reshape(ns, D_MODEL)
    new_p = [[] for _ in range(6)]
    new_s = [[] for _ in range(6)]
    for l in range(DEPTH):
        w_in_l = _win_prep(P['w_in'], l)
        w_out_l = _cast_bf16(P['w_out'], l)
        s5p = _s5_params(P['ssm_a_re'][l], P['ssm_a_im'][l], P['ssm_log_dt'][l], P['ssm_b_re'][l], P['ssm_b_im'][l],
                         P['ssm_c_re'][l], P['ssm_c_im'][l], nsteps)
        h = _ffn_block(h, h_tail, P, 'ffn1', l, FFN_ROWS, m_p, ns, False)
        h_tail = None
        mixed, st_p = _prompt_mixer(h, P, w_in_l, w_out_l, s5p, l, n, t, rope_p, tm)
        h, st_s = _sample_mixer(h, m_p, mixed, P, w_in_l, w_out_l, s5p, l, caches, rope_s)
        h = _ffn_block(h, None, P, 'ffn2', l, FFN_ROWS, m_p, ns, l == DEPTH - 1)
        for i in range(6):
            new_p[i].append(st_p[i])
            new_s[i].append(st_s[i])
    y_p, y_s = h
    new_p = [jnp.stack(a, axis=0) for a in new_p]
    new_s = [jnp.stack(a, axis=0) for a in new_s]
    return (y_p.reshape(n, t, D_MODEL), y_s.reshape(ns, ts, D_MODEL), new_p[0], new_s[0], new_p[1], new_s[1],
            new_p[2], new_s[2], new_p[3], new_s[3], new_p[4], new_s[4], new_p[5], new_s[5])


def kernel(x_prompt, x_sample, state_ssm, cache_swa_kv, cache_dil0_kv, cache_dil1_kv, cache_dil2_kv, state_hgrn, ffn1_norm_pre, ffn1_w_gate, ffn1_w_up, ffn1_w_down, ffn1_norm_post, mix_norm_pre, w_in, ssm_a_re, ssm_a_im, ssm_log_dt, ssm_b_re, ssm_b_im, ssm_c_re, ssm_c_im, ssm_d, ssm_w_glu, ssm_b_glu, swa_sinks, hgrn_lower_bounds, out_norm_a, out_norm_b, out_norm_c, out_norm_d, w_out, mix_norm_post, ffn2_norm_pre, ffn2_w_gate, ffn2_w_up, ffn2_w_down, ffn2_norm_post):
    P = dict(ffn1_norm_pre=ffn1_norm_pre, ffn1_w_gate=ffn1_w_gate, ffn1_w_up=ffn1_w_up, ffn1_w_down=ffn1_w_down,
             ffn1_norm_post=ffn1_norm_post, mix_norm_pre=mix_norm_pre, w_in=w_in, ssm_a_re=ssm_a_re,
             ssm_a_im=ssm_a_im, ssm_log_dt=ssm_log_dt, ssm_b_re=ssm_b_re, ssm_b_im=ssm_b_im, ssm_c_re=ssm_c_re,
             ssm_c_im=ssm_c_im, ssm_d=ssm_d, ssm_w_glu=ssm_w_glu, ssm_b_glu=ssm_b_glu, swa_sinks=swa_sinks,
             hgrn_lower_bounds=hgrn_lower_bounds, out_norm_a=out_norm_a, out_norm_b=out_norm_b,
             out_norm_c=out_norm_c, out_norm_d=out_norm_d, w_out=w_out, mix_norm_post=mix_norm_post,
             ffn2_norm_pre=ffn2_norm_pre, ffn2_w_gate=ffn2_w_gate, ffn2_w_up=ffn2_w_up, ffn2_w_down=ffn2_w_down,
             ffn2_norm_post=ffn2_norm_post)
    caches = (state_ssm, cache_swa_kv, cache_dil0_kv, cache_dil1_kv, cache_dil2_kv, state_hgrn)
    return _forward(x_prompt, x_sample, caches, P)
```

```python
import functools
import math

import numpy as np
import jax
import jax.numpy as jnp
from jax import lax
from jax.experimental import pallas as pl
from jax.experimental.pallas import tpu as pltpu

F32 = jnp.float32
BF16 = jnp.bfloat16

D_MODEL = 2048
DEPTH = 2
PAST_LEN = 16384
HEAD_DIM = 64
ROPE_THETA = 10000.0
D_FF = 5504
NORM_EPS = 1e-6
NEG_BIG = -1e30
LB_FLOOR = 1e-30
SSM_GROUP = 16
SSM_STATE = 64
SSM_WIDTH = 448
SSM_GROUPS = 28
SWA_HEADS = 8
SWA_KV_HEADS = 2
SWA_GROUP = 4
SWA_WIDTH = 512
SWA_KV_WIDTH = 128
WINDOW = 128
DIL_PAIRS = ((128, 1), (512, 4), (2048, 16))
DIL_HPG = 3
DIL_GW = DIL_HPG * HEAD_DIM
DIL_WIDTH = 576
DIL_KV_WIDTH = 192
HGRN_HEAD_DIM = 128
HGRN_WIDTH = 512
HGRN_HEADS = 4
IN_COLS = 4224
ATTN_SCALE = HEAD_DIM ** -0.5

LANES = 128
FF_TILE = 256
N_FF_TILES = -(-D_FF // FF_TILE)
DOWN_TILE = 256
N_DOWN_TILES = D_MODEL // DOWN_TILE
FFN_ROWS = 1040
VMEM_LIMIT = 60 * 1024 * 1024

B_SLOT = SWA_WIDTH + 2 * SWA_KV_WIDTH
C_SLOT = 384
C_K = 256
OFF_B = 0
OFF_C = OFF_B + B_SLOT
OFF_D = 2048
OFF_U = OFF_D + 4 * HGRN_WIDTH
U_SLOT = 512
PROJ_COLS = 4608
WIN_TILE = 1536

S5_CHUNK = 16
S5_NPOW = S5_CHUNK + 1
HGRN_CHUNK = 128
HGRN_LEVELS = (64, 32, 16, 8, 4, 2, 1)


def _proj_segments():
    src = np.cumsum([0, SSM_WIDTH, SWA_WIDTH, SWA_KV_WIDTH, SWA_KV_WIDTH, DIL_WIDTH, DIL_KV_WIDTH, DIL_KV_WIDTH,
                     HGRN_WIDTH, HGRN_WIDTH, HGRN_WIDTH, HGRN_WIDTH])
    s_u, s_qb, s_kb, s_vb, s_qc, s_kc, s_vc, s_qd = src[:8]
    segs = [(OFF_B, s_qb, SWA_WIDTH), (OFF_B + SWA_WIDTH, s_kb, SWA_KV_WIDTH),
            (OFF_B + SWA_WIDTH + SWA_KV_WIDTH, s_vb, SWA_KV_WIDTH)]
    for g in range(3):
        base = OFF_C + g * C_SLOT
        segs += [(base, s_qc + g * DIL_GW, DIL_GW), (base + C_K, s_kc + g * HEAD_DIM, HEAD_DIM),
                 (base + C_K + HEAD_DIM, s_vc + g * HEAD_DIM, HEAD_DIM)]
    segs += [(OFF_D, s_qd, 4 * HGRN_WIDTH), (OFF_U, s_u, SSM_WIDTH)]
    return [(int(a), int(b), int(c)) for a, b, c in segs]


def _rotary_lane_mask():
    m = np.zeros((PROJ_COLS,), np.float32)
    m[OFF_B:OFF_B + SWA_WIDTH + SWA_KV_WIDTH] = 1.0
    for g in range(3):
        base = OFF_C + g * C_SLOT
        m[base:base + DIL_GW] = 1.0
        m[base + C_K:base + C_K + HEAD_DIM] = 1.0
    return m


def _cparams(sem):
    return pltpu.CompilerParams(dimension_semantics=sem, vmem_limit_bytes=VMEM_LIMIT)


def _v3(a):
    return a.reshape(a.shape[0], 1, a.shape[1])


def _rms(x, g):
    return x * lax.rsqrt(jnp.mean(x * x, axis=-1, keepdims=True) + NORM_EPS) * g


def _sigmoid(x):
    return 1.0 / (1.0 + jnp.exp(-x))


def _silu(x):
    return x * _sigmoid(x)


def _dot(a, b):
    return jnp.dot(a, b, preferred_element_type=F32)


def _dot_nt(a, b):
    return lax.dot_general(a, b, (((1,), (1,)), ((), ())), preferred_element_type=F32)


def _dot_tn(a, b):
    return lax.dot_general(a, b, (((0,), (0,)), ((), ())), preferred_element_type=F32)


def _ffn_kernel(*refs, tail, n_tail, tail_in, tail_out):
    refs = list(refs)
    x_ref = refs.pop(0)
    xs_ref = refs.pop(0) if tail_in else None
    gpre_ref, wg_ref, wu_ref, wd_ref, gpost_ref, o_ref = refs[:6]
    os_ref = refs[6] if tail_out else None
    xn_ref, hid_ref = refs[-2:]
    s = pl.program_id(1)
    in_last_block = pl.program_id(0) == pl.num_programs(0) - 1
    tail_rows = slice(tail, tail + n_tail)

    @pl.when(s == 0)
    def _():
        xn_ref[...] = _rms(x_ref[...], gpre_ref[...]).astype(BF16)
        if tail_in:

            @pl.when(in_last_block)
            def _():
                xn_ref[tail_rows, :] = _rms(xs_ref[...], gpre_ref[...]).astype(BF16)

    @pl.when(s < N_FF_TILES)
    def _():
        xn = xn_ref[...]
        gate = _dot(xn, wg_ref[...].astype(BF16))
        up = _dot(xn, wu_ref[...].astype(BF16))
        hid_ref[s] = (_silu(gate) * up).astype(BF16)

    for j in range(N_DOWN_TILES):

        @pl.when(s == N_FF_TILES + j)
        def _(j=j):
            hid = jnp.concatenate([hid_ref[f] for f in range(N_FF_TILES)], axis=1)[:, :D_FF]
            o_ref[:, j * DOWN_TILE:(j + 1) * DOWN_TILE] = _dot(hid, wd_ref[...].astype(BF16))

    @pl.when(s == pl.num_programs(1) - 1)
    def _():
        delta = 0.5 * _rms(o_ref[...], gpost_ref[...])
        res = x_ref[...] + delta
        o_ref[...] = res
        if tail_in:

            @pl.when(in_last_block)
            def _():
                o_ref[tail_rows, :] = xs_ref[...] + delta[tail_rows]
        if tail_out:

            @pl.when(in_last_block)
            def _():
                os_ref[...] = res[tail_rows]


def _ffn(x, x_tail, gpre, wg, wu, wd, gpost, layer, tm, n_main, n_tail, tail_out):
    m = n_main + n_tail
    nblk = pl.cdiv(m, tm)
    tail = n_main - (nblk - 1) * tm
    assert 0 <= tail and tail + n_tail <= tm and tail % 8 == 0
    last_ff = N_FF_TILES - 1
    in_specs = [pl.BlockSpec((tm, D_MODEL), lambda i, s: (i, 0))]
    args = [x]
    if x_tail is not None:
        in_specs.append(pl.BlockSpec((n_tail, D_MODEL), lambda i, s: (0, 0)))
        args.append(x_tail)
    in_specs += [
        pl.BlockSpec((None, 1, D_MODEL), lambda i, s: (layer, 0, 0)),
        pl.BlockSpec((None, D_MODEL, FF_TILE), lambda i, s: (layer, 0, jnp.minimum(s, last_ff))),
        pl.BlockSpec((None, D_MODEL, FF_TILE), lambda i, s: (layer, 0, jnp.minimum(s, last_ff))),
        pl.BlockSpec((None, D_FF, DOWN_TILE), lambda i, s: (layer, 0, jnp.maximum(s - N_FF_TILES, 0))),
        pl.BlockSpec((None, 1, D_MODEL), lambda i, s: (layer, 0, 0)),
    ]
    args += [_v3(gpre), wg, wu, wd, _v3(gpost)]
    out_main = pl.BlockSpec((tm, D_MODEL), lambda i, s: (i, 0), pipeline_mode=pl.Buffered(1))
    if tail_out:
        out_specs = [out_main, pl.BlockSpec((n_tail, D_MODEL), lambda i, s: (0, 0))]
        out_shape = [jax.ShapeDtypeStruct((n_main, D_MODEL), F32), jax.ShapeDtypeStruct((n_tail, D_MODEL), F32)]
    else:
        out_specs = out_main
        out_shape = jax.ShapeDtypeStruct((m, D_MODEL), F32)
    return pl.pallas_call(
        functools.partial(_ffn_kernel, tail=tail, n_tail=n_tail, tail_in=x_tail is not None, tail_out=tail_out),
        grid=(nblk, N_FF_TILES + N_DOWN_TILES),
        in_specs=in_specs,
        out_specs=out_specs,
        out_shape=out_shape,
        scratch_shapes=[pltpu.VMEM((tm, D_MODEL), BF16), pltpu.VMEM((N_FF_TILES, tm, FF_TILE), BF16)],
        compiler_params=_cparams(("arbitrary", "arbitrary")),
    )(*args)


def _win_prep_kernel(w_ref, o_ref):
    o_ref[...] = jnp.zeros_like(o_ref)
    for dst, src, width in _proj_segments():
        o_ref[:, dst:dst + width] = w_ref[:, src:src + width].astype(BF16)


def _win_prep(w_in, layer):
    rows = 256
    return pl.pallas_call(
        _win_prep_kernel,
        grid=(D_MODEL // rows,),
        in_specs=[pl.BlockSpec((None, rows, IN_COLS), lambda i: (layer, i, 0))],
        out_specs=pl.BlockSpec((rows, PROJ_COLS), lambda i: (i, 0)),
        out_shape=jax.ShapeDtypeStruct((D_MODEL, PROJ_COLS), BF16),
        compiler_params=_cparams(("parallel",)),
    )(w_in)


def _cast_kernel(w_ref, o_ref):
    o_ref[...] = w_ref[...].astype(BF16)


def _cast_bf16(w, layer):
    _, r, c = w.shape
    rows = 256
    return pl.pallas_call(
        _cast_kernel,
        grid=(r // rows,),
        in_specs=[pl.BlockSpec((None, rows, c), lambda i: (layer, i, 0))],
        out_specs=pl.BlockSpec((rows, c), lambda i: (i, 0)),
        out_shape=jax.ShapeDtypeStruct((r, c), BF16),
        compiler_params=_cparams(("parallel",)),
    )(w)


def _win_kernel(x_ref, g_ref, w_ref, cos_ref, sin_ref, o_ref, *rest, chunks):
    if chunks:
        u_ref, xn_ref, st_ref = rest
    else:
        (xn_ref,) = rest
    j = pl.program_id(1)

    @pl.when(j == 0)
    def _():
        xn_ref[...] = _rms(x_ref[...], g_ref[...]).astype(BF16)

    rot = _rotary_lane_mask()
    for tile in range(PROJ_COLS // WIN_TILE):

        @pl.when(j == tile)
        def _(tile=tile):
            y = _dot(xn_ref[...], w_ref[...])
            cos = cos_ref[...]
            sin = sin_ref[...]
            lane = lax.broadcasted_iota(jnp.int32, cos.shape, 1)
            first_half = (lane % HEAD_DIM) < (HEAD_DIM // 2)
            for c in range(WIN_TILE // LANES):
                sl = slice(c * LANES, (c + 1) * LANES)
                flags = rot[tile * WIN_TILE + c * LANES:tile * WIN_TILE + (c + 1) * LANES]
                x = y[:, sl]
                lo, hi = bool(flags[:HEAD_DIM].all()), bool(flags[HEAD_DIM:].all())
                if not (lo or hi):
                    o_ref[:, sl] = x
                    continue
                partner = jnp.where(first_half,
                                    pltpu.roll(x, LANES - HEAD_DIM // 2, axis=1),
                                    pltpu.roll(x, HEAD_DIM // 2, axis=1))
                rotated = x * cos + partner * sin
                if lo and hi:
                    o_ref[:, sl] = rotated
                else:
                    is_rot = (lane < HEAD_DIM) if lo else (lane >= HEAD_DIM)
                    o_ref[:, sl] = jnp.where(is_rot, rotated, x)
            if chunks and tile == OFF_U // WIN_TILE:
                base = OFF_U % WIN_TILE
                for lb in range(U_SLOT // LANES):
                    st_ref[lb] = y[:, base + lb * LANES:base + (lb + 1) * LANES]
                _s5_rows_to_chunks(st_ref, u_ref)


def _win(h, g, w, cos_t, sin_t, layer, tm, chunks, row0=0):
    m = cos_t.shape[0]
    blk0 = row0 // tm
    cw = S5_CHUNK * SSM_GROUP
    out_specs = [pl.BlockSpec((tm, WIN_TILE), lambda i, j: (i, j))]
    out_shape = [jax.ShapeDtypeStruct((m, PROJ_COLS), F32)]
    scratch = [pltpu.VMEM((tm, D_MODEL), BF16)]
    if chunks:
        out_specs.append(pl.BlockSpec((SSM_GROUPS, tm // S5_CHUNK, cw), lambda i, j: (0, i, 0)))
        out_shape.append(jax.ShapeDtypeStruct((SSM_GROUPS, m // S5_CHUNK, cw), F32))
        scratch.append(pltpu.VMEM((U_SLOT // LANES, tm, LANES), F32))
    return pl.pallas_call(
        functools.partial(_win_kernel, chunks=chunks),
        grid=(m // tm, PROJ_COLS // WIN_TILE),
        in_specs=[
            pl.BlockSpec((tm, D_MODEL), lambda i, j: (i + blk0, 0)),
            pl.BlockSpec((None, 1, D_MODEL), lambda i, j: (layer, 0, 0)),
            pl.BlockSpec((D_MODEL, WIN_TILE), lambda i, j: (0, j)),
            pl.BlockSpec((tm, LANES), lambda i, j: (i, 0)),
            pl.BlockSpec((tm, LANES), lambda i, j: (i, 0)),
        ],
        out_specs=out_specs,
        out_shape=out_shape,
        scratch_shapes=scratch,
        compiler_params=_cparams(("parallel", "arbitrary")),
    )(h, _v3(g), w, cos_t, sin_t)


def _rope_tables(pos):
    half = HEAD_DIM // 2
    inv_freq = ROPE_THETA ** (-jnp.arange(half, dtype=F32) / half)
    ang = pos.astype(F32)[:, None] * inv_freq[None, :]
    cos, sin = jnp.cos(ang), jnp.sin(ang)
    reps = LANES // HEAD_DIM
    cos_t = jnp.tile(jnp.concatenate([cos, cos], axis=1), (1, reps))
    sin_t = jnp.tile(jnp.concatenate([-sin, sin], axis=1), (1, reps))
    return cos_t, sin_t


def _band_heads(q, kk, vv, valid, sinks):
    w = WINDOW
    heads = q.shape[1] // HEAD_DIM
    qs = jnp.concatenate([q[:, g * HEAD_DIM:(g + 1) * HEAD_DIM] for g in range(heads)], axis=0).astype(BF16)
    s = _dot_nt(kk, qs) * ATTN_SCALE
    s = jnp.where(jnp.concatenate([valid] * heads, axis=1), s, NEG_BIG)
    m = jnp.max(s, axis=0, keepdims=True)
    if sinks is not None:
        sink_row = jnp.concatenate([jnp.broadcast_to(sinks[g], (1, w)) for g in range(heads)], axis=1)
        m = jnp.maximum(m, sink_row)
    p = jnp.exp(s - m)
    l = jnp.sum(p, axis=0, keepdims=True)
    if sinks is not None:
        l = l + jnp.exp(sink_row - m)
    ot = _dot_tn(vv, p.astype(BF16)) / l
    lse = m + jnp.log(l)
    return ([ot[:, g * w:(g + 1) * w] for g in range(heads)], [lse[:, g * w:(g + 1) * w] for g in range(heads)])


def _untranspose(blocks):
    pairs = [jnp.concatenate(blocks[i:i + 2], axis=0).T for i in range(0, len(blocks), 2)]
    return jnp.concatenate(pairs, axis=1)


def _band_mask(no_prev):
    w = WINDOW
    k = lax.broadcasted_iota(jnp.int32, (2 * w, w), 0)
    q = lax.broadcasted_iota(jnp.int32, (2 * w, w), 1)
    valid = (k >= q) & (k <= q + w)
    if no_prev is False:
        return valid
    return valid & (k >= jnp.where(no_prev, w, 0))


def _swa_prompt_kernel(cur_ref, prev_ref, sink_ref, o_ref, *, nsub):
    w = WINDOW
    gw = SWA_GROUP * HEAD_DIM
    first = pl.program_id(1) == 0
    for j in range(nsub):
        valid = _band_mask(first if j == 0 else False)
        cur = cur_ref[0, j * w:(j + 1) * w, :]
        prev = prev_ref[0] if j == 0 else cur_ref[0, (j - 1) * w:j * w, SWA_WIDTH:B_SLOT]
        outs = []
        for h in range(SWA_KV_HEADS):
            ks = slice(SWA_WIDTH + h * HEAD_DIM, SWA_WIDTH + (h + 1) * HEAD_DIM)
            vs = slice(SWA_WIDTH + SWA_KV_WIDTH + h * HEAD_DIM, SWA_WIDTH + SWA_KV_WIDTH + (h + 1) * HEAD_DIM)
            kk = jnp.concatenate([prev[:, h * HEAD_DIM:(h + 1) * HEAD_DIM], cur[:, ks]], axis=0).astype(BF16)
            vv = jnp.concatenate([prev[:, SWA_KV_WIDTH + h * HEAD_DIM:SWA_KV_WIDTH + (h + 1) * HEAD_DIM],
                                  cur[:, vs]], axis=0).astype(BF16)
            sinks = [sink_ref[:, h * gw + g * HEAD_DIM:h * gw + g * HEAD_DIM + 1] for g in range(SWA_GROUP)]
            o, _ = _band_heads(cur[:, h * gw:(h + 1) * gw], kk, vv, valid, sinks)
            outs += o
        o_ref[0, j * w:(j + 1) * w, :] = _untranspose(outs)


def _swa_prompt(proj, sink):
    n, t, _ = proj.shape
    nsub = 2
    rows = WINDOW * nsub
    return pl.pallas_call(
        functools.partial(_swa_prompt_kernel, nsub=nsub),
        grid=(n, t // rows),
        in_specs=[
            pl.BlockSpec((1, rows, B_SLOT), lambda b, i: (b, i, OFF_B // B_SLOT)),
            pl.BlockSpec((1, WINDOW, 2 * SWA_KV_WIDTH),
                         lambda b, i: (b, jnp.maximum(i * nsub - 1, 0), (OFF_B + SWA_WIDTH) // (2 * SWA_KV_WIDTH))),
            pl.BlockSpec((1, SWA_WIDTH), lambda b, i: (0, 0)),
        ],
        out_specs=pl.BlockSpec((1, rows, SWA_WIDTH), lambda b, i: (b, i, 0)),
        out_shape=jax.ShapeDtypeStruct((n, t, SWA_WIDTH), F32),
        compiler_params=_cparams(("parallel", "arbitrary")),
    )(proj, proj, sink)


def _dil_prompt_kernel(cur_ref, prev_ref, o_ref, cur3_ref, out3_ref, *, dil, nsub):
    nl = C_SLOT // LANES
    span = WINDOW * dil
    first = pl.program_id(1) == 0
    for c in range(nl):
        cur3_ref[c] = cur_ref[0, :, c * LANES:(c + 1) * LANES]

    def one(j, r):
        def rows(base):
            return pl.ds(base + r, WINDOW, stride=dil) if dil > 1 else pl.ds(base, WINDOW)

        valid = _band_mask(first if j == 0 else False)
        cur = rows(j * span)
        q = jnp.concatenate([cur3_ref[0, cur, :], cur3_ref[1, cur, :]], axis=1)[:, 0:DIL_GW]
        kv = cur3_ref[2, cur, :]
        pkv = prev_ref[0, rows(0), :] if j == 0 else cur3_ref[2, rows((j - 1) * span), :]
        kk = jnp.concatenate([pkv[:, 0:HEAD_DIM], kv[:, 0:HEAD_DIM]], axis=0).astype(BF16)
        vv = jnp.concatenate([pkv[:, HEAD_DIM:], kv[:, HEAD_DIM:]], axis=0).astype(BF16)
        o, lse = _band_heads(q, kk, vv, valid, None)
        res = _untranspose(o + [jnp.broadcast_to(x, (HEAD_DIM, WINDOW)) for x in lse])
        for c in range(nl):
            out3_ref[c, cur, :] = res[:, c * LANES:(c + 1) * LANES]

    for j in range(nsub):
        if dil <= 4:
            for r in range(dil):
                one(j, r)
        else:
            lax.fori_loop(0, dil, lambda r, c, j=j: (one(j, r), c)[1], 0, unroll=2)
    for c in range(nl):
        o_ref[0, :, c * LANES:(c + 1) * LANES] = out3_ref[c]


def _dil_prompt(proj, group, dil):
    n, t, _ = proj.shape
    span = WINDOW * dil
    nsub = max(1, 4 // dil)
    rows = span * nsub
    slot = OFF_C + group * C_SLOT
    return pl.pallas_call(
        functools.partial(_dil_prompt_kernel, dil=dil, nsub=nsub),
        grid=(n, t // rows),
        in_specs=[
            pl.BlockSpec((1, rows, C_SLOT), lambda b, i: (b, i, slot // C_SLOT)),
            pl.BlockSpec((1, span, LANES), lambda b, i: (b, jnp.maximum(i * nsub - 1, 0), (slot + C_K) // LANES)),
        ],
        out_specs=pl.BlockSpec((1, rows, 2 * DIL_GW), lambda b, i: (b, i, 0)),
        out_shape=jax.ShapeDtypeStruct((n, t, 2 * DIL_GW), F32),
        scratch_shapes=[pltpu.VMEM((C_SLOT // LANES, rows, LANES), F32),
                        pltpu.VMEM((2 * DIL_GW // LANES, rows, LANES), F32)],
        compiler_params=_cparams(("parallel", "arbitrary")),
    )(proj, proj)


def _decode_heads(q, knew, vnew, kt, vt, dil, sinks):
    s = _dot(q.astype(BF16), kt.astype(BF16)) * ATTN_SCALE
    if dil > 1:
        row = lax.broadcasted_iota(jnp.int32, s.shape, 1)
        s = jnp.where(row % dil == 0, s, NEG_BIG)
    s_new = jnp.sum(q * knew, axis=-1, keepdims=True) * ATTN_SCALE
    m = jnp.maximum(jnp.max(s, axis=-1, keepdims=True), s_new)
    if sinks is not None:
        m = jnp.maximum(m, sinks)
    p = jnp.exp(s - m)
    p_new = jnp.exp(s_new - m)
    l = jnp.sum(p, axis=-1, keepdims=True) + p_new
    if sinks is not None:
        l = l + jnp.exp(sinks - m)
    o = (_dot_nt(p.astype(BF16), vt.astype(BF16)) + p_new * vnew) / l
    return o, m + jnp.log(l)


def _rows_of(x, heads):
    return jnp.concatenate([x[:, g * HEAD_DIM:(g + 1) * HEAD_DIM] for g in range(heads)], axis=0)


def _swa_sample_kernel(p_ref, cache_ref, sink_ref, o_ref, *, bn):
    for b in range(bn):
        row = p_ref[b:b + 1, :]
        outs = []
        for h in range(SWA_KV_HEADS):
            gw = SWA_GROUP * HEAD_DIM
            q = _rows_of(row[:, h * gw:(h + 1) * gw], SWA_GROUP)
            knew = row[:, SWA_WIDTH + h * HEAD_DIM:SWA_WIDTH + (h + 1) * HEAD_DIM]
            vnew = row[:, SWA_WIDTH + SWA_KV_WIDTH + h * HEAD_DIM:SWA_WIDTH + SWA_KV_WIDTH + (h + 1) * HEAD_DIM]
            sinks = _rows_of(sink_ref[:, h * gw:(h + 1) * gw], SWA_GROUP)[:, 0:1]
            o, _ = _decode_heads(q, knew, vnew, cache_ref[b, 0, h], cache_ref[b, 1, h], 1, sinks)
            outs += [o[g:g + 1] for g in range(SWA_GROUP)]
        o_ref[b:b + 1, :] = jnp.concatenate(outs, axis=1)


def _swa_sample(proj, cache, sink, layer):
    n = proj.shape[0]
    bn = 8
    return pl.pallas_call(
        functools.partial(_swa_sample_kernel, bn=bn),
        grid=(n // bn,),
        in_specs=[
            pl.BlockSpec((bn, B_SLOT), lambda i: (i, OFF_B // B_SLOT)),
            pl.BlockSpec((None, bn, 2, SWA_KV_HEADS, HEAD_DIM, WINDOW), lambda i: (layer, i, 0, 0, 0, 0)),
            pl.BlockSpec((1, SWA_WIDTH), lambda i: (0, 0)),
        ],
        out_specs=pl.BlockSpec((bn, SWA_WIDTH), lambda i: (i, 0)),
        out_shape=jax.ShapeDtypeStruct((n, SWA_WIDTH), F32),
        compiler_params=_cparams(("parallel",)),
    )(proj, cache, sink)


def _dil_sample_kernel(p_ref, cache_ref, o_ref, *, bn, dil):
    for b in range(bn):
        row = p_ref[b:b + 1, :]
        q = _rows_of(row[:, 0:DIL_GW], DIL_HPG)
        o, lse = _decode_heads(q, row[:, C_K:C_K + HEAD_DIM], row[:, C_K + HEAD_DIM:C_K + 2 * HEAD_DIM],
                               cache_ref[b, 0], cache_ref[b, 1], dil, None)
        lse = jnp.broadcast_to(lse, (DIL_HPG, HEAD_DIM))
        o_ref[b:b + 1, :] = jnp.concatenate([o[g:g + 1] for g in range(DIL_HPG)]
                                            + [lse[g:g + 1] for g in range(DIL_HPG)], axis=1)


def _dil_sample(proj, cache, group, dil, layer):
    n = proj.shape[0]
    bn = 8
    slot = OFF_C + group * C_SLOT
    return pl.pallas_call(
        functools.partial(_dil_sample_kernel, bn=bn, dil=dil),
        grid=(n // bn,),
        in_specs=[
            pl.BlockSpec((bn, C_SLOT), lambda i: (i, slot // C_SLOT)),
            pl.BlockSpec((None, bn, 2, HEAD_DIM, WINDOW * dil), lambda i: (layer, i, 0, 0, 0)),
        ],
        out_specs=pl.BlockSpec((bn, 2 * DIL_GW), lambda i: (i, 0)),
        out_shape=jax.ShapeDtypeStruct((n, 2 * DIL_GW), F32),
        compiler_params=_cparams(("parallel",)),
    )(proj, cache)


def _s5_param_kernel(are_ref, aim_ref, ldt_ref, bre_ref, bim_ref, cre_ref, cim_ref,
                     lam_re_ref, lam_im_ref, cl_re_ref, cl_im_ref, r_re_ref, r_im_ref, k_ref):
    ar = are_ref[0]
    ai = aim_ref[0]
    dt = jnp.exp(ldt_ref[0])
    nrow = lam_re_ref.shape[1]
    ri = lax.broadcasted_iota(jnp.int32, (nrow, 1), 0)
    pw = jnp.where(ri < S5_NPOW, ri, jnp.left_shift(S5_CHUNK, jnp.maximum(ri - S5_NPOW, 0))).astype(F32)
    mag = jnp.exp(pw * (ar * dt))
    ang = pw * (ai * dt)
    lam_re = mag * jnp.cos(ang)
    lam_im = mag * jnp.sin(ang)
    lam_re_ref[0] = lam_re
    lam_im_ref[0] = lam_im
    l1r, l1i = lam_re[1:2], lam_im[1:2]
    den = ar * ar + ai * ai
    z_re = ((l1r - 1.0) * ar + l1i * ai) / den
    z_im = (l1i * ar - (l1r - 1.0) * ai) / den
    b_re, b_im = bre_ref[0], bim_ref[0]
    bb_re = z_re * b_re - z_im * b_im
    bb_im = z_re * b_im + z_im * b_re
    c_re, c_im = cre_ref[0], cim_ref[0]
    cls_re, cls_im = [], []
    for d in range(S5_NPOW):
        lr, li = lam_re[d:d + 1], lam_im[d:d + 1]
        cr = c_re * lr - c_im * li
        ci = c_re * li + c_im * lr
        cl_re_ref[0, d * SSM_GROUP:(d + 1) * SSM_GROUP, :] = cr
        cl_im_ref[0, d * SSM_GROUP:(d + 1) * SSM_GROUP, :] = ci
        if d < S5_CHUNK:
            cls_re.append(cr)
            cls_im.append(ci)
    for s in range(S5_CHUNK):
        lr = lam_re[S5_CHUNK - 1 - s:S5_CHUNK - s]
        li = lam_im[S5_CHUNK - 1 - s:S5_CHUNK - s]
        r_re_ref[0, s * SSM_GROUP:(s + 1) * SSM_GROUP, :] = lr * bb_re - li * bb_im
        r_im_ref[0, s * SSM_GROUP:(s + 1) * SSM_GROUP, :] = lr * bb_im + li * bb_re
    call_re = jnp.concatenate(cls_re, axis=0)
    call_im = jnp.concatenate(cls_im, axis=0)
    hp = lax.Precision.HIGHEST
    nt = (((1,), (1,)), ((), ()))
    kt = (lax.dot_general(bb_re, call_re, nt, precision=hp, preferred_element_type=F32)
          - lax.dot_general(bb_im, call_im, nt, precision=hp, preferred_element_type=F32))
    cw = S5_CHUNK * SSM_GROUP
    for s in range(S5_CHUNK):
        if s == 0:
            blk = kt
        else:
            blk = jnp.concatenate([jnp.zeros((SSM_GROUP, s * SSM_GROUP), F32), kt[:, :cw - s * SSM_GROUP]], axis=1)
        k_ref[0, s * SSM_GROUP:(s + 1) * SSM_GROUP, :] = blk


def _s5_params(a_re, a_im, log_dt, b_re, b_im, c_re, c_im, nsteps):
    g = SSM_GROUPS
    nrow = ((S5_NPOW + nsteps + 7) // 8) * 8
    row = lambda i: (i, 0, 0)
    shp = lambda r, c: jax.ShapeDtypeStruct((g, r, c), F32)
    ncl = S5_NPOW * SSM_GROUP
    nr = S5_CHUNK * SSM_GROUP
    return pl.pallas_call(
        _s5_param_kernel,
        grid=(g,),
        in_specs=[
            pl.BlockSpec((1, 1, SSM_STATE), row),
            pl.BlockSpec((1, 1, SSM_STATE), row),
            pl.BlockSpec((1, 1, 1), row),
            pl.BlockSpec((1, SSM_GROUP, SSM_STATE), row),
            pl.BlockSpec((1, SSM_GROUP, SSM_STATE), row),
            pl.BlockSpec((1, SSM_GROUP, SSM_STATE), row),
            pl.BlockSpec((1, SSM_GROUP, SSM_STATE), row),
        ],
        out_specs=[
            pl.BlockSpec((1, nrow, SSM_STATE), row), pl.BlockSpec((1, nrow, SSM_STATE), row),
            pl.BlockSpec((1, ncl, SSM_STATE), row), pl.BlockSpec((1, ncl, SSM_STATE), row),
            pl.BlockSpec((1, nr, SSM_STATE), row), pl.BlockSpec((1, nr, SSM_STATE), row),
            pl.BlockSpec((1, nr, nr), row),
        ],
        out_shape=[shp(nrow, SSM_STATE), shp(nrow, SSM_STATE), shp(ncl, SSM_STATE), shp(ncl, SSM_STATE),
                   shp(nr, SSM_STATE), shp(nr, SSM_STATE), shp(nr, nr)],
        compiler_params=_cparams(("parallel",)),
    )(a_re.reshape(g, 1, SSM_STATE), a_im.reshape(g, 1, SSM_STATE), log_dt.reshape(g, 1, 1),
      jnp.swapaxes(b_re, 1, 2), jnp.swapaxes(b_im, 1, 2), c_re, c_im)


def _s5_rows_to_chunks(st_ref, u_ref):
    nc = st_ref.shape[1] // S5_CHUNK
    per_blk = LANES // SSM_GROUP
    slot = lax.broadcasted_iota(jnp.int32, (nc, LANES), 1) // SSM_GROUP
    for g in range(SSM_GROUPS):
        acc = [jnp.zeros((nc, LANES), F32) for _ in range(S5_CHUNK // per_blk)]
        for s in range(S5_CHUNK):
            v = st_ref[g // per_blk, pl.ds(s, nc, stride=S5_CHUNK), :]
            shift = (SSM_GROUP * (s % per_blk) - SSM_GROUP * (g % per_blk)) % LANES
            if shift:
                v = pltpu.roll(v, shift, axis=1)
            acc[s // per_blk] = jnp.where(slot == s % per_blk, v, acc[s // per_blk])
        for k, a in enumerate(acc):
            u_ref[g, :, k * LANES:(k + 1) * LANES] = a


def _s5_chunks_to_rows(y_ref, st_ref):
    nc = y_ref.shape[1]
    per_blk = LANES // SSM_GROUP
    slot = lax.broadcasted_iota(jnp.int32, (nc, LANES), 1) // SSM_GROUP
    nblk = -(-SSM_WIDTH // LANES)
    for lb in range(nblk):
        groups = range(lb * per_blk, min((lb + 1) * per_blk, SSM_GROUPS))
        for s in range(S5_CHUNK):
            acc = jnp.zeros((nc, LANES), F32)
            for g in groups:
                v = y_ref[g, :, (s // per_blk) * LANES:(s // per_blk + 1) * LANES]
                shift = (SSM_GROUP * (g % per_blk) - SSM_GROUP * (s % per_blk)) % LANES
                if shift:
                    v = pltpu.roll(v, shift, axis=1)
                acc = jnp.where(slot == g % per_blk, v, acc)
            st_ref[lb, pl.ds(s, nc, stride=S5_CHUNK), :] = acc
    return jnp.concatenate([st_ref[lb][:, 0:min(LANES, SSM_WIDTH - lb * LANES)] for lb in range(nblk)], axis=1)


def _s5_prompt_kernel(u_ref, mt_ref, r_re_ref, r_im_ref, cl_re_ref, cl_im_ref, lam_re_ref, lam_im_ref, d_ref,
                      y_ref, xre_ref, xim_ref, *, nseq, nchunk):
    u = u_ref[0]
    ub = u.astype(BF16)
    y = _dot(ub, mt_ref[0].astype(BF16))
    xr = _dot(ub, r_re_ref[0].astype(BF16))
    xi = _dot(ub, r_im_ref[0].astype(BF16))
    rows = nseq * nchunk
    cidx = lax.broadcasted_iota(jnp.int32, (rows, SSM_STATE), 0) % nchunk
    lam_re, lam_im = lam_re_ref[0], lam_im_ref[0]
    for k in range(int(math.log2(nchunk))):
        sh = 1 << k
        lr = lam_re[S5_NPOW + k:S5_NPOW + k + 1]
        li = lam_im[S5_NPOW + k:S5_NPOW + k + 1]
        ok = cidx >= sh
        pr = jnp.where(ok, pltpu.roll(xr, sh, axis=0), 0.0)
        pi = jnp.where(ok, pltpu.roll(xi, sh, axis=0), 0.0)
        xr, xi = xr + lr * pr - li * pi, xi + lr * pi + li * pr
    ok = cidx >= 1
    er = jnp.where(ok, pltpu.roll(xr, 1, axis=0), 0.0)
    ei = jnp.where(ok, pltpu.roll(xi, 1, axis=0), 0.0)
    cl_re = cl_re_ref[0][SSM_GROUP:, :]
    cl_im = cl_im_ref[0][SSM_GROUP:, :]
    y = y + _dot_nt(er.astype(BF16), cl_re.astype(BF16)) - _dot_nt(ei.astype(BF16), cl_im.astype(BF16))
    y_ref[0] = y + d_ref[0] * u
    last = [s * nchunk + nchunk - 1 for s in range(nseq)]
    xre_ref[0] = jnp.concatenate([xr[i:i + 1] for i in last], axis=0)
    xim_ref[0] = jnp.concatenate([xi[i:i + 1] for i in last], axis=0)


def _s5_prompt(u, mt, r_re, r_im, cl_re, cl_im, lam_re, lam_im, d_t, nseq, nchunk):
    g = SSM_GROUPS
    rows = nseq * nchunk
    cw = S5_CHUNK * SSM_GROUP
    row = lambda i: (i, 0, 0)
    full = lambda a: pl.BlockSpec((1,) + a.shape[1:], row)
    return pl.pallas_call(
        functools.partial(_s5_prompt_kernel, nseq=nseq, nchunk=nchunk),
        grid=(g,),
        in_specs=[full(u), full(mt), full(r_re), full(r_im), full(cl_re), full(cl_im), full(lam_re), full(lam_im),
                  full(d_t)],
        out_specs=[pl.BlockSpec((1, rows, cw), row), pl.BlockSpec((1, nseq, SSM_STATE), row),
                   pl.BlockSpec((1, nseq, SSM_STATE), row)],
        out_shape=[jax.ShapeDtypeStruct((g, rows, cw), F32), jax.ShapeDtypeStruct((g, nseq, SSM_STATE), F32),
                   jax.ShapeDtypeStruct((g, nseq, SSM_STATE), F32)],
        compiler_params=_cparams(("parallel",)),
    )(u, mt, r_re, r_im, cl_re, cl_im, lam_re, lam_im, d_t)


def _s5_sample_kernel(u_ref, x0r_ref, x0i_ref, r_re_ref, r_im_ref, cl_re_ref, cl_im_ref, lam_re_ref, lam_im_ref,
                      d_ref, y_ref, xre_ref, xim_ref):
    u = u_ref[0]
    ub = u.astype(BF16)
    lo = (S5_CHUNK - 1) * SSM_GROUP
    bb_re = r_re_ref[0][lo:lo + SSM_GROUP, :]
    bb_im = r_im_ref[0][lo:lo + SSM_GROUP, :]
    lr, li = lam_re_ref[0][1:2], lam_im_ref[0][1:2]
    x0r, x0i = x0r_ref[0], x0i_ref[0]
    xr = _dot(ub, bb_re.astype(BF16)) + (lr * x0r - li * x0i)
    xi = _dot(ub, bb_im.astype(BF16)) + (lr * x0i + li * x0r)
    c_re = cl_re_ref[0][:SSM_GROUP, :]
    c_im = cl_im_ref[0][:SSM_GROUP, :]
    y_ref[0] = (_dot_nt(xr.astype(BF16), c_re.astype(BF16)) - _dot_nt(xi.astype(BF16), c_im.astype(BF16))
                + d_ref[0] * u)
    xre_ref[0] = xr
    xim_ref[0] = xi


def _s5_sample(u, x0r, x0i, r_re, r_im, cl_re, cl_im, lam_re, lam_im, d):
    g, n = u.shape[0], u.shape[1]
    row = lambda i: (i, 0, 0)
    full = lambda a: pl.BlockSpec((1,) + a.shape[1:], row)
    return pl.pallas_call(
        _s5_sample_kernel,
        grid=(g,),
        in_specs=[full(u), full(x0r), full(x0i), full(r_re), full(r_im), full(cl_re), full(cl_im), full(lam_re),
                  full(lam_im), full(d)],
        out_specs=[pl.BlockSpec((1, n, SSM_GROUP), row), pl.BlockSpec((1, n, SSM_STATE), row),
                   pl.BlockSpec((1, n, SSM_STATE), row)],
        out_shape=[jax.ShapeDtypeStruct((g, n, SSM_GROUP), F32), jax.ShapeDtypeStruct((g, n, SSM_STATE), F32),
                   jax.ShapeDtypeStruct((g, n, SSM_STATE), F32)],
        compiler_params=_cparams(("parallel",)),
    )(u, x0r, x0i, r_re, r_im, cl_re, cl_im, lam_re, lam_im, d)


def _log_sigmoid(x):
    return jnp.minimum(x, 0.0) - jnp.log1p(jnp.exp(-jnp.abs(x)))


def _logaddexp(a, b):
    return jnp.maximum(a, b) + jnp.log1p(jnp.exp(-jnp.abs(a - b)))


def _hgrn_gates(fpre, lb):
    log_f = _logaddexp(jnp.log(jnp.maximum(lb, LB_FLOOR)), jnp.log1p(-lb) + _log_sigmoid(fpre))
    k = (1.0 - lb) * _sigmoid(-fpre)
    return log_f, k


def _hgrn_scan_matrix():
    c = HGRN_CHUNK
    tri = np.tril(np.ones((c, c), np.float32))
    t = np.arange(c)
    mats = [tri]
    for m in HGRN_LEVELS:
        mats.append(tri[(t // (2 * m)) * (2 * m) + m - 1])
    return np.concatenate(mats, axis=0)


def _hgrn_prompt_kernel(p_ref, lbraw_ref, gn_ref, scan_ref, o_ref, s_ref, st_ref, *, layer):
    cidx = pl.program_id(1)
    c = HGRN_CHUNK
    hd = HGRN_HEAD_DIM

    @pl.when(cidx == 0)
    def _():
        st_ref[...] = jnp.zeros_like(st_ref)

    a = lbraw_ref[...]
    e = jnp.exp(a - jnp.max(a, axis=0, keepdims=True))
    p = e / jnp.sum(e, axis=0, keepdims=True)
    run = p[0:1]
    for l in range(1, layer + 1):
        run = run + p[l:l + 1]
    lb_all = run - p[0:1]

    r = lax.broadcasted_iota(jnp.int32, (c, c), 0)
    s = lax.broadcasted_iota(jnp.int32, (c, c), 1)
    pairs = []
    for m in HGRN_LEVELS:
        sh = int(math.log2(2 * m))
        pairs.append(((r >> sh) == (s >> sh)) & ((r & m) != 0) & ((s & m) == 0))
    scan = scan_ref[...]
    for h in range(HGRN_HEADS):
        col = lambda part: p_ref[0, :, part * HGRN_WIDTH + h * hd:part * HGRN_WIDTH + (h + 1) * hd]
        log_f, k = _hgrn_gates(col(1), lb_all[:, h * hd:(h + 1) * hd])
        q = _silu(col(0))
        v = col(2)
        hi = log_f.astype(BF16)
        rest = log_f - hi.astype(F32)
        mid = rest.astype(BF16)
        lo = (rest - mid.astype(F32)).astype(BF16)
        cum3 = _dot(scan, jnp.concatenate([hi, mid, lo], axis=1))
        cum = cum3[:, 0:hd] + cum3[:, hd:2 * hd] + cum3[:, 2 * hd:3 * hd]
        b = cum[0:c]
        att = jnp.where(r == s, jnp.sum(q * k, axis=-1, keepdims=True), 0.0)
        for lvl, m in enumerate(HGRN_LEVELS):
            d = b - cum[(lvl + 1) * c:(lvl + 2) * c]
            qs = q * jnp.exp(jnp.minimum(d, 0.0))
            ks = k * jnp.exp(jnp.minimum(-d, 0.0))
            att = jnp.where(pairs[lvl], _dot_nt(qs.astype(BF16), ks.astype(BF16)), att)
        st = st_ref[h]
        vb = v.astype(BF16)
        o = _dot(att.astype(BF16), vb) + _dot_nt((q * jnp.exp(b)).astype(BF16), st.astype(BF16))
        bl = b[c - 1:c]
        st_new = jnp.exp(bl) * st + _dot_tn(vb, (k * jnp.exp(bl - b)).astype(BF16))
        st_ref[h] = st_new
        o = o * lax.rsqrt(jnp.mean(o * o, axis=-1, keepdims=True) + NORM_EPS)
        o_ref[0, :, h * hd:(h + 1) * hd] = o * gn_ref[:, h * hd:(h + 1) * hd] * _silu(col(3))

    @pl.when(cidx == pl.num_programs(1) - 1)
    def _():
        s_ref[0] = st_ref[...]


def _hgrn_prompt(proj, lbraw, gn, layer):
    n, t, _ = proj.shape
    c = HGRN_CHUNK
    hd = HGRN_HEAD_DIM
    scan = jnp.asarray(_hgrn_scan_matrix(), dtype=BF16)
    return pl.pallas_call(
        functools.partial(_hgrn_prompt_kernel, layer=layer),
        grid=(n, t // c),
        in_specs=[pl.BlockSpec((1, c, 4 * HGRN_WIDTH), lambda b, j: (b, j, OFF_D // (4 * HGRN_WIDTH))),
                  pl.BlockSpec((DEPTH, HGRN_WIDTH), lambda b, j: (0, 0)),
                  pl.BlockSpec((None, 1, HGRN_WIDTH), lambda b, j: (layer, 0, 0)),
                  pl.BlockSpec(scan.shape, lambda b, j: (0, 0))],
        out_specs=[pl.BlockSpec((1, c, HGRN_WIDTH), lambda b, j: (b, j, 0)),
                   pl.BlockSpec((1, HGRN_HEADS, hd, hd), lambda b, j: (b, 0, 0, 0))],
        out_shape=[jax.ShapeDtypeStruct((n, t, HGRN_WIDTH), F32),
                   jax.ShapeDtypeStruct((n, HGRN_HEADS, hd, hd), F32)],
        scratch_shapes=[pltpu.VMEM((HGRN_HEADS, hd, hd), F32)],
        compiler_params=_cparams(("parallel", "arbitrary")),
    )(proj, lbraw, _v3(gn), scan)


def _hgrn_sample_kernel(qt_ref, ft_ref, lbt_ref, v_ref, g_ref, gn_ref, s0_ref, o_ref, s_ref, *, layer, n):
    hd = HGRN_HEAD_DIM
    a = lbt_ref[0]
    e = jnp.exp(a - jnp.max(a, axis=1, keepdims=True))
    p = e / jnp.sum(e, axis=1, keepdims=True)
    run = p[:, 0:1]
    for l in range(1, layer + 1):
        run = run + p[:, l:l + 1]
    lb = run - p[:, 0:1]
    log_f, k = _hgrn_gates(ft_ref[0], lb)
    f = jnp.exp(log_f)
    q = _silu(qt_ref[0])
    v = v_ref[...]
    gate = g_ref[...]
    rows = []
    for b in range(n):
        s_new = f[:, b:b + 1] * s0_ref[b, 0] + k[:, b:b + 1] * v[b:b + 1, :]
        s_ref[b, 0] = s_new
        rows.append(jnp.sum(q[:, b:b + 1] * s_new, axis=0, keepdims=True))
    o = jnp.concatenate(rows, axis=0)
    o = o * lax.rsqrt(jnp.mean(o * o, axis=-1, keepdims=True) + NORM_EPS)
    o_ref[...] = o * gn_ref[...] * _silu(gate)


def _hgrn_sample(qt, ft, lbt, proj, gn, s0, layer):
    n = proj.shape[0]
    hd = HGRN_HEAD_DIM
    v_blk = (OFF_D + 2 * HGRN_WIDTH) // hd
    g_blk = (OFF_D + 3 * HGRN_WIDTH) // hd
    return pl.pallas_call(
        functools.partial(_hgrn_sample_kernel, layer=layer, n=n),
        grid=(HGRN_HEADS,),
        in_specs=[pl.BlockSpec((1, hd, n), lambda h: (h, 0, 0)),
                  pl.BlockSpec((1, hd, n), lambda h: (h, 0, 0)),
                  pl.BlockSpec((1, hd, DEPTH), lambda h: (h, 0, 0)),
                  pl.BlockSpec((n, hd), lambda h: (0, v_blk + h)),
                  pl.BlockSpec((n, hd), lambda h: (0, g_blk + h)),
                  pl.BlockSpec((None, 1, hd), lambda h: (layer, 0, h)),
                  pl.BlockSpec((None, n, 1, hd, hd), lambda h: (layer, 0, h, 0, 0))],
        out_specs=[pl.BlockSpec((n, hd), lambda h: (0, h)),
                   pl.BlockSpec((n, 1, hd, hd), lambda h: (0, h, 0, 0))],
        out_shape=[jax.ShapeDtypeStruct((n, HGRN_WIDTH), F32),
                   jax.ShapeDtypeStruct((n, HGRN_HEADS, hd, hd), F32)],
        compiler_params=_cparams(("parallel",)),
    )(qt, ft, lbt, proj, proj, _v3(gn), s0)


def _gelu_tanh(x):
    return 0.5 * x * (1.0 + jnp.tanh(math.sqrt(2.0 / math.pi) * (x + 0.044715 * (x * x * x))))


def _mixout_kernel(h_ref, ya_ref, ob_ref, oc0_ref, oc1_ref, oc2_ref, od_ref,
                   wglu_ref, bglu_ref, ga_ref, gb_ref, gc_ref, wo_ref, gpost_ref, *rest, chunked, into):
    rest = list(rest)
    if into:
        rest.pop(0)
    o_ref = rest[0]
    ya = _s5_chunks_to_rows(ya_ref, rest[1]) if chunked else ya_ref[...]
    z = _gelu_tanh(ya)
    gate = _sigmoid(_dot(z.astype(BF16), wglu_ref[...].astype(BF16)) + bglu_ref[...])
    out_a = _rms(z * gate, ga_ref[...])
    out_b = _rms(ob_ref[...], gb_ref[...])
    ocs = [oc0_ref, oc1_ref, oc2_ref]
    ls = [r[:, DIL_GW:2 * DIL_GW] for r in ocs]
    m = jnp.maximum(jnp.maximum(ls[0], ls[1]), ls[2])
    es = [jnp.exp(l - m) for l in ls]
    den = es[0] + es[1] + es[2]
    cs = [ocs[gi][:, 0:DIL_GW] * (es[gi] / den) for gi in range(3)]
    ss = sum(jnp.sum(c * c, axis=-1, keepdims=True) for c in cs)
    inv = lax.rsqrt(ss / DIL_WIDTH + NORM_EPS)
    gc = gc_ref[...]
    o1, o2, o3 = SSM_WIDTH, SSM_WIDTH + SWA_WIDTH, SSM_WIDTH + SWA_WIDTH + DIL_WIDTH
    wo = lambda lo, hi: wo_ref[lo:hi, :]
    mix = _dot(out_a.astype(BF16), wo(0, o1)) + _dot(out_b.astype(BF16), wo(o1, o2))
    for gi in range(3):
        c = cs[gi] * inv * gc[:, gi * DIL_GW:(gi + 1) * DIL_GW]
        mix = mix + _dot(c.astype(BF16), wo(o2 + gi * DIL_GW, o2 + (gi + 1) * DIL_GW))
    mix = mix + _dot(od_ref[...].astype(BF16), wo(o3, D_MODEL))
    o_ref[...] = h_ref[...] + _rms(mix, gpost_ref[...])


def _mixout(h, ya, ob, ocs, od, lp_all, w_out_l, layer, tm, row0=0, into=None):
    m = ob.shape[0]
    blk0 = row0 // tm
    rowblk = lambda a: pl.BlockSpec((tm, a.shape[1]), lambda i: (i, 0))
    vec = lambda a: pl.BlockSpec((None, 1, a.shape[1]), lambda i: (layer, 0, 0))
    mat = lambda a: pl.BlockSpec((None,) + a.shape[1:], lambda i: (layer, 0, 0))
    whole = lambda a: pl.BlockSpec(a.shape, lambda i: (0, 0), pipeline_mode=pl.Buffered(1))
    acts = [h, ya, ob, *ocs, od]
    names = ['ssm_w_glu', 'ssm_b_glu', 'out_norm_a', 'out_norm_b', 'out_norm_c', 'w_out', 'mix_norm_post']
    params = [w_out_l if k == 'w_out' else lp_all[k] for k in names]
    spec = lambda k, p: whole(p) if k == 'w_out' else (mat(p) if p.ndim == 3 else vec(p))
    act_specs = [rowblk(a) for a in acts]
    act_specs[0] = pl.BlockSpec((tm, D_MODEL), lambda i: (i + blk0, 0))
    scratch = []
    chunked = ya.ndim == 3
    if chunked:
        act_specs[1] = pl.BlockSpec((SSM_GROUPS, tm // S5_CHUNK, ya.shape[2]), lambda i: (0, i, 0))
        scratch = [pltpu.VMEM((-(-SSM_WIDTH // LANES), tm, LANES), F32)]
    args = [*acts, *[_v3(p) if p.ndim == 2 and k != 'w_out' else p for k, p in zip(names, params)]]
    in_specs = act_specs + [spec(k, p) for k, p in zip(names, params)]
    aliases = {}
    if into is not None:
        aliases = {len(args): 0}
        args.append(into)
        in_specs.append(pl.BlockSpec(memory_space=pl.ANY))
    return pl.pallas_call(
        functools.partial(_mixout_kernel, chunked=chunked, into=into is not None),
        grid=(m // tm,),
        in_specs=in_specs,
        out_specs=pl.BlockSpec((tm, D_MODEL), lambda i: (i + blk0, 0)),
        out_shape=jax.ShapeDtypeStruct((h.shape[0], D_MODEL), F32),
        input_output_aliases=aliases,
        scratch_shapes=scratch,
        compiler_params=_cparams(("parallel",)),
    )(*args)


def _ffn_block(x, x_tail, P, name, layer, tm, n_main, n_tail, tail_out):
    return _ffn(x, x_tail, P[name + '_norm_pre'], P[name + '_w_gate'], P[name + '_w_up'], P[name + '_w_down'],
                P[name + '_norm_post'], layer, tm, n_main, n_tail, tail_out)


def _prompt_mixer(h, P, w_in_l, w_out_l, s5p, layer, n, t, rope, tm):
    m = n * t
    proj2, u = _win(h, P['mix_norm_pre'], w_in_l, rope[0], rope[1], layer, tm, True)
    proj = proj2.reshape(n, t, PROJ_COLS)

    lam_re, lam_im, cl_re, cl_im, r_re, r_im, kmat = s5p
    nchunk = t // S5_CHUNK
    d_t = jnp.tile(P['ssm_d'][layer], (1, S5_CHUNK)).reshape(SSM_GROUPS, 1, S5_CHUNK * SSM_GROUP)
    y, xre, xim = _s5_prompt(u, kmat, r_re, r_im, cl_re, cl_im, lam_re, lam_im, d_t, n, nchunk)
    ssm_new = jnp.stack([xre, xim], axis=-1).transpose(1, 0, 2, 3)

    sink = jnp.repeat(P['swa_sinks'][layer], HEAD_DIM).reshape(1, SWA_WIDTH)
    ob = _swa_prompt(proj, sink).reshape(m, SWA_WIDTH)
    keep = min(WINDOW, t)
    kv = proj[:, t - keep:, OFF_B + SWA_WIDTH:OFF_B + B_SLOT]
    swa_new = kv.reshape(n, keep, 2, SWA_KV_HEADS, HEAD_DIM)

    ocs, dil_new = [], []
    for gi, (win, dil) in enumerate(DIL_PAIRS):
        ocs.append(_dil_prompt(proj, gi, dil).reshape(m, 2 * DIL_GW))
        keep = min(win, t)
        lo = OFF_C + gi * C_SLOT + C_K
        dil_new.append(proj[:, t - keep:, lo:lo + 2 * HEAD_DIM].reshape(n, keep, 2, HEAD_DIM))

    od, st = _hgrn_prompt(proj, P['hgrn_lower_bounds'], P['out_norm_d'], layer)
    hgrn_new = jnp.swapaxes(st, -1, -2)

    h = _mixout(h, y, ob, ocs, od.reshape(m, HGRN_WIDTH), P, w_out_l, layer, min(tm, 512))
    return h, (ssm_new, swa_new, dil_new[0], dil_new[1], dil_new[2], hgrn_new)


def _sample_mixer(h, row0, into, P, w_in_l, w_out_l, s5p, layer, caches, rope):
    n = rope[0].shape[0]
    state_ssm, cache_swa, cache_d0, cache_d1, cache_d2, state_hgrn = caches
    (proj,) = _win(h, P['mix_norm_pre'], w_in_l, rope[0], rope[1], layer, n, False, row0)

    lam_re, lam_im, cl_re, cl_im, r_re, r_im, _ = s5p
    u = proj[:, OFF_U:OFF_U + SSM_WIDTH].reshape(n, SSM_GROUPS, SSM_GROUP).transpose(1, 0, 2)
    x0 = state_ssm[layer].transpose(1, 0, 2, 3)
    y, xre, xim = _s5_sample(u, x0[..., 0], x0[..., 1], r_re, r_im, cl_re, cl_im, lam_re, lam_im,
                             P['ssm_d'][layer].reshape(SSM_GROUPS, 1, SSM_GROUP))
    ya = y.transpose(1, 0, 2).reshape(n, SSM_WIDTH)
    ssm_new = jnp.stack([xre, xim], axis=-1).transpose(1, 0, 2, 3)

    sink = jnp.repeat(P['swa_sinks'][layer], HEAD_DIM).reshape(1, SWA_WIDTH)
    ob = _swa_sample(proj, cache_swa.transpose(0, 1, 3, 4, 5, 2), sink, layer)
    swa_new = proj[:, OFF_B + SWA_WIDTH:OFF_B + B_SLOT].reshape(n, 1, 2, SWA_KV_HEADS, HEAD_DIM)

    ocs, dil_new = [], []
    for gi, (buf, (win, dil)) in enumerate(zip((cache_d0, cache_d1, cache_d2), DIL_PAIRS)):
        ocs.append(_dil_sample(proj, buf.transpose(0, 1, 3, 4, 2), gi, dil, layer))
        lo = OFF_C + gi * C_SLOT + C_K
        dil_new.append(proj[:, lo:lo + 2 * HEAD_DIM].reshape(n, 1, 2, HEAD_DIM))

    part = lambda i: proj[:, OFF_D + i * HGRN_WIDTH:OFF_D + (i + 1) * HGRN_WIDTH].reshape(n, HGRN_HEADS, HGRN_HEAD_DIM)
    cm = lambda a: a.transpose(1, 2, 0)
    lbt = P['hgrn_lower_bounds'].reshape(DEPTH, HGRN_HEADS, HGRN_HEAD_DIM).transpose(1, 2, 0)
    od, hgrn_new = _hgrn_sample(cm(part(0)), cm(part(1)), lbt, proj, P['out_norm_d'], state_hgrn, layer)

    h = _mixout(h, ya, ob, ocs, od, P, w_out_l, layer, n, row0, into)
    return h, (ssm_new, swa_new, dil_new[0], dil_new[1], dil_new[2], hgrn_new)


def _forward(x_prompt, x_sample, caches, P):
    n, t, _ = x_prompt.shape
    ns, ts, _ = x_sample.shape
    assert ts == 1
    tm = 1024
    m_p = n * t
    rope_p = _rope_tables(jnp.tile(jnp.arange(t, dtype=jnp.int32), n))
    rope_s = _rope_tables(jnp.full((ns,), PAST_LEN, dtype=jnp.int32))
    nsteps = int(math.log2(t // S5_CHUNK))
    h = x_prompt.reshape(m_p, D_MODEL)
    h_tail = x_sample.reshape(ns, D_MODEL)
    new_p = [[] for _ in range(6)]
    new_s = [[] for _ in range(6)]
    for l in range(DEPTH):
        w_in_l = _win_prep(P['w_in'], l)
        w_out_l = _cast_bf16(P['w_out'], l)
        s5p = _s5_params(P['ssm_a_re'][l], P['ssm_a_im'][l], P['ssm_log_dt'][l], P['ssm_b_re'][l], P['ssm_b_im'][l],
                         P['ssm_c_re'][l], P['ssm_c_im'][l], nsteps)
        h = _ffn_block(h, h_tail, P, 'ffn1', l, FFN_ROWS, m_p, ns, False)
        h_tail = None
        mixed, st_p = _prompt_mixer(h, P, w_in_l, w_out_l, s5p, l, n, t, rope_p, tm)
        h, st_s = _sample_mixer(h, m_p, mixed, P, w_in_l, w_out_l, s5p, l, caches, rope_s)
        h = _ffn_block(h, None, P, 'ffn2', l, FFN_ROWS, m_p, ns, l == DEPTH - 1)
        for i in range(6):
            new_p[i].append(st_p[i])
            new_s[i].append(st_s[i])
    y_p, y_s = h
    new_p = [jnp.stack(a, axis=0) for a in new_p]
    new_s = [jnp.stack(a, axis=0) for a in new_s]
    return (y_p.reshape(n, t, D_MODEL), y_s.reshape(ns, ts, D_MODEL), new_p[0], new_s[0], new_p[1], new_s[1],
            new_p[2], new_s[2], new_p[3], new_s[3], new_p[4], new_s[4], new_p[5], new_s[5])


def kernel(x_prompt, x_sample, state_ssm, cache_swa_kv, cache_dil0_kv, cache_dil1_kv, cache_dil2_kv, state_hgrn, ffn1_norm_pre, ffn1_w_gate, ffn1_w_up, ffn1_w_down, ffn1_norm_post, mix_norm_pre, w_in, ssm_a_re, ssm_a_im, ssm_log_dt, ssm_b_re, ssm_b_im, ssm_c_re, ssm_c_im, ssm_d, ssm_w_glu, ssm_b_glu, swa_sinks, hgrn_lower_bounds, out_norm_a, out_norm_b, out_norm_c, out_norm_d, w_out, mix_norm_post, ffn2_norm_pre, ffn2_w_gate, ffn2_w_up, ffn2_w_down, ffn2_norm_post):
    P = dict(ffn1_norm_pre=ffn1_norm_pre, ffn1_w_gate=ffn1_w_gate, ffn1_w_up=ffn1_w_up, ffn1_w_down=ffn1_w_down,
             ffn1_norm_post=ffn1_norm_post, mix_norm_pre=mix_norm_pre, w_in=w_in, ssm_a_re=ssm_a_re,
             ssm_a_im=ssm_a_im, ssm_log_dt=ssm_log_dt, ssm_b_re=ssm_b_re, ssm_b_im=ssm_b_im, ssm_c_re=ssm_c_re,
             ssm_c_im=ssm_c_im, ssm_d=ssm_d, ssm_w_glu=ssm_w_glu, ssm_b_glu=ssm_b_glu, swa_sinks=swa_sinks,
             hgrn_lower_bounds=hgrn_lower_bounds, out_norm_a=out_norm_a, out_norm_b=out_norm_b,
             out_norm_c=out_norm_c, out_norm_d=out_norm_d, w_out=w_out, mix_norm_post=mix_norm_post,
             ffn2_norm_pre=ffn2_norm_pre, ffn2_w_gate=ffn2_w_gate, ffn2_w_up=ffn2_w_up, ffn2_w_down=ffn2_w_down,
             ffn2_norm_post=ffn2_norm_post)
    caches = (state_ssm, cache_swa_kv, cache_dil0_kv, cache_dil1_kv, cache_dil2_kv, state_hgrn)
    return _forward(x_prompt, x_sample, caches, P)
```

```python
import functools
import math

import numpy as np
import jax
import jax.numpy as jnp
from jax import lax
from jax.experimental import pallas as pl
from jax.experimental.pallas import tpu as pltpu

F32 = jnp.float32
BF16 = jnp.bfloat16

D_MODEL = 2048
DEPTH = 2
PAST_LEN = 16384
HEAD_DIM = 64
ROPE_THETA = 10000.0
D_FF = 5504
NORM_EPS = 1e-6
NEG_BIG = -1e30
LB_FLOOR = 1e-30
SSM_GROUP = 16
SSM_STATE = 64
SSM_WIDTH = 448
SSM_GROUPS = 28
SWA_HEADS = 8
SWA_KV_HEADS = 2
SWA_GROUP = 4
SWA_WIDTH = 512
SWA_KV_WIDTH = 128
WINDOW = 128
DIL_PAIRS = ((128, 1), (512, 4), (2048, 16))
DIL_HPG = 3
DIL_GW = DIL_HPG * HEAD_DIM
DIL_WIDTH = 576
DIL_KV_WIDTH = 192
HGRN_HEAD_DIM = 128
HGRN_WIDTH = 512
HGRN_HEADS = 4
IN_COLS = 4224
ATTN_SCALE = HEAD_DIM ** -0.5

LANES = 128
FF_TILE = 256
N_FF_TILES = -(-D_FF // FF_TILE)
DOWN_TILE = 256
N_DOWN_TILES = D_MODEL // DOWN_TILE
FFN_ROWS = 1040
VMEM_LIMIT = 60 * 1024 * 1024

B_SLOT = SWA_WIDTH + 2 * SWA_KV_WIDTH
C_SLOT = 384
C_K = 256
OFF_B = 0
OFF_C = OFF_B + B_SLOT
OFF_D = (2048, 2560, 3584, 4096)
OFF_U = 3072
U_SLOT = 512
PROJ_COLS = 4608
WIN_TILE = 1536

S5_CHUNK = 16
S5_NPOW = S5_CHUNK + 1
HGRN_CHUNK = 128
HGRN_LEVELS = (64, 32, 16, 8, 4, 2, 1)


def _proj_segments():
    src = np.cumsum([0, SSM_WIDTH, SWA_WIDTH, SWA_KV_WIDTH, SWA_KV_WIDTH, DIL_WIDTH, DIL_KV_WIDTH, DIL_KV_WIDTH,
                     HGRN_WIDTH, HGRN_WIDTH, HGRN_WIDTH, HGRN_WIDTH])
    s_u, s_qb, s_kb, s_vb, s_qc, s_kc, s_vc, s_qd = src[:8]
    segs = [(OFF_B, s_qb, SWA_WIDTH), (OFF_B + SWA_WIDTH, s_kb, SWA_KV_WIDTH),
            (OFF_B + SWA_WIDTH + SWA_KV_WIDTH, s_vb, SWA_KV_WIDTH)]
    for g in range(3):
        base = OFF_C + g * C_SLOT
        segs += [(base, s_qc + g * DIL_GW, DIL_GW), (base + C_K, s_kc + g * HEAD_DIM, HEAD_DIM),
                 (base + C_K + HEAD_DIM, s_vc + g * HEAD_DIM, HEAD_DIM)]
    segs += [(off, s_qd + i * HGRN_WIDTH, HGRN_WIDTH) for i, off in enumerate(OFF_D)]
    segs += [(OFF_U, s_u, SSM_WIDTH)]
    return [(int(a), int(b), int(c)) for a, b, c in segs]


def _rotary_lane_mask():
    m = np.zeros((PROJ_COLS,), np.float32)
    m[OFF_B:OFF_B + SWA_WIDTH + SWA_KV_WIDTH] = 1.0
    for g in range(3):
        base = OFF_C + g * C_SLOT
        m[base:base + DIL_GW] = 1.0
        m[base + C_K:base + C_K + HEAD_DIM] = 1.0
    return m


def _cparams(sem):
    return pltpu.CompilerParams(dimension_semantics=sem, vmem_limit_bytes=VMEM_LIMIT)


def _v3(a):
    return a.reshape(a.shape[0], 1, a.shape[1])


def _rms(x, g):
    return x * lax.rsqrt(jnp.mean(x * x, axis=-1, keepdims=True) + NORM_EPS) * g


def _sigmoid(x):
    return 1.0 / (1.0 + jnp.exp(-x))


def _silu(x):
    return x * _sigmoid(x)


def _dot(a, b):
    return jnp.dot(a, b, preferred_element_type=F32)


def _dot_nt(a, b):
    return lax.dot_general(a, b, (((1,), (1,)), ((), ())), preferred_element_type=F32)


def _dot_tn(a, b):
    return lax.dot_general(a, b, (((0,), (0,)), ((), ())), preferred_element_type=F32)


def _ffn_kernel(*refs, tail, n_tail, tail_in, tail_out):
    refs = list(refs)
    x_ref = refs.pop(0)
    xs_ref = refs.pop(0) if tail_in else None
    gpre_ref, wg_ref, wu_ref, wd_ref, gpost_ref, o_ref = refs[:6]
    os_ref = refs[6] if tail_out else None
    xn_ref, hid_ref = refs[-2:]
    s = pl.program_id(1)
    in_last_block = pl.program_id(0) == pl.num_programs(0) - 1
    tail_rows = slice(tail, tail + n_tail)

    @pl.when(s == 0)
    def _():
        xn_ref[...] = _rms(x_ref[...], gpre_ref[...]).astype(BF16)
        if tail_in:

            @pl.when(in_last_block)
            def _():
                xn_ref[tail_rows, :] = _rms(xs_ref[...], gpre_ref[...]).astype(BF16)

    @pl.when(s < N_FF_TILES)
    def _():
        xn = xn_ref[...]
        gate = _dot(xn, wg_ref[...].astype(BF16))
        up = _dot(xn, wu_ref[...].astype(BF16))
        hid_ref[s] = (_silu(gate) * up).astype(BF16)

    for j in range(N_DOWN_TILES):

        @pl.when(s == N_FF_TILES + j)
        def _(j=j):
            hid = jnp.concatenate([hid_ref[f] for f in range(N_FF_TILES)], axis=1)[:, :D_FF]
            o_ref[:, j * DOWN_TILE:(j + 1) * DOWN_TILE] = _dot(hid, wd_ref[...].astype(BF16))

    @pl.when(s == pl.num_programs(1) - 1)
    def _():
        delta = 0.5 * _rms(o_ref[...], gpost_ref[...])
        res = x_ref[...] + delta
        o_ref[...] = res
        if tail_in or tail_out:

            @pl.when(in_last_block)
            def _():
                tail_res = xs_ref[...] + delta[tail_rows] if tail_in else res[tail_rows]
                if tail_out:
                    os_ref[...] = tail_res
                else:
                    o_ref[tail_rows, :] = tail_res


def _ffn(x, x_tail, gpre, wg, wu, wd, gpost, layer, tm, n_main, n_tail, tail_out):
    m = n_main + n_tail
    nblk = pl.cdiv(m, tm)
    tail = n_main - (nblk - 1) * tm
    assert 0 <= tail and tail + n_tail <= tm and tail % 8 == 0
    last_ff = N_FF_TILES - 1
    in_specs = [pl.BlockSpec((tm, D_MODEL), lambda i, s: (i, 0))]
    args = [x]
    if x_tail is not None:
        in_specs.append(pl.BlockSpec((n_tail, D_MODEL), lambda i, s: (0, 0)))
        args.append(x_tail)
    in_specs += [
        pl.BlockSpec((None, 1, D_MODEL), lambda i, s: (layer, 0, 0)),
        pl.BlockSpec((None, D_MODEL, FF_TILE), lambda i, s: (layer, 0, jnp.minimum(s, last_ff))),
        pl.BlockSpec((None, D_MODEL, FF_TILE), lambda i, s: (layer, 0, jnp.minimum(s, last_ff))),
        pl.BlockSpec((None, D_FF, DOWN_TILE), lambda i, s: (layer, 0, jnp.maximum(s - N_FF_TILES, 0))),
        pl.BlockSpec((None, 1, D_MODEL), lambda i, s: (layer, 0, 0)),
    ]
    args += [_v3(gpre), wg, wu, wd, _v3(gpost)]
    out_main = pl.BlockSpec((tm, D_MODEL), lambda i, s: (i, 0), pipeline_mode=pl.Buffered(1))
    if tail_out:
        out_specs = [out_main, pl.BlockSpec((n_tail, D_MODEL), lambda i, s: (0, 0))]
        out_shape = [jax.ShapeDtypeStruct((n_main, D_MODEL), F32), jax.ShapeDtypeStruct((n_tail, D_MODEL), F32)]
    else:
        out_specs = out_main
        out_shape = jax.ShapeDtypeStruct((m, D_MODEL), F32)
    return pl.pallas_call(
        functools.partial(_ffn_kernel, tail=tail, n_tail=n_tail, tail_in=x_tail is not None, tail_out=tail_out),
        grid=(nblk, N_FF_TILES + N_DOWN_TILES),
        in_specs=in_specs,
        out_specs=out_specs,
        out_shape=out_shape,
        scratch_shapes=[pltpu.VMEM((tm, D_MODEL), BF16), pltpu.VMEM((N_FF_TILES, tm, FF_TILE), BF16)],
        compiler_params=_cparams(("arbitrary", "arbitrary")),
    )(*args)


def _win_prep_kernel(w_ref, o_ref):
    o_ref[...] = jnp.zeros_like(o_ref)
    for dst, src, width in _proj_segments():
        o_ref[:, dst:dst + width] = w_ref[:, src:src + width].astype(BF16)


def _win_prep(w_in, layer):
    rows = 256
    return pl.pallas_call(
        _win_prep_kernel,
        grid=(D_MODEL // rows,),
        in_specs=[pl.BlockSpec((None, rows, IN_COLS), lambda i: (layer, i, 0))],
        out_specs=pl.BlockSpec((rows, PROJ_COLS), lambda i: (i, 0)),
        out_shape=jax.ShapeDtypeStruct((D_MODEL, PROJ_COLS), BF16),
        compiler_params=_cparams(("parallel",)),
    )(w_in)


def _cast_kernel(w_ref, o_ref):
    o_ref[...] = w_ref[...].astype(BF16)


def _cast_bf16(w, layer):
    _, r, c = w.shape
    rows = 256
    return pl.pallas_call(
        _cast_kernel,
        grid=(r // rows,),
        in_specs=[pl.BlockSpec((None, rows, c), lambda i: (layer, i, 0))],
        out_specs=pl.BlockSpec((rows, c), lambda i: (i, 0)),
        out_shape=jax.ShapeDtypeStruct((r, c), BF16),
        compiler_params=_cparams(("parallel",)),
    )(w)


def _win_kernel(x_ref, g_ref, w_ref, cos_ref, sin_ref, o_ref, *rest, chunks):
    if chunks:
        u_ref, xn_ref, st_ref = rest
    else:
        (xn_ref,) = rest
    j = pl.program_id(1)

    @pl.when(j == 0)
    def _():
        xn_ref[...] = _rms(x_ref[...], g_ref[...]).astype(BF16)

    rot = _rotary_lane_mask()
    for tile in range(PROJ_COLS // WIN_TILE):

        @pl.when(j == tile)
        def _(tile=tile):
            y = _dot(xn_ref[...], w_ref[...])
            cos = cos_ref[...]
            sin = sin_ref[...]
            lane = lax.broadcasted_iota(jnp.int32, cos.shape, 1)
            first_half = (lane % HEAD_DIM) < (HEAD_DIM // 2)
            for c in range(WIN_TILE // LANES):
                sl = slice(c * LANES, (c + 1) * LANES)
                flags = rot[tile * WIN_TILE + c * LANES:tile * WIN_TILE + (c + 1) * LANES]
                x = y[:, sl]
                lo, hi = bool(flags[:HEAD_DIM].all()), bool(flags[HEAD_DIM:].all())
                if not (lo or hi):
                    o_ref[:, sl] = x
                    continue
                partner = jnp.where(first_half,
                                    pltpu.roll(x, LANES - HEAD_DIM // 2, axis=1),
                                    pltpu.roll(x, HEAD_DIM // 2, axis=1))
                rotated = x * cos + partner * sin
                if lo and hi:
                    o_ref[:, sl] = rotated
                else:
                    is_rot = (lane < HEAD_DIM) if lo else (lane >= HEAD_DIM)
                    o_ref[:, sl] = jnp.where(is_rot, rotated, x)
            if chunks and tile == OFF_U // WIN_TILE:
                base = OFF_U % WIN_TILE
                for lb in range(U_SLOT // LANES):
                    st_ref[lb] = y[:, base + lb * LANES:base + (lb + 1) * LANES]
                _s5_rows_to_chunks(st_ref, u_ref)


def _win(h, g, w, cos_t, sin_t, layer, tm, chunks, row0=0):
    m = cos_t.shape[0]
    blk0 = row0 // tm
    cw = S5_CHUNK * SSM_GROUP
    out_specs = [pl.BlockSpec((tm, WIN_TILE), lambda i, j: (i, j))]
    out_shape = [jax.ShapeDtypeStruct((m, PROJ_COLS), F32)]
    scratch = [pltpu.VMEM((tm, D_MODEL), BF16)]
    if chunks:
        out_specs.append(pl.BlockSpec((SSM_GROUPS, tm // S5_CHUNK, cw), lambda i, j: (0, i, 0)))
        out_shape.append(jax.ShapeDtypeStruct((SSM_GROUPS, m // S5_CHUNK, cw), F32))
        scratch.append(pltpu.VMEM((U_SLOT // LANES, tm, LANES), F32))
    return pl.pallas_call(
        functools.partial(_win_kernel, chunks=chunks),
        grid=(m // tm, PROJ_COLS // WIN_TILE),
        in_specs=[
            pl.BlockSpec((tm, D_MODEL), lambda i, j: (i + blk0, 0)),
            pl.BlockSpec((None, 1, D_MODEL), lambda i, j: (layer, 0, 0)),
            pl.BlockSpec((D_MODEL, WIN_TILE), lambda i, j: (0, j)),
            pl.BlockSpec((tm, LANES), lambda i, j: (i, 0)),
            pl.BlockSpec((tm, LANES), lambda i, j: (i, 0)),
        ],
        out_specs=out_specs,
        out_shape=out_shape,
        scratch_shapes=scratch,
        compiler_params=_cparams(("parallel", "arbitrary")),
    )(h, _v3(g), w, cos_t, sin_t)


def _rope_tables(pos):
    half = HEAD_DIM // 2
    inv_freq = ROPE_THETA ** (-jnp.arange(half, dtype=F32) / half)
    ang = pos.astype(F32)[:, None] * inv_freq[None, :]
    cos, sin = jnp.cos(ang), jnp.sin(ang)
    reps = LANES // HEAD_DIM
    cos_t = jnp.tile(jnp.concatenate([cos, cos], axis=1), (1, reps))
    sin_t = jnp.tile(jnp.concatenate([-sin, sin], axis=1), (1, reps))
    return cos_t, sin_t


def _band_heads(q, kk, vv, valid, sinks):
    w = WINDOW
    heads = q.shape[1] // HEAD_DIM
    qs = jnp.concatenate([q[:, g * HEAD_DIM:(g + 1) * HEAD_DIM] for g in range(heads)], axis=0).astype(BF16)
    s = _dot_nt(kk, qs) * ATTN_SCALE
    s = jnp.where(jnp.concatenate([valid] * heads, axis=1), s, NEG_BIG)
    m = jnp.max(s, axis=0, keepdims=True)
    if sinks is not None:
        sink_row = jnp.concatenate([jnp.broadcast_to(sinks[g], (1, w)) for g in range(heads)], axis=1)
        m = jnp.maximum(m, sink_row)
    p = jnp.exp(s - m)
    l = jnp.sum(p, axis=0, keepdims=True)
    if sinks is not None:
        l = l + jnp.exp(sink_row - m)
    ot = _dot_tn(vv, p.astype(BF16)) / l
    lse = m + jnp.log(l)
    return ([ot[:, g * w:(g + 1) * w] for g in range(heads)], [lse[:, g * w:(g + 1) * w] for g in range(heads)])


def _untranspose(blocks):
    pairs = [jnp.concatenate(blocks[i:i + 2], axis=0).T for i in range(0, len(blocks), 2)]
    return jnp.concatenate(pairs, axis=1)


def _band_mask(no_prev):
    w = WINDOW
    k = lax.broadcasted_iota(jnp.int32, (2 * w, w), 0)
    q = lax.broadcasted_iota(jnp.int32, (2 * w, w), 1)
    valid = (k >= q) & (k <= q + w)
    if no_prev is False:
        return valid
    return valid & (k >= jnp.where(no_prev, w, 0))


def _swa_prompt_kernel(cur_ref, prev_ref, sink_ref, o_ref, *, nsub):
    w = WINDOW
    gw = SWA_GROUP * HEAD_DIM
    first = pl.program_id(1) == 0
    for j in range(nsub):
        valid = _band_mask(first if j == 0 else False)
        cur = cur_ref[0, j * w:(j + 1) * w, :]
        prev = prev_ref[0] if j == 0 else cur_ref[0, (j - 1) * w:j * w, SWA_WIDTH:B_SLOT]
        outs = []
        for h in range(SWA_KV_HEADS):
            ks = slice(SWA_WIDTH + h * HEAD_DIM, SWA_WIDTH + (h + 1) * HEAD_DIM)
            vs = slice(SWA_WIDTH + SWA_KV_WIDTH + h * HEAD_DIM, SWA_WIDTH + SWA_KV_WIDTH + (h + 1) * HEAD_DIM)
            kk = jnp.concatenate([prev[:, h * HEAD_DIM:(h + 1) * HEAD_DIM], cur[:, ks]], axis=0).astype(BF16)
            vv = jnp.concatenate([prev[:, SWA_KV_WIDTH + h * HEAD_DIM:SWA_KV_WIDTH + (h + 1) * HEAD_DIM],
                                  cur[:, vs]], axis=0).astype(BF16)
            sinks = [sink_ref[:, h * gw + g * HEAD_DIM:h * gw + g * HEAD_DIM + 1] for g in range(SWA_GROUP)]
            o, _ = _band_heads(cur[:, h * gw:(h + 1) * gw], kk, vv, valid, sinks)
            outs += o
        o_ref[0, j * w:(j + 1) * w, :] = _untranspose(outs)


def _swa_prompt(proj, sink):
    n, t, _ = proj.shape
    nsub = 2
    rows = WINDOW * nsub
    return pl.pallas_call(
        functools.partial(_swa_prompt_kernel, nsub=nsub),
        grid=(n, t // rows),
        in_specs=[
            pl.BlockSpec((1, rows, B_SLOT), lambda b, i: (b, i, OFF_B // B_SLOT)),
            pl.BlockSpec((1, WINDOW, 2 * SWA_KV_WIDTH),
                         lambda b, i: (b, jnp.maximum(i * nsub - 1, 0), (OFF_B + SWA_WIDTH) // (2 * SWA_KV_WIDTH))),
            pl.BlockSpec((1, SWA_WIDTH), lambda b, i: (0, 0)),
        ],
        out_specs=pl.BlockSpec((1, rows, SWA_WIDTH), lambda b, i: (b, i, 0)),
        out_shape=jax.ShapeDtypeStruct((n, t, SWA_WIDTH), F32),
        compiler_params=_cparams(("parallel", "arbitrary")),
    )(proj, proj, sink)


def _dil_prompt_kernel(cur_ref, prev_ref, o_ref, cur3_ref, out3_ref, *, dil, nsub):
    nl = C_SLOT // LANES
    span = WINDOW * dil
    first = pl.program_id(1) == 0
    for c in range(nl):
        cur3_ref[c] = cur_ref[0, :, c * LANES:(c + 1) * LANES]

    def one(j, r):
        def rows(base):
            return pl.ds(base + r, WINDOW, stride=dil) if dil > 1 else pl.ds(base, WINDOW)

        valid = _band_mask(first if j == 0 else False)
        cur = rows(j * span)
        q = jnp.concatenate([cur3_ref[0, cur, :], cur3_ref[1, cur, :]], axis=1)[:, 0:DIL_GW]
        kv = cur3_ref[2, cur, :]
        pkv = prev_ref[0, rows(0), :] if j == 0 else cur3_ref[2, rows((j - 1) * span), :]
        kk = jnp.concatenate([pkv[:, 0:HEAD_DIM], kv[:, 0:HEAD_DIM]], axis=0).astype(BF16)
        vv = jnp.concatenate([pkv[:, HEAD_DIM:], kv[:, HEAD_DIM:]], axis=0).astype(BF16)
        o, lse = _band_heads(q, kk, vv, valid, None)
        res = _untranspose(o + [jnp.broadcast_to(x, (HEAD_DIM, WINDOW)) for x in lse])
        for c in range(nl):
            out3_ref[c, cur, :] = res[:, c * LANES:(c + 1) * LANES]

    for j in range(nsub):
        if dil <= 4:
            for r in range(dil):
                one(j, r)
        else:
            lax.fori_loop(0, dil, lambda r, c, j=j: (one(j, r), c)[1], 0, unroll=2)
    for c in range(nl):
        o_ref[0, :, c * LANES:(c + 1) * LANES] = out3_ref[c]


def _dil_prompt(proj, group, dil):
    n, t, _ = proj.shape
    span = WINDOW * dil
    nsub = max(1, 4 // dil)
    rows = span * nsub
    slot = OFF_C + group * C_SLOT
    return pl.pallas_call(
        functools.partial(_dil_prompt_kernel, dil=dil, nsub=nsub),
        grid=(n, t // rows),
        in_specs=[
            pl.BlockSpec((1, rows, C_SLOT), lambda b, i: (b, i, slot // C_SLOT)),
            pl.BlockSpec((1, span, LANES), lambda b, i: (b, jnp.maximum(i * nsub - 1, 0), (slot + C_K) // LANES)),
        ],
        out_specs=pl.BlockSpec((1, rows, 2 * DIL_GW), lambda b, i: (b, i, 0)),
        out_shape=jax.ShapeDtypeStruct((n, t, 2 * DIL_GW), F32),
        scratch_shapes=[pltpu.VMEM((C_SLOT // LANES, rows, LANES), F32),
                        pltpu.VMEM((2 * DIL_GW // LANES, rows, LANES), F32)],
        compiler_params=_cparams(("parallel", "arbitrary")),
    )(proj, proj)


def _decode_heads(q, knew, vnew, kt, vt, dil, sinks):
    s = _dot(q.astype(BF16), kt.astype(BF16)) * ATTN_SCALE
    if dil > 1:
        row = lax.broadcasted_iota(jnp.int32, s.shape, 1)
        s = jnp.where(row % dil == 0, s, NEG_BIG)
    s_new = jnp.sum(q * knew, axis=-1, keepdims=True) * ATTN_SCALE
    m = jnp.maximum(jnp.max(s, axis=-1, keepdims=True), s_new)
    if sinks is not None:
        m = jnp.maximum(m, sinks)
    p = jnp.exp(s - m)
    p_new = jnp.exp(s_new - m)
    l = jnp.sum(p, axis=-1, keepdims=True) + p_new
    if sinks is not None:
        l = l + jnp.exp(sinks - m)
    o = (_dot_nt(p.astype(BF16), vt.astype(BF16)) + p_new * vnew) / l
    return o, m + jnp.log(l)


def _rows_of(x, heads):
    return jnp.concatenate([x[:, g * HEAD_DIM:(g + 1) * HEAD_DIM] for g in range(heads)], axis=0)


def _swa_sample_kernel(p_ref, cache_ref, sink_ref, o_ref, *, bn):
    for b in range(bn):
        row = p_ref[b:b + 1, :]
        outs = []
        for h in range(SWA_KV_HEADS):
            gw = SWA_GROUP * HEAD_DIM
            q = _rows_of(row[:, h * gw:(h + 1) * gw], SWA_GROUP)
            knew = row[:, SWA_WIDTH + h * HEAD_DIM:SWA_WIDTH + (h + 1) * HEAD_DIM]
            vnew = row[:, SWA_WIDTH + SWA_KV_WIDTH + h * HEAD_DIM:SWA_WIDTH + SWA_KV_WIDTH + (h + 1) * HEAD_DIM]
            sinks = _rows_of(sink_ref[:, h * gw:(h + 1) * gw], SWA_GROUP)[:, 0:1]
            o, _ = _decode_heads(q, knew, vnew, cache_ref[b, 0, h], cache_ref[b, 1, h], 1, sinks)
            outs += [o[g:g + 1] for g in range(SWA_GROUP)]
        o_ref[b:b + 1, :] = jnp.concatenate(outs, axis=1)


def _swa_sample(proj, cache, sink, layer):
    n = proj.shape[0]
    bn = 8
    return pl.pallas_call(
        functools.partial(_swa_sample_kernel, bn=bn),
        grid=(n // bn,),
        in_specs=[
            pl.BlockSpec((bn, B_SLOT), lambda i: (i, OFF_B // B_SLOT)),
            pl.BlockSpec((None, bn, 2, SWA_KV_HEADS, HEAD_DIM, WINDOW), lambda i: (layer, i, 0, 0, 0, 0)),
            pl.BlockSpec((1, SWA_WIDTH), lambda i: (0, 0)),
        ],
        out_specs=pl.BlockSpec((bn, SWA_WIDTH), lambda i: (i, 0)),
        out_shape=jax.ShapeDtypeStruct((n, SWA_WIDTH), F32),
        compiler_params=_cparams(("parallel",)),
    )(proj, cache, sink)


def _dil_sample_kernel(p_ref, cache_ref, o_ref, *, bn, dil):
    for b in range(bn):
        row = p_ref[b:b + 1, :]
        q = _rows_of(row[:, 0:DIL_GW], DIL_HPG)
        o, lse = _decode_heads(q, row[:, C_K:C_K + HEAD_DIM], row[:, C_K + HEAD_DIM:C_K + 2 * HEAD_DIM],
                               cache_ref[b, 0], cache_ref[b, 1], dil, None)
        lse = jnp.broadcast_to(lse, (DIL_HPG, HEAD_DIM))
        o_ref[b:b + 1, :] = jnp.concatenate([o[g:g + 1] for g in range(DIL_HPG)]
                                            + [lse[g:g + 1] for g in range(DIL_HPG)], axis=1)


def _dil_sample(proj, cache, group, dil, layer):
    n = proj.shape[0]
    bn = 8
    slot = OFF_C + group * C_SLOT
    return pl.pallas_call(
        functools.partial(_dil_sample_kernel, bn=bn, dil=dil),
        grid=(n // bn,),
        in_specs=[
            pl.BlockSpec((bn, C_SLOT), lambda i: (i, slot // C_SLOT)),
            pl.BlockSpec((None, bn, 2, HEAD_DIM, WINDOW * dil), lambda i: (layer, i, 0, 0, 0)),
        ],
        out_specs=pl.BlockSpec((bn, 2 * DIL_GW), lambda i: (i, 0)),
        out_shape=jax.ShapeDtypeStruct((n, 2 * DIL_GW), F32),
        compiler_params=_cparams(("parallel",)),
    )(proj, cache)


def _s5_param_kernel(are_ref, aim_ref, ldt_ref, bre_ref, bim_ref, cre_ref, cim_ref,
                     lam_re_ref, lam_im_ref, cl_re_ref, cl_im_ref, r_re_ref, r_im_ref, k_ref):
    ar = are_ref[0]
    ai = aim_ref[0]
    dt = jnp.exp(ldt_ref[0])
    nrow = lam_re_ref.shape[1]
    ri = lax.broadcasted_iota(jnp.int32, (nrow, 1), 0)
    pw = jnp.where(ri < S5_NPOW, ri, jnp.left_shift(S5_CHUNK, jnp.maximum(ri - S5_NPOW, 0))).astype(F32)
    mag = jnp.exp(pw * (ar * dt))
    ang = pw * (ai * dt)
    lam_re = mag * jnp.cos(ang)
    lam_im = mag * jnp.sin(ang)
    lam_re_ref[0] = lam_re
    lam_im_ref[0] = lam_im
    l1r, l1i = lam_re[1:2], lam_im[1:2]
    den = ar * ar + ai * ai
    z_re = ((l1r - 1.0) * ar + l1i * ai) / den
    z_im = (l1i * ar - (l1r - 1.0) * ai) / den
    b_re, b_im = bre_ref[0], bim_ref[0]
    bb_re = z_re * b_re - z_im * b_im
    bb_im = z_re * b_im + z_im * b_re
    c_re, c_im = cre_ref[0], cim_ref[0]
    cls_re, cls_im = [], []
    for d in range(S5_NPOW):
        lr, li = lam_re[d:d + 1], lam_im[d:d + 1]
        cr = c_re * lr - c_im * li
        ci = c_re * li + c_im * lr
        cl_re_ref[0, d * SSM_GROUP:(d + 1) * SSM_GROUP, :] = cr
        cl_im_ref[0, d * SSM_GROUP:(d + 1) * SSM_GROUP, :] = ci
        if d < S5_CHUNK:
            cls_re.append(cr)
            cls_im.append(ci)
    for s in range(S5_CHUNK):
        lr = lam_re[S5_CHUNK - 1 - s:S5_CHUNK - s]
        li = lam_im[S5_CHUNK - 1 - s:S5_CHUNK - s]
        r_re_ref[0, s * SSM_GROUP:(s + 1) * SSM_GROUP, :] = lr * bb_re - li * bb_im
        r_im_ref[0, s * SSM_GROUP:(s + 1) * SSM_GROUP, :] = lr * bb_im + li * bb_re
    call_re = jnp.concatenate(cls_re, axis=0)
    call_im = jnp.concatenate(cls_im, axis=0)
    hp = lax.Precision.HIGHEST
    nt = (((1,), (1,)), ((), ()))
    kt = (lax.dot_general(bb_re, call_re, nt, precision=hp, preferred_element_type=F32)
          - lax.dot_general(bb_im, call_im, nt, precision=hp, preferred_element_type=F32))
    cw = S5_CHUNK * SSM_GROUP
    for s in range(S5_CHUNK):
        if s == 0:
            blk = kt
        else:
            blk = jnp.concatenate([jnp.zeros((SSM_GROUP, s * SSM_GROUP), F32), kt[:, :cw - s * SSM_GROUP]], axis=1)
        k_ref[0, s * SSM_GROUP:(s + 1) * SSM_GROUP, :] = blk


def _s5_params(a_re, a_im, log_dt, b_re, b_im, c_re, c_im, nsteps):
    g = SSM_GROUPS
    nrow = ((S5_NPOW + nsteps + 7) // 8) * 8
    row = lambda i: (i, 0, 0)
    shp = lambda r, c: jax.ShapeDtypeStruct((g, r, c), F32)
    ncl = S5_NPOW * SSM_GROUP
    nr = S5_CHUNK * SSM_GROUP
    return pl.pallas_call(
        _s5_param_kernel,
        grid=(g,),
        in_specs=[
            pl.BlockSpec((1, 1, SSM_STATE), row),
            pl.BlockSpec((1, 1, SSM_STATE), row),
            pl.BlockSpec((1, 1, 1), row),
            pl.BlockSpec((1, SSM_GROUP, SSM_STATE), row),
            pl.BlockSpec((1, SSM_GROUP, SSM_STATE), row),
            pl.BlockSpec((1, SSM_GROUP, SSM_STATE), row),
            pl.BlockSpec((1, SSM_GROUP, SSM_STATE), row),
        ],
        out_specs=[
            pl.BlockSpec((1, nrow, SSM_STATE), row), pl.BlockSpec((1, nrow, SSM_STATE), row),
            pl.BlockSpec((1, ncl, SSM_STATE), row), pl.BlockSpec((1, ncl, SSM_STATE), row),
            pl.BlockSpec((1, nr, SSM_STATE), row), pl.BlockSpec((1, nr, SSM_STATE), row),
            pl.BlockSpec((1, nr, nr), row),
        ],
        out_shape=[shp(nrow, SSM_STATE), shp(nrow, SSM_STATE), shp(ncl, SSM_STATE), shp(ncl, SSM_STATE),
                   shp(nr, SSM_STATE), shp(nr, SSM_STATE), shp(nr, nr)],
        compiler_params=_cparams(("parallel",)),
    )(a_re.reshape(g, 1, SSM_STATE), a_im.reshape(g, 1, SSM_STATE), log_dt.reshape(g, 1, 1),
      jnp.swapaxes(b_re, 1, 2), jnp.swapaxes(b_im, 1, 2), c_re, c_im)


def _s5_rows_to_chunks(st_ref, u_ref):
    nc = st_ref.shape[1] // S5_CHUNK
    per_blk = LANES // SSM_GROUP
    slot = lax.broadcasted_iota(jnp.int32, (nc, LANES), 1) // SSM_GROUP
    for g in range(SSM_GROUPS):
        acc = [jnp.zeros((nc, LANES), F32) for _ in range(S5_CHUNK // per_blk)]
        for s in range(S5_CHUNK):
            v = st_ref[g // per_blk, pl.ds(s, nc, stride=S5_CHUNK), :]
            shift = (SSM_GROUP * (s % per_blk) - SSM_GROUP * (g % per_blk)) % LANES
            if shift:
                v = pltpu.roll(v, shift, axis=1)
            acc[s // per_blk] = jnp.where(slot == s % per_blk, v, acc[s // per_blk])
        for k, a in enumerate(acc):
            u_ref[g, :, k * LANES:(k + 1) * LANES] = a


def _s5_chunks_to_rows(y_ref, st_ref):
    nc = y_ref.shape[1]
    per_blk = LANES // SSM_GROUP
    slot = lax.broadcasted_iota(jnp.int32, (nc, LANES), 1) // SSM_GROUP
    nblk = -(-SSM_WIDTH // LANES)
    for lb in range(nblk):
        groups = range(lb * per_blk, min((lb + 1) * per_blk, SSM_GROUPS))
        for s in range(S5_CHUNK):
            acc = jnp.zeros((nc, LANES), F32)
            for g in groups:
                v = y_ref[g, :, (s // per_blk) * LANES:(s // per_blk + 1) * LANES]
                shift = (SSM_GROUP * (g % per_blk) - SSM_GROUP * (s % per_blk)) % LANES
                if shift:
                    v = pltpu.roll(v, shift, axis=1)
                acc = jnp.where(slot == g % per_blk, v, acc)
            st_ref[lb, pl.ds(s, nc, stride=S5_CHUNK), :] = acc
    return jnp.concatenate([st_ref[lb][:, 0:min(LANES, SSM_WIDTH - lb * LANES)] for lb in range(nblk)], axis=1)


def _s5_prompt_kernel(u_ref, mt_ref, r_re_ref, r_im_ref, cl_re_ref, cl_im_ref, lam_re_ref, lam_im_ref, d_ref,
                      y_ref, xre_ref, xim_ref, *, nseq, nchunk):
    u = u_ref[0]
    ub = u.astype(BF16)
    y = _dot(ub, mt_ref[0].astype(BF16))
    xr = _dot(ub, r_re_ref[0].astype(BF16))
    xi = _dot(ub, r_im_ref[0].astype(BF16))
    rows = nseq * nchunk
    cidx = lax.broadcasted_iota(jnp.int32, (rows, SSM_STATE), 0) % nchunk
    lam_re, lam_im = lam_re_ref[0], lam_im_ref[0]
    for k in range(int(math.log2(nchunk))):
        sh = 1 << k
        lr = lam_re[S5_NPOW + k:S5_NPOW + k + 1]
        li = lam_im[S5_NPOW + k:S5_NPOW + k + 1]
        ok = cidx >= sh
        pr = jnp.where(ok, pltpu.roll(xr, sh, axis=0), 0.0)
        pi = jnp.where(ok, pltpu.roll(xi, sh, axis=0), 0.0)
        xr, xi = xr + lr * pr - li * pi, xi + lr * pi + li * pr
    ok = cidx >= 1
    er = jnp.where(ok, pltpu.roll(xr, 1, axis=0), 0.0)
    ei = jnp.where(ok, pltpu.roll(xi, 1, axis=0), 0.0)
    cl_re = cl_re_ref[0][SSM_GROUP:, :]
    cl_im = cl_im_ref[0][SSM_GROUP:, :]
    y = y + _dot_nt(er.astype(BF16), cl_re.astype(BF16)) - _dot_nt(ei.astype(BF16), cl_im.astype(BF16))
    y_ref[0] = y + d_ref[0] * u
    last = [s * nchunk + nchunk - 1 for s in range(nseq)]
    xre_ref[0] = jnp.concatenate([xr[i:i + 1] for i in last], axis=0)
    xim_ref[0] = jnp.concatenate([xi[i:i + 1] for i in last], axis=0)


def _s5_prompt(u, mt, r_re, r_im, cl_re, cl_im, lam_re, lam_im, d_t, nseq, nchunk):
    g = SSM_GROUPS
    rows = nseq * nchunk
    cw = S5_CHUNK * SSM_GROUP
    row = lambda i: (i, 0, 0)
    full = lambda a: pl.BlockSpec((1,) + a.shape[1:], row)
    return pl.pallas_call(
        functools.partial(_s5_prompt_kernel, nseq=nseq, nchunk=nchunk),
        grid=(g,),
        in_specs=[full(u), full(mt), full(r_re), full(r_im), full(cl_re), full(cl_im), full(lam_re), full(lam_im),
                  full(d_t)],
        out_specs=[pl.BlockSpec((1, rows, cw), row), pl.BlockSpec((1, nseq, SSM_STATE), row),
                   pl.BlockSpec((1, nseq, SSM_STATE), row)],
        out_shape=[jax.ShapeDtypeStruct((g, rows, cw), F32), jax.ShapeDtypeStruct((g, nseq, SSM_STATE), F32),
                   jax.ShapeDtypeStruct((g, nseq, SSM_STATE), F32)],
        compiler_params=_cparams(("parallel",)),
    )(u, mt, r_re, r_im, cl_re, cl_im, lam_re, lam_im, d_t)


def _s5_sample_kernel(u_ref, x0r_ref, x0i_ref, r_re_ref, r_im_ref, cl_re_ref, cl_im_ref, lam_re_ref, lam_im_ref,
                      d_ref, y_ref, xre_ref, xim_ref):
    u = u_ref[0]
    ub = u.astype(BF16)
    lo = (S5_CHUNK - 1) * SSM_GROUP
    bb_re = r_re_ref[0][lo:lo + SSM_GROUP, :]
    bb_im = r_im_ref[0][lo:lo + SSM_GROUP, :]
    lr, li = lam_re_ref[0][1:2], lam_im_ref[0][1:2]
    x0r, x0i = x0r_ref[0], x0i_ref[0]
    xr = _dot(ub, bb_re.astype(BF16)) + (lr * x0r - li * x0i)
    xi = _dot(ub, bb_im.astype(BF16)) + (lr * x0i + li * x0r)
    c_re = cl_re_ref[0][:SSM_GROUP, :]
    c_im = cl_im_ref[0][:SSM_GROUP, :]
    y_ref[0] = (_dot_nt(xr.astype(BF16), c_re.astype(BF16)) - _dot_nt(xi.astype(BF16), c_im.astype(BF16))
                + d_ref[0] * u)
    xre_ref[0] = xr
    xim_ref[0] = xi


def _s5_sample(u, x0r, x0i, r_re, r_im, cl_re, cl_im, lam_re, lam_im, d):
    g, n = u.shape[0], u.shape[1]
    row = lambda i: (i, 0, 0)
    full = lambda a: pl.BlockSpec((1,) + a.shape[1:], row)
    return pl.pallas_call(
        _s5_sample_kernel,
        grid=(g,),
        in_specs=[full(u), full(x0r), full(x0i), full(r_re), full(r_im), full(cl_re), full(cl_im), full(lam_re),
                  full(lam_im), full(d)],
        out_specs=[pl.BlockSpec((1, n, SSM_GROUP), row), pl.BlockSpec((1, n, SSM_STATE), row),
                   pl.BlockSpec((1, n, SSM_STATE), row)],
        out_shape=[jax.ShapeDtypeStruct((g, n, SSM_GROUP), F32), jax.ShapeDtypeStruct((g, n, SSM_STATE), F32),
                   jax.ShapeDtypeStruct((g, n, SSM_STATE), F32)],
        compiler_params=_cparams(("parallel",)),
    )(u, x0r, x0i, r_re, r_im, cl_re, cl_im, lam_re, lam_im, d)


def _log_sigmoid(x):
    return jnp.minimum(x, 0.0) - jnp.log1p(jnp.exp(-jnp.abs(x)))


def _logaddexp(a, b):
    return jnp.maximum(a, b) + jnp.log1p(jnp.exp(-jnp.abs(a - b)))


def _hgrn_gates(fpre, lb):
    log_f = _logaddexp(jnp.log(jnp.maximum(lb, LB_FLOOR)), jnp.log1p(-lb) + _log_sigmoid(fpre))
    k = (1.0 - lb) * _sigmoid(-fpre)
    return log_f, k


def _hgrn_scan_matrix():
    c = HGRN_CHUNK
    tri = np.tril(np.ones((c, c), np.float32))
    t = np.arange(c)
    mats = [tri]
    for m in HGRN_LEVELS:
        mats.append(tri[(t // (2 * m)) * (2 * m) + m - 1])
    return np.concatenate(mats, axis=0)


def _hgrn_prompt_kernel(q_ref, f_ref, i_ref, g_ref, lbraw_ref, gn_ref, scan_ref, o_ref, s_ref, st_ref, *, layer):
    parts = (q_ref, f_ref, i_ref, g_ref)
    cidx = pl.program_id(1)
    c = HGRN_CHUNK
    hd = HGRN_HEAD_DIM

    @pl.when(cidx == 0)
    def _():
        st_ref[...] = jnp.zeros_like(st_ref)

    a = lbraw_ref[...]
    e = jnp.exp(a - jnp.max(a, axis=0, keepdims=True))
    p = e / jnp.sum(e, axis=0, keepdims=True)
    run = p[0:1]
    for l in range(1, layer + 1):
        run = run + p[l:l + 1]
    lb_all = run - p[0:1]

    r = lax.broadcasted_iota(jnp.int32, (c, c), 0)
    s = lax.broadcasted_iota(jnp.int32, (c, c), 1)
    pairs = []
    for m in HGRN_LEVELS:
        sh = int(math.log2(2 * m))
        pairs.append(((r >> sh) == (s >> sh)) & ((r & m) != 0) & ((s & m) == 0))
    scan = scan_ref[...]
    for h in range(HGRN_HEADS):
        col = lambda part: parts[part][0, :, h * hd:(h + 1) * hd]
        log_f, k = _hgrn_gates(col(1), lb_all[:, h * hd:(h + 1) * hd])
        q = _silu(col(0))
        v = col(2)
        hi = log_f.astype(BF16)
        rest = log_f - hi.astype(F32)
        mid = rest.astype(BF16)
        lo = (rest - mid.astype(F32)).astype(BF16)
        cum3 = _dot(scan, jnp.concatenate([hi, mid, lo], axis=1))
        cum = cum3[:, 0:hd] + cum3[:, hd:2 * hd] + cum3[:, 2 * hd:3 * hd]
        b = cum[0:c]
        att = jnp.where(r == s, jnp.sum(q * k, axis=-1, keepdims=True), 0.0)
        for lvl, m in enumerate(HGRN_LEVELS):
            d = b - cum[(lvl + 1) * c:(lvl + 2) * c]
            qs = q * jnp.exp(jnp.minimum(d, 0.0))
            ks = k * jnp.exp(jnp.minimum(-d, 0.0))
            att = jnp.where(pairs[lvl], _dot_nt(qs.astype(BF16), ks.astype(BF16)), att)
        st = st_ref[h]
        vb = v.astype(BF16)
        o = _dot(att.astype(BF16), vb) + _dot_nt((q * jnp.exp(b)).astype(BF16), st.astype(BF16))
        bl = b[c - 1:c]
        st_new = jnp.exp(bl) * st + _dot_tn(vb, (k * jnp.exp(bl - b)).astype(BF16))
        st_ref[h] = st_new
        o = o * lax.rsqrt(jnp.mean(o * o, axis=-1, keepdims=True) + NORM_EPS)
        o_ref[0, :, h * hd:(h + 1) * hd] = o * gn_ref[:, h * hd:(h + 1) * hd] * _silu(col(3))

    @pl.when(cidx == pl.num_programs(1) - 1)
    def _():
        s_ref[0] = st_ref[...]


def _hgrn_prompt(proj, lbraw, gn, layer):
    n, t, _ = proj.shape
    c = HGRN_CHUNK
    hd = HGRN_HEAD_DIM
    scan = jnp.asarray(_hgrn_scan_matrix(), dtype=BF16)
    return pl.pallas_call(
        functools.partial(_hgrn_prompt_kernel, layer=layer),
        grid=(n, t // c),
        in_specs=[pl.BlockSpec((1, c, HGRN_WIDTH), lambda b, j, blk=off // HGRN_WIDTH: (b, j, blk)) for off in OFF_D]
                 + [pl.BlockSpec((DEPTH, HGRN_WIDTH), lambda b, j: (0, 0)),
                  pl.BlockSpec((None, 1, HGRN_WIDTH), lambda b, j: (layer, 0, 0)),
                  pl.BlockSpec(scan.shape, lambda b, j: (0, 0))],
        out_specs=[pl.BlockSpec((1, c, HGRN_WIDTH), lambda b, j: (b, j, 0)),
                   pl.BlockSpec((1, HGRN_HEADS, hd, hd), lambda b, j: (b, 0, 0, 0))],
        out_shape=[jax.ShapeDtypeStruct((n, t, HGRN_WIDTH), F32),
                   jax.ShapeDtypeStruct((n, HGRN_HEADS, hd, hd), F32)],
        scratch_shapes=[pltpu.VMEM((HGRN_HEADS, hd, hd), F32)],
        compiler_params=_cparams(("parallel", "arbitrary")),
    )(proj, proj, proj, proj, lbraw, _v3(gn), scan)


def _hgrn_sample_kernel(qt_ref, ft_ref, lbt_ref, v_ref, g_ref, gn_ref, s0_ref, o_ref, s_ref, *, layer, n):
    hd = HGRN_HEAD_DIM
    a = lbt_ref[0]
    e = jnp.exp(a - jnp.max(a, axis=1, keepdims=True))
    p = e / jnp.sum(e, axis=1, keepdims=True)
    run = p[:, 0:1]
    for l in range(1, layer + 1):
        run = run + p[:, l:l + 1]
    lb = run - p[:, 0:1]
    log_f, k = _hgrn_gates(ft_ref[0], lb)
    f = jnp.exp(log_f)
    q = _silu(qt_ref[0])
    v = v_ref[...]
    gate = g_ref[...]
    rows = []
    for b in range(n):
        s_new = f[:, b:b + 1] * s0_ref[b, 0] + k[:, b:b + 1] * v[b:b + 1, :]
        s_ref[b, 0] = s_new
        rows.append(jnp.sum(q[:, b:b + 1] * s_new, axis=0, keepdims=True))
    o = jnp.concatenate(rows, axis=0)
    o = o * lax.rsqrt(jnp.mean(o * o, axis=-1, keepdims=True) + NORM_EPS)
    o_ref[...] = o * gn_ref[...] * _silu(gate)


def _hgrn_sample(qt, ft, lbt, proj, gn, s0, layer):
    n = proj.shape[0]
    hd = HGRN_HEAD_DIM
    v_blk = OFF_D[2] // hd
    g_blk = OFF_D[3] // hd
    return pl.pallas_call(
        functools.partial(_hgrn_sample_kernel, layer=layer, n=n),
        grid=(HGRN_HEADS,),
        in_specs=[pl.BlockSpec((1, hd, n), lambda h: (h, 0, 0)),
                  pl.BlockSpec((1, hd, n), lambda h: (h, 0, 0)),
                  pl.BlockSpec((1, hd, DEPTH), lambda h: (h, 0, 0)),
                  pl.BlockSpec((n, hd), lambda h: (0, v_blk + h)),
                  pl.BlockSpec((n, hd), lambda h: (0, g_blk + h)),
                  pl.BlockSpec((None, 1, hd), lambda h: (layer, 0, h)),
                  pl.BlockSpec((None, n, 1, hd, hd), lambda h: (layer, 0, h, 0, 0))],
        out_specs=[pl.BlockSpec((n, hd), lambda h: (0, h)),
                   pl.BlockSpec((n, 1, hd, hd), lambda h: (0, h, 0, 0))],
        out_shape=[jax.ShapeDtypeStruct((n, HGRN_WIDTH), F32),
                   jax.ShapeDtypeStruct((n, HGRN_HEADS, hd, hd), F32)],
        compiler_params=_cparams(("parallel",)),
    )(qt, ft, lbt, proj, proj, _v3(gn), s0)


def _gelu_tanh(x):
    return 0.5 * x * (1.0 + jnp.tanh(math.sqrt(2.0 / math.pi) * (x + 0.044715 * (x * x * x))))


def _mixout_kernel(h_ref, ya_ref, ob_ref, oc0_ref, oc1_ref, oc2_ref, od_ref,
                   wglu_ref, bglu_ref, ga_ref, gb_ref, gc_ref, wo_ref, gpost_ref, o_ref, *scratch):
    ya = _s5_chunks_to_rows(ya_ref, scratch[0]) if scratch else ya_ref[...]
    z = _gelu_tanh(ya)
    gate = _sigmoid(_dot(z.astype(BF16), wglu_ref[...].astype(BF16)) + bglu_ref[...])
    out_a = _rms(z * gate, ga_ref[...])
    out_b = _rms(ob_ref[...], gb_ref[...])
    ocs = [oc0_ref, oc1_ref, oc2_ref]
    ls = [r[:, DIL_GW:2 * DIL_GW] for r in ocs]
    m = jnp.maximum(jnp.maximum(ls[0], ls[1]), ls[2])
    es = [jnp.exp(l - m) for l in ls]
    den = es[0] + es[1] + es[2]
    cs = [ocs[gi][:, 0:DIL_GW] * (es[gi] / den) for gi in range(3)]
    ss = sum(jnp.sum(c * c, axis=-1, keepdims=True) for c in cs)
    inv = lax.rsqrt(ss / DIL_WIDTH + NORM_EPS)
    gc = gc_ref[...]
    o1, o2, o3 = SSM_WIDTH, SSM_WIDTH + SWA_WIDTH, SSM_WIDTH + SWA_WIDTH + DIL_WIDTH
    wo = lambda lo, hi: wo_ref[lo:hi, :]
    mix = _dot(out_a.astype(BF16), wo(0, o1)) + _dot(out_b.astype(BF16), wo(o1, o2))
    for gi in range(3):
        c = cs[gi] * inv * gc[:, gi * DIL_GW:(gi + 1) * DIL_GW]
        mix = mix + _dot(c.astype(BF16), wo(o2 + gi * DIL_GW, o2 + (gi + 1) * DIL_GW))
    mix = mix + _dot(od_ref[...].astype(BF16), wo(o3, D_MODEL))
    o_ref[...] = h_ref[...] + _rms(mix, gpost_ref[...])


def _mixout(h, ya, ob, ocs, od, lp_all, w_out_l, layer, tm, row0=0):
    m = ob.shape[0]
    blk0 = row0 // tm
    rowblk = lambda a: pl.BlockSpec((tm, a.shape[1]), lambda i: (i, 0))
    vec = lambda a: pl.BlockSpec((None, 1, a.shape[1]), lambda i: (layer, 0, 0))
    mat = lambda a: pl.BlockSpec((None,) + a.shape[1:], lambda i: (layer, 0, 0))
    whole = lambda a: pl.BlockSpec(a.shape, lambda i: (0, 0), pipeline_mode=pl.Buffered(1))
    acts = [h, ya, ob, *ocs, od]
    names = ['ssm_w_glu', 'ssm_b_glu', 'out_norm_a', 'out_norm_b', 'out_norm_c', 'w_out', 'mix_norm_post']
    params = [w_out_l if k == 'w_out' else lp_all[k] for k in names]
    spec = lambda k, p: whole(p) if k == 'w_out' else (mat(p) if p.ndim == 3 else vec(p))
    act_specs = [rowblk(a) for a in acts]
    act_specs[0] = pl.BlockSpec((tm, D_MODEL), lambda i: (i + blk0, 0))
    scratch = []
    chunked = ya.ndim == 3
    if chunked:
        act_specs[1] = pl.BlockSpec((SSM_GROUPS, tm // S5_CHUNK, ya.shape[2]), lambda i: (0, i, 0))
        scratch = [pltpu.VMEM((-(-SSM_WIDTH // LANES), tm, LANES), F32)]
    args = [*acts, *[_v3(p) if p.ndim == 2 and k != 'w_out' else p for k, p in zip(names, params)]]
    in_specs = act_specs + [spec(k, p) for k, p in zip(names, params)]
    return pl.pallas_call(
        _mixout_kernel,
        grid=(m // tm,),
        in_specs=in_specs,
        out_specs=pl.BlockSpec((tm, D_MODEL), lambda i: (i, 0)),
        out_shape=jax.ShapeDtypeStruct((m, D_MODEL), F32),
        scratch_shapes=scratch,
        compiler_params=_cparams(("parallel",)),
    )(*args)


def _ffn_block(x, x_tail, P, name, layer, tm, n_main, n_tail, tail_out):
    return _ffn(x, x_tail, P[name + '_norm_pre'], P[name + '_w_gate'], P[name + '_w_up'], P[name + '_w_down'],
                P[name + '_norm_post'], layer, tm, n_main, n_tail, tail_out)


def _prompt_mixer(h, P, w_in_l, w_out_l, s5p, layer, n, t, rope, tm):
    m = n * t
    proj2, u = _win(h, P['mix_norm_pre'], w_in_l, rope[0], rope[1], layer, tm, True)
    proj = proj2.reshape(n, t, PROJ_COLS)

    lam_re, lam_im, cl_re, cl_im, r_re, r_im, kmat = s5p
    nchunk = t // S5_CHUNK
    d_t = jnp.tile(P['ssm_d'][layer], (1, S5_CHUNK)).reshape(SSM_GROUPS, 1, S5_CHUNK * SSM_GROUP)
    y, xre, xim = _s5_prompt(u, kmat, r_re, r_im, cl_re, cl_im, lam_re, lam_im, d_t, n, nchunk)
    ssm_new = jnp.stack([xre, xim], axis=-1).transpose(1, 0, 2, 3)

    sink = jnp.repeat(P['swa_sinks'][layer], HEAD_DIM).reshape(1, SWA_WIDTH)
    ob = _swa_prompt(proj, sink).reshape(m, SWA_WIDTH)
    keep = min(WINDOW, t)
    kv = proj[:, t - keep:, OFF_B + SWA_WIDTH:OFF_B + B_SLOT]
    swa_new = kv.reshape(n, keep, 2, SWA_KV_HEADS, HEAD_DIM)

    ocs, dil_new = [], []
    for gi, (win, dil) in enumerate(DIL_PAIRS):
        ocs.append(_dil_prompt(proj, gi, dil).reshape(m, 2 * DIL_GW))
        keep = min(win, t)
        lo = OFF_C + gi * C_SLOT + C_K
        dil_new.append(proj[:, t - keep:, lo:lo + 2 * HEAD_DIM].reshape(n, keep, 2, HEAD_DIM))

    od, st = _hgrn_prompt(proj, P['hgrn_lower_bounds'], P['out_norm_d'], layer)
    hgrn_new = jnp.swapaxes(st, -1, -2)

    h = _mixout(h, y, ob, ocs, od.reshape(m, HGRN_WIDTH), P, w_out_l, layer, min(tm, 512))
    return h, (ssm_new, swa_new, dil_new[0], dil_new[1], dil_new[2], hgrn_new)


def _sample_mixer(h, row0, P, w_in_l, w_out_l, s5p, layer, caches, rope):
    n = rope[0].shape[0]
    state_ssm, cache_swa, cache_d0, cache_d1, cache_d2, state_hgrn = caches
    (proj,) = _win(h, P['mix_norm_pre'], w_in_l, rope[0], rope[1], layer, n, False, row0)

    lam_re, lam_im, cl_re, cl_im, r_re, r_im, _ = s5p
    u = proj[:, OFF_U:OFF_U + SSM_WIDTH].reshape(n, SSM_GROUPS, SSM_GROUP).transpose(1, 0, 2)
    x0 = state_ssm[layer].transpose(1, 0, 2, 3)
    y, xre, xim = _s5_sample(u, x0[..., 0], x0[..., 1], r_re, r_im, cl_re, cl_im, lam_re, lam_im,
                             P['ssm_d'][layer].reshape(SSM_GROUPS, 1, SSM_GROUP))
    ya = y.transpose(1, 0, 2).reshape(n, SSM_WIDTH)
    ssm_new = jnp.stack([xre, xim], axis=-1).transpose(1, 0, 2, 3)

    sink = jnp.repeat(P['swa_sinks'][layer], HEAD_DIM).reshape(1, SWA_WIDTH)
    ob = _swa_sample(proj, cache_swa.transpose(0, 1, 3, 4, 5, 2), sink, layer)
    swa_new = proj[:, OFF_B + SWA_WIDTH:OFF_B + B_SLOT].reshape(n, 1, 2, SWA_KV_HEADS, HEAD_DIM)

    ocs, dil_new = [], []
    for gi, (buf, (win, dil)) in enumerate(zip((cache_d0, cache_d1, cache_d2), DIL_PAIRS)):
        ocs.append(_dil_sample(proj, buf.transpose(0, 1, 3, 4, 2), gi, dil, layer))
        lo = OFF_C + gi * C_SLOT + C_K
        dil_new.append(proj[:, lo:lo + 2 * HEAD_DIM].reshape(n, 1, 2, HEAD_DIM))

    part = lambda i: proj[:, OFF_D[i]:OFF_D[i] + HGRN_WIDTH].reshape(n, HGRN_HEADS, HGRN_HEAD_DIM)
    cm = lambda a: a.transpose(1, 2, 0)
    lbt = P['hgrn_lower_bounds'].reshape(DEPTH, HGRN_HEADS, HGRN_HEAD_DIM).transpose(1, 2, 0)
    od, hgrn_new = _hgrn_sample(cm(part(0)), cm(part(1)), lbt, proj, P['out_norm_d'], state_hgrn, layer)

    h = _mixout(h, ya, ob, ocs, od, P, w_out_l, layer, n, row0)
    return h, (ssm_new, swa_new, dil_new[0], dil_new[1], dil_new[2], hgrn_new)


def _forward(x_prompt, x_sample, caches, P):
    n, t, _ = x_prompt.shape
    ns, ts, _ = x_sample.shape
    assert ts == 1
    tm = 1024
    m_p = n * t
    rope_p = _rope_tables(jnp.tile(jnp.arange(t, dtype=jnp.int32), n))
    rope_s = _rope_tables(jnp.full((ns,), PAST_LEN, dtype=jnp.int32))
    nsteps = int(math.log2(t // S5_CHUNK))
    h = x_prompt.reshape(m_p, D_MODEL)
    h_tail = x_sample.reshape(ns, D_MODEL)
    new_p = [[] for _ in range(6)]
    new_s = [[] for _ in range(6)]
    for l in range(DEPTH):
        w_in_l = _win_prep(P['w_in'], l)
        w_out_l = _cast_bf16(P['w_out'], l)
        s5p = _s5_params(P['ssm_a_re'][l], P['ssm_a_im'][l], P['ssm_log_dt'][l], P['ssm_b_re'][l], P['ssm_b_im'][l],
                         P['ssm_c_re'][l], P['ssm_c_im'][l], nsteps)
        h = _ffn_block(h, h_tail, P, 'ffn1', l, FFN_ROWS, m_p, ns, False)
        mixed_p, st_p = _prompt_mixer(h, P, w_in_l, w_out_l, s5p, l, n, t, rope_p, tm)
        mixed_s, st_s = _sample_mixer(h, m_p, P, w_in_l, w_out_l, s5p, l, caches, rope_s)
        h = _ffn_block(mixed_p, mixed_s, P, 'ffn2', l, FFN_ROWS, m_p, ns, l == DEPTH - 1)
        h_tail = None
        for i in range(6):
            new_p[i].append(st_p[i])
            new_s[i].append(st_s[i])
    y_p, y_s = h
    new_p = [jnp.stack(a, axis=0) for a in new_p]
    new_s = [jnp.stack(a, axis=0) for a in new_s]
    return (y_p.reshape(n, t, D_MODEL), y_s.reshape(ns, ts, D_MODEL), new_p[0], new_s[0], new_p[1], new_s[1],
            new_p[2], new_s[2], new_p[3], new_s[3], new_p[4], new_s[4], new_p[5], new_s[5])


def kernel(x_prompt, x_sample, state_ssm, cache_swa_kv, cache_dil0_kv, cache_dil1_kv, cache_dil2_kv, state_hgrn, ffn1_norm_pre, ffn1_w_gate, ffn1_w_up, ffn1_w_down, ffn1_norm_post, mix_norm_pre, w_in, ssm_a_re, ssm_a_im, ssm_log_dt, ssm_b_re, ssm_b_im, ssm_c_re, ssm_c_im, ssm_d, ssm_w_glu, ssm_b_glu, swa_sinks, hgrn_lower_bounds, out_norm_a, out_norm_b, out_norm_c, out_norm_d, w_out, mix_norm_post, ffn2_norm_pre, ffn2_w_gate, ffn2_w_up, ffn2_w_down, ffn2_norm_post):
    P = dict(ffn1_norm_pre=ffn1_norm_pre, ffn1_w_gate=ffn1_w_gate, ffn1_w_up=ffn1_w_up, ffn1_w_down=ffn1_w_down,
             ffn1_norm_post=ffn1_norm_post, mix_norm_pre=mix_norm_pre, w_in=w_in, ssm_a_re=ssm_a_re,
             ssm_a_im=ssm_a_im, ssm_log_dt=ssm_log_dt, ssm_b_re=ssm_b_re, ssm_b_im=ssm_b_im, ssm_c_re=ssm_c_re,
             ssm_c_im=ssm_c_im, ssm_d=ssm_d, ssm_w_glu=ssm_w_glu, ssm_b_glu=ssm_b_glu, swa_sinks=swa_sinks,
             hgrn_lower_bounds=hgrn_lower_bounds, out_norm_a=out_norm_a, out_norm_b=out_norm_b,
             out_norm_c=out_norm_c, out_norm_d=out_norm_d, w_out=w_out, mix_norm_post=mix_norm_post,
             ffn2_norm_pre=ffn2_norm_pre, ffn2_w_gate=ffn2_w_gate, ffn2_w_up=ffn2_w_up, ffn2_w_down=ffn2_w_down,
             ffn2_norm_post=ffn2_norm_post)
    caches = (state_ssm, cache_swa_kv, cache_dil0_kv, cache_dil1_kv, cache_dil2_kv, state_hgrn)
    return _forward(x_prompt, x_sample, caches, P)
```

```python
import functools
import math

import numpy as np
import jax
import jax.numpy as jnp
from jax import lax
from jax.experimental import pallas as pl
from jax.experimental.pallas import tpu as pltpu

F32 = jnp.float32
BF16 = jnp.bfloat16

D_MODEL = 2048
DEPTH = 2
PAST_LEN = 16384
HEAD_DIM = 64
ROPE_THETA = 10000.0
D_FF = 5504
NORM_EPS = 1e-6
NEG_BIG = -1e30
LB_FLOOR = 1e-30
SSM_GROUP = 16
SSM_STATE = 64
SSM_WIDTH = 448
SSM_GROUPS = 28
SWA_HEADS = 8
SWA_KV_HEADS = 2
SWA_GROUP = 4
SWA_WIDTH = 512
SWA_KV_WIDTH = 128
WINDOW = 128
DIL_PAIRS = ((128, 1), (512, 4), (2048, 16))
DIL_HPG = 3
DIL_GW = DIL_HPG * HEAD_DIM
DIL_WIDTH = 576
DIL_KV_WIDTH = 192
HGRN_HEAD_DIM = 128
HGRN_WIDTH = 512
HGRN_HEADS = 4
IN_COLS = 4224
ATTN_SCALE = HEAD_DIM ** -0.5

LANES = 128
FF_TILE = 256
N_FF_TILES = -(-D_FF // FF_TILE)
DOWN_TILE = 256
N_DOWN_TILES = D_MODEL // DOWN_TILE
FFN_ROWS = 1040
VMEM_LIMIT = 60 * 1024 * 1024

B_SLOT = SWA_WIDTH + 2 * SWA_KV_WIDTH
C_SLOT = 384
C_K = 256
OFF_B = 0
OFF_C = OFF_B + B_SLOT
OFF_D = (2048, 2560, 3584, 4096)
OFF_U = 3072
U_SLOT = 512
PROJ_COLS = 4608
WIN_TILE = 1536

S5_CHUNK = 16
S5_NPOW = S5_CHUNK + 1
HGRN_CHUNK = 128
HGRN_LEVELS = (64, 32, 16, 8, 4, 2, 1)


def _proj_segments():
    src = np.cumsum([0, SSM_WIDTH, SWA_WIDTH, SWA_KV_WIDTH, SWA_KV_WIDTH, DIL_WIDTH, DIL_KV_WIDTH, DIL_KV_WIDTH,
                     HGRN_WIDTH, HGRN_WIDTH, HGRN_WIDTH, HGRN_WIDTH])
    s_u, s_qb, s_kb, s_vb, s_qc, s_kc, s_vc, s_qd = src[:8]
    segs = [(OFF_B, s_qb, SWA_WIDTH), (OFF_B + SWA_WIDTH, s_kb, SWA_KV_WIDTH),
            (OFF_B + SWA_WIDTH + SWA_KV_WIDTH, s_vb, SWA_KV_WIDTH)]
    for g in range(3):
        base = OFF_C + g * C_SLOT
        segs += [(base, s_qc + g * DIL_GW, DIL_GW), (base + C_K, s_kc + g * HEAD_DIM, HEAD_DIM),
                 (base + C_K + HEAD_DIM, s_vc + g * HEAD_DIM, HEAD_DIM)]
    segs += [(off, s_qd + i * HGRN_WIDTH, HGRN_WIDTH) for i, off in enumerate(OFF_D)]
    segs += [(OFF_U, s_u, SSM_WIDTH)]
    return [(int(a), int(b), int(c)) for a, b, c in segs]


def _rotary_lane_mask():
    m = np.zeros((PROJ_COLS,), np.float32)
    m[OFF_B:OFF_B + SWA_WIDTH + SWA_KV_WIDTH] = 1.0
    for g in range(3):
        base = OFF_C + g * C_SLOT
        m[base:base + DIL_GW] = 1.0
        m[base + C_K:base + C_K + HEAD_DIM] = 1.0
    return m


def _cparams(sem):
    return pltpu.CompilerParams(dimension_semantics=sem, vmem_limit_bytes=VMEM_LIMIT)


def _v3(a):
    return a.reshape(a.shape[0], 1, a.shape[1])


def _rms(x, g):
    return x * lax.rsqrt(jnp.mean(x * x, axis=-1, keepdims=True) + NORM_EPS) * g


def _sigmoid(x):
    return 1.0 / (1.0 + jnp.exp(-x))


def _silu(x):
    return x * _sigmoid(x)


def _dot(a, b):
    return jnp.dot(a, b, preferred_element_type=F32)


def _dot_nt(a, b):
    return lax.dot_general(a, b, (((1,), (1,)), ((), ())), preferred_element_type=F32)


def _dot_tn(a, b):
    return lax.dot_general(a, b, (((0,), (0,)), ((), ())), preferred_element_type=F32)


def _ffn_kernel(*refs, tail, n_tail, tail_in, tail_out):
    refs = list(refs)
    x_ref = refs.pop(0)
    xs_ref = refs.pop(0) if tail_in else None
    gpre_ref, wg_ref, wu_ref, wd_ref, gpost_ref, o_ref = refs[:6]
    os_ref = refs[6] if tail_out else None
    xn_ref, hid_ref = refs[-2:]
    s = pl.program_id(1)
    in_last_block = pl.program_id(0) == pl.num_programs(0) - 1
    tail_rows = slice(tail, tail + n_tail)

    @pl.when(s == 0)
    def _():
        xn_ref[...] = _rms(x_ref[...], gpre_ref[...]).astype(BF16)
        if tail_in:

            @pl.when(in_last_block)
            def _():
                xn_ref[tail_rows, :] = _rms(xs_ref[...], gpre_ref[...]).astype(BF16)

    @pl.when(s < N_FF_TILES)
    def _():
        xn = xn_ref[...]
        gate = _dot(xn, wg_ref[...].astype(BF16))
        up = _dot(xn, wu_ref[...].astype(BF16))
        hid_ref[s] = (_silu(gate) * up).astype(BF16)

    for j in range(N_DOWN_TILES):

        @pl.when(s == N_FF_TILES + j)
        def _(j=j):
            hid = jnp.concatenate([hid_ref[f] for f in range(N_FF_TILES)], axis=1)[:, :D_FF]
            o_ref[:, j * DOWN_TILE:(j + 1) * DOWN_TILE] = _dot(hid, wd_ref[...])

    @pl.when(s == pl.num_programs(1) - 1)
    def _():
        delta = 0.5 * _rms(o_ref[...], gpost_ref[...])
        res = x_ref[...] + delta
        o_ref[...] = res
        if tail_in or tail_out:

            @pl.when(in_last_block)
            def _():
                tail_res = xs_ref[...] + delta[tail_rows] if tail_in else res[tail_rows]
                if tail_out:
                    os_ref[...] = tail_res
                else:
                    o_ref[tail_rows, :] = tail_res


def _ffn(x, x_tail, gpre, wg, wu, wd, gpost, layer, tm, n_main, n_tail, tail_out):
    m = n_main + n_tail
    nblk = pl.cdiv(m, tm)
    tail = n_main - (nblk - 1) * tm
    assert 0 <= tail and tail + n_tail <= tm and tail % 8 == 0
    last_ff = N_FF_TILES - 1
    in_specs = [pl.BlockSpec((tm, D_MODEL), lambda i, s: (i, 0))]
    args = [x]
    if x_tail is not None:
        in_specs.append(pl.BlockSpec((n_tail, D_MODEL), lambda i, s: (0, 0)))
        args.append(x_tail)
    in_specs += [
        pl.BlockSpec((None, 1, D_MODEL), lambda i, s: (layer, 0, 0)),
        pl.BlockSpec((None, D_MODEL, FF_TILE), lambda i, s: (layer, 0, jnp.minimum(s, last_ff))),
        pl.BlockSpec((None, D_MODEL, FF_TILE), lambda i, s: (layer, 0, jnp.minimum(s, last_ff))),
        pl.BlockSpec((None, D_FF, DOWN_TILE), lambda i, s: (jnp.maximum(s - N_FF_TILES, 0), 0, 0)),
        pl.BlockSpec((None, 1, D_MODEL), lambda i, s: (layer, 0, 0)),
    ]
    args += [_v3(gpre), wg, wu, wd, _v3(gpost)]
    out_main = pl.BlockSpec((tm, D_MODEL), lambda i, s: (i, 0), pipeline_mode=pl.Buffered(1))
    if tail_out:
        out_specs = [out_main, pl.BlockSpec((n_tail, D_MODEL), lambda i, s: (0, 0))]
        out_shape = [jax.ShapeDtypeStruct((n_main, D_MODEL), F32), jax.ShapeDtypeStruct((n_tail, D_MODEL), F32)]
    else:
        out_specs = out_main
        out_shape = jax.ShapeDtypeStruct((m, D_MODEL), F32)
    return pl.pallas_call(
        functools.partial(_ffn_kernel, tail=tail, n_tail=n_tail, tail_in=x_tail is not None, tail_out=tail_out),
        grid=(nblk, N_FF_TILES + N_DOWN_TILES),
        in_specs=in_specs,
        out_specs=out_specs,
        out_shape=out_shape,
        scratch_shapes=[pltpu.VMEM((tm, D_MODEL), BF16), pltpu.VMEM((N_FF_TILES, tm, FF_TILE), BF16)],
        compiler_params=_cparams(("arbitrary", "arbitrary")),
    )(*args)


def _win_prep_kernel(w_ref, o_ref):
    o_ref[...] = jnp.zeros_like(o_ref)
    for dst, src, width in _proj_segments():
        o_ref[:, dst:dst + width] = w_ref[:, src:src + width].astype(BF16)


def _win_prep(w_in, layer):
    rows = 256
    return pl.pallas_call(
        _win_prep_kernel,
        grid=(D_MODEL // rows,),
        in_specs=[pl.BlockSpec((None, rows, IN_COLS), lambda i: (layer, i, 0))],
        out_specs=pl.BlockSpec((rows, PROJ_COLS), lambda i: (i, 0)),
        out_shape=jax.ShapeDtypeStruct((D_MODEL, PROJ_COLS), BF16),
        compiler_params=_cparams(("parallel",)),
    )(w_in)


def _cast_kernel(w_ref, o_ref):
    o_ref[...] = w_ref[...].astype(BF16)


def _cast_bf16(w, layer):
    _, r, c = w.shape
    rows = 256
    return pl.pallas_call(
        _cast_kernel,
        grid=(r // rows,),
        in_specs=[pl.BlockSpec((None, rows, c), lambda i: (layer, i, 0))],
        out_specs=pl.BlockSpec((rows, c), lambda i: (i, 0)),
        out_shape=jax.ShapeDtypeStruct((r, c), BF16),
        compiler_params=_cparams(("parallel",)),
    )(w)


def _cast_col_tiles(w, layer, tile, row_blocks):
    _, r, c = w.shape
    rows = r // row_blocks
    return pl.pallas_call(
        _cast_kernel,
        grid=(row_blocks, c // tile),
        in_specs=[pl.BlockSpec((None, rows, tile), lambda i, j: (layer, i, j))],
        out_specs=pl.BlockSpec((None, rows, tile), lambda i, j: (j, i, 0)),
        out_shape=jax.ShapeDtypeStruct((c // tile, r, tile), BF16),
        compiler_params=_cparams(("parallel", "parallel")),
    )(w)


def _win_kernel(x_ref, g_ref, w_ref, cos_ref, sin_ref, o_ref, *rest, chunks):
    if chunks:
        u_ref, xn_ref, st_ref = rest
    else:
        (xn_ref,) = rest
    j = pl.program_id(1)

    @pl.when(j == 0)
    def _():
        xn_ref[...] = _rms(x_ref[...], g_ref[...]).astype(BF16)

    rot = _rotary_lane_mask()
    for tile in range(PROJ_COLS // WIN_TILE):

        @pl.when(j == tile)
        def _(tile=tile):
            y = _dot(xn_ref[...], w_ref[...])
            cos = cos_ref[...]
            sin = sin_ref[...]
            lane = lax.broadcasted_iota(jnp.int32, cos.shape, 1)
            first_half = (lane % HEAD_DIM) < (HEAD_DIM // 2)
            for c in range(WIN_TILE // LANES):
                sl = slice(c * LANES, (c + 1) * LANES)
                flags = rot[tile * WIN_TILE + c * LANES:tile * WIN_TILE + (c + 1) * LANES]
                x = y[:, sl]
                lo, hi = bool(flags[:HEAD_DIM].all()), bool(flags[HEAD_DIM:].all())
                if not (lo or hi):
                    o_ref[:, sl] = x
                    continue
                partner = jnp.where(first_half,
                                    pltpu.roll(x, LANES - HEAD_DIM // 2, axis=1),
                                    pltpu.roll(x, HEAD_DIM // 2, axis=1))
                rotated = x * cos + partner * sin
                if lo and hi:
                    o_ref[:, sl] = rotated
                else:
                    is_rot = (lane < HEAD_DIM) if lo else (lane >= HEAD_DIM)
                    o_ref[:, sl] = jnp.where(is_rot, rotated, x)
            if chunks and tile == OFF_U // WIN_TILE:
                base = OFF_U % WIN_TILE
                for lb in range(U_SLOT // LANES):
                    st_ref[lb] = y[:, base + lb * LANES:base + (lb + 1) * LANES]
                _s5_rows_to_chunks(st_ref, u_ref)


def _win(h, g, w, cos_t, sin_t, layer, tm, chunks, row0=0):
    m = cos_t.shape[0]
    blk0 = row0 // tm
    cw = S5_CHUNK * SSM_GROUP
    out_specs = [pl.BlockSpec((tm, WIN_TILE), lambda i, j: (i, j))]
    out_shape = [jax.ShapeDtypeStruct((m, PROJ_COLS), F32)]
    scratch = [pltpu.VMEM((tm, D_MODEL), BF16)]
    if chunks:
        out_specs.append(pl.BlockSpec((SSM_GROUPS, tm // S5_CHUNK, cw), lambda i, j: (0, i, 0)))
        out_shape.append(jax.ShapeDtypeStruct((SSM_GROUPS, m // S5_CHUNK, cw), F32))
        scratch.append(pltpu.VMEM((U_SLOT // LANES, tm, LANES), F32))
    return pl.pallas_call(
        functools.partial(_win_kernel, chunks=chunks),
        grid=(m // tm, PROJ_COLS // WIN_TILE),
        in_specs=[
            pl.BlockSpec((tm, D_MODEL), lambda i, j: (i + blk0, 0)),
            pl.BlockSpec((None, 1, D_MODEL), lambda i, j: (layer, 0, 0)),
            pl.BlockSpec((D_MODEL, WIN_TILE), lambda i, j: (0, j)),
            pl.BlockSpec((tm, LANES), lambda i, j: (i, 0)),
            pl.BlockSpec((tm, LANES), lambda i, j: (i, 0)),
        ],
        out_specs=out_specs,
        out_shape=out_shape,
        scratch_shapes=scratch,
        compiler_params=_cparams(("parallel", "arbitrary")),
    )(h, _v3(g), w, cos_t, sin_t)


def _rope_tables(pos):
    half = HEAD_DIM // 2
    inv_freq = ROPE_THETA ** (-jnp.arange(half, dtype=F32) / half)
    ang = pos.astype(F32)[:, None] * inv_freq[None, :]
    cos, sin = jnp.cos(ang), jnp.sin(ang)
    reps = LANES // HEAD_DIM
    cos_t = jnp.tile(jnp.concatenate([cos, cos], axis=1), (1, reps))
    sin_t = jnp.tile(jnp.concatenate([-sin, sin], axis=1), (1, reps))
    return cos_t, sin_t


def _band_heads(q, kk, vv, valid, sinks):
    w = WINDOW
    heads = q.shape[1] // HEAD_DIM
    qs = jnp.concatenate([q[:, g * HEAD_DIM:(g + 1) * HEAD_DIM] for g in range(heads)], axis=0).astype(BF16)
    s = _dot_nt(kk, qs) * ATTN_SCALE
    s = jnp.where(jnp.concatenate([valid] * heads, axis=1), s, NEG_BIG)
    m = jnp.max(s, axis=0, keepdims=True)
    if sinks is not None:
        sink_row = jnp.concatenate([jnp.broadcast_to(sinks[g], (1, w)) for g in range(heads)], axis=1)
        m = jnp.maximum(m, sink_row)
    p = jnp.exp(s - m)
    l = jnp.sum(p, axis=0, keepdims=True)
    if sinks is not None:
        l = l + jnp.exp(sink_row - m)
    ot = _dot_tn(vv, p.astype(BF16)) / l
    lse = m + jnp.log(l)
    return ([ot[:, g * w:(g + 1) * w] for g in range(heads)], [lse[:, g * w:(g + 1) * w] for g in range(heads)])


def _untranspose(blocks):
    pairs = [jnp.concatenate(blocks[i:i + 2], axis=0).T for i in range(0, len(blocks), 2)]
    return jnp.concatenate(pairs, axis=1)


def _band_mask(no_prev):
    w = WINDOW
    k = lax.broadcasted_iota(jnp.int32, (2 * w, w), 0)
    q = lax.broadcasted_iota(jnp.int32, (2 * w, w), 1)
    valid = (k >= q) & (k <= q + w)
    if no_prev is False:
        return valid
    return valid & (k >= jnp.where(no_prev, w, 0))


def _swa_prompt_kernel(cur_ref, prev_ref, sink_ref, o_ref, *, nsub):
    w = WINDOW
    gw = SWA_GROUP * HEAD_DIM
    first = pl.program_id(1) == 0
    for j in range(nsub):
        valid = _band_mask(first if j == 0 else False)
        cur = cur_ref[0, j * w:(j + 1) * w, :]
        prev = prev_ref[0] if j == 0 else cur_ref[0, (j - 1) * w:j * w, SWA_WIDTH:B_SLOT]
        outs = []
        for h in range(SWA_KV_HEADS):
            ks = slice(SWA_WIDTH + h * HEAD_DIM, SWA_WIDTH + (h + 1) * HEAD_DIM)
            vs = slice(SWA_WIDTH + SWA_KV_WIDTH + h * HEAD_DIM, SWA_WIDTH + SWA_KV_WIDTH + (h + 1) * HEAD_DIM)
            kk = jnp.concatenate([prev[:, h * HEAD_DIM:(h + 1) * HEAD_DIM], cur[:, ks]], axis=0).astype(BF16)
            vv = jnp.concatenate([prev[:, SWA_KV_WIDTH + h * HEAD_DIM:SWA_KV_WIDTH + (h + 1) * HEAD_DIM],
                                  cur[:, vs]], axis=0).astype(BF16)
            sinks = [sink_ref[:, h * gw + g * HEAD_DIM:h * gw + g * HEAD_DIM + 1] for g in range(SWA_GROUP)]
            o, _ = _band_heads(cur[:, h * gw:(h + 1) * gw], kk, vv, valid, sinks)
            outs += o
        o_ref[0, j * w:(j + 1) * w, :] = _untranspose(outs)


def _swa_prompt(proj, sink):
    n, t, _ = proj.shape
    nsub = 2
    rows = WINDOW * nsub
    return pl.pallas_call(
        functools.partial(_swa_prompt_kernel, nsub=nsub),
        grid=(n, t // rows),
        in_specs=[
            pl.BlockSpec((1, rows, B_SLOT), lambda b, i: (b, i, OFF_B // B_SLOT)),
            pl.BlockSpec((1, WINDOW, 2 * SWA_KV_WIDTH),
                         lambda b, i: (b, jnp.maximum(i * nsub - 1, 0), (OFF_B + SWA_WIDTH) // (2 * SWA_KV_WIDTH))),
            pl.BlockSpec((1, SWA_WIDTH), lambda b, i: (0, 0)),
        ],
        out_specs=pl.BlockSpec((1, rows, SWA_WIDTH), lambda b, i: (b, i, 0)),
        out_shape=jax.ShapeDtypeStruct((n, t, SWA_WIDTH), F32),
        compiler_params=_cparams(("parallel", "arbitrary")),
    )(proj, proj, sink)


def _dil_prompt_kernel(cur_ref, prev_ref, o_ref, cur3_ref, out3_ref, *, dil, nsub):
    nl = C_SLOT // LANES
    span = WINDOW * dil
    first = pl.program_id(1) == 0
    for c in range(nl):
        cur3_ref[c] = cur_ref[0, :, c * LANES:(c + 1) * LANES]

    def one(j, r):
        def rows(base):
            return pl.ds(base + r, WINDOW, stride=dil) if dil > 1 else pl.ds(base, WINDOW)

        valid = _band_mask(first if j == 0 else False)
        cur = rows(j * span)
        q = jnp.concatenate([cur3_ref[0, cur, :], cur3_ref[1, cur, :]], axis=1)[:, 0:DIL_GW]
        kv = cur3_ref[2, cur, :]
        pkv = prev_ref[0, rows(0), :] if j == 0 else cur3_ref[2, rows((j - 1) * span), :]
        kk = jnp.concatenate([pkv[:, 0:HEAD_DIM], kv[:, 0:HEAD_DIM]], axis=0).astype(BF16)
        vv = jnp.concatenate([pkv[:, HEAD_DIM:], kv[:, HEAD_DIM:]], axis=0).astype(BF16)
        o, lse = _band_heads(q, kk, vv, valid, None)
        res = _untranspose(o + [jnp.broadcast_to(x, (HEAD_DIM, WINDOW)) for x in lse])
        for c in range(nl):
            out3_ref[c, cur, :] = res[:, c * LANES:(c + 1) * LANES]

    for j in range(nsub):
        if dil <= 4:
            for r in range(dil):
                one(j, r)
        else:
            lax.fori_loop(0, dil, lambda r, c, j=j: (one(j, r), c)[1], 0, unroll=2)
    for c in range(nl):
        o_ref[0, :, c * LANES:(c + 1) * LANES] = out3_ref[c]


def _dil_prompt(proj, group, dil):
    n, t, _ = proj.shape
    span = WINDOW * dil
    nsub = max(1, 4 // dil)
    rows = span * nsub
    slot = OFF_C + group * C_SLOT
    return pl.pallas_call(
        functools.partial(_dil_prompt_kernel, dil=dil, nsub=nsub),
        grid=(n, t // rows),
        in_specs=[
            pl.BlockSpec((1, rows, C_SLOT), lambda b, i: (b, i, slot // C_SLOT)),
            pl.BlockSpec((1, span, LANES), lambda b, i: (b, jnp.maximum(i * nsub - 1, 0), (slot + C_K) // LANES)),
        ],
        out_specs=pl.BlockSpec((1, rows, 2 * DIL_GW), lambda b, i: (b, i, 0)),
        out_shape=jax.ShapeDtypeStruct((n, t, 2 * DIL_GW), F32),
        scratch_shapes=[pltpu.VMEM((C_SLOT // LANES, rows, LANES), F32),
                        pltpu.VMEM((2 * DIL_GW // LANES, rows, LANES), F32)],
        compiler_params=_cparams(("parallel", "arbitrary")),
    )(proj, proj)


def _decode_heads(q, knew, vnew, kt, vt, dil, sinks):
    s = _dot(q.astype(BF16), kt.astype(BF16)) * ATTN_SCALE
    if dil > 1:
        row = lax.broadcasted_iota(jnp.int32, s.shape, 1)
        s = jnp.where(row % dil == 0, s, NEG_BIG)
    s_new = jnp.sum(q * knew, axis=-1, keepdims=True) * ATTN_SCALE
    m = jnp.maximum(jnp.max(s, axis=-1, keepdims=True), s_new)
    if sinks is not None:
        m = jnp.maximum(m, sinks)
    p = jnp.exp(s - m)
    p_new = jnp.exp(s_new - m)
    l = jnp.sum(p, axis=-1, keepdims=True) + p_new
    if sinks is not None:
        l = l + jnp.exp(sinks - m)
    o = (_dot_nt(p.astype(BF16), vt.astype(BF16)) + p_new * vnew) / l
    return o, m + jnp.log(l)


def _rows_of(x, heads):
    return jnp.concatenate([x[:, g * HEAD_DIM:(g + 1) * HEAD_DIM] for g in range(heads)], axis=0)


def _swa_sample_kernel(p_ref, cache_ref, sink_ref, o_ref, *, bn):
    for b in range(bn):
        row = p_ref[b:b + 1, :]
        outs = []
        for h in range(SWA_KV_HEADS):
            gw = SWA_GROUP * HEAD_DIM
            q = _rows_of(row[:, h * gw:(h + 1) * gw], SWA_GROUP)
            knew = row[:, SWA_WIDTH + h * HEAD_DIM:SWA_WIDTH + (h + 1) * HEAD_DIM]
            vnew = row[:, SWA_WIDTH + SWA_KV_WIDTH + h * HEAD_DIM:SWA_WIDTH + SWA_KV_WIDTH + (h + 1) * HEAD_DIM]
            sinks = _rows_of(sink_ref[:, h * gw:(h + 1) * gw], SWA_GROUP)[:, 0:1]
            o, _ = _decode_heads(q, knew, vnew, cache_ref[b, 0, h], cache_ref[b, 1, h], 1, sinks)
            outs += [o[g:g + 1] for g in range(SWA_GROUP)]
        o_ref[b:b + 1, :] = jnp.concatenate(outs, axis=1)


def _swa_sample(proj, cache, sink, layer):
    n = proj.shape[0]
    bn = 8
    return pl.pallas_call(
        functools.partial(_swa_sample_kernel, bn=bn),
        grid=(n // bn,),
        in_specs=[
            pl.BlockSpec((bn, B_SLOT), lambda i: (i, OFF_B // B_SLOT)),
            pl.BlockSpec((None, bn, 2, SWA_KV_HEADS, HEAD_DIM, WINDOW), lambda i: (layer, i, 0, 0, 0, 0)),
            pl.BlockSpec((1, SWA_WIDTH), lambda i: (0, 0)),
        ],
        out_specs=pl.BlockSpec((bn, SWA_WIDTH), lambda i: (i, 0)),
        out_shape=jax.ShapeDtypeStruct((n, SWA_WIDTH), F32),
        compiler_params=_cparams(("parallel",)),
    )(proj, cache, sink)


def _dil_sample_kernel(p_ref, cache_ref, o_ref, *, bn, dil):
    for b in range(bn):
        row = p_ref[b:b + 1, :]
        q = _rows_of(row[:, 0:DIL_GW], DIL_HPG)
        o, lse = _decode_heads(q, row[:, C_K:C_K + HEAD_DIM], row[:, C_K + HEAD_DIM:C_K + 2 * HEAD_DIM],
                               cache_ref[b, 0], cache_ref[b, 1], dil, None)
        lse = jnp.broadcast_to(lse, (DIL_HPG, HEAD_DIM))
        o_ref[b:b + 1, :] = jnp.concatenate([o[g:g + 1] for g in range(DIL_HPG)]
                                            + [lse[g:g + 1] for g in range(DIL_HPG)], axis=1)


def _dil_sample(proj, cache, group, dil, layer):
    n = proj.shape[0]
    bn = 8
    slot = OFF_C + group * C_SLOT
    return pl.pallas_call(
        functools.partial(_dil_sample_kernel, bn=bn, dil=dil),
        grid=(n // bn,),
        in_specs=[
            pl.BlockSpec((bn, C_SLOT), lambda i: (i, slot // C_SLOT)),
            pl.BlockSpec((None, bn, 2, HEAD_DIM, WINDOW * dil), lambda i: (layer, i, 0, 0, 0)),
        ],
        out_specs=pl.BlockSpec((bn, 2 * DIL_GW), lambda i: (i, 0)),
        out_shape=jax.ShapeDtypeStruct((n, 2 * DIL_GW), F32),
        compiler_params=_cparams(("parallel",)),
    )(proj, cache)


def _s5_param_kernel(are_ref, aim_ref, ldt_ref, bre_ref, bim_ref, cre_ref, cim_ref,
                     lam_re_ref, lam_im_ref, cl_re_ref, cl_im_ref, r_re_ref, r_im_ref, k_ref):
    ar = are_ref[0]
    ai = aim_ref[0]
    dt = jnp.exp(ldt_ref[0])
    nrow = lam_re_ref.shape[1]
    ri = lax.broadcasted_iota(jnp.int32, (nrow, 1), 0)
    pw = jnp.where(ri < S5_NPOW, ri, jnp.left_shift(S5_CHUNK, jnp.maximum(ri - S5_NPOW, 0))).astype(F32)
    mag = jnp.exp(pw * (ar * dt))
    ang = pw * (ai * dt)
    lam_re = mag * jnp.cos(ang)
    lam_im = mag * jnp.sin(ang)
    lam_re_ref[0] = lam_re
    lam_im_ref[0] = lam_im
    l1r, l1i = lam_re[1:2], lam_im[1:2]
    den = ar * ar + ai * ai
    z_re = ((l1r - 1.0) * ar + l1i * ai) / den
    z_im = (l1i * ar - (l1r - 1.0) * ai) / den
    b_re, b_im = bre_ref[0], bim_ref[0]
    bb_re = z_re * b_re - z_im * b_im
    bb_im = z_re * b_im + z_im * b_re
    c_re, c_im = cre_ref[0], cim_ref[0]
    cls_re, cls_im = [], []
    for d in range(S5_NPOW):
        lr, li = lam_re[d:d + 1], lam_im[d:d + 1]
        cr = c_re * lr - c_im * li
        ci = c_re * li + c_im * lr
        cl_re_ref[0, d * SSM_GROUP:(d + 1) * SSM_GROUP, :] = cr
        cl_im_ref[0, d * SSM_GROUP:(d + 1) * SSM_GROUP, :] = ci
        if d < S5_CHUNK:
            cls_re.append(cr)
            cls_im.append(ci)
    for s in range(S5_CHUNK):
        lr = lam_re[S5_CHUNK - 1 - s:S5_CHUNK - s]
        li = lam_im[S5_CHUNK - 1 - s:S5_CHUNK - s]
        r_re_ref[0, s * SSM_GROUP:(s + 1) * SSM_GROUP, :] = lr * bb_re - li * bb_im
        r_im_ref[0, s * SSM_GROUP:(s + 1) * SSM_GROUP, :] = lr * bb_im + li * bb_re
    call_re = jnp.concatenate(cls_re, axis=0)
    call_im = jnp.concatenate(cls_im, axis=0)
    hp = lax.Precision.HIGHEST
    nt = (((1,), (1,)), ((), ()))
    kt = (lax.dot_general(bb_re, call_re, nt, precision=hp, preferred_element_type=F32)
          - lax.dot_general(bb_im, call_im, nt, precision=hp, preferred_element_type=F32))
    cw = S5_CHUNK * SSM_GROUP
    for s in range(S5_CHUNK):
        if s == 0:
            blk = kt
        else:
            blk = jnp.concatenate([jnp.zeros((SSM_GROUP, s * SSM_GROUP), F32), kt[:, :cw - s * SSM_GROUP]], axis=1)
        k_ref[0, s * SSM_GROUP:(s + 1) * SSM_GROUP, :] = blk


def _s5_params(a_re, a_im, log_dt, b_re, b_im, c_re, c_im, nsteps):
    g = SSM_GROUPS
    nrow = ((S5_NPOW + nsteps + 7) // 8) * 8
    row = lambda i: (i, 0, 0)
    shp = lambda r, c: jax.ShapeDtypeStruct((g, r, c), F32)
    ncl = S5_NPOW * SSM_GROUP
    nr = S5_CHUNK * SSM_GROUP
    return pl.pallas_call(
        _s5_param_kernel,
        grid=(g,),
        in_specs=[
            pl.BlockSpec((1, 1, SSM_STATE), row),
            pl.BlockSpec((1, 1, SSM_STATE), row),
            pl.BlockSpec((1, 1, 1), row),
            pl.BlockSpec((1, SSM_GROUP, SSM_STATE), row),
            pl.BlockSpec((1, SSM_GROUP, SSM_STATE), row),
            pl.BlockSpec((1, SSM_GROUP, SSM_STATE), row),
            pl.BlockSpec((1, SSM_GROUP, SSM_STATE), row),
        ],
        out_specs=[
            pl.BlockSpec((1, nrow, SSM_STATE), row), pl.BlockSpec((1, nrow, SSM_STATE), row),
            pl.BlockSpec((1, ncl, SSM_STATE), row), pl.BlockSpec((1, ncl, SSM_STATE), row),
            pl.BlockSpec((1, nr, SSM_STATE), row), pl.BlockSpec((1, nr, SSM_STATE), row),
            pl.BlockSpec((1, nr, nr), row),
        ],
        out_shape=[shp(nrow, SSM_STATE), shp(nrow, SSM_STATE), shp(ncl, SSM_STATE), shp(ncl, SSM_STATE),
                   shp(nr, SSM_STATE), shp(nr, SSM_STATE), shp(nr, nr)],
        compiler_params=_cparams(("parallel",)),
    )(a_re.reshape(g, 1, SSM_STATE), a_im.reshape(g, 1, SSM_STATE), log_dt.reshape(g, 1, 1),
      jnp.swapaxes(b_re, 1, 2), jnp.swapaxes(b_im, 1, 2), c_re, c_im)


def _s5_rows_to_chunks(st_ref, u_ref):
    nc = st_ref.shape[1] // S5_CHUNK
    per_blk = LANES // SSM_GROUP
    slot = lax.broadcasted_iota(jnp.int32, (nc, LANES), 1) // SSM_GROUP
    for g in range(SSM_GROUPS):
        acc = [jnp.zeros((nc, LANES), F32) for _ in range(S5_CHUNK // per_blk)]
        for s in range(S5_CHUNK):
            v = st_ref[g // per_blk, pl.ds(s, nc, stride=S5_CHUNK), :]
            shift = (SSM_GROUP * (s % per_blk) - SSM_GROUP * (g % per_blk)) % LANES
            if shift:
                v = pltpu.roll(v, shift, axis=1)
            acc[s // per_blk] = jnp.where(slot == s % per_blk, v, acc[s // per_blk])
        for k, a in enumerate(acc):
            u_ref[g, :, k * LANES:(k + 1) * LANES] = a


def _s5_chunks_to_rows(y_ref, st_ref):
    nc = y_ref.shape[1]
    per_blk = LANES // SSM_GROUP
    slot = lax.broadcasted_iota(jnp.int32, (nc, LANES), 1) // SSM_GROUP
    nblk = -(-SSM_WIDTH // LANES)
    for lb in range(nblk):
        groups = range(lb * per_blk, min((lb + 1) * per_blk, SSM_GROUPS))
        for s in range(S5_CHUNK):
            acc = jnp.zeros((nc, LANES), F32)
            for g in groups:
                v = y_ref[g, :, (s // per_blk) * LANES:(s // per_blk + 1) * LANES]
                shift = (SSM_GROUP * (g % per_blk) - SSM_GROUP * (s % per_blk)) % LANES
                if shift:
                    v = pltpu.roll(v, shift, axis=1)
                acc = jnp.where(slot == g % per_blk, v, acc)
            st_ref[lb, pl.ds(s, nc, stride=S5_CHUNK), :] = acc
    return jnp.concatenate([st_ref[lb][:, 0:min(LANES, SSM_WIDTH - lb * LANES)] for lb in range(nblk)], axis=1)


def _s5_prompt_kernel(u_ref, mt_ref, r_re_ref, r_im_ref, cl_re_ref, cl_im_ref, lam_re_ref, lam_im_ref, d_ref,
                      y_ref, xre_ref, xim_ref, *, nseq, nchunk):
    u = u_ref[0]
    ub = u.astype(BF16)
    y = _dot(ub, mt_ref[0].astype(BF16))
    xr = _dot(ub, r_re_ref[0].astype(BF16))
    xi = _dot(ub, r_im_ref[0].astype(BF16))
    rows = nseq * nchunk
    cidx = lax.broadcasted_iota(jnp.int32, (rows, SSM_STATE), 0) % nchunk
    lam_re, lam_im = lam_re_ref[0], lam_im_ref[0]
    for k in range(int(math.log2(nchunk))):
        sh = 1 << k
        lr = lam_re[S5_NPOW + k:S5_NPOW + k + 1]
        li = lam_im[S5_NPOW + k:S5_NPOW + k + 1]
        ok = cidx >= sh
        pr = jnp.where(ok, pltpu.roll(xr, sh, axis=0), 0.0)
        pi = jnp.where(ok, pltpu.roll(xi, sh, axis=0), 0.0)
        xr, xi = xr + lr * pr - li * pi, xi + lr * pi + li * pr
    ok = cidx >= 1
    er = jnp.where(ok, pltpu.roll(xr, 1, axis=0), 0.0)
    ei = jnp.where(ok, pltpu.roll(xi, 1, axis=0), 0.0)
    cl_re = cl_re_ref[0][SSM_GROUP:, :]
    cl_im = cl_im_ref[0][SSM_GROUP:, :]
    y = y + _dot_nt(er.astype(BF16), cl_re.astype(BF16)) - _dot_nt(ei.astype(BF16), cl_im.astype(BF16))
    y_ref[0] = y + d_ref[0] * u
    last = [s * nchunk + nchunk - 1 for s in range(nseq)]
    xre_ref[0] = jnp.concatenate([xr[i:i + 1] for i in last], axis=0)
    xim_ref[0] = jnp.concatenate([xi[i:i + 1] for i in last], axis=0)


def _s5_prompt(u, mt, r_re, r_im, cl_re, cl_im, lam_re, lam_im, d_t, nseq, nchunk):
    g = SSM_GROUPS
    rows = nseq * nchunk
    cw = S5_CHUNK * SSM_GROUP
    row = lambda i: (i, 0, 0)
    full = lambda a: pl.BlockSpec((1,) + a.shape[1:], row)
    return pl.pallas_call(
        functools.partial(_s5_prompt_kernel, nseq=nseq, nchunk=nchunk),
        grid=(g,),
        in_specs=[full(u), full(mt), full(r_re), full(r_im), full(cl_re), full(cl_im), full(lam_re), full(lam_im),
                  full(d_t)],
        out_specs=[pl.BlockSpec((1, rows, cw), row), pl.BlockSpec((1, nseq, SSM_STATE), row),
                   pl.BlockSpec((1, nseq, SSM_STATE), row)],
        out_shape=[jax.ShapeDtypeStruct((g, rows, cw), F32), jax.ShapeDtypeStruct((g, nseq, SSM_STATE), F32),
                   jax.ShapeDtypeStruct((g, nseq, SSM_STATE), F32)],
        compiler_params=_cparams(("parallel",)),
    )(u, mt, r_re, r_im, cl_re, cl_im, lam_re, lam_im, d_t)


def _s5_sample_kernel(u_ref, x0r_ref, x0i_ref, r_re_ref, r_im_ref, cl_re_ref, cl_im_ref, lam_re_ref, lam_im_ref,
                      d_ref, y_ref, xre_ref, xim_ref):
    u = u_ref[0]
    ub = u.astype(BF16)
    lo = (S5_CHUNK - 1) * SSM_GROUP
    bb_re = r_re_ref[0][lo:lo + SSM_GROUP, :]
    bb_im = r_im_ref[0][lo:lo + SSM_GROUP, :]
    lr, li = lam_re_ref[0][1:2], lam_im_ref[0][1:2]
    x0r, x0i = x0r_ref[0], x0i_ref[0]
    xr = _dot(ub, bb_re.astype(BF16)) + (lr * x0r - li * x0i)
    xi = _dot(ub, bb_im.astype(BF16)) + (lr * x0i + li * x0r)
    c_re = cl_re_ref[0][:SSM_GROUP, :]
    c_im = cl_im_ref[0][:SSM_GROUP, :]
    y_ref[0] = (_dot_nt(xr.astype(BF16), c_re.astype(BF16)) - _dot_nt(xi.astype(BF16), c_im.astype(BF16))
                + d_ref[0] * u)
    xre_ref[0] = xr
    xim_ref[0] = xi


def _s5_sample(u, x0r, x0i, r_re, r_im, cl_re, cl_im, lam_re, lam_im, d):
    g, n = u.shape[0], u.shape[1]
    row = lambda i: (i, 0, 0)
    full = lambda a: pl.BlockSpec((1,) + a.shape[1:], row)
    return pl.pallas_call(
        _s5_sample_kernel,
        grid=(g,),
        in_specs=[full(u), full(x0r), full(x0i), full(r_re), full(r_im), full(cl_re), full(cl_im), full(lam_re),
                  full(lam_im), full(d)],
        out_specs=[pl.BlockSpec((1, n, SSM_GROUP), row), pl.BlockSpec((1, n, SSM_STATE), row),
                   pl.BlockSpec((1, n, SSM_STATE), row)],
        out_shape=[jax.ShapeDtypeStruct((g, n, SSM_GROUP), F32), jax.ShapeDtypeStruct((g, n, SSM_STATE), F32),
                   jax.ShapeDtypeStruct((g, n, SSM_STATE), F32)],
        compiler_params=_cparams(("parallel",)),
    )(u, x0r, x0i, r_re, r_im, cl_re, cl_im, lam_re, lam_im, d)


def _log_sigmoid(x):
    return jnp.minimum(x, 0.0) - jnp.log1p(jnp.exp(-jnp.abs(x)))


def _logaddexp(a, b):
    return jnp.maximum(a, b) + jnp.log1p(jnp.exp(-jnp.abs(a - b)))


def _hgrn_gates(fpre, lb):
    log_f = _logaddexp(jnp.log(jnp.maximum(lb, LB_FLOOR)), jnp.log1p(-lb) + _log_sigmoid(fpre))
    k = (1.0 - lb) * _sigmoid(-fpre)
    return log_f, k


def _hgrn_scan_matrix():
    c = HGRN_CHUNK
    tri = np.tril(np.ones((c, c), np.float32))
    t = np.arange(c)
    mats = [tri]
    for m in HGRN_LEVELS:
        mats.append(tri[(t // (2 * m)) * (2 * m) + m - 1])
    return np.concatenate(mats, axis=0)


def _hgrn_prompt_kernel(q_ref, f_ref, i_ref, g_ref, lbraw_ref, gn_ref, scan_ref, o_ref, s_ref, st_ref, *, layer):
    parts = (q_ref, f_ref, i_ref, g_ref)
    cidx = pl.program_id(1)
    c = HGRN_CHUNK
    hd = HGRN_HEAD_DIM

    @pl.when(cidx == 0)
    def _():
        st_ref[...] = jnp.zeros_like(st_ref)

    a = lbraw_ref[...]
    e = jnp.exp(a - jnp.max(a, axis=0, keepdims=True))
    p = e / jnp.sum(e, axis=0, keepdims=True)
    run = p[0:1]
    for l in range(1, layer + 1):
        run = run + p[l:l + 1]
    lb_all = run - p[0:1]

    r = lax.broadcasted_iota(jnp.int32, (c, c), 0)
    s = lax.broadcasted_iota(jnp.int32, (c, c), 1)
    pairs = []
    for m in HGRN_LEVELS:
        sh = int(math.log2(2 * m))
        pairs.append(((r >> sh) == (s >> sh)) & ((r & m) != 0) & ((s & m) == 0))
    scan = scan_ref[...]
    for h in range(HGRN_HEADS):
        col = lambda part: parts[part][0, :, h * hd:(h + 1) * hd]
        log_f, k = _hgrn_gates(col(1), lb_all[:, h * hd:(h + 1) * hd])
        q = _silu(col(0))
        v = col(2)
        hi = log_f.astype(BF16)
        rest = log_f - hi.astype(F32)
        mid = rest.astype(BF16)
        lo = (rest - mid.astype(F32)).astype(BF16)
        cum3 = _dot(scan, jnp.concatenate([hi, mid, lo], axis=1))
        cum = cum3[:, 0:hd] + cum3[:, hd:2 * hd] + cum3[:, 2 * hd:3 * hd]
        b = cum[0:c]
        att = jnp.where(r == s, jnp.sum(q * k, axis=-1, keepdims=True), 0.0)
        for lvl, m in enumerate(HGRN_LEVELS):
            d = b - cum[(lvl + 1) * c:(lvl + 2) * c]
            qs = q * jnp.exp(jnp.minimum(d, 0.0))
            ks = k * jnp.exp(jnp.minimum(-d, 0.0))
            att = jnp.where(pairs[lvl], _dot_nt(qs.astype(BF16), ks.astype(BF16)), att)
        st = st_ref[h]
        vb = v.astype(BF16)
        o = _dot(att.astype(BF16), vb) + _dot_nt((q * jnp.exp(b)).astype(BF16), st.astype(BF16))
        bl = b[c - 1:c]
        st_new = jnp.exp(bl) * st + _dot_tn(vb, (k * jnp.exp(bl - b)).astype(BF16))
        st_ref[h] = st_new
        o = o * lax.rsqrt(jnp.mean(o * o, axis=-1, keepdims=True) + NORM_EPS)
        o_ref[0, :, h * hd:(h + 1) * hd] = o * gn_ref[:, h * hd:(h + 1) * hd] * _silu(col(3))

    @pl.when(cidx == pl.num_programs(1) - 1)
    def _():
        s_ref[0] = st_ref[...]


def _hgrn_prompt(proj, lbraw, gn, layer):
    n, t, _ = proj.shape
    c = HGRN_CHUNK
    hd = HGRN_HEAD_DIM
    scan = jnp.asarray(_hgrn_scan_matrix(), dtype=BF16)
    return pl.pallas_call(
        functools.partial(_hgrn_prompt_kernel, layer=layer),
        grid=(n, t // c),
        in_specs=[pl.BlockSpec((1, c, HGRN_WIDTH), lambda b, j, blk=off // HGRN_WIDTH: (b, j, blk)) for off in OFF_D]
                 + [pl.BlockSpec((DEPTH, HGRN_WIDTH), lambda b, j: (0, 0)),
                  pl.BlockSpec((None, 1, HGRN_WIDTH), lambda b, j: (layer, 0, 0)),
                  pl.BlockSpec(scan.shape, lambda b, j: (0, 0))],
        out_specs=[pl.BlockSpec((1, c, HGRN_WIDTH), lambda b, j: (b, j, 0)),
                   pl.BlockSpec((1, HGRN_HEADS, hd, hd), lambda b, j: (b, 0, 0, 0))],
        out_shape=[jax.ShapeDtypeStruct((n, t, HGRN_WIDTH), F32),
                   jax.ShapeDtypeStruct((n, HGRN_HEADS, hd, hd), F32)],
        scratch_shapes=[pltpu.VMEM((HGRN_HEADS, hd, hd), F32)],
        compiler_params=_cparams(("parallel", "arbitrary")),
    )(proj, proj, proj, proj, lbraw, _v3(gn), scan)


def _hgrn_sample_kernel(qt_ref, ft_ref, lbt_ref, v_ref, g_ref, gn_ref, s0_ref, o_ref, s_ref, *, layer, n):
    hd = HGRN_HEAD_DIM
    a = lbt_ref[0]
    e = jnp.exp(a - jnp.max(a, axis=1, keepdims=True))
    p = e / jnp.sum(e, axis=1, keepdims=True)
    run = p[:, 0:1]
    for l in range(1, layer + 1):
        run = run + p[:, l:l + 1]
    lb = run - p[:, 0:1]
    log_f, k = _hgrn_gates(ft_ref[0], lb)
    f = jnp.exp(log_f)
    q = _silu(qt_ref[0])
    v = v_ref[...]
    gate = g_ref[...]
    rows = []
    for b in range(n):
        s_new = f[:, b:b + 1] * s0_ref[b, 0] + k[:, b:b + 1] * v[b:b + 1, :]
        s_ref[b, 0] = s_new
        rows.append(jnp.sum(q[:, b:b + 1] * s_new, axis=0, keepdims=True))
    o = jnp.concatenate(rows, axis=0)
    o = o * lax.rsqrt(jnp.mean(o * o, axis=-1, keepdims=True) + NORM_EPS)
    o_ref[...] = o * gn_ref[...] * _silu(gate)


def _hgrn_sample(qt, ft, lbt, proj, gn, s0, layer):
    n = proj.shape[0]
    hd = HGRN_HEAD_DIM
    v_blk = OFF_D[2] // hd
    g_blk = OFF_D[3] // hd
    return pl.pallas_call(
        functools.partial(_hgrn_sample_kernel, layer=layer, n=n),
        grid=(HGRN_HEADS,),
        in_specs=[pl.BlockSpec((1, hd, n), lambda h: (h, 0, 0)),
                  pl.BlockSpec((1, hd, n), lambda h: (h, 0, 0)),
                  pl.BlockSpec((1, hd, DEPTH), lambda h: (h, 0, 0)),
                  pl.BlockSpec((n, hd), lambda h: (0, v_blk + h)),
                  pl.BlockSpec((n, hd), lambda h: (0, g_blk + h)),
                  pl.BlockSpec((None, 1, hd), lambda h: (layer, 0, h)),
                  pl.BlockSpec((None, n, 1, hd, hd), lambda h: (layer, 0, h, 0, 0))],
        out_specs=[pl.BlockSpec((n, hd), lambda h: (0, h)),
                   pl.BlockSpec((n, 1, hd, hd), lambda h: (0, h, 0, 0))],
        out_shape=[jax.ShapeDtypeStruct((n, HGRN_WIDTH), F32),
                   jax.ShapeDtypeStruct((n, HGRN_HEADS, hd, hd), F32)],
        compiler_params=_cparams(("parallel",)),
    )(qt, ft, lbt, proj, proj, _v3(gn), s0)


def _gelu_tanh(x):
    return 0.5 * x * (1.0 + jnp.tanh(math.sqrt(2.0 / math.pi) * (x + 0.044715 * (x * x * x))))


def _mixout_kernel(h_ref, ya_ref, ob_ref, oc0_ref, oc1_ref, oc2_ref, od_ref,
                   wglu_ref, bglu_ref, ga_ref, gb_ref, gc_ref, wo_ref, gpost_ref, o_ref, *scratch):
    ya = _s5_chunks_to_rows(ya_ref, scratch[0]) if scratch else ya_ref[...]
    z = _gelu_tanh(ya)
    gate = _sigmoid(_dot(z.astype(BF16), wglu_ref[...].astype(BF16)) + bglu_ref[...])
    out_a = _rms(z * gate, ga_ref[...])
    out_b = _rms(ob_ref[...], gb_ref[...])
    ocs = [oc0_ref, oc1_ref, oc2_ref]
    ls = [r[:, DIL_GW:2 * DIL_GW] for r in ocs]
    m = jnp.maximum(jnp.maximum(ls[0], ls[1]), ls[2])
    es = [jnp.exp(l - m) for l in ls]
    den = es[0] + es[1] + es[2]
    cs = [ocs[gi][:, 0:DIL_GW] * (es[gi] / den) for gi in range(3)]
    ss = sum(jnp.sum(c * c, axis=-1, keepdims=True) for c in cs)
    inv = lax.rsqrt(ss / DIL_WIDTH + NORM_EPS)
    gc = gc_ref[...]
    o1, o2, o3 = SSM_WIDTH, SSM_WIDTH + SWA_WIDTH, SSM_WIDTH + SWA_WIDTH + DIL_WIDTH
    wo = lambda lo, hi: wo_ref[lo:hi, :]
    mix = _dot(out_a.astype(BF16), wo(0, o1)) + _dot(out_b.astype(BF16), wo(o1, o2))
    for gi in range(3):
        c = cs[gi] * inv * gc[:, gi * DIL_GW:(gi + 1) * DIL_GW]
        mix = mix + _dot(c.astype(BF16), wo(o2 + gi * DIL_GW, o2 + (gi + 1) * DIL_GW))
    mix = mix + _dot(od_ref[...].astype(BF16), wo(o3, D_MODEL))
    o_ref[...] = h_ref[...] + _rms(mix, gpost_ref[...])


def _mixout(h, ya, ob, ocs, od, lp_all, w_out_l, layer, tm, row0=0):
    m = ob.shape[0]
    blk0 = row0 // tm
    rowblk = lambda a: pl.BlockSpec((tm, a.shape[1]), lambda i: (i, 0))
    vec = lambda a: pl.BlockSpec((None, 1, a.shape[1]), lambda i: (layer, 0, 0))
    mat = lambda a: pl.BlockSpec((None,) + a.shape[1:], lambda i: (layer, 0, 0))
    whole = lambda a: pl.BlockSpec(a.shape, lambda i: (0, 0), pipeline_mode=pl.Buffered(1))
    acts = [h, ya, ob, *ocs, od]
    names = ['ssm_w_glu', 'ssm_b_glu', 'out_norm_a', 'out_norm_b', 'out_norm_c', 'w_out', 'mix_norm_post']
    params = [w_out_l if k == 'w_out' else lp_all[k] for k in names]
    spec = lambda k, p: whole(p) if k == 'w_out' else (mat(p) if p.ndim == 3 else vec(p))
    act_specs = [rowblk(a) for a in acts]
    act_specs[0] = pl.BlockSpec((tm, D_MODEL), lambda i: (i + blk0, 0))
    scratch = []
    chunked = ya.ndim == 3
    if chunked:
        act_specs[1] = pl.BlockSpec((SSM_GROUPS, tm // S5_CHUNK, ya.shape[2]), lambda i: (0, i, 0))
        scratch = [pltpu.VMEM((-(-SSM_WIDTH // LANES), tm, LANES), F32)]
    args = [*acts, *[_v3(p) if p.ndim == 2 and k != 'w_out' else p for k, p in zip(names, params)]]
    in_specs = act_specs + [spec(k, p) for k, p in zip(names, params)]
    return pl.pallas_call(
        _mixout_kernel,
        grid=(m // tm,),
        in_specs=in_specs,
        out_specs=pl.BlockSpec((tm, D_MODEL), lambda i: (i, 0)),
        out_shape=jax.ShapeDtypeStruct((m, D_MODEL), F32),
        scratch_shapes=scratch,
        compiler_params=_cparams(("parallel",)),
    )(*args)


def _ffn_block(x, x_tail, P, name, layer, tm, n_main, n_tail, tail_out):
    wd = _cast_col_tiles(P[name + '_w_down'], layer, DOWN_TILE, 8)
    return _ffn(x, x_tail, P[name + '_norm_pre'], P[name + '_w_gate'], P[name + '_w_up'], wd,
                P[name + '_norm_post'], layer, tm, n_main, n_tail, tail_out)


def _prompt_mixer(h, P, w_in_l, w_out_l, s5p, layer, n, t, rope, tm):
    m = n * t
    proj2, u = _win(h, P['mix_norm_pre'], w_in_l, rope[0], rope[1], layer, tm, True)
    proj = proj2.reshape(n, t, PROJ_COLS)

    lam_re, lam_im, cl_re, cl_im, r_re, r_im, kmat = s5p
    nchunk = t // S5_CHUNK
    d_t = jnp.tile(P['ssm_d'][layer], (1, S5_CHUNK)).reshape(SSM_GROUPS, 1, S5_CHUNK * SSM_GROUP)
    y, xre, xim = _s5_prompt(u, kmat, r_re, r_im, cl_re, cl_im, lam_re, lam_im, d_t, n, nchunk)
    ssm_new = jnp.stack([xre, xim], axis=-1).transpose(1, 0, 2, 3)

    sink = jnp.repeat(P['swa_sinks'][layer], HEAD_DIM).reshape(1, SWA_WIDTH)
    ob = _swa_prompt(proj, sink).reshape(m, SWA_WIDTH)
    keep = min(WINDOW, t)
    kv = proj[:, t - keep:, OFF_B + SWA_WIDTH:OFF_B + B_SLOT]
    swa_new = kv.reshape(n, keep, 2, SWA_KV_HEADS, HEAD_DIM)

    ocs, dil_new = [], []
    for gi, (win, dil) in enumerate(DIL_PAIRS):
        ocs.append(_dil_prompt(proj, gi, dil).reshape(m, 2 * DIL_GW))
        keep = min(win, t)
        lo = OFF_C + gi * C_SLOT + C_K
        dil_new.append(proj[:, t - keep:, lo:lo + 2 * HEAD_DIM].reshape(n, keep, 2, HEAD_DIM))

    od, st = _hgrn_prompt(proj, P['hgrn_lower_bounds'], P['out_norm_d'], layer)
    hgrn_new = jnp.swapaxes(st, -1, -2)

    h = _mixout(h, y, ob, ocs, od.reshape(m, HGRN_WIDTH), P, w_out_l, layer, min(tm, 512))
    return h, (ssm_new, swa_new, dil_new[0], dil_new[1], dil_new[2], hgrn_new)


def _sample_mixer(h, row0, P, w_in_l, w_out_l, s5p, layer, caches, rope):
    n = rope[0].shape[0]
    state_ssm, cache_swa, cache_d0, cache_d1, cache_d2, state_hgrn = caches
    (proj,) = _win(h, P['mix_norm_pre'], w_in_l, rope[0], rope[1], layer, n, False, row0)

    lam_re, lam_im, cl_re, cl_im, r_re, r_im, _ = s5p
    u = proj[:, OFF_U:OFF_U + SSM_WIDTH].reshape(n, SSM_GROUPS, SSM_GROUP).transpose(1, 0, 2)
    x0 = state_ssm[layer].transpose(1, 0, 2, 3)
    y, xre, xim = _s5_sample(u, x0[..., 0], x0[..., 1], r_re, r_im, cl_re, cl_im, lam_re, lam_im,
                             P['ssm_d'][layer].reshape(SSM_GROUPS, 1, SSM_GROUP))
    ya = y.transpose(1, 0, 2).reshape(n, SSM_WIDTH)
    ssm_new = jnp.stack([xre, xim], axis=-1).transpose(1, 0, 2, 3)

    sink = jnp.repeat(P['swa_sinks'][layer], HEAD_DIM).reshape(1, SWA_WIDTH)
    ob = _swa_sample(proj, cache_swa.transpose(0, 1, 3, 4, 5, 2), sink, layer)
    swa_new = proj[:, OFF_B + SWA_WIDTH:OFF_B + B_SLOT].reshape(n, 1, 2, SWA_KV_HEADS, HEAD_DIM)

    ocs, dil_new = [], []
    for gi, (buf, (win, dil)) in enumerate(zip((cache_d0, cache_d1, cache_d2), DIL_PAIRS)):
        ocs.append(_dil_sample(proj, buf.transpose(0, 1, 3, 4, 2), gi, dil, layer))
        lo = OFF_C + gi * C_SLOT + C_K
        dil_new.append(proj[:, lo:lo + 2 * HEAD_DIM].reshape(n, 1, 2, HEAD_DIM))

    part = lambda i: proj[:, OFF_D[i]:OFF_D[i] + HGRN_WIDTH].reshape(n, HGRN_HEADS, HGRN_HEAD_DIM)
    cm = lambda a: a.transpose(1, 2, 0)
    lbt = P['hgrn_lower_bounds'].reshape(DEPTH, HGRN_HEADS, HGRN_HEAD_DIM).transpose(1, 2, 0)
    od, hgrn_new = _hgrn_sample(cm(part(0)), cm(part(1)), lbt, proj, P['out_norm_d'], state_hgrn, layer)

    h = _mixout(h, ya, ob, ocs, od, P, w_out_l, layer, n, row0)
    return h, (ssm_new, swa_new, dil_new[0], dil_new[1], dil_new[2], hgrn_new)


def _forward(x_prompt, x_sample, caches, P):
    n, t, _ = x_prompt.shape
    ns, ts, _ = x_sample.shape
    assert ts == 1
    tm = 1024
    m_p = n * t
    rope_p = _rope_tables(jnp.tile(jnp.arange(t, dtype=jnp.int32), n))
    rope_s = _rope_tables(jnp.full((ns,), PAST_LEN, dtype=jnp.int32))
    nsteps = int(math.log2(t // S5_CHUNK))
    h = x_prompt.reshape(m_p, D_MODEL)
    h_tail = x_sample.reshape(ns, D_MODEL)
    new_p = [[] for _ in range(6)]
    new_s = [[] for _ in range(6)]
    for l in range(DEPTH):
        w_in_l = _win_prep(P['w_in'], l)
        w_out_l = _cast_bf16(P['w_out'], l)
        s5p = _s5_params(P['ssm_a_re'][l], P['ssm_a_im'][l], P['ssm_log_dt'][l], P['ssm_b_re'][l], P['ssm_b_im'][l],
                         P['ssm_c_re'][l], P['ssm_c_im'][l], nsteps)
        h = _ffn_block(h, h_tail, P, 'ffn1', l, FFN_ROWS, m_p, ns, False)
        mixed_p, st_p = _prompt_mixer(h, P, w_in_l, w_out_l, s5p, l, n, t, rope_p, tm)
        mixed_s, st_s = _sample_mixer(h, m_p, P, w_in_l, w_out_l, s5p, l, caches, rope_s)
        h = _ffn_block(mixed_p, mixed_s, P, 'ffn2', l, FFN_ROWS, m_p, ns, l == DEPTH - 1)
        h_tail = None
        for i in range(6):
            new_p[i].append(st_p[i])
            new_s[i].append(st_s[i])
    y_p, y_s = h
    new_p = [jnp.stack(a, axis=0) for a in new_p]
    new_s = [jnp.stack(a, axis=0) for a in new_s]
    return (y_p.reshape(n, t, D_MODEL), y_s.reshape(ns, ts, D_MODEL), new_p[0], new_s[0], new_p[1], new_s[1],
            new_p[2], new_s[2], new_p[3], new_s[3], new_p[4], new_s[4], new_p[5], new_s[5])


def kernel(x_prompt, x_sample, state_ssm, cache_swa_kv, cache_dil0_kv, cache_dil1_kv, cache_dil2_kv, state_hgrn, ffn1_norm_pre, ffn1_w_gate, ffn1_w_up, ffn1_w_down, ffn1_norm_post, mix_norm_pre, w_in, ssm_a_re, ssm_a_im, ssm_log_dt, ssm_b_re, ssm_b_im, ssm_c_re, ssm_c_im, ssm_d, ssm_w_glu, ssm_b_glu, swa_sinks, hgrn_lower_bounds, out_norm_a, out_norm_b, out_norm_c, out_norm_d, w_out, mix_norm_post, ffn2_norm_pre, ffn2_w_gate, ffn2_w_up, ffn2_w_down, ffn2_norm_post):
    P = dict(ffn1_norm_pre=ffn1_norm_pre, ffn1_w_gate=ffn1_w_gate, ffn1_w_up=ffn1_w_up, ffn1_w_down=ffn1_w_down,
             ffn1_norm_post=ffn1_norm_post, mix_norm_pre=mix_norm_pre, w_in=w_in, ssm_a_re=ssm_a_re,
             ssm_a_im=ssm_a_im, ssm_log_dt=ssm_log_dt, ssm_b_re=ssm_b_re, ssm_b_im=ssm_b_im, ssm_c_re=ssm_c_re,
             ssm_c_im=ssm_c_im, ssm_d=ssm_d, ssm_w_glu=ssm_w_glu, ssm_b_glu=ssm_b_glu, swa_sinks=swa_sinks,
             hgrn_lower_bounds=hgrn_lower_bounds, out_norm_a=out_norm_a, out_norm_b=out_norm_b,
             out_norm_c=out_norm_c, out_norm_d=out_norm_d, w_out=w_out, mix_norm_post=mix_norm_post,
             ffn2_norm_pre=ffn2_norm_pre, ffn2_w_gate=ffn2_w_gate, ffn2_w_up=ffn2_w_up, ffn2_w_down=ffn2_w_down,
             ffn2_norm_post=ffn2_norm_post)
    caches = (state_ssm, cache_swa_kv, cache_dil0_kv, cache_dil1_kv, cache_dil2_kv, state_hgrn)
    return _forward(x_prompt, x_sample, caches, P)
```

```python
import functools
import math

import numpy as np
import jax
import jax.numpy as jnp
from jax import lax
from jax.experimental import pallas as pl
from jax.experimental.pallas import tpu as pltpu

F32 = jnp.float32
BF16 = jnp.bfloat16

D_MODEL = 2048
DEPTH = 2
PAST_LEN = 16384
HEAD_DIM = 64
ROPE_THETA = 10000.0
D_FF = 5504
NORM_EPS = 1e-6
NEG_BIG = -1e30
LB_FLOOR = 1e-30
SSM_GROUP = 16
SSM_STATE = 64
SSM_WIDTH = 448
SSM_GROUPS = 28
SWA_HEADS = 8
SWA_KV_HEADS = 2
SWA_GROUP = 4
SWA_WIDTH = 512
SWA_KV_WIDTH = 128
WINDOW = 128
DIL_PAIRS = ((128, 1), (512, 4), (2048, 16))
DIL_HPG = 3
DIL_GW = DIL_HPG * HEAD_DIM
DIL_WIDTH = 576
DIL_KV_WIDTH = 192
HGRN_HEAD_DIM = 128
HGRN_WIDTH = 512
HGRN_HEADS = 4
IN_COLS = 4224
ATTN_SCALE = HEAD_DIM ** -0.5

LANES = 128
FF_TILE = 256
N_FF_TILES = -(-D_FF // FF_TILE)
DOWN_TILE = 256
N_DOWN_TILES = D_MODEL // DOWN_TILE
FFN_ROWS = 1040
VMEM_LIMIT = 60 * 1024 * 1024

B_SLOT = SWA_WIDTH + 2 * SWA_KV_WIDTH
C_SLOT = 384
C_K = 256
OFF_B = 0
OFF_C = OFF_B + B_SLOT
OFF_D = (2048, 2560, 3584, 4096)
OFF_U = 3072
U_SLOT = 512
PROJ_COLS = 4608
WIN_TILE = 1536

S5_CHUNK = 16
S5_NPOW = S5_CHUNK + 1
HGRN_CHUNK = 128
HGRN_LEVELS = (64, 32, 16, 8, 4, 2, 1)


def _proj_segments():
    src = np.cumsum([0, SSM_WIDTH, SWA_WIDTH, SWA_KV_WIDTH, SWA_KV_WIDTH, DIL_WIDTH, DIL_KV_WIDTH, DIL_KV_WIDTH,
                     HGRN_WIDTH, HGRN_WIDTH, HGRN_WIDTH, HGRN_WIDTH])
    s_u, s_qb, s_kb, s_vb, s_qc, s_kc, s_vc, s_qd = src[:8]
    segs = [(OFF_B, s_qb, SWA_WIDTH), (OFF_B + SWA_WIDTH, s_kb, SWA_KV_WIDTH),
            (OFF_B + SWA_WIDTH + SWA_KV_WIDTH, s_vb, SWA_KV_WIDTH)]
    for g in range(3):
        base = OFF_C + g * C_SLOT
        segs += [(base, s_qc + g * DIL_GW, DIL_GW), (base + C_K, s_kc + g * HEAD_DIM, HEAD_DIM),
                 (base + C_K + HEAD_DIM, s_vc + g * HEAD_DIM, HEAD_DIM)]
    segs += [(off, s_qd + i * HGRN_WIDTH, HGRN_WIDTH) for i, off in enumerate(OFF_D)]
    segs += [(OFF_U, s_u, SSM_WIDTH)]
    return [(int(a), int(b), int(c)) for a, b, c in segs]


def _rotary_lane_mask():
    m = np.zeros((PROJ_COLS,), np.float32)
    m[OFF_B:OFF_B + SWA_WIDTH + SWA_KV_WIDTH] = 1.0
    for g in range(3):
        base = OFF_C + g * C_SLOT
        m[base:base + DIL_GW] = 1.0
        m[base + C_K:base + C_K + HEAD_DIM] = 1.0
    return m


def _cparams(sem):
    return pltpu.CompilerParams(dimension_semantics=sem, vmem_limit_bytes=VMEM_LIMIT)


def _v3(a):
    return a.reshape(a.shape[0], 1, a.shape[1])


def _rms(x, g):
    return x * lax.rsqrt(jnp.mean(x * x, axis=-1, keepdims=True) + NORM_EPS) * g


def _sigmoid(x):
    return 1.0 / (1.0 + jnp.exp(-x))


def _silu(x):
    return x * _sigmoid(x)


def _dot(a, b):
    return jnp.dot(a, b, preferred_element_type=F32)


def _dot_nt(a, b):
    return lax.dot_general(a, b, (((1,), (1,)), ((), ())), preferred_element_type=F32)


def _dot_tn(a, b):
    return lax.dot_general(a, b, (((0,), (0,)), ((), ())), preferred_element_type=F32)


def _ffn_kernel(*refs, tail, n_tail, tail_in, tail_out):
    refs = list(refs)
    x_ref = refs.pop(0)
    xs_ref = refs.pop(0) if tail_in else None
    gpre_ref, wg_ref, wu_ref, wd_ref, gpost_ref, o_ref = refs[:6]
    os_ref = refs[6] if tail_out else None
    xn_ref, hid_ref = refs[-2:]
    s = pl.program_id(1)
    in_last_block = pl.program_id(0) == pl.num_programs(0) - 1
    tail_rows = slice(tail, tail + n_tail)

    @pl.when(s == 0)
    def _():
        xn_ref[...] = _rms(x_ref[...], gpre_ref[...]).astype(BF16)
        if tail_in:

            @pl.when(in_last_block)
            def _():
                xn_ref[tail_rows, :] = _rms(xs_ref[...], gpre_ref[...]).astype(BF16)

    @pl.when(s < N_FF_TILES)
    def _():
        xn = xn_ref[...]
        gate = _dot(xn, wg_ref[...].astype(BF16))
        up = _dot(xn, wu_ref[...].astype(BF16))
        hid_ref[s] = (_silu(gate) * up).astype(BF16)

    for j in range(N_DOWN_TILES):

        @pl.when(s == N_FF_TILES + j)
        def _(j=j):
            hid = jnp.concatenate([hid_ref[f] for f in range(N_FF_TILES)], axis=1)[:, :D_FF]
            o_ref[:, j * DOWN_TILE:(j + 1) * DOWN_TILE] = _dot(hid, wd_ref[...].astype(BF16))

    @pl.when(s == pl.num_programs(1) - 1)
    def _():
        delta = 0.5 * _rms(o_ref[...], gpost_ref[...])
        res = x_ref[...] + delta
        o_ref[...] = res
        if tail_in or tail_out:

            @pl.when(in_last_block)
            def _():
                tail_res = xs_ref[...] + delta[tail_rows] if tail_in else res[tail_rows]
                if tail_out:
                    os_ref[...] = tail_res
                else:
                    o_ref[tail_rows, :] = tail_res


def _ffn(x, x_tail, gpre, wg, wu, wd, gpost, layer, tm, n_main, n_tail, tail_out):
    m = n_main + n_tail
    nblk = pl.cdiv(m, tm)
    tail = n_main - (nblk - 1) * tm
    assert 0 <= tail and tail + n_tail <= tm and tail % 8 == 0
    last_ff = N_FF_TILES - 1
    in_specs = [pl.BlockSpec((tm, D_MODEL), lambda i, s: (i, 0))]
    args = [x]
    if x_tail is not None:
        in_specs.append(pl.BlockSpec((n_tail, D_MODEL), lambda i, s: (0, 0)))
        args.append(x_tail)
    in_specs += [
        pl.BlockSpec((None, 1, D_MODEL), lambda i, s: (layer, 0, 0)),
        pl.BlockSpec((None, D_MODEL, FF_TILE), lambda i, s: (layer, 0, jnp.minimum(s, last_ff))),
        pl.BlockSpec((None, D_MODEL, FF_TILE), lambda i, s: (layer, 0, jnp.minimum(s, last_ff))),
        pl.BlockSpec((None, D_FF, DOWN_TILE), lambda i, s: (layer, 0, jnp.maximum(s - N_FF_TILES, 0))),
        pl.BlockSpec((None, 1, D_MODEL), lambda i, s: (layer, 0, 0)),
    ]
    args += [_v3(gpre), wg, wu, wd, _v3(gpost)]
    out_main = pl.BlockSpec((tm, D_MODEL), lambda i, s: (i, 0), pipeline_mode=pl.Buffered(1))
    if tail_out:
        out_specs = [out_main, pl.BlockSpec((n_tail, D_MODEL), lambda i, s: (0, 0))]
        out_shape = [jax.ShapeDtypeStruct((n_main, D_MODEL), F32), jax.ShapeDtypeStruct((n_tail, D_MODEL), F32)]
    else:
        out_specs = out_main
        out_shape = jax.ShapeDtypeStruct((m, D_MODEL), F32)
    return pl.pallas_call(
        functools.partial(_ffn_kernel, tail=tail, n_tail=n_tail, tail_in=x_tail is not None, tail_out=tail_out),
        grid=(nblk, N_FF_TILES + N_DOWN_TILES),
        in_specs=in_specs,
        out_specs=out_specs,
        out_shape=out_shape,
        scratch_shapes=[pltpu.VMEM((tm, D_MODEL), BF16), pltpu.VMEM((N_FF_TILES, tm, FF_TILE), BF16)],
        compiler_params=_cparams(("arbitrary", "arbitrary")),
    )(*args)


def _win_prep_kernel(w_ref, o_ref):
    o_ref[...] = jnp.zeros_like(o_ref)
    for dst, src, width in _proj_segments():
        o_ref[:, dst:dst + width] = w_ref[:, src:src + width].astype(BF16)


def _win_prep(w_in, layer):
    rows = 256
    return pl.pallas_call(
        _win_prep_kernel,
        grid=(D_MODEL // rows,),
        in_specs=[pl.BlockSpec((None, rows, IN_COLS), lambda i: (layer, i, 0))],
        out_specs=pl.BlockSpec((rows, PROJ_COLS), lambda i: (i, 0)),
        out_shape=jax.ShapeDtypeStruct((D_MODEL, PROJ_COLS), BF16),
        compiler_params=_cparams(("parallel",)),
    )(w_in)


def _cast_kernel(w_ref, o_ref):
    o_ref[...] = w_ref[...].astype(BF16)


def _cast_bf16(w, layer):
    _, r, c = w.shape
    rows = 256
    return pl.pallas_call(
        _cast_kernel,
        grid=(r // rows,),
        in_specs=[pl.BlockSpec((None, rows, c), lambda i: (layer, i, 0))],
        out_specs=pl.BlockSpec((rows, c), lambda i: (i, 0)),
        out_shape=jax.ShapeDtypeStruct((r, c), BF16),
        compiler_params=_cparams(("parallel",)),
    )(w)


def _win_kernel(x_ref, g_ref, w_ref, cos_ref, sin_ref, o_ref, *rest, chunks):
    if chunks:
        u_ref, xn_ref, st_ref = rest
    else:
        (xn_ref,) = rest
    j = pl.program_id(1)

    @pl.when(j == 0)
    def _():
        xn_ref[...] = _rms(x_ref[...], g_ref[...]).astype(BF16)

    rot = _rotary_lane_mask()
    for tile in range(PROJ_COLS // WIN_TILE):

        @pl.when(j == tile)
        def _(tile=tile):
            y = _dot(xn_ref[...], w_ref[...])
            cos = cos_ref[...]
            sin = sin_ref[...]
            lane = lax.broadcasted_iota(jnp.int32, cos.shape, 1)
            first_half = (lane % HEAD_DIM) < (HEAD_DIM // 2)
            for c in range(WIN_TILE // LANES):
                sl = slice(c * LANES, (c + 1) * LANES)
                flags = rot[tile * WIN_TILE + c * LANES:tile * WIN_TILE + (c + 1) * LANES]
                x = y[:, sl]
                lo, hi = bool(flags[:HEAD_DIM].all()), bool(flags[HEAD_DIM:].all())
                if not (lo or hi):
                    o_ref[:, sl] = x
                    continue
                partner = jnp.where(first_half,
                                    pltpu.roll(x, LANES - HEAD_DIM // 2, axis=1),
                                    pltpu.roll(x, HEAD_DIM // 2, axis=1))
                rotated = x * cos + partner * sin
                if lo and hi:
                    o_ref[:, sl] = rotated
                else:
                    is_rot = (lane < HEAD_DIM) if lo else (lane >= HEAD_DIM)
                    o_ref[:, sl] = jnp.where(is_rot, rotated, x)
            if chunks and tile == OFF_U // WIN_TILE:
                base = OFF_U % WIN_TILE
                for lb in range(U_SLOT // LANES):
                    st_ref[lb] = y[:, base + lb * LANES:base + (lb + 1) * LANES]
                _s5_rows_to_chunks(st_ref, u_ref)


def _win(h, g, w, cos_t, sin_t, layer, tm, chunks, row0=0):
    m = cos_t.shape[0]
    blk0 = row0 // tm
    cw = S5_CHUNK * SSM_GROUP
    out_specs = [pl.BlockSpec((tm, WIN_TILE), lambda i, j: (i, j))]
    out_shape = [jax.ShapeDtypeStruct((m, PROJ_COLS), F32)]
    scratch = [pltpu.VMEM((tm, D_MODEL), BF16)]
    if chunks:
        out_specs.append(pl.BlockSpec((SSM_GROUPS, tm // S5_CHUNK, cw), lambda i, j: (0, i, 0)))
        out_shape.append(jax.ShapeDtypeStruct((SSM_GROUPS, m // S5_CHUNK, cw), F32))
        scratch.append(pltpu.VMEM((U_SLOT // LANES, tm, LANES), F32))
    return pl.pallas_call(
        functools.partial(_win_kernel, chunks=chunks),
        grid=(m // tm, PROJ_COLS // WIN_TILE),
        in_specs=[
            pl.BlockSpec((tm, D_MODEL), lambda i, j: (i + blk0, 0)),
            pl.BlockSpec((None, 1, D_MODEL), lambda i, j: (layer, 0, 0)),
            pl.BlockSpec((D_MODEL, WIN_TILE), lambda i, j: (0, j)),
            pl.BlockSpec((tm, LANES), lambda i, j: (i, 0)),
            pl.BlockSpec((tm, LANES), lambda i, j: (i, 0)),
        ],
        out_specs=out_specs,
        out_shape=out_shape,
        scratch_shapes=scratch,
        compiler_params=_cparams(("parallel", "arbitrary")),
    )(h, _v3(g), w, cos_t, sin_t)


def _rope_tables(pos):
    half = HEAD_DIM // 2
    inv_freq = ROPE_THETA ** (-jnp.arange(half, dtype=F32) / half)
    ang = pos.astype(F32)[:, None] * inv_freq[None, :]
    cos, sin = jnp.cos(ang), jnp.sin(ang)
    reps = LANES // HEAD_DIM
    cos_t = jnp.tile(jnp.concatenate([cos, cos], axis=1), (1, reps))
    sin_t = jnp.tile(jnp.concatenate([-sin, sin], axis=1), (1, reps))
    return cos_t, sin_t


def _band_heads(q, kk, vv, valid, sinks):
    w = WINDOW
    heads = q.shape[1] // HEAD_DIM
    qs = jnp.concatenate([q[:, g * HEAD_DIM:(g + 1) * HEAD_DIM] for g in range(heads)], axis=0).astype(BF16)
    s = _dot_nt(kk, qs) * ATTN_SCALE
    s = jnp.where(jnp.concatenate([valid] * heads, axis=1), s, NEG_BIG)
    m = jnp.max(s, axis=0, keepdims=True)
    if sinks is not None:
        sink_row = jnp.concatenate([jnp.broadcast_to(sinks[g], (1, w)) for g in range(heads)], axis=1)
        m = jnp.maximum(m, sink_row)
    p = jnp.exp(s - m)
    l = jnp.sum(p, axis=0, keepdims=True)
    if sinks is not None:
        l = l + jnp.exp(sink_row - m)
    ot = _dot_tn(vv, p.astype(BF16)) / l
    lse = m + jnp.log(l)
    return ([ot[:, g * w:(g + 1) * w] for g in range(heads)], [lse[:, g * w:(g + 1) * w] for g in range(heads)])


def _untranspose(blocks):
    pairs = [jnp.concatenate(blocks[i:i + 2], axis=0).T for i in range(0, len(blocks), 2)]
    return jnp.concatenate(pairs, axis=1)


def _band_mask(no_prev):
    w = WINDOW
    k = lax.broadcasted_iota(jnp.int32, (2 * w, w), 0)
    q = lax.broadcasted_iota(jnp.int32, (2 * w, w), 1)
    valid = (k >= q) & (k <= q + w)
    if no_prev is False:
        return valid
    return valid & (k >= jnp.where(no_prev, w, 0))


def _swa_prompt_kernel(cur_ref, prev_ref, sink_ref, o_ref, *, nsub):
    w = WINDOW
    gw = SWA_GROUP * HEAD_DIM
    first = pl.program_id(1) == 0
    for j in range(nsub):
        valid = _band_mask(first if j == 0 else False)
        cur = cur_ref[0, j * w:(j + 1) * w, :]
        prev = prev_ref[0] if j == 0 else cur_ref[0, (j - 1) * w:j * w, SWA_WIDTH:B_SLOT]
        outs = []
        for h in range(SWA_KV_HEADS):
            ks = slice(SWA_WIDTH + h * HEAD_DIM, SWA_WIDTH + (h + 1) * HEAD_DIM)
            vs = slice(SWA_WIDTH + SWA_KV_WIDTH + h * HEAD_DIM, SWA_WIDTH + SWA_KV_WIDTH + (h + 1) * HEAD_DIM)
            kk = jnp.concatenate([prev[:, h * HEAD_DIM:(h + 1) * HEAD_DIM], cur[:, ks]], axis=0).astype(BF16)
            vv = jnp.concatenate([prev[:, SWA_KV_WIDTH + h * HEAD_DIM:SWA_KV_WIDTH + (h + 1) * HEAD_DIM],
                                  cur[:, vs]], axis=0).astype(BF16)
            sinks = [sink_ref[:, h * gw + g * HEAD_DIM:h * gw + g * HEAD_DIM + 1] for g in range(SWA_GROUP)]
            o, _ = _band_heads(cur[:, h * gw:(h + 1) * gw], kk, vv, valid, sinks)
            outs += o
        o_ref[0, j * w:(j + 1) * w, :] = _untranspose(outs)


def _swa_prompt(proj, sink):
    n, t, _ = proj.shape
    nsub = 2
    rows = WINDOW * nsub
    return pl.pallas_call(
        functools.partial(_swa_prompt_kernel, nsub=nsub),
        grid=(n, t // rows),
        in_specs=[
            pl.BlockSpec((1, rows, B_SLOT), lambda b, i: (b, i, OFF_B // B_SLOT)),
            pl.BlockSpec((1, WINDOW, 2 * SWA_KV_WIDTH),
                         lambda b, i: (b, jnp.maximum(i * nsub - 1, 0), (OFF_B + SWA_WIDTH) // (2 * SWA_KV_WIDTH))),
            pl.BlockSpec((1, SWA_WIDTH), lambda b, i: (0, 0)),
        ],
        out_specs=pl.BlockSpec((1, rows, SWA_WIDTH), lambda b, i: (b, i, 0)),
        out_shape=jax.ShapeDtypeStruct((n, t, SWA_WIDTH), F32),
        compiler_params=_cparams(("parallel", "arbitrary")),
    )(proj, proj, sink)


def _dil_prompt_kernel(cur_ref, prev_ref, o_ref, cur3_ref, out3_ref, *, dil, nsub):
    nl = C_SLOT // LANES
    span = WINDOW * dil
    first = pl.program_id(1) == 0
    for c in range(nl):
        cur3_ref[c] = cur_ref[0, :, c * LANES:(c + 1) * LANES]

    def one(j, r):
        def rows(base):
            return pl.ds(base + r, WINDOW, stride=dil) if dil > 1 else pl.ds(base, WINDOW)

        valid = _band_mask(first if j == 0 else False)
        cur = rows(j * span)
        q = jnp.concatenate([cur3_ref[0, cur, :], cur3_ref[1, cur, :]], axis=1)[:, 0:DIL_GW]
        kv = cur3_ref[2, cur, :]
        pkv = prev_ref[0, rows(0), :] if j == 0 else cur3_ref[2, rows((j - 1) * span), :]
        kk = jnp.concatenate([pkv[:, 0:HEAD_DIM], kv[:, 0:HEAD_DIM]], axis=0).astype(BF16)
        vv = jnp.concatenate([pkv[:, HEAD_DIM:], kv[:, HEAD_DIM:]], axis=0).astype(BF16)
        o, lse = _band_heads(q, kk, vv, valid, None)
        res = _untranspose(o + [jnp.broadcast_to(x, (HEAD_DIM, WINDOW)) for x in lse])
        for c in range(nl):
            out3_ref[c, cur, :] = res[:, c * LANES:(c + 1) * LANES]

    for j in range(nsub):
        if dil <= 4:
            for r in range(dil):
                one(j, r)
        else:
            lax.fori_loop(0, dil, lambda r, c, j=j: (one(j, r), c)[1], 0, unroll=2)
    for c in range(nl):
        o_ref[0, :, c * LANES:(c + 1) * LANES] = out3_ref[c]


def _dil_prompt(proj, group, dil):
    n, t, _ = proj.shape
    span = WINDOW * dil
    nsub = max(1, 4 // dil)
    rows = span * nsub
    slot = OFF_C + group * C_SLOT
    return pl.pallas_call(
        functools.partial(_dil_prompt_kernel, dil=dil, nsub=nsub),
        grid=(n, t // rows),
        in_specs=[
            pl.BlockSpec((1, rows, C_SLOT), lambda b, i: (b, i, slot // C_SLOT)),
            pl.BlockSpec((1, span, LANES), lambda b, i: (b, jnp.maximum(i * nsub - 1, 0), (slot + C_K) // LANES)),
        ],
        out_specs=pl.BlockSpec((1, rows, 2 * DIL_GW), lambda b, i: (b, i, 0)),
        out_shape=jax.ShapeDtypeStruct((n, t, 2 * DIL_GW), F32),
        scratch_shapes=[pltpu.VMEM((C_SLOT // LANES, rows, LANES), F32),
                        pltpu.VMEM((2 * DIL_GW // LANES, rows, LANES), F32)],
        compiler_params=_cparams(("parallel", "arbitrary")),
    )(proj, proj)


def _decode_heads(q, knew, vnew, kt, vt, dil, sinks):
    s = _dot(q.astype(BF16), kt.astype(BF16)) * ATTN_SCALE
    if dil > 1:
        row = lax.broadcasted_iota(jnp.int32, s.shape, 1)
        s = jnp.where(row % dil == 0, s, NEG_BIG)
    s_new = jnp.sum(q * knew, axis=-1, keepdims=True) * ATTN_SCALE
    m = jnp.maximum(jnp.max(s, axis=-1, keepdims=True), s_new)
    if sinks is not None:
        m = jnp.maximum(m, sinks)
    p = jnp.exp(s - m)
    p_new = jnp.exp(s_new - m)
    l = jnp.sum(p, axis=-1, keepdims=True) + p_new
    if sinks is not None:
        l = l + jnp.exp(sinks - m)
    o = (_dot_nt(p.astype(BF16), vt.astype(BF16)) + p_new * vnew) / l
    return o, m + jnp.log(l)


def _rows_of(x, heads):
    return jnp.concatenate([x[:, g * HEAD_DIM:(g + 1) * HEAD_DIM] for g in range(heads)], axis=0)


def _swa_sample_kernel(p_ref, cache_ref, sink_ref, o_ref, *, bn):
    for b in range(bn):
        row = p_ref[b:b + 1, :]
        outs = []
        for h in range(SWA_KV_HEADS):
            gw = SWA_GROUP * HEAD_DIM
            q = _rows_of(row[:, h * gw:(h + 1) * gw], SWA_GROUP)
            knew = row[:, SWA_WIDTH + h * HEAD_DIM:SWA_WIDTH + (h + 1) * HEAD_DIM]
            vnew = row[:, SWA_WIDTH + SWA_KV_WIDTH + h * HEAD_DIM:SWA_WIDTH + SWA_KV_WIDTH + (h + 1) * HEAD_DIM]
            sinks = _rows_of(sink_ref[:, h * gw:(h + 1) * gw], SWA_GROUP)[:, 0:1]
            o, _ = _decode_heads(q, knew, vnew, cache_ref[b, 0, h], cache_ref[b, 1, h], 1, sinks)
            outs += [o[g:g + 1] for g in range(SWA_GROUP)]
        o_ref[b:b + 1, :] = jnp.concatenate(outs, axis=1)


def _swa_sample(proj, cache, sink, layer):
    n = proj.shape[0]
    bn = 8
    return pl.pallas_call(
        functools.partial(_swa_sample_kernel, bn=bn),
        grid=(n // bn,),
        in_specs=[
            pl.BlockSpec((bn, B_SLOT), lambda i: (i, OFF_B // B_SLOT)),
            pl.BlockSpec((None, bn, 2, SWA_KV_HEADS, HEAD_DIM, WINDOW), lambda i: (layer, i, 0, 0, 0, 0)),
            pl.BlockSpec((1, SWA_WIDTH), lambda i: (0, 0)),
        ],
        out_specs=pl.BlockSpec((bn, SWA_WIDTH), lambda i: (i, 0)),
        out_shape=jax.ShapeDtypeStruct((n, SWA_WIDTH), F32),
        compiler_params=_cparams(("parallel",)),
    )(proj, cache, sink)


def _dil_sample_kernel(p_ref, cache_ref, o_ref, *, bn, dil):
    for b in range(bn):
        row = p_ref[b:b + 1, :]
        q = _rows_of(row[:, 0:DIL_GW], DIL_HPG)
        o, lse = _decode_heads(q, row[:, C_K:C_K + HEAD_DIM], row[:, C_K + HEAD_DIM:C_K + 2 * HEAD_DIM],
                               cache_ref[b, 0], cache_ref[b, 1], dil, None)
        lse = jnp.broadcast_to(lse, (DIL_HPG, HEAD_DIM))
        o_ref[b:b + 1, :] = jnp.concatenate([o[g:g + 1] for g in range(DIL_HPG)]
                                            + [lse[g:g + 1] for g in range(DIL_HPG)], axis=1)


def _dil_sample(proj, cache, group, dil, layer):
    n = proj.shape[0]
    bn = 8
    slot = OFF_C + group * C_SLOT
    return pl.pallas_call(
        functools.partial(_dil_sample_kernel, bn=bn, dil=dil),
        grid=(n // bn,),
        in_specs=[
            pl.BlockSpec((bn, C_SLOT), lambda i: (i, slot // C_SLOT)),
            pl.BlockSpec((None, bn, 2, HEAD_DIM, WINDOW * dil), lambda i: (layer, i, 0, 0, 0)),
        ],
        out_specs=pl.BlockSpec((bn, 2 * DIL_GW), lambda i: (i, 0)),
        out_shape=jax.ShapeDtypeStruct((n, 2 * DIL_GW), F32),
        compiler_params=_cparams(("parallel",)),
    )(proj, cache)


def _s5_param_kernel(are_ref, aim_ref, ldt_ref, bre_ref, bim_ref, cre_ref, cim_ref,
                     lam_re_ref, lam_im_ref, cl_re_ref, cl_im_ref, r_re_ref, r_im_ref, k_ref):
    ar = are_ref[0]
    ai = aim_ref[0]
    dt = jnp.exp(ldt_ref[0])
    nrow = lam_re_ref.shape[1]
    ri = lax.broadcasted_iota(jnp.int32, (nrow, 1), 0)
    pw = jnp.where(ri < S5_NPOW, ri, jnp.left_shift(S5_CHUNK, jnp.maximum(ri - S5_NPOW, 0))).astype(F32)
    mag = jnp.exp(pw * (ar * dt))
    ang = pw * (ai * dt)
    lam_re = mag * jnp.cos(ang)
    lam_im = mag * jnp.sin(ang)
    lam_re_ref[0] = lam_re
    lam_im_ref[0] = lam_im
    l1r, l1i = lam_re[1:2], lam_im[1:2]
    den = ar * ar + ai * ai
    z_re = ((l1r - 1.0) * ar + l1i * ai) / den
    z_im = (l1i * ar - (l1r - 1.0) * ai) / den
    b_re, b_im = bre_ref[0], bim_ref[0]
    bb_re = z_re * b_re - z_im * b_im
    bb_im = z_re * b_im + z_im * b_re
    c_re, c_im = cre_ref[0], cim_ref[0]
    cls_re, cls_im = [], []
    for d in range(S5_NPOW):
        lr, li = lam_re[d:d + 1], lam_im[d:d + 1]
        cr = c_re * lr - c_im * li
        ci = c_re * li + c_im * lr
        cl_re_ref[0, d * SSM_GROUP:(d + 1) * SSM_GROUP, :] = cr
        cl_im_ref[0, d * SSM_GROUP:(d + 1) * SSM_GROUP, :] = ci
        if d < S5_CHUNK:
            cls_re.append(cr)
            cls_im.append(ci)
    for s in range(S5_CHUNK):
        lr = lam_re[S5_CHUNK - 1 - s:S5_CHUNK - s]
        li = lam_im[S5_CHUNK - 1 - s:S5_CHUNK - s]
        r_re_ref[0, s * SSM_GROUP:(s + 1) * SSM_GROUP, :] = lr * bb_re - li * bb_im
        r_im_ref[0, s * SSM_GROUP:(s + 1) * SSM_GROUP, :] = lr * bb_im + li * bb_re
    call_re = jnp.concatenate(cls_re, axis=0)
    call_im = jnp.concatenate(cls_im, axis=0)
    hp = lax.Precision.HIGHEST
    nt = (((1,), (1,)), ((), ()))
    kt = (lax.dot_general(bb_re, call_re, nt, precision=hp, preferred_element_type=F32)
          - lax.dot_general(bb_im, call_im, nt, precision=hp, preferred_element_type=F32))
    cw = S5_CHUNK * SSM_GROUP
    for s in range(S5_CHUNK):
        if s == 0:
            blk = kt
        else:
            blk = jnp.concatenate([jnp.zeros((SSM_GROUP, s * SSM_GROUP), F32), kt[:, :cw - s * SSM_GROUP]], axis=1)
        k_ref[0, s * SSM_GROUP:(s + 1) * SSM_GROUP, :] = blk


def _s5_params(a_re, a_im, log_dt, b_re, b_im, c_re, c_im, nsteps):
    g = SSM_GROUPS
    nrow = ((S5_NPOW + nsteps + 7) // 8) * 8
    row = lambda i: (i, 0, 0)
    shp = lambda r, c: jax.ShapeDtypeStruct((g, r, c), F32)
    ncl = S5_NPOW * SSM_GROUP
    nr = S5_CHUNK * SSM_GROUP
    return pl.pallas_call(
        _s5_param_kernel,
        grid=(g,),
        in_specs=[
            pl.BlockSpec((1, 1, SSM_STATE), row),
            pl.BlockSpec((1, 1, SSM_STATE), row),
            pl.BlockSpec((1, 1, 1), row),
            pl.BlockSpec((1, SSM_GROUP, SSM_STATE), row),
            pl.BlockSpec((1, SSM_GROUP, SSM_STATE), row),
            pl.BlockSpec((1, SSM_GROUP, SSM_STATE), row),
            pl.BlockSpec((1, SSM_GROUP, SSM_STATE), row),
        ],
        out_specs=[
            pl.BlockSpec((1, nrow, SSM_STATE), row), pl.BlockSpec((1, nrow, SSM_STATE), row),
            pl.BlockSpec((1, ncl, SSM_STATE), row), pl.BlockSpec((1, ncl, SSM_STATE), row),
            pl.BlockSpec((1, nr, SSM_STATE), row), pl.BlockSpec((1, nr, SSM_STATE), row),
            pl.BlockSpec((1, nr, nr), row),
        ],
        out_shape=[shp(nrow, SSM_STATE), shp(nrow, SSM_STATE), shp(ncl, SSM_STATE), shp(ncl, SSM_STATE),
                   shp(nr, SSM_STATE), shp(nr, SSM_STATE), shp(nr, nr)],
        compiler_params=_cparams(("parallel",)),
    )(a_re.reshape(g, 1, SSM_STATE), a_im.reshape(g, 1, SSM_STATE), log_dt.reshape(g, 1, 1),
      jnp.swapaxes(b_re, 1, 2), jnp.swapaxes(b_im, 1, 2), c_re, c_im)


def _s5_rows_to_chunks(st_ref, u_ref):
    nc = st_ref.shape[1] // S5_CHUNK
    per_blk = LANES // SSM_GROUP
    slot = lax.broadcasted_iota(jnp.int32, (nc, LANES), 1) // SSM_GROUP
    for g in range(SSM_GROUPS):
        acc = [jnp.zeros((nc, LANES), F32) for _ in range(S5_CHUNK // per_blk)]
        for s in range(S5_CHUNK):
            v = st_ref[g // per_blk, pl.ds(s, nc, stride=S5_CHUNK), :]
            shift = (SSM_GROUP * (s % per_blk) - SSM_GROUP * (g % per_blk)) % LANES
            if shift:
                v = pltpu.roll(v, shift, axis=1)
            acc[s // per_blk] = jnp.where(slot == s % per_blk, v, acc[s // per_blk])
        for k, a in enumerate(acc):
            u_ref[g, :, k * LANES:(k + 1) * LANES] = a


def _s5_chunks_to_rows(y_ref, st_ref):
    nc = y_ref.shape[1]
    per_blk = LANES // SSM_GROUP
    slot = lax.broadcasted_iota(jnp.int32, (nc, LANES), 1) // SSM_GROUP
    nblk = -(-SSM_WIDTH // LANES)
    for lb in range(nblk):
        groups = range(lb * per_blk, min((lb + 1) * per_blk, SSM_GROUPS))
        for s in range(S5_CHUNK):
            acc = jnp.zeros((nc, LANES), F32)
            for g in groups:
                v = y_ref[g, :, (s // per_blk) * LANES:(s // per_blk + 1) * LANES]
                shift = (SSM_GROUP * (g % per_blk) - SSM_GROUP * (s % per_blk)) % LANES
                if shift:
                    v = pltpu.roll(v, shift, axis=1)
                acc = jnp.where(slot == g % per_blk, v, acc)
            st_ref[lb, pl.ds(s, nc, stride=S5_CHUNK), :] = acc
    return jnp.concatenate([st_ref[lb][:, 0:min(LANES, SSM_WIDTH - lb * LANES)] for lb in range(nblk)], axis=1)


def _s5_prompt_kernel(u_ref, mt_ref, r_re_ref, r_im_ref, cl_re_ref, cl_im_ref, lam_re_ref, lam_im_ref, d_ref,
                      y_ref, xre_ref, xim_ref, *, nseq, nchunk):
    u = u_ref[0]
    ub = u.astype(BF16)
    y = _dot(ub, mt_ref[0].astype(BF16))
    xr = _dot(ub, r_re_ref[0].astype(BF16))
    xi = _dot(ub, r_im_ref[0].astype(BF16))
    rows = nseq * nchunk
    cidx = lax.broadcasted_iota(jnp.int32, (rows, SSM_STATE), 0) % nchunk
    lam_re, lam_im = lam_re_ref[0], lam_im_ref[0]
    for k in range(int(math.log2(nchunk))):
        sh = 1 << k
        lr = lam_re[S5_NPOW + k:S5_NPOW + k + 1]
        li = lam_im[S5_NPOW + k:S5_NPOW + k + 1]
        ok = cidx >= sh
        pr = jnp.where(ok, pltpu.roll(xr, sh, axis=0), 0.0)
        pi = jnp.where(ok, pltpu.roll(xi, sh, axis=0), 0.0)
        xr, xi = xr + lr * pr - li * pi, xi + lr * pi + li * pr
    ok = cidx >= 1
    er = jnp.where(ok, pltpu.roll(xr, 1, axis=0), 0.0)
    ei = jnp.where(ok, pltpu.roll(xi, 1, axis=0), 0.0)
    cl_re = cl_re_ref[0][SSM_GROUP:, :]
    cl_im = cl_im_ref[0][SSM_GROUP:, :]
    y = y + _dot_nt(er.astype(BF16), cl_re.astype(BF16)) - _dot_nt(ei.astype(BF16), cl_im.astype(BF16))
    y_ref[0] = y + d_ref[0] * u
    last = [s * nchunk + nchunk - 1 for s in range(nseq)]
    xre_ref[0] = jnp.concatenate([xr[i:i + 1] for i in last], axis=0)
    xim_ref[0] = jnp.concatenate([xi[i:i + 1] for i in last], axis=0)


def _s5_prompt(u, mt, r_re, r_im, cl_re, cl_im, lam_re, lam_im, d_t, nseq, nchunk):
    g = SSM_GROUPS
    rows = nseq * nchunk
    cw = S5_CHUNK * SSM_GROUP
    row = lambda i: (i, 0, 0)
    full = lambda a: pl.BlockSpec((1,) + a.shape[1:], row)
    return pl.pallas_call(
        functools.partial(_s5_prompt_kernel, nseq=nseq, nchunk=nchunk),
        grid=(g,),
        in_specs=[full(u), full(mt), full(r_re), full(r_im), full(cl_re), full(cl_im), full(lam_re), full(lam_im),
                  full(d_t)],
        out_specs=[pl.BlockSpec((1, rows, cw), row), pl.BlockSpec((1, nseq, SSM_STATE), row),
                   pl.BlockSpec((1, nseq, SSM_STATE), row)],
        out_shape=[jax.ShapeDtypeStruct((g, rows, cw), F32), jax.ShapeDtypeStruct((g, nseq, SSM_STATE), F32),
                   jax.ShapeDtypeStruct((g, nseq, SSM_STATE), F32)],
        compiler_params=_cparams(("parallel",)),
    )(u, mt, r_re, r_im, cl_re, cl_im, lam_re, lam_im, d_t)


def _s5_sample_kernel(u_ref, x0r_ref, x0i_ref, r_re_ref, r_im_ref, cl_re_ref, cl_im_ref, lam_re_ref, lam_im_ref,
                      d_ref, y_ref, xre_ref, xim_ref):
    u = u_ref[0]
    ub = u.astype(BF16)
    lo = (S5_CHUNK - 1) * SSM_GROUP
    bb_re = r_re_ref[0][lo:lo + SSM_GROUP, :]
    bb_im = r_im_ref[0][lo:lo + SSM_GROUP, :]
    lr, li = lam_re_ref[0][1:2], lam_im_ref[0][1:2]
    x0r, x0i = x0r_ref[0], x0i_ref[0]
    xr = _dot(ub, bb_re.astype(BF16)) + (lr * x0r - li * x0i)
    xi = _dot(ub, bb_im.astype(BF16)) + (lr * x0i + li * x0r)
    c_re = cl_re_ref[0][:SSM_GROUP, :]
    c_im = cl_im_ref[0][:SSM_GROUP, :]
    y_ref[0] = (_dot_nt(xr.astype(BF16), c_re.astype(BF16)) - _dot_nt(xi.astype(BF16), c_im.astype(BF16))
                + d_ref[0] * u)
    xre_ref[0] = xr
    xim_ref[0] = xi


def _s5_sample(u, x0r, x0i, r_re, r_im, cl_re, cl_im, lam_re, lam_im, d):
    g, n = u.shape[0], u.shape[1]
    row = lambda i: (i, 0, 0)
    full = lambda a: pl.BlockSpec((1,) + a.shape[1:], row)
    return pl.pallas_call(
        _s5_sample_kernel,
        grid=(g,),
        in_specs=[full(u), full(x0r), full(x0i), full(r_re), full(r_im), full(cl_re), full(cl_im), full(lam_re),
                  full(lam_im), full(d)],
        out_specs=[pl.BlockSpec((1, n, SSM_GROUP), row), pl.BlockSpec((1, n, SSM_STATE), row),
                   pl.BlockSpec((1, n, SSM_STATE), row)],
        out_shape=[jax.ShapeDtypeStruct((g, n, SSM_GROUP), F32), jax.ShapeDtypeStruct((g, n, SSM_STATE), F32),
                   jax.ShapeDtypeStruct((g, n, SSM_STATE), F32)],
        compiler_params=_cparams(("parallel",)),
    )(u, x0r, x0i, r_re, r_im, cl_re, cl_im, lam_re, lam_im, d)


def _log_sigmoid(x):
    return jnp.minimum(x, 0.0) - jnp.log1p(jnp.exp(-jnp.abs(x)))


def _logaddexp(a, b):
    return jnp.maximum(a, b) + jnp.log1p(jnp.exp(-jnp.abs(a - b)))


def _hgrn_gates(fpre, lb):
    log_f = _logaddexp(jnp.log(jnp.maximum(lb, LB_FLOOR)), jnp.log1p(-lb) + _log_sigmoid(fpre))
    k = (1.0 - lb) * _sigmoid(-fpre)
    return log_f, k


def _hgrn_scan_matrix():
    c = HGRN_CHUNK
    tri = np.tril(np.ones((c, c), np.float32))
    t = np.arange(c)
    mats = [tri]
    for m in HGRN_LEVELS:
        mats.append(tri[(t // (2 * m)) * (2 * m) + m - 1])
    return np.concatenate(mats, axis=0)


def _hgrn_prompt_kernel(q_ref, f_ref, i_ref, g_ref, lbraw_ref, gn_ref, scan_ref, o_ref, s_ref, st_ref, *,
                        layer, nsub):
    parts = (q_ref, f_ref, i_ref, g_ref)
    cidx = pl.program_id(1)
    c = HGRN_CHUNK
    hd = HGRN_HEAD_DIM

    @pl.when(cidx == 0)
    def _():
        st_ref[...] = jnp.zeros_like(st_ref)

    a = lbraw_ref[...]
    e = jnp.exp(a - jnp.max(a, axis=0, keepdims=True))
    p = e / jnp.sum(e, axis=0, keepdims=True)
    run = p[0:1]
    for l in range(1, layer + 1):
        run = run + p[l:l + 1]
    lb_all = run - p[0:1]

    r = lax.broadcasted_iota(jnp.int32, (c, c), 0)
    s = lax.broadcasted_iota(jnp.int32, (c, c), 1)
    pairs = []
    for m in HGRN_LEVELS:
        sh = int(math.log2(2 * m))
        pairs.append(((r >> sh) == (s >> sh)) & ((r & m) != 0) & ((s & m) == 0))
    scan = scan_ref[...]
    for h, sub in [(h, sub) for h in range(HGRN_HEADS) for sub in range(nsub)]:
        rows = slice(sub * c, (sub + 1) * c)
        col = lambda part: parts[part][0, rows, h * hd:(h + 1) * hd]
        log_f, k = _hgrn_gates(col(1), lb_all[:, h * hd:(h + 1) * hd])
        q = _silu(col(0))
        v = col(2)
        hi = log_f.astype(BF16)
        rest = log_f - hi.astype(F32)
        mid = rest.astype(BF16)
        lo = (rest - mid.astype(F32)).astype(BF16)
        cum3 = _dot(scan, jnp.concatenate([hi, mid, lo], axis=1))
        cum = cum3[:, 0:hd] + cum3[:, hd:2 * hd] + cum3[:, 2 * hd:3 * hd]
        b = cum[0:c]
        att = jnp.where(r == s, jnp.sum(q * k, axis=-1, keepdims=True), 0.0)
        for lvl, m in enumerate(HGRN_LEVELS):
            d = b - cum[(lvl + 1) * c:(lvl + 2) * c]
            qs = q * jnp.exp(jnp.minimum(d, 0.0))
            ks = k * jnp.exp(jnp.minimum(-d, 0.0))
            att = jnp.where(pairs[lvl], _dot_nt(qs.astype(BF16), ks.astype(BF16)), att)
        st = st_ref[h]
        vb = v.astype(BF16)
        o = _dot(att.astype(BF16), vb) + _dot_nt((q * jnp.exp(b)).astype(BF16), st.astype(BF16))
        bl = b[c - 1:c]
        st_new = jnp.exp(bl) * st + _dot_tn(vb, (k * jnp.exp(bl - b)).astype(BF16))
        st_ref[h] = st_new
        o = o * lax.rsqrt(jnp.mean(o * o, axis=-1, keepdims=True) + NORM_EPS)
        o_ref[0, rows, h * hd:(h + 1) * hd] = o * gn_ref[:, h * hd:(h + 1) * hd] * _silu(col(3))

    @pl.when(cidx == pl.num_programs(1) - 1)
    def _():
        s_ref[0] = st_ref[...]


def _hgrn_prompt(proj, lbraw, gn, layer):
    n, t, _ = proj.shape
    c = HGRN_CHUNK
    hd = HGRN_HEAD_DIM
    scan = jnp.asarray(_hgrn_scan_matrix(), dtype=BF16)
    nsub = 2
    c = c * nsub
    return pl.pallas_call(
        functools.partial(_hgrn_prompt_kernel, layer=layer, nsub=nsub),
        grid=(n, t // c),
        in_specs=[pl.BlockSpec((1, c, HGRN_WIDTH), lambda b, j, blk=off // HGRN_WIDTH: (b, j, blk)) for off in OFF_D]
                 + [pl.BlockSpec((DEPTH, HGRN_WIDTH), lambda b, j: (0, 0)),
                  pl.BlockSpec((None, 1, HGRN_WIDTH), lambda b, j: (layer, 0, 0)),
                  pl.BlockSpec(scan.shape, lambda b, j: (0, 0))],
        out_specs=[pl.BlockSpec((1, c, HGRN_WIDTH), lambda b, j: (b, j, 0)),
                   pl.BlockSpec((1, HGRN_HEADS, hd, hd), lambda b, j: (b, 0, 0, 0))],
        out_shape=[jax.ShapeDtypeStruct((n, t, HGRN_WIDTH), F32),
                   jax.ShapeDtypeStruct((n, HGRN_HEADS, hd, hd), F32)],
        scratch_shapes=[pltpu.VMEM((HGRN_HEADS, hd, hd), F32)],
        compiler_params=_cparams(("parallel", "arbitrary")),
    )(proj, proj, proj, proj, lbraw, _v3(gn), scan)


def _hgrn_sample_kernel(qt_ref, ft_ref, lbt_ref, v_ref, g_ref, gn_ref, s0_ref, o_ref, s_ref, *, layer, n):
    hd = HGRN_HEAD_DIM
    a = lbt_ref[0]
    e = jnp.exp(a - jnp.max(a, axis=1, keepdims=True))
    p = e / jnp.sum(e, axis=1, keepdims=True)
    run = p[:, 0:1]
    for l in range(1, layer + 1):
        run = run + p[:, l:l + 1]
    lb = run - p[:, 0:1]
    log_f, k = _hgrn_gates(ft_ref[0], lb)
    f = jnp.exp(log_f)
    q = _silu(qt_ref[0])
    v = v_ref[...]
    gate = g_ref[...]
    rows = []
    for b in range(n):
        s_new = f[:, b:b + 1] * s0_ref[b, 0] + k[:, b:b + 1] * v[b:b + 1, :]
        s_ref[b, 0] = s_new
        rows.append(jnp.sum(q[:, b:b + 1] * s_new, axis=0, keepdims=True))
    o = jnp.concatenate(rows, axis=0)
    o = o * lax.rsqrt(jnp.mean(o * o, axis=-1, keepdims=True) + NORM_EPS)
    o_ref[...] = o * gn_ref[...] * _silu(gate)


def _hgrn_sample(qt, ft, lbt, proj, gn, s0, layer):
    n = proj.shape[0]
    hd = HGRN_HEAD_DIM
    v_blk = OFF_D[2] // hd
    g_blk = OFF_D[3] // hd
    return pl.pallas_call(
        functools.partial(_hgrn_sample_kernel, layer=layer, n=n),
        grid=(HGRN_HEADS,),
        in_specs=[pl.BlockSpec((1, hd, n), lambda h: (h, 0, 0)),
                  pl.BlockSpec((1, hd, n), lambda h: (h, 0, 0)),
                  pl.BlockSpec((1, hd, DEPTH), lambda h: (h, 0, 0)),
                  pl.BlockSpec((n, hd), lambda h: (0, v_blk + h)),
                  pl.BlockSpec((n, hd), lambda h: (0, g_blk + h)),
                  pl.BlockSpec((None, 1, hd), lambda h: (layer, 0, h)),
                  pl.BlockSpec((None, n, 1, hd, hd), lambda h: (layer, 0, h, 0, 0))],
        out_specs=[pl.BlockSpec((n, hd), lambda h: (0, h)),
                   pl.BlockSpec((n, 1, hd, hd), lambda h: (0, h, 0, 0))],
        out_shape=[jax.ShapeDtypeStruct((n, HGRN_WIDTH), F32),
                   jax.ShapeDtypeStruct((n, HGRN_HEADS, hd, hd), F32)],
        compiler_params=_cparams(("parallel",)),
    )(qt, ft, lbt, proj, proj, _v3(gn), s0)


def _gelu_tanh(x):
    return 0.5 * x * (1.0 + jnp.tanh(math.sqrt(2.0 / math.pi) * (x + 0.044715 * (x * x * x))))


def _mixout_kernel(h_ref, ya_ref, ob_ref, oc0_ref, oc1_ref, oc2_ref, od_ref,
                   wglu_ref, bglu_ref, ga_ref, gb_ref, gc_ref, wo_ref, gpost_ref, o_ref, *scratch):
    ya = _s5_chunks_to_rows(ya_ref, scratch[0]) if scratch else ya_ref[...]
    z = _gelu_tanh(ya)
    gate = _sigmoid(_dot(z.astype(BF16), wglu_ref[...].astype(BF16)) + bglu_ref[...])
    out_a = _rms(z * gate, ga_ref[...])
    out_b = _rms(ob_ref[...], gb_ref[...])
    ocs = [oc0_ref, oc1_ref, oc2_ref]
    ls = [r[:, DIL_GW:2 * DIL_GW] for r in ocs]
    m = jnp.maximum(jnp.maximum(ls[0], ls[1]), ls[2])
    es = [jnp.exp(l - m) for l in ls]
    den = es[0] + es[1] + es[2]
    cs = [ocs[gi][:, 0:DIL_GW] * (es[gi] / den) for gi in range(3)]
    ss = sum(jnp.sum(c * c, axis=-1, keepdims=True) for c in cs)
    inv = lax.rsqrt(ss / DIL_WIDTH + NORM_EPS)
    gc = gc_ref[...]
    o1, o2, o3 = SSM_WIDTH, SSM_WIDTH + SWA_WIDTH, SSM_WIDTH + SWA_WIDTH + DIL_WIDTH
    wo = lambda lo, hi: wo_ref[lo:hi, :]
    mix = _dot(out_a.astype(BF16), wo(0, o1)) + _dot(out_b.astype(BF16), wo(o1, o2))
    for gi in range(3):
        c = cs[gi] * inv * gc[:, gi * DIL_GW:(gi + 1) * DIL_GW]
        mix = mix + _dot(c.astype(BF16), wo(o2 + gi * DIL_GW, o2 + (gi + 1) * DIL_GW))
    mix = mix + _dot(od_ref[...].astype(BF16), wo(o3, D_MODEL))
    o_ref[...] = h_ref[...] + _rms(mix, gpost_ref[...])


def _mixout(h, ya, ob, ocs, od, lp_all, w_out_l, layer, tm, row0=0):
    m = ob.shape[0]
    blk0 = row0 // tm
    rowblk = lambda a: pl.BlockSpec((tm, a.shape[1]), lambda i: (i, 0))
    vec = lambda a: pl.BlockSpec((None, 1, a.shape[1]), lambda i: (layer, 0, 0))
    mat = lambda a: pl.BlockSpec((None,) + a.shape[1:], lambda i: (layer, 0, 0))
    whole = lambda a: pl.BlockSpec(a.shape, lambda i: (0, 0), pipeline_mode=pl.Buffered(1))
    acts = [h, ya, ob, *ocs, od]
    names = ['ssm_w_glu', 'ssm_b_glu', 'out_norm_a', 'out_norm_b', 'out_norm_c', 'w_out', 'mix_norm_post']
    params = [w_out_l if k == 'w_out' else lp_all[k] for k in names]
    spec = lambda k, p: whole(p) if k == 'w_out' else (mat(p) if p.ndim == 3 else vec(p))
    act_specs = [rowblk(a) for a in acts]
    act_specs[0] = pl.BlockSpec((tm, D_MODEL), lambda i: (i + blk0, 0))
    scratch = []
    chunked = ya.ndim == 3
    if chunked:
        act_specs[1] = pl.BlockSpec((SSM_GROUPS, tm // S5_CHUNK, ya.shape[2]), lambda i: (0, i, 0))
        scratch = [pltpu.VMEM((-(-SSM_WIDTH // LANES), tm, LANES), F32)]
    args = [*acts, *[_v3(p) if p.ndim == 2 and k != 'w_out' else p for k, p in zip(names, params)]]
    in_specs = act_specs + [spec(k, p) for k, p in zip(names, params)]
    return pl.pallas_call(
        _mixout_kernel,
        grid=(m // tm,),
        in_specs=in_specs,
        out_specs=pl.BlockSpec((tm, D_MODEL), lambda i: (i, 0)),
        out_shape=jax.ShapeDtypeStruct((m, D_MODEL), F32),
        scratch_shapes=scratch,
        compiler_params=_cparams(("parallel",)),
    )(*args)


def _ffn_block(x, x_tail, P, name, layer, tm, n_main, n_tail, tail_out):
    return _ffn(x, x_tail, P[name + '_norm_pre'], P[name + '_w_gate'], P[name + '_w_up'], P[name + '_w_down'],
                P[name + '_norm_post'], layer, tm, n_main, n_tail, tail_out)


def _prompt_mixer(h, P, w_in_l, w_out_l, s5p, layer, n, t, rope, tm):
    m = n * t
    proj2, u = _win(h, P['mix_norm_pre'], w_in_l, rope[0], rope[1], layer, tm, True)
    proj = proj2.reshape(n, t, PROJ_COLS)

    lam_re, lam_im, cl_re, cl_im, r_re, r_im, kmat = s5p
    nchunk = t // S5_CHUNK
    d_t = jnp.tile(P['ssm_d'][layer], (1, S5_CHUNK)).reshape(SSM_GROUPS, 1, S5_CHUNK * SSM_GROUP)
    y, xre, xim = _s5_prompt(u, kmat, r_re, r_im, cl_re, cl_im, lam_re, lam_im, d_t, n, nchunk)
    ssm_new = jnp.stack([xre, xim], axis=-1).transpose(1, 0, 2, 3)

    sink = jnp.repeat(P['swa_sinks'][layer], HEAD_DIM).reshape(1, SWA_WIDTH)
    ob = _swa_prompt(proj, sink).reshape(m, SWA_WIDTH)
    keep = min(WINDOW, t)
    kv = proj[:, t - keep:, OFF_B + SWA_WIDTH:OFF_B + B_SLOT]
    swa_new = kv.reshape(n, keep, 2, SWA_KV_HEADS, HEAD_DIM)

    ocs, dil_new = [], []
    for gi, (win, dil) in enumerate(DIL_PAIRS):
        ocs.append(_dil_prompt(proj, gi, dil).reshape(m, 2 * DIL_GW))
        keep = min(win, t)
        lo = OFF_C + gi * C_SLOT + C_K
        dil_new.append(proj[:, t - keep:, lo:lo + 2 * HEAD_DIM].reshape(n, keep, 2, HEAD_DIM))

    od, st = _hgrn_prompt(proj, P['hgrn_lower_bounds'], P['out_norm_d'], layer)
    hgrn_new = jnp.swapaxes(st, -1, -2)

    h = _mixout(h, y, ob, ocs, od.reshape(m, HGRN_WIDTH), P, w_out_l, layer, min(tm, 512))
    return h, (ssm_new, swa_new, dil_new[0], dil_new[1], dil_new[2], hgrn_new)


def _sample_mixer(h, row0, P, w_in_l, w_out_l, s5p, layer, caches, rope):
    n = rope[0].shape[0]
    state_ssm, cache_swa, cache_d0, cache_d1, cache_d2, state_hgrn = caches
    (proj,) = _win(h, P['mix_norm_pre'], w_in_l, rope[0], rope[1], layer, n, False, row0)

    lam_re, lam_im, cl_re, cl_im, r_re, r_im, _ = s5p
    u = proj[:, OFF_U:OFF_U + SSM_WIDTH].reshape(n, SSM_GROUPS, SSM_GROUP).transpose(1, 0, 2)
    x0 = state_ssm[layer].transpose(1, 0, 2, 3)
    y, xre, xim = _s5_sample(u, x0[..., 0], x0[..., 1], r_re, r_im, cl_re, cl_im, lam_re, lam_im,
                             P['ssm_d'][layer].reshape(SSM_GROUPS, 1, SSM_GROUP))
    ya = y.transpose(1, 0, 2).reshape(n, SSM_WIDTH)
    ssm_new = jnp.stack([xre, xim], axis=-1).transpose(1, 0, 2, 3)

    sink = jnp.repeat(P['swa_sinks'][layer], HEAD_DIM).reshape(1, SWA_WIDTH)
    ob = _swa_sample(proj, cache_swa.transpose(0, 1, 3, 4, 5, 2), sink, layer)
    swa_new = proj[:, OFF_B + SWA_WIDTH:OFF_B + B_SLOT].reshape(n, 1, 2, SWA_KV_HEADS, HEAD_DIM)

    ocs, dil_new = [], []
    for gi, (buf, (win, dil)) in enumerate(zip((cache_d0, cache_d1, cache_d2), DIL_PAIRS)):
        ocs.append(_dil_sample(proj, buf.transpose(0, 1, 3, 4, 2), gi, dil, layer))
        lo = OFF_C + gi * C_SLOT + C_K
        dil_new.append(proj[:, lo:lo + 2 * HEAD_DIM].reshape(n, 1, 2, HEAD_DIM))

    part = lambda i: proj[:, OFF_D[i]:OFF_D[i] + HGRN_WIDTH].reshape(n, HGRN_HEADS, HGRN_HEAD_DIM)
    cm = lambda a: a.transpose(1, 2, 0)
    lbt = P['hgrn_lower_bounds'].reshape(DEPTH, HGRN_HEADS, HGRN_HEAD_DIM).transpose(1, 2, 0)
    od, hgrn_new = _hgrn_sample(cm(part(0)), cm(part(1)), lbt, proj, P['out_norm_d'], state_hgrn, layer)

    h = _mixout(h, ya, ob, ocs, od, P, w_out_l, layer, n, row0)
    return h, (ssm_new, swa_new, dil_new[0], dil_new[1], dil_new[2], hgrn_new)


def _forward(x_prompt, x_sample, caches, P):
    n, t, _ = x_prompt.shape
    ns, ts, _ = x_sample.shape
    assert ts == 1
    tm = 1024
    m_p = n * t
    rope_p = _rope_tables(jnp.tile(jnp.arange(t, dtype=jnp.int32), n))
    rope_s = _rope_tables(jnp.full((ns,), PAST_LEN, dtype=jnp.int32))
    nsteps = int(math.log2(t // S5_CHUNK))
    h = x_prompt.reshape(m_p, D_MODEL)
    h_tail = x_sample.reshape(ns, D_MODEL)
    new_p = [[] for _ in range(6)]
    new_s = [[] for _ in range(6)]
    for l in range(DEPTH):
        w_in_l = _win_prep(P['w_in'], l)
        w_out_l = _cast_bf16(P['w_out'], l)
        s5p = _s5_params(P['ssm_a_re'][l], P['ssm_a_im'][l], P['ssm_log_dt'][l], P['ssm_b_re'][l], P['ssm_b_im'][l],
                         P['ssm_c_re'][l], P['ssm_c_im'][l], nsteps)
        h = _ffn_block(h, h_tail, P, 'ffn1', l, FFN_ROWS, m_p, ns, False)
        mixed_p, st_p = _prompt_mixer(h, P, w_in_l, w_out_l, s5p, l, n, t, rope_p, tm)
        mixed_s, st_s = _sample_mixer(h, m_p, P, w_in_l, w_out_l, s5p, l, caches, rope_s)
        h = _ffn_block(mixed_p, mixed_s, P, 'ffn2', l, FFN_ROWS, m_p, ns, l == DEPTH - 1)
        h_tail = None
        for i in range(6):
            new_p[i].append(st_p[i])
            new_s[i].append(st_s[i])
    y_p, y_s = h
    new_p = [jnp.stack(a, axis=0) for a in new_p]
    new_s = [jnp.stack(a, axis=0) for a in new_s]
    return (y_p.reshape(n, t, D_MODEL), y_s.reshape(ns, ts, D_MODEL), new_p[0], new_s[0], new_p[1], new_s[1],
            new_p[2], new_s[2], new_p[3], new_s[3], new_p[4], new_s[4], new_p[5], new_s[5])


def kernel(x_prompt, x_sample, state_ssm, cache_swa_kv, cache_dil0_kv, cache_dil1_kv, cache_dil2_kv, state_hgrn, ffn1_norm_pre, ffn1_w_gate, ffn1_w_up, ffn1_w_down, ffn1_norm_post, mix_norm_pre, w_in, ssm_a_re, ssm_a_im, ssm_log_dt, ssm_b_re, ssm_b_im, ssm_c_re, ssm_c_im, ssm_d, ssm_w_glu, ssm_b_glu, swa_sinks, hgrn_lower_bounds, out_norm_a, out_norm_b, out_norm_c, out_norm_d, w_out, mix_norm_post, ffn2_norm_pre, ffn2_w_gate, ffn2_w_up, ffn2_w_down, ffn2_norm_post):
    P = dict(ffn1_norm_pre=ffn1_norm_pre, ffn1_w_gate=ffn1_w_gate, ffn1_w_up=ffn1_w_up, ffn1_w_down=ffn1_w_down,
             ffn1_norm_post=ffn1_norm_post, mix_norm_pre=mix_norm_pre, w_in=w_in, ssm_a_re=ssm_a_re,
             ssm_a_im=ssm_a_im, ssm_log_dt=ssm_log_dt, ssm_b_re=ssm_b_re, ssm_b_im=ssm_b_im, ssm_c_re=ssm_c_re,
             ssm_c_im=ssm_c_im, ssm_d=ssm_d, ssm_w_glu=ssm_w_glu, ssm_b_glu=ssm_b_glu, swa_sinks=swa_sinks,
             hgrn_lower_bounds=hgrn_lower_bounds, out_norm_a=out_norm_a, out_norm_b=out_norm_b,
             out_norm_c=out_norm_c, out_norm_d=out_norm_d, w_out=w_out, mix_norm_post=mix_norm_post,
             ffn2_norm_pre=ffn2_norm_pre, ffn2_w_gate=ffn2_w_gate, ffn2_w_up=ffn2_w_up, ffn2_w_down=ffn2_w_down,
             ffn2_norm_post=ffn2_norm_post)
    caches = (state_ssm, cache_swa_kv, cache_dil0_kv, cache_dil1_kv, cache_dil2_kv, state_hgrn)
    return _forward(x_prompt, x_sample, caches, P)
```

```python
import functools
import math

import numpy as np
import jax
import jax.numpy as jnp
from jax import lax
from jax.experimental import pallas as pl
from jax.experimental.pallas import tpu as pltpu

F32 = jnp.float32
BF16 = jnp.bfloat16

D_MODEL = 2048
DEPTH = 2
PAST_LEN = 16384
HEAD_DIM = 64
ROPE_THETA = 10000.0
D_FF = 5504
NORM_EPS = 1e-6
NEG_BIG = -1e30
LB_FLOOR = 1e-30
SSM_GROUP = 16
SSM_STATE = 64
SSM_WIDTH = 448
SSM_GROUPS = 28
SWA_HEADS = 8
SWA_KV_HEADS = 2
SWA_GROUP = 4
SWA_WIDTH = 512
SWA_KV_WIDTH = 128
WINDOW = 128
DIL_PAIRS = ((128, 1), (512, 4), (2048, 16))
DIL_HPG = 3
DIL_GW = DIL_HPG * HEAD_DIM
DIL_WIDTH = 576
DIL_KV_WIDTH = 192
HGRN_HEAD_DIM = 128
HGRN_WIDTH = 512
HGRN_HEADS = 4
IN_COLS = 4224
ATTN_SCALE = HEAD_DIM ** -0.5

LANES = 128
FF_TILE = 256
N_FF_TILES = -(-D_FF // FF_TILE)
DOWN_TILE = 256
N_DOWN_TILES = D_MODEL // DOWN_TILE
FFN_ROWS = 1040
VMEM_LIMIT = 60 * 1024 * 1024

B_SLOT = SWA_WIDTH + 2 * SWA_KV_WIDTH
C_SLOT = 384
C_K = 256
OFF_B = 0
OFF_C = OFF_B + B_SLOT
OFF_D = (2048, 2560, 3584, 4096)
OFF_U = 3072
U_SLOT = 512
PROJ_COLS = 4608
WIN_TILE = 1536

S5_CHUNK = 16
S5_NPOW = S5_CHUNK + 1
HGRN_CHUNK = 128
HGRN_LEVELS = (64, 32, 16, 8, 4, 2, 1)


def _proj_segments():
    src = np.cumsum([0, SSM_WIDTH, SWA_WIDTH, SWA_KV_WIDTH, SWA_KV_WIDTH, DIL_WIDTH, DIL_KV_WIDTH, DIL_KV_WIDTH,
                     HGRN_WIDTH, HGRN_WIDTH, HGRN_WIDTH, HGRN_WIDTH])
    s_u, s_qb, s_kb, s_vb, s_qc, s_kc, s_vc, s_qd = src[:8]
    segs = [(OFF_B, s_qb, SWA_WIDTH), (OFF_B + SWA_WIDTH, s_kb, SWA_KV_WIDTH),
            (OFF_B + SWA_WIDTH + SWA_KV_WIDTH, s_vb, SWA_KV_WIDTH)]
    for g in range(3):
        base = OFF_C + g * C_SLOT
        segs += [(base, s_qc + g * DIL_GW, DIL_GW), (base + C_K, s_kc + g * HEAD_DIM, HEAD_DIM),
                 (base + C_K + HEAD_DIM, s_vc + g * HEAD_DIM, HEAD_DIM)]
    segs += [(off, s_qd + i * HGRN_WIDTH, HGRN_WIDTH) for i, off in enumerate(OFF_D)]
    segs += [(OFF_U, s_u, SSM_WIDTH)]
    return [(int(a), int(b), int(c)) for a, b, c in segs]


def _rotary_lane_mask():
    m = np.zeros((PROJ_COLS,), np.float32)
    m[OFF_B:OFF_B + SWA_WIDTH + SWA_KV_WIDTH] = 1.0
    for g in range(3):
        base = OFF_C + g * C_SLOT
        m[base:base + DIL_GW] = 1.0
        m[base + C_K:base + C_K + HEAD_DIM] = 1.0
    return m


def _cparams(sem):
    return pltpu.CompilerParams(dimension_semantics=sem, vmem_limit_bytes=VMEM_LIMIT)


def _v3(a):
    return a.reshape(a.shape[0], 1, a.shape[1])


def _rms(x, g):
    return x * lax.rsqrt(jnp.mean(x * x, axis=-1, keepdims=True) + NORM_EPS) * g


def _sigmoid(x):
    return 1.0 / (1.0 + jnp.exp(-x))


def _silu(x):
    return x * _sigmoid(x)


def _dot(a, b):
    return jnp.dot(a, b, preferred_element_type=F32)


def _dot_nt(a, b):
    return lax.dot_general(a, b, (((1,), (1,)), ((), ())), preferred_element_type=F32)


def _dot_tn(a, b):
    return lax.dot_general(a, b, (((0,), (0,)), ((), ())), preferred_element_type=F32)


def _ffn_kernel(*refs, tail, n_tail, tail_in, tail_out):
    refs = list(refs)
    x_ref = refs.pop(0)
    xs_ref = refs.pop(0) if tail_in else None
    gpre_ref, wg_ref, wu_ref, wd_ref, gpost_ref, o_ref = refs[:6]
    os_ref = refs[6] if tail_out else None
    xn_ref, hid_ref = refs[-2:]
    s = pl.program_id(1)
    in_last_block = pl.program_id(0) == pl.num_programs(0) - 1
    tail_rows = slice(tail, tail + n_tail)

    @pl.when(s == 0)
    def _():
        xn_ref[...] = _rms(x_ref[...], gpre_ref[...]).astype(BF16)
        if tail_in:

            @pl.when(in_last_block)
            def _():
                xn_ref[tail_rows, :] = _rms(xs_ref[...], gpre_ref[...]).astype(BF16)

    @pl.when(s < N_FF_TILES)
    def _():
        xn = xn_ref[...]
        gate = _dot(xn, wg_ref[...].astype(BF16))
        up = _dot(xn, wu_ref[...].astype(BF16))
        hid_ref[s] = (_silu(gate) * up).astype(BF16)

    for j in range(N_DOWN_TILES):

        @pl.when(s == N_FF_TILES + j)
        def _(j=j):
            hid = jnp.concatenate([hid_ref[f] for f in range(N_FF_TILES)], axis=1)[:, :D_FF]
            o_ref[:, j * DOWN_TILE:(j + 1) * DOWN_TILE] = _dot(hid, wd_ref[...].astype(BF16))

    @pl.when(s == pl.num_programs(1) - 1)
    def _():
        delta = 0.5 * _rms(o_ref[...], gpost_ref[...])
        res = x_ref[...] + delta
        o_ref[...] = res
        if tail_in or tail_out:

            @pl.when(in_last_block)
            def _():
                tail_res = xs_ref[...] + delta[tail_rows] if tail_in else res[tail_rows]
                if tail_out:
                    os_ref[...] = tail_res
                else:
                    o_ref[tail_rows, :] = tail_res


def _ffn(x, x_tail, gpre, wg, wu, wd, gpost, layer, tm, n_main, n_tail, tail_out):
    m = n_main + n_tail
    nblk = pl.cdiv(m, tm)
    tail = n_main - (nblk - 1) * tm
    assert 0 <= tail and tail + n_tail <= tm and tail % 8 == 0
    last_ff = N_FF_TILES - 1
    in_specs = [pl.BlockSpec((tm, D_MODEL), lambda i, s: (i, 0))]
    args = [x]
    if x_tail is not None:
        in_specs.append(pl.BlockSpec((n_tail, D_MODEL), lambda i, s: (0, 0)))
        args.append(x_tail)
    in_specs += [
        pl.BlockSpec((None, 1, D_MODEL), lambda i, s: (layer, 0, 0)),
        pl.BlockSpec((None, D_MODEL, FF_TILE), lambda i, s: (layer, 0, jnp.minimum(s, last_ff))),
        pl.BlockSpec((None, D_MODEL, FF_TILE), lambda i, s: (layer, 0, jnp.minimum(s, last_ff))),
        pl.BlockSpec((None, D_FF, DOWN_TILE), lambda i, s: (layer, 0, jnp.maximum(s - N_FF_TILES, 0))),
        pl.BlockSpec((None, 1, D_MODEL), lambda i, s: (layer, 0, 0)),
    ]
    args += [_v3(gpre), wg, wu, wd, _v3(gpost)]
    out_main = pl.BlockSpec((tm, D_MODEL), lambda i, s: (i, 0), pipeline_mode=pl.Buffered(1))
    if tail_out:
        out_specs = [out_main, pl.BlockSpec((n_tail, D_MODEL), lambda i, s: (0, 0))]
        out_shape = [jax.ShapeDtypeStruct((n_main, D_MODEL), F32), jax.ShapeDtypeStruct((n_tail, D_MODEL), F32)]
    else:
        out_specs = out_main
        out_shape = jax.ShapeDtypeStruct((m, D_MODEL), F32)
    return pl.pallas_call(
        functools.partial(_ffn_kernel, tail=tail, n_tail=n_tail, tail_in=x_tail is not None, tail_out=tail_out),
        grid=(nblk, N_FF_TILES + N_DOWN_TILES),
        in_specs=in_specs,
        out_specs=out_specs,
        out_shape=out_shape,
        scratch_shapes=[pltpu.VMEM((tm, D_MODEL), BF16), pltpu.VMEM((N_FF_TILES, tm, FF_TILE), BF16)],
        compiler_params=_cparams(("arbitrary", "arbitrary")),
    )(*args)


def _win_prep_kernel(w_ref, o_ref):
    o_ref[...] = jnp.zeros_like(o_ref)
    for dst, src, width in _proj_segments():
        o_ref[:, dst:dst + width] = w_ref[:, src:src + width].astype(BF16)


def _win_prep(w_in, layer):
    rows = 256
    return pl.pallas_call(
        _win_prep_kernel,
        grid=(D_MODEL // rows,),
        in_specs=[pl.BlockSpec((None, rows, IN_COLS), lambda i: (layer, i, 0))],
        out_specs=pl.BlockSpec((rows, PROJ_COLS), lambda i: (i, 0)),
        out_shape=jax.ShapeDtypeStruct((D_MODEL, PROJ_COLS), BF16),
        compiler_params=_cparams(("parallel",)),
    )(w_in)


def _cast_kernel(w_ref, o_ref):
    o_ref[...] = w_ref[...].astype(BF16)


def _cast_bf16(w, layer):
    _, r, c = w.shape
    rows = 256
    return pl.pallas_call(
        _cast_kernel,
        grid=(r // rows,),
        in_specs=[pl.BlockSpec((None, rows, c), lambda i: (layer, i, 0))],
        out_specs=pl.BlockSpec((rows, c), lambda i: (i, 0)),
        out_shape=jax.ShapeDtypeStruct((r, c), BF16),
        compiler_params=_cparams(("parallel",)),
    )(w)


def _win_kernel(x_ref, g_ref, w_ref, cos_ref, sin_ref, o_ref, *rest, chunks):
    if chunks:
        u_ref, xn_ref, st_ref = rest
    else:
        (xn_ref,) = rest
    j = pl.program_id(1)

    @pl.when(j == 0)
    def _():
        xn_ref[...] = _rms(x_ref[...], g_ref[...]).astype(BF16)

    rot = _rotary_lane_mask()
    for tile in range(PROJ_COLS // WIN_TILE):

        @pl.when(j == tile)
        def _(tile=tile):
            y = _dot(xn_ref[...], w_ref[...])
            cos = cos_ref[...]
            sin = sin_ref[...]
            lane = lax.broadcasted_iota(jnp.int32, cos.shape, 1)
            first_half = (lane % HEAD_DIM) < (HEAD_DIM // 2)
            for c in range(WIN_TILE // LANES):
                sl = slice(c * LANES, (c + 1) * LANES)
                flags = rot[tile * WIN_TILE + c * LANES:tile * WIN_TILE + (c + 1) * LANES]
                x = y[:, sl]
                lo, hi = bool(flags[:HEAD_DIM].all()), bool(flags[HEAD_DIM:].all())
                if not (lo or hi):
                    o_ref[:, sl] = x
                    continue
                partner = jnp.where(first_half,
                                    pltpu.roll(x, LANES - HEAD_DIM // 2, axis=1),
                                    pltpu.roll(x, HEAD_DIM // 2, axis=1))
                rotated = x * cos + partner * sin
                if lo and hi:
                    o_ref[:, sl] = rotated
                else:
                    is_rot = (lane < HEAD_DIM) if lo else (lane >= HEAD_DIM)
                    o_ref[:, sl] = jnp.where(is_rot, rotated, x)
            if chunks and tile == OFF_U // WIN_TILE:
                base = OFF_U % WIN_TILE
                for lb in range(U_SLOT // LANES):
                    st_ref[lb] = y[:, base + lb * LANES:base + (lb + 1) * LANES]
                _s5_rows_to_chunks(st_ref, u_ref)


def _win(h, g, w, cos_t, sin_t, layer, tm, chunks, row0=0):
    m = cos_t.shape[0]
    blk0 = row0 // tm
    cw = S5_CHUNK * SSM_GROUP
    out_specs = [pl.BlockSpec((tm, WIN_TILE), lambda i, j: (i, j))]
    out_shape = [jax.ShapeDtypeStruct((m, PROJ_COLS), F32)]
    scratch = [pltpu.VMEM((tm, D_MODEL), BF16)]
    if chunks:
        out_specs.append(pl.BlockSpec((SSM_GROUPS, tm // S5_CHUNK, cw), lambda i, j: (0, i, 0)))
        out_shape.append(jax.ShapeDtypeStruct((SSM_GROUPS, m // S5_CHUNK, cw), F32))
        scratch.append(pltpu.VMEM((U_SLOT // LANES, tm, LANES), F32))
    return pl.pallas_call(
        functools.partial(_win_kernel, chunks=chunks),
        grid=(m // tm, PROJ_COLS // WIN_TILE),
        in_specs=[
            pl.BlockSpec((tm, D_MODEL), lambda i, j: (i + blk0, 0)),
            pl.BlockSpec((None, 1, D_MODEL), lambda i, j: (layer, 0, 0)),
            pl.BlockSpec((D_MODEL, WIN_TILE), lambda i, j: (0, j)),
            pl.BlockSpec((tm, LANES), lambda i, j: (i, 0)),
            pl.BlockSpec((tm, LANES), lambda i, j: (i, 0)),
        ],
        out_specs=out_specs,
        out_shape=out_shape,
        scratch_shapes=scratch,
        compiler_params=_cparams(("parallel", "arbitrary")),
    )(h, _v3(g), w, cos_t, sin_t)


def _rope_tables(pos):
    half = HEAD_DIM // 2
    inv_freq = ROPE_THETA ** (-jnp.arange(half, dtype=F32) / half)
    ang = pos.astype(F32)[:, None] * inv_freq[None, :]
    cos, sin = jnp.cos(ang), jnp.sin(ang)
    reps = LANES // HEAD_DIM
    cos_t = jnp.tile(jnp.concatenate([cos, cos], axis=1), (1, reps))
    sin_t = jnp.tile(jnp.concatenate([-sin, sin], axis=1), (1, reps))
    return cos_t, sin_t


def _band_heads(q, kk, vv, valid, sinks):
    w = WINDOW
    heads = q.shape[1] // HEAD_DIM
    qs = jnp.concatenate([q[:, g * HEAD_DIM:(g + 1) * HEAD_DIM] for g in range(heads)], axis=0).astype(BF16)
    s = _dot_nt(kk, qs) * ATTN_SCALE
    s = jnp.where(jnp.concatenate([valid] * heads, axis=1), s, NEG_BIG)
    m = jnp.max(s, axis=0, keepdims=True)
    if sinks is not None:
        sink_row = jnp.concatenate([jnp.broadcast_to(sinks[g], (1, w)) for g in range(heads)], axis=1)
        m = jnp.maximum(m, sink_row)
    p = jnp.exp(s - m)
    l = jnp.sum(p, axis=0, keepdims=True)
    if sinks is not None:
        l = l + jnp.exp(sink_row - m)
    ot = _dot_tn(vv, p.astype(BF16)) / l
    lse = m + jnp.log(l)
    return ([ot[:, g * w:(g + 1) * w] for g in range(heads)], [lse[:, g * w:(g + 1) * w] for g in range(heads)])


def _untranspose(blocks):
    pairs = [jnp.concatenate(blocks[i:i + 2], axis=0).T for i in range(0, len(blocks), 2)]
    return jnp.concatenate(pairs, axis=1)


def _band_mask(no_prev):
    w = WINDOW
    k = lax.broadcasted_iota(jnp.int32, (2 * w, w), 0)
    q = lax.broadcasted_iota(jnp.int32, (2 * w, w), 1)
    valid = (k >= q) & (k <= q + w)
    if no_prev is False:
        return valid
    return valid & (k >= jnp.where(no_prev, w, 0))


def _swa_prompt_kernel(cur_ref, prev_ref, sink_ref, o_ref, *, nsub):
    w = WINDOW
    gw = SWA_GROUP * HEAD_DIM
    first = pl.program_id(1) == 0
    for j in range(nsub):
        valid = _band_mask(first if j == 0 else False)
        cur = cur_ref[0, j * w:(j + 1) * w, :]
        prev = prev_ref[0] if j == 0 else cur_ref[0, (j - 1) * w:j * w, SWA_WIDTH:B_SLOT]
        outs = []
        for h in range(SWA_KV_HEADS):
            ks = slice(SWA_WIDTH + h * HEAD_DIM, SWA_WIDTH + (h + 1) * HEAD_DIM)
            vs = slice(SWA_WIDTH + SWA_KV_WIDTH + h * HEAD_DIM, SWA_WIDTH + SWA_KV_WIDTH + (h + 1) * HEAD_DIM)
            kk = jnp.concatenate([prev[:, h * HEAD_DIM:(h + 1) * HEAD_DIM], cur[:, ks]], axis=0).astype(BF16)
            vv = jnp.concatenate([prev[:, SWA_KV_WIDTH + h * HEAD_DIM:SWA_KV_WIDTH + (h + 1) * HEAD_DIM],
                                  cur[:, vs]], axis=0).astype(BF16)
            sinks = [sink_ref[:, h * gw + g * HEAD_DIM:h * gw + g * HEAD_DIM + 1] for g in range(SWA_GROUP)]
            o, _ = _band_heads(cur[:, h * gw:(h + 1) * gw], kk, vv, valid, sinks)
            outs += o
        o_ref[0, j * w:(j + 1) * w, :] = _untranspose(outs)


def _swa_prompt(proj, sink):
    n, t, _ = proj.shape
    nsub = 4
    rows = WINDOW * nsub
    return pl.pallas_call(
        functools.partial(_swa_prompt_kernel, nsub=nsub),
        grid=(n, t // rows),
        in_specs=[
            pl.BlockSpec((1, rows, B_SLOT), lambda b, i: (b, i, OFF_B // B_SLOT)),
            pl.BlockSpec((1, WINDOW, 2 * SWA_KV_WIDTH),
                         lambda b, i: (b, jnp.maximum(i * nsub - 1, 0), (OFF_B + SWA_WIDTH) // (2 * SWA_KV_WIDTH))),
            pl.BlockSpec((1, SWA_WIDTH), lambda b, i: (0, 0)),
        ],
        out_specs=pl.BlockSpec((1, rows, SWA_WIDTH), lambda b, i: (b, i, 0)),
        out_shape=jax.ShapeDtypeStruct((n, t, SWA_WIDTH), F32),
        compiler_params=_cparams(("parallel", "arbitrary")),
    )(proj, proj, sink)


def _dil_prompt_kernel(cur_ref, prev_ref, o_ref, cur3_ref, out3_ref, *, dil, nsub):
    nl = C_SLOT // LANES
    span = WINDOW * dil
    first = pl.program_id(1) == 0
    for c in range(nl):
        cur3_ref[c] = cur_ref[0, :, c * LANES:(c + 1) * LANES]

    def one(j, r):
        def rows(base):
            return pl.ds(base + r, WINDOW, stride=dil) if dil > 1 else pl.ds(base, WINDOW)

        valid = _band_mask(first if j == 0 else False)
        cur = rows(j * span)
        q = jnp.concatenate([cur3_ref[0, cur, :], cur3_ref[1, cur, :]], axis=1)[:, 0:DIL_GW]
        kv = cur3_ref[2, cur, :]
        pkv = prev_ref[0, rows(0), :] if j == 0 else cur3_ref[2, rows((j - 1) * span), :]
        kk = jnp.concatenate([pkv[:, 0:HEAD_DIM], kv[:, 0:HEAD_DIM]], axis=0).astype(BF16)
        vv = jnp.concatenate([pkv[:, HEAD_DIM:], kv[:, HEAD_DIM:]], axis=0).astype(BF16)
        o, lse = _band_heads(q, kk, vv, valid, None)
        res = _untranspose(o + [jnp.broadcast_to(x, (HEAD_DIM, WINDOW)) for x in lse])
        for c in range(nl):
            out3_ref[c, cur, :] = res[:, c * LANES:(c + 1) * LANES]

    for j in range(nsub):
        if dil <= 4:
            for r in range(dil):
                one(j, r)
        else:
            lax.fori_loop(0, dil, lambda r, c, j=j: (one(j, r), c)[1], 0, unroll=2)
    for c in range(nl):
        o_ref[0, :, c * LANES:(c + 1) * LANES] = out3_ref[c]


def _dil_prompt(proj, group, dil):
    n, t, _ = proj.shape
    span = WINDOW * dil
    nsub = max(1, 8 // dil)
    rows = span * nsub
    slot = OFF_C + group * C_SLOT
    return pl.pallas_call(
        functools.partial(_dil_prompt_kernel, dil=dil, nsub=nsub),
        grid=(n, t // rows),
        in_specs=[
            pl.BlockSpec((1, rows, C_SLOT), lambda b, i: (b, i, slot // C_SLOT)),
            pl.BlockSpec((1, span, LANES), lambda b, i: (b, jnp.maximum(i * nsub - 1, 0), (slot + C_K) // LANES)),
        ],
        out_specs=pl.BlockSpec((1, rows, 2 * DIL_GW), lambda b, i: (b, i, 0)),
        out_shape=jax.ShapeDtypeStruct((n, t, 2 * DIL_GW), F32),
        scratch_shapes=[pltpu.VMEM((C_SLOT // LANES, rows, LANES), F32),
                        pltpu.VMEM((2 * DIL_GW // LANES, rows, LANES), F32)],
        compiler_params=_cparams(("parallel", "arbitrary")),
    )(proj, proj)


def _decode_heads(q, knew, vnew, kt, vt, dil, sinks):
    s = _dot(q.astype(BF16), kt.astype(BF16)) * ATTN_SCALE
    if dil > 1:
        row = lax.broadcasted_iota(jnp.int32, s.shape, 1)
        s = jnp.where(row % dil == 0, s, NEG_BIG)
    s_new = jnp.sum(q * knew, axis=-1, keepdims=True) * ATTN_SCALE
    m = jnp.maximum(jnp.max(s, axis=-1, keepdims=True), s_new)
    if sinks is not None:
        m = jnp.maximum(m, sinks)
    p = jnp.exp(s - m)
    p_new = jnp.exp(s_new - m)
    l = jnp.sum(p, axis=-1, keepdims=True) + p_new
    if sinks is not None:
        l = l + jnp.exp(sinks - m)
    o = (_dot_nt(p.astype(BF16), vt.astype(BF16)) + p_new * vnew) / l
    return o, m + jnp.log(l)


def _rows_of(x, heads):
    return jnp.concatenate([x[:, g * HEAD_DIM:(g + 1) * HEAD_DIM] for g in range(heads)], axis=0)


def _swa_sample_kernel(p_ref, cache_ref, sink_ref, o_ref, *, bn):
    for b in range(bn):
        row = p_ref[b:b + 1, :]
        outs = []
        for h in range(SWA_KV_HEADS):
            gw = SWA_GROUP * HEAD_DIM
            q = _rows_of(row[:, h * gw:(h + 1) * gw], SWA_GROUP)
            knew = row[:, SWA_WIDTH + h * HEAD_DIM:SWA_WIDTH + (h + 1) * HEAD_DIM]
            vnew = row[:, SWA_WIDTH + SWA_KV_WIDTH + h * HEAD_DIM:SWA_WIDTH + SWA_KV_WIDTH + (h + 1) * HEAD_DIM]
            sinks = _rows_of(sink_ref[:, h * gw:(h + 1) * gw], SWA_GROUP)[:, 0:1]
            o, _ = _decode_heads(q, knew, vnew, cache_ref[b, 0, h], cache_ref[b, 1, h], 1, sinks)
            outs += [o[g:g + 1] for g in range(SWA_GROUP)]
        o_ref[b:b + 1, :] = jnp.concatenate(outs, axis=1)


def _swa_sample(proj, cache, sink, layer):
    n = proj.shape[0]
    bn = 8
    return pl.pallas_call(
        functools.partial(_swa_sample_kernel, bn=bn),
        grid=(n // bn,),
        in_specs=[
            pl.BlockSpec((bn, B_SLOT), lambda i: (i, OFF_B // B_SLOT)),
            pl.BlockSpec((None, bn, 2, SWA_KV_HEADS, HEAD_DIM, WINDOW), lambda i: (layer, i, 0, 0, 0, 0)),
            pl.BlockSpec((1, SWA_WIDTH), lambda i: (0, 0)),
        ],
        out_specs=pl.BlockSpec((bn, SWA_WIDTH), lambda i: (i, 0)),
        out_shape=jax.ShapeDtypeStruct((n, SWA_WIDTH), F32),
        compiler_params=_cparams(("parallel",)),
    )(proj, cache, sink)


def _dil_sample_kernel(p_ref, cache_ref, o_ref, *, bn, dil):
    for b in range(bn):
        row = p_ref[b:b + 1, :]
        q = _rows_of(row[:, 0:DIL_GW], DIL_HPG)
        o, lse = _decode_heads(q, row[:, C_K:C_K + HEAD_DIM], row[:, C_K + HEAD_DIM:C_K + 2 * HEAD_DIM],
                               cache_ref[b, 0], cache_ref[b, 1], dil, None)
        lse = jnp.broadcast_to(lse, (DIL_HPG, HEAD_DIM))
        o_ref[b:b + 1, :] = jnp.concatenate([o[g:g + 1] for g in range(DIL_HPG)]
                                            + [lse[g:g + 1] for g in range(DIL_HPG)], axis=1)


def _dil_sample(proj, cache, group, dil, layer):
    n = proj.shape[0]
    bn = 8
    slot = OFF_C + group * C_SLOT
    return pl.pallas_call(
        functools.partial(_dil_sample_kernel, bn=bn, dil=dil),
        grid=(n // bn,),
        in_specs=[
            pl.BlockSpec((bn, C_SLOT), lambda i: (i, slot // C_SLOT)),
            pl.BlockSpec((None, bn, 2, HEAD_DIM, WINDOW * dil), lambda i: (layer, i, 0, 0, 0)),
        ],
        out_specs=pl.BlockSpec((bn, 2 * DIL_GW), lambda i: (i, 0)),
        out_shape=jax.ShapeDtypeStruct((n, 2 * DIL_GW), F32),
        compiler_params=_cparams(("parallel",)),
    )(proj, cache)


def _s5_param_kernel(are_ref, aim_ref, ldt_ref, bre_ref, bim_ref, cre_ref, cim_ref,
                     lam_re_ref, lam_im_ref, cl_re_ref, cl_im_ref, r_re_ref, r_im_ref, k_ref):
    ar = are_ref[0]
    ai = aim_ref[0]
    dt = jnp.exp(ldt_ref[0])
    nrow = lam_re_ref.shape[1]
    ri = lax.broadcasted_iota(jnp.int32, (nrow, 1), 0)
    pw = jnp.where(ri < S5_NPOW, ri, jnp.left_shift(S5_CHUNK, jnp.maximum(ri - S5_NPOW, 0))).astype(F32)
    mag = jnp.exp(pw * (ar * dt))
    ang = pw * (ai * dt)
    lam_re = mag * jnp.cos(ang)
    lam_im = mag * jnp.sin(ang)
    lam_re_ref[0] = lam_re
    lam_im_ref[0] = lam_im
    l1r, l1i = lam_re[1:2], lam_im[1:2]
    den = ar * ar + ai * ai
    z_re = ((l1r - 1.0) * ar + l1i * ai) / den
    z_im = (l1i * ar - (l1r - 1.0) * ai) / den
    b_re, b_im = bre_ref[0], bim_ref[0]
    bb_re = z_re * b_re - z_im * b_im
    bb_im = z_re * b_im + z_im * b_re
    c_re, c_im = cre_ref[0], cim_ref[0]
    cls_re, cls_im = [], []
    for d in range(S5_NPOW):
        lr, li = lam_re[d:d + 1], lam_im[d:d + 1]
        cr = c_re * lr - c_im * li
        ci = c_re * li + c_im * lr
        cl_re_ref[0, d * SSM_GROUP:(d + 1) * SSM_GROUP, :] = cr
        cl_im_ref[0, d * SSM_GROUP:(d + 1) * SSM_GROUP, :] = ci
        if d < S5_CHUNK:
            cls_re.append(cr)
            cls_im.append(ci)
    for s in range(S5_CHUNK):
        lr = lam_re[S5_CHUNK - 1 - s:S5_CHUNK - s]
        li = lam_im[S5_CHUNK - 1 - s:S5_CHUNK - s]
        r_re_ref[0, s * SSM_GROUP:(s + 1) * SSM_GROUP, :] = lr * bb_re - li * bb_im
        r_im_ref[0, s * SSM_GROUP:(s + 1) * SSM_GROUP, :] = lr * bb_im + li * bb_re
    call_re = jnp.concatenate(cls_re, axis=0)
    call_im = jnp.concatenate(cls_im, axis=0)
    hp = lax.Precision.HIGHEST
    nt = (((1,), (1,)), ((), ()))
    kt = (lax.dot_general(bb_re, call_re, nt, precision=hp, preferred_element_type=F32)
          - lax.dot_general(bb_im, call_im, nt, precision=hp, preferred_element_type=F32))
    cw = S5_CHUNK * SSM_GROUP
    for s in range(S5_CHUNK):
        if s == 0:
            blk = kt
        else:
            blk = jnp.concatenate([jnp.zeros((SSM_GROUP, s * SSM_GROUP), F32), kt[:, :cw - s * SSM_GROUP]], axis=1)
        k_ref[0, s * SSM_GROUP:(s + 1) * SSM_GROUP, :] = blk


def _s5_params(a_re, a_im, log_dt, b_re, b_im, c_re, c_im, nsteps):
    g = SSM_GROUPS
    nrow = ((S5_NPOW + nsteps + 7) // 8) * 8
    row = lambda i: (i, 0, 0)
    shp = lambda r, c: jax.ShapeDtypeStruct((g, r, c), F32)
    ncl = S5_NPOW * SSM_GROUP
    nr = S5_CHUNK * SSM_GROUP
    return pl.pallas_call(
        _s5_param_kernel,
        grid=(g,),
        in_specs=[
            pl.BlockSpec((1, 1, SSM_STATE), row),
            pl.BlockSpec((1, 1, SSM_STATE), row),
            pl.BlockSpec((1, 1, 1), row),
            pl.BlockSpec((1, SSM_GROUP, SSM_STATE), row),
            pl.BlockSpec((1, SSM_GROUP, SSM_STATE), row),
            pl.BlockSpec((1, SSM_GROUP, SSM_STATE), row),
            pl.BlockSpec((1, SSM_GROUP, SSM_STATE), row),
        ],
        out_specs=[
            pl.BlockSpec((1, nrow, SSM_STATE), row), pl.BlockSpec((1, nrow, SSM_STATE), row),
            pl.BlockSpec((1, ncl, SSM_STATE), row), pl.BlockSpec((1, ncl, SSM_STATE), row),
            pl.BlockSpec((1, nr, SSM_STATE), row), pl.BlockSpec((1, nr, SSM_STATE), row),
            pl.BlockSpec((1, nr, nr), row),
        ],
        out_shape=[shp(nrow, SSM_STATE), shp(nrow, SSM_STATE), shp(ncl, SSM_STATE), shp(ncl, SSM_STATE),
                   shp(nr, SSM_STATE), shp(nr, SSM_STATE), shp(nr, nr)],
        compiler_params=_cparams(("parallel",)),
    )(a_re.reshape(g, 1, SSM_STATE), a_im.reshape(g, 1, SSM_STATE), log_dt.reshape(g, 1, 1),
      jnp.swapaxes(b_re, 1, 2), jnp.swapaxes(b_im, 1, 2), c_re, c_im)


def _s5_rows_to_chunks(st_ref, u_ref):
    nc = st_ref.shape[1] // S5_CHUNK
    per_blk = LANES // SSM_GROUP
    slot = lax.broadcasted_iota(jnp.int32, (nc, LANES), 1) // SSM_GROUP
    for g in range(SSM_GROUPS):
        acc = [jnp.zeros((nc, LANES), F32) for _ in range(S5_CHUNK // per_blk)]
        for s in range(S5_CHUNK):
            v = st_ref[g // per_blk, pl.ds(s, nc, stride=S5_CHUNK), :]
            shift = (SSM_GROUP * (s % per_blk) - SSM_GROUP * (g % per_blk)) % LANES
            if shift:
                v = pltpu.roll(v, shift, axis=1)
            acc[s // per_blk] = jnp.where(slot == s % per_blk, v, acc[s // per_blk])
        for k, a in enumerate(acc):
            u_ref[g, :, k * LANES:(k + 1) * LANES] = a


def _s5_chunks_to_rows(y_ref, st_ref):
    nc = y_ref.shape[1]
    per_blk = LANES // SSM_GROUP
    slot = lax.broadcasted_iota(jnp.int32, (nc, LANES), 1) // SSM_GROUP
    nblk = -(-SSM_WIDTH // LANES)
    for lb in range(nblk):
        groups = range(lb * per_blk, min((lb + 1) * per_blk, SSM_GROUPS))
        for s in range(S5_CHUNK):
            acc = jnp.zeros((nc, LANES), F32)
            for g in groups:
                v = y_ref[g, :, (s // per_blk) * LANES:(s // per_blk + 1) * LANES]
                shift = (SSM_GROUP * (g % per_blk) - SSM_GROUP * (s % per_blk)) % LANES
                if shift:
                    v = pltpu.roll(v, shift, axis=1)
                acc = jnp.where(slot == g % per_blk, v, acc)
            st_ref[lb, pl.ds(s, nc, stride=S5_CHUNK), :] = acc
    return jnp.concatenate([st_ref[lb][:, 0:min(LANES, SSM_WIDTH - lb * LANES)] for lb in range(nblk)], axis=1)


def _s5_prompt_kernel(u_ref, mt_ref, r_re_ref, r_im_ref, cl_re_ref, cl_im_ref, lam_re_ref, lam_im_ref, d_ref,
                      y_ref, xre_ref, xim_ref, *, nseq, nchunk):
    u = u_ref[0]
    ub = u.astype(BF16)
    y = _dot(ub, mt_ref[0].astype(BF16))
    xr = _dot(ub, r_re_ref[0].astype(BF16))
    xi = _dot(ub, r_im_ref[0].astype(BF16))
    rows = nseq * nchunk
    cidx = lax.broadcasted_iota(jnp.int32, (rows, SSM_STATE), 0) % nchunk
    lam_re, lam_im = lam_re_ref[0], lam_im_ref[0]
    for k in range(int(math.log2(nchunk))):
        sh = 1 << k
        lr = lam_re[S5_NPOW + k:S5_NPOW + k + 1]
        li = lam_im[S5_NPOW + k:S5_NPOW + k + 1]
        ok = cidx >= sh
        pr = jnp.where(ok, pltpu.roll(xr, sh, axis=0), 0.0)
        pi = jnp.where(ok, pltpu.roll(xi, sh, axis=0), 0.0)
        xr, xi = xr + lr * pr - li * pi, xi + lr * pi + li * pr
    ok = cidx >= 1
    er = jnp.where(ok, pltpu.roll(xr, 1, axis=0), 0.0)
    ei = jnp.where(ok, pltpu.roll(xi, 1, axis=0), 0.0)
    cl_re = cl_re_ref[0][SSM_GROUP:, :]
    cl_im = cl_im_ref[0][SSM_GROUP:, :]
    y = y + _dot_nt(er.astype(BF16), cl_re.astype(BF16)) - _dot_nt(ei.astype(BF16), cl_im.astype(BF16))
    y_ref[0] = y + d_ref[0] * u
    last = [s * nchunk + nchunk - 1 for s in range(nseq)]
    xre_ref[0] = jnp.concatenate([xr[i:i + 1] for i in last], axis=0)
    xim_ref[0] = jnp.concatenate([xi[i:i + 1] for i in last], axis=0)


def _s5_prompt(u, mt, r_re, r_im, cl_re, cl_im, lam_re, lam_im, d_t, nseq, nchunk):
    g = SSM_GROUPS
    rows = nseq * nchunk
    cw = S5_CHUNK * SSM_GROUP
    row = lambda i: (i, 0, 0)
    full = lambda a: pl.BlockSpec((1,) + a.shape[1:], row)
    return pl.pallas_call(
        functools.partial(_s5_prompt_kernel, nseq=nseq, nchunk=nchunk),
        grid=(g,),
        in_specs=[full(u), full(mt), full(r_re), full(r_im), full(cl_re), full(cl_im), full(lam_re), full(lam_im),
                  full(d_t)],
        out_specs=[pl.BlockSpec((1, rows, cw), row), pl.BlockSpec((1, nseq, SSM_STATE), row),
                   pl.BlockSpec((1, nseq, SSM_STATE), row)],
        out_shape=[jax.ShapeDtypeStruct((g, rows, cw), F32), jax.ShapeDtypeStruct((g, nseq, SSM_STATE), F32),
                   jax.ShapeDtypeStruct((g, nseq, SSM_STATE), F32)],
        compiler_params=_cparams(("parallel",)),
    )(u, mt, r_re, r_im, cl_re, cl_im, lam_re, lam_im, d_t)


def _s5_sample_kernel(u_ref, x0r_ref, x0i_ref, r_re_ref, r_im_ref, cl_re_ref, cl_im_ref, lam_re_ref, lam_im_ref,
                      d_ref, y_ref, xre_ref, xim_ref):
    u = u_ref[0]
    ub = u.astype(BF16)
    lo = (S5_CHUNK - 1) * SSM_GROUP
    bb_re = r_re_ref[0][lo:lo + SSM_GROUP, :]
    bb_im = r_im_ref[0][lo:lo + SSM_GROUP, :]
    lr, li = lam_re_ref[0][1:2], lam_im_ref[0][1:2]
    x0r, x0i = x0r_ref[0], x0i_ref[0]
    xr = _dot(ub, bb_re.astype(BF16)) + (lr * x0r - li * x0i)
    xi = _dot(ub, bb_im.astype(BF16)) + (lr * x0i + li * x0r)
    c_re = cl_re_ref[0][:SSM_GROUP, :]
    c_im = cl_im_ref[0][:SSM_GROUP, :]
    y_ref[0] = (_dot_nt(xr.astype(BF16), c_re.astype(BF16)) - _dot_nt(xi.astype(BF16), c_im.astype(BF16))
                + d_ref[0] * u)
    xre_ref[0] = xr
    xim_ref[0] = xi


def _s5_sample(u, x0r, x0i, r_re, r_im, cl_re, cl_im, lam_re, lam_im, d):
    g, n = u.shape[0], u.shape[1]
    row = lambda i: (i, 0, 0)
    full = lambda a: pl.BlockSpec((1,) + a.shape[1:], row)
    return pl.pallas_call(
        _s5_sample_kernel,
        grid=(g,),
        in_specs=[full(u), full(x0r), full(x0i), full(r_re), full(r_im), full(cl_re), full(cl_im), full(lam_re),
                  full(lam_im), full(d)],
        out_specs=[pl.BlockSpec((1, n, SSM_GROUP), row), pl.BlockSpec((1, n, SSM_STATE), row),
                   pl.BlockSpec((1, n, SSM_STATE), row)],
        out_shape=[jax.ShapeDtypeStruct((g, n, SSM_GROUP), F32), jax.ShapeDtypeStruct((g, n, SSM_STATE), F32),
                   jax.ShapeDtypeStruct((g, n, SSM_STATE), F32)],
        compiler_params=_cparams(("parallel",)),
    )(u, x0r, x0i, r_re, r_im, cl_re, cl_im, lam_re, lam_im, d)


def _log_sigmoid(x):
    return jnp.minimum(x, 0.0) - jnp.log1p(jnp.exp(-jnp.abs(x)))


def _logaddexp(a, b):
    return jnp.maximum(a, b) + jnp.log1p(jnp.exp(-jnp.abs(a - b)))


def _hgrn_gates(fpre, lb):
    log_f = _logaddexp(jnp.log(jnp.maximum(lb, LB_FLOOR)), jnp.log1p(-lb) + _log_sigmoid(fpre))
    k = (1.0 - lb) * _sigmoid(-fpre)
    return log_f, k


def _hgrn_scan_matrix():
    c = HGRN_CHUNK
    tri = np.tril(np.ones((c, c), np.float32))
    t = np.arange(c)
    mats = [tri]
    for m in HGRN_LEVELS:
        mats.append(tri[(t // (2 * m)) * (2 * m) + m - 1])
    return np.concatenate(mats, axis=0)


def _hgrn_prompt_kernel(q_ref, f_ref, i_ref, g_ref, lbraw_ref, gn_ref, scan_ref, o_ref, s_ref, st_ref, *,
                        layer, nsub):
    parts = (q_ref, f_ref, i_ref, g_ref)
    cidx = pl.program_id(1)
    c = HGRN_CHUNK
    hd = HGRN_HEAD_DIM

    @pl.when(cidx == 0)
    def _():
        st_ref[...] = jnp.zeros_like(st_ref)

    a = lbraw_ref[...]
    e = jnp.exp(a - jnp.max(a, axis=0, keepdims=True))
    p = e / jnp.sum(e, axis=0, keepdims=True)
    run = p[0:1]
    for l in range(1, layer + 1):
        run = run + p[l:l + 1]
    lb_all = run - p[0:1]

    r = lax.broadcasted_iota(jnp.int32, (c, c), 0)
    s = lax.broadcasted_iota(jnp.int32, (c, c), 1)
    pairs = []
    for m in HGRN_LEVELS:
        sh = int(math.log2(2 * m))
        pairs.append(((r >> sh) == (s >> sh)) & ((r & m) != 0) & ((s & m) == 0))
    scan = scan_ref[...]
    for h, sub in [(h, sub) for h in range(HGRN_HEADS) for sub in range(nsub)]:
        rows = slice(sub * c, (sub + 1) * c)
        col = lambda part: parts[part][0, rows, h * hd:(h + 1) * hd]
        log_f, k = _hgrn_gates(col(1), lb_all[:, h * hd:(h + 1) * hd])
        q = _silu(col(0))
        v = col(2)
        hi = log_f.astype(BF16)
        rest = log_f - hi.astype(F32)
        mid = rest.astype(BF16)
        lo = (rest - mid.astype(F32)).astype(BF16)
        cum3 = _dot(scan, jnp.concatenate([hi, mid, lo], axis=1))
        cum = cum3[:, 0:hd] + cum3[:, hd:2 * hd] + cum3[:, 2 * hd:3 * hd]
        b = cum[0:c]
        att = jnp.where(r == s, jnp.sum(q * k, axis=-1, keepdims=True), 0.0)
        for lvl, m in enumerate(HGRN_LEVELS):
            d = b - cum[(lvl + 1) * c:(lvl + 2) * c]
            qs = q * jnp.exp(jnp.minimum(d, 0.0))
            ks = k * jnp.exp(jnp.minimum(-d, 0.0))
            att = jnp.where(pairs[lvl], _dot_nt(qs.astype(BF16), ks.astype(BF16)), att)
        st = st_ref[h]
        vb = v.astype(BF16)
        o = _dot(att.astype(BF16), vb) + _dot_nt((q * jnp.exp(b)).astype(BF16), st.astype(BF16))
        bl = b[c - 1:c]
        st_new = jnp.exp(bl) * st + _dot_tn(vb, (k * jnp.exp(bl - b)).astype(BF16))
        st_ref[h] = st_new
        o = o * lax.rsqrt(jnp.mean(o * o, axis=-1, keepdims=True) + NORM_EPS)
        o_ref[0, rows, h * hd:(h + 1) * hd] = o * gn_ref[:, h * hd:(h + 1) * hd] * _silu(col(3))

    @pl.when(cidx == pl.num_programs(1) - 1)
    def _():
        s_ref[0] = st_ref[...]


def _hgrn_prompt(proj, lbraw, gn, layer):
    n, t, _ = proj.shape
    c = HGRN_CHUNK
    hd = HGRN_HEAD_DIM
    scan = jnp.asarray(_hgrn_scan_matrix(), dtype=BF16)
    nsub = 2
    c = c * nsub
    return pl.pallas_call(
        functools.partial(_hgrn_prompt_kernel, layer=layer, nsub=nsub),
        grid=(n, t // c),
        in_specs=[pl.BlockSpec((1, c, HGRN_WIDTH), lambda b, j, blk=off // HGRN_WIDTH: (b, j, blk)) for off in OFF_D]
                 + [pl.BlockSpec((DEPTH, HGRN_WIDTH), lambda b, j: (0, 0)),
                  pl.BlockSpec((None, 1, HGRN_WIDTH), lambda b, j: (layer, 0, 0)),
                  pl.BlockSpec(scan.shape, lambda b, j: (0, 0))],
        out_specs=[pl.BlockSpec((1, c, HGRN_WIDTH), lambda b, j: (b, j, 0)),
                   pl.BlockSpec((1, HGRN_HEADS, hd, hd), lambda b, j: (b, 0, 0, 0))],
        out_shape=[jax.ShapeDtypeStruct((n, t, HGRN_WIDTH), F32),
                   jax.ShapeDtypeStruct((n, HGRN_HEADS, hd, hd), F32)],
        scratch_shapes=[pltpu.VMEM((HGRN_HEADS, hd, hd), F32)],
        compiler_params=_cparams(("parallel", "arbitrary")),
    )(proj, proj, proj, proj, lbraw, _v3(gn), scan)


def _hgrn_sample_kernel(qt_ref, ft_ref, lbt_ref, v_ref, g_ref, gn_ref, s0_ref, o_ref, s_ref, *, layer, n):
    hd = HGRN_HEAD_DIM
    a = lbt_ref[0]
    e = jnp.exp(a - jnp.max(a, axis=1, keepdims=True))
    p = e / jnp.sum(e, axis=1, keepdims=True)
    run = p[:, 0:1]
    for l in range(1, layer + 1):
        run = run + p[:, l:l + 1]
    lb = run - p[:, 0:1]
    log_f, k = _hgrn_gates(ft_ref[0], lb)
    f = jnp.exp(log_f)
    q = _silu(qt_ref[0])
    v = v_ref[...]
    gate = g_ref[...]
    rows = []
    for b in range(n):
        s_new = f[:, b:b + 1] * s0_ref[b, 0] + k[:, b:b + 1] * v[b:b + 1, :]
        s_ref[b, 0] = s_new
        rows.append(jnp.sum(q[:, b:b + 1] * s_new, axis=0, keepdims=True))
    o = jnp.concatenate(rows, axis=0)
    o = o * lax.rsqrt(jnp.mean(o * o, axis=-1, keepdims=True) + NORM_EPS)
    o_ref[...] = o * gn_ref[...] * _silu(gate)


def _hgrn_sample(qt, ft, lbt, proj, gn, s0, layer):
    n = proj.shape[0]
    hd = HGRN_HEAD_DIM
    v_blk = OFF_D[2] // hd
    g_blk = OFF_D[3] // hd
    return pl.pallas_call(
        functools.partial(_hgrn_sample_kernel, layer=layer, n=n),
        grid=(HGRN_HEADS,),
        in_specs=[pl.BlockSpec((1, hd, n), lambda h: (h, 0, 0)),
                  pl.BlockSpec((1, hd, n), lambda h: (h, 0, 0)),
                  pl.BlockSpec((1, hd, DEPTH), lambda h: (h, 0, 0)),
                  pl.BlockSpec((n, hd), lambda h: (0, v_blk + h)),
                  pl.BlockSpec((n, hd), lambda h: (0, g_blk + h)),
                  pl.BlockSpec((None, 1, hd), lambda h: (layer, 0, h)),
                  pl.BlockSpec((None, n, 1, hd, hd), lambda h: (layer, 0, h, 0, 0))],
        out_specs=[pl.BlockSpec((n, hd), lambda h: (0, h)),
                   pl.BlockSpec((n, 1, hd, hd), lambda h: (0, h, 0, 0))],
        out_shape=[jax.ShapeDtypeStruct((n, HGRN_WIDTH), F32),
                   jax.ShapeDtypeStruct((n, HGRN_HEADS, hd, hd), F32)],
        compiler_params=_cparams(("parallel",)),
    )(qt, ft, lbt, proj, proj, _v3(gn), s0)


def _gelu_tanh(x):
    return 0.5 * x * (1.0 + jnp.tanh(math.sqrt(2.0 / math.pi) * (x + 0.044715 * (x * x * x))))


def _mixout_kernel(h_ref, ya_ref, ob_ref, oc0_ref, oc1_ref, oc2_ref, od_ref,
                   wglu_ref, bglu_ref, ga_ref, gb_ref, gc_ref, wo_ref, gpost_ref, o_ref, *scratch):
    ya = _s5_chunks_to_rows(ya_ref, scratch[0]) if scratch else ya_ref[...]
    z = _gelu_tanh(ya)
    gate = _sigmoid(_dot(z.astype(BF16), wglu_ref[...].astype(BF16)) + bglu_ref[...])
    out_a = _rms(z * gate, ga_ref[...])
    out_b = _rms(ob_ref[...], gb_ref[...])
    ocs = [oc0_ref, oc1_ref, oc2_ref]
    ls = [r[:, DIL_GW:2 * DIL_GW] for r in ocs]
    m = jnp.maximum(jnp.maximum(ls[0], ls[1]), ls[2])
    es = [jnp.exp(l - m) for l in ls]
    den = es[0] + es[1] + es[2]
    cs = [ocs[gi][:, 0:DIL_GW] * (es[gi] / den) for gi in range(3)]
    ss = sum(jnp.sum(c * c, axis=-1, keepdims=True) for c in cs)
    inv = lax.rsqrt(ss / DIL_WIDTH + NORM_EPS)
    gc = gc_ref[...]
    o1, o2, o3 = SSM_WIDTH, SSM_WIDTH + SWA_WIDTH, SSM_WIDTH + SWA_WIDTH + DIL_WIDTH
    wo = lambda lo, hi: wo_ref[lo:hi, :]
    mix = _dot(out_a.astype(BF16), wo(0, o1)) + _dot(out_b.astype(BF16), wo(o1, o2))
    for gi in range(3):
        c = cs[gi] * inv * gc[:, gi * DIL_GW:(gi + 1) * DIL_GW]
        mix = mix + _dot(c.astype(BF16), wo(o2 + gi * DIL_GW, o2 + (gi + 1) * DIL_GW))
    mix = mix + _dot(od_ref[...].astype(BF16), wo(o3, D_MODEL))
    o_ref[...] = h_ref[...] + _rms(mix, gpost_ref[...])


def _mixout(h, ya, ob, ocs, od, lp_all, w_out_l, layer, tm, row0=0):
    m = ob.shape[0]
    blk0 = row0 // tm
    rowblk = lambda a: pl.BlockSpec((tm, a.shape[1]), lambda i: (i, 0))
    vec = lambda a: pl.BlockSpec((None, 1, a.shape[1]), lambda i: (layer, 0, 0))
    mat = lambda a: pl.BlockSpec((None,) + a.shape[1:], lambda i: (layer, 0, 0))
    whole = lambda a: pl.BlockSpec(a.shape, lambda i: (0, 0), pipeline_mode=pl.Buffered(1))
    acts = [h, ya, ob, *ocs, od]
    names = ['ssm_w_glu', 'ssm_b_glu', 'out_norm_a', 'out_norm_b', 'out_norm_c', 'w_out', 'mix_norm_post']
    params = [w_out_l if k == 'w_out' else lp_all[k] for k in names]
    spec = lambda k, p: whole(p) if k == 'w_out' else (mat(p) if p.ndim == 3 else vec(p))
    act_specs = [rowblk(a) for a in acts]
    act_specs[0] = pl.BlockSpec((tm, D_MODEL), lambda i: (i + blk0, 0))
    scratch = []
    chunked = ya.ndim == 3
    if chunked:
        act_specs[1] = pl.BlockSpec((SSM_GROUPS, tm // S5_CHUNK, ya.shape[2]), lambda i: (0, i, 0))
        scratch = [pltpu.VMEM((-(-SSM_WIDTH // LANES), tm, LANES), F32)]
    args = [*acts, *[_v3(p) if p.ndim == 2 and k != 'w_out' else p for k, p in zip(names, params)]]
    in_specs = act_specs + [spec(k, p) for k, p in zip(names, params)]
    return pl.pallas_call(
        _mixout_kernel,
        grid=(m // tm,),
        in_specs=in_specs,
        out_specs=pl.BlockSpec((tm, D_MODEL), lambda i: (i, 0)),
        out_shape=jax.ShapeDtypeStruct((m, D_MODEL), F32),
        scratch_shapes=scratch,
        compiler_params=_cparams(("parallel",)),
    )(*args)


def _ffn_block(x, x_tail, P, name, layer, tm, n_main, n_tail, tail_out):
    return _ffn(x, x_tail, P[name + '_norm_pre'], P[name + '_w_gate'], P[name + '_w_up'], P[name + '_w_down'],
                P[name + '_norm_post'], layer, tm, n_main, n_tail, tail_out)


def _prompt_mixer(h, P, w_in_l, w_out_l, s5p, layer, n, t, rope, tm):
    m = n * t
    proj2, u = _win(h, P['mix_norm_pre'], w_in_l, rope[0], rope[1], layer, tm, True)
    proj = proj2.reshape(n, t, PROJ_COLS)

    lam_re, lam_im, cl_re, cl_im, r_re, r_im, kmat = s5p
    nchunk = t // S5_CHUNK
    d_t = jnp.tile(P['ssm_d'][layer], (1, S5_CHUNK)).reshape(SSM_GROUPS, 1, S5_CHUNK * SSM_GROUP)
    y, xre, xim = _s5_prompt(u, kmat, r_re, r_im, cl_re, cl_im, lam_re, lam_im, d_t, n, nchunk)
    ssm_new = jnp.stack([xre, xim], axis=-1).transpose(1, 0, 2, 3)

    sink = jnp.repeat(P['swa_sinks'][layer], HEAD_DIM).reshape(1, SWA_WIDTH)
    ob = _swa_prompt(proj, sink).reshape(m, SWA_WIDTH)
    keep = min(WINDOW, t)
    kv = proj[:, t - keep:, OFF_B + SWA_WIDTH:OFF_B + B_SLOT]
    swa_new = kv.reshape(n, keep, 2, SWA_KV_HEADS, HEAD_DIM)

    ocs, dil_new = [], []
    for gi, (win, dil) in enumerate(DIL_PAIRS):
        ocs.append(_dil_prompt(proj, gi, dil).reshape(m, 2 * DIL_GW))
        keep = min(win, t)
        lo = OFF_C + gi * C_SLOT + C_K
        dil_new.append(proj[:, t - keep:, lo:lo + 2 * HEAD_DIM].reshape(n, keep, 2, HEAD_DIM))

    od, st = _hgrn_prompt(proj, P['hgrn_lower_bounds'], P['out_norm_d'], layer)
    hgrn_new = jnp.swapaxes(st, -1, -2)

    h = _mixout(h, y, ob, ocs, od.reshape(m, HGRN_WIDTH), P, w_out_l, layer, min(tm, 512))
    return h, (ssm_new, swa_new, dil_new[0], dil_new[1], dil_new[2], hgrn_new)


def _sample_mixer(h, row0, P, w_in_l, w_out_l, s5p, layer, caches, rope):
    n = rope[0].shape[0]
    state_ssm, cache_swa, cache_d0, cache_d1, cache_d2, state_hgrn = caches
    (proj,) = _win(h, P['mix_norm_pre'], w_in_l, rope[0], rope[1], layer, n, False, row0)

    lam_re, lam_im, cl_re, cl_im, r_re, r_im, _ = s5p
    u = proj[:, OFF_U:OFF_U + SSM_WIDTH].reshape(n, SSM_GROUPS, SSM_GROUP).transpose(1, 0, 2)
    x0 = state_ssm[layer].transpose(1, 0, 2, 3)
    y, xre, xim = _s5_sample(u, x0[..., 0], x0[..., 1], r_re, r_im, cl_re, cl_im, lam_re, lam_im,
                             P['ssm_d'][layer].reshape(SSM_GROUPS, 1, SSM_GROUP))
    ya = y.transpose(1, 0, 2).reshape(n, SSM_WIDTH)
    ssm_new = jnp.stack([xre, xim], axis=-1).transpose(1, 0, 2, 3)

    sink = jnp.repeat(P['swa_sinks'][layer], HEAD_DIM).reshape(1, SWA_WIDTH)
    ob = _swa_sample(proj, cache_swa.transpose(0, 1, 3, 4, 5, 2), sink, layer)
    swa_new = proj[:, OFF_B + SWA_WIDTH:OFF_B + B_SLOT].reshape(n, 1, 2, SWA_KV_HEADS, HEAD_DIM)

    ocs, dil_new = [], []
    for gi, (buf, (win, dil)) in enumerate(zip((cache_d0, cache_d1, cache_d2), DIL_PAIRS)):
        ocs.append(_dil_sample(proj, buf.transpose(0, 1, 3, 4, 2), gi, dil, layer))
        lo = OFF_C + gi * C_SLOT + C_K
        dil_new.append(proj[:, lo:lo + 2 * HEAD_DIM].reshape(n, 1, 2, HEAD_DIM))

    part = lambda i: proj[:, OFF_D[i]:OFF_D[i] + HGRN_WIDTH].reshape(n, HGRN_HEADS, HGRN_HEAD_DIM)
    cm = lambda a: a.transpose(1, 2, 0)
    lbt = P['hgrn_lower_bounds'].reshape(DEPTH, HGRN_HEADS, HGRN_HEAD_DIM).transpose(1, 2, 0)
    od, hgrn_new = _hgrn_sample(cm(part(0)), cm(part(1)), lbt, proj, P['out_norm_d'], state_hgrn, layer)

    h = _mixout(h, ya, ob, ocs, od, P, w_out_l, layer, n, row0)
    return h, (ssm_new, swa_new, dil_new[0], dil_new[1], dil_new[2], hgrn_new)


def _forward(x_prompt, x_sample, caches, P):
    n, t, _ = x_prompt.shape
    ns, ts, _ = x_sample.shape
    assert ts == 1
    tm = 1024
    m_p = n * t
    rope_p = _rope_tables(jnp.tile(jnp.arange(t, dtype=jnp.int32), n))
    rope_s = _rope_tables(jnp.full((ns,), PAST_LEN, dtype=jnp.int32))
    nsteps = int(math.log2(t // S5_CHUNK))
    h = x_prompt.reshape(m_p, D_MODEL)
    h_tail = x_sample.reshape(ns, D_MODEL)
    new_p = [[] for _ in range(6)]
    new_s = [[] for _ in range(6)]
    for l in range(DEPTH):
        w_in_l = _win_prep(P['w_in'], l)
        w_out_l = _cast_bf16(P['w_out'], l)
        s5p = _s5_params(P['ssm_a_re'][l], P['ssm_a_im'][l], P['ssm_log_dt'][l], P['ssm_b_re'][l], P['ssm_b_im'][l],
                         P['ssm_c_re'][l], P['ssm_c_im'][l], nsteps)
        h = _ffn_block(h, h_tail, P, 'ffn1', l, FFN_ROWS, m_p, ns, False)
        mixed_p, st_p = _prompt_mixer(h, P, w_in_l, w_out_l, s5p, l, n, t, rope_p, tm)
        mixed_s, st_s = _sample_mixer(h, m_p, P, w_in_l, w_out_l, s5p, l, caches, rope_s)
        h = _ffn_block(mixed_p, mixed_s, P, 'ffn2', l, FFN_ROWS, m_p, ns, l == DEPTH - 1)
        h_tail = None
        for i in range(6):
            new_p[i].append(st_p[i])
            new_s[i].append(st_s[i])
    y_p, y_s = h
    new_p = [jnp.stack(a, axis=0) for a in new_p]
    new_s = [jnp.stack(a, axis=0) for a in new_s]
    return (y_p.reshape(n, t, D_MODEL), y_s.reshape(ns, ts, D_MODEL), new_p[0], new_s[0], new_p[1], new_s[1],
            new_p[2], new_s[2], new_p[3], new_s[3], new_p[4], new_s[4], new_p[5], new_s[5])


def kernel(x_prompt, x_sample, state_ssm, cache_swa_kv, cache_dil0_kv, cache_dil1_kv, cache_dil2_kv, state_hgrn, ffn1_norm_pre, ffn1_w_gate, ffn1_w_up, ffn1_w_down, ffn1_norm_post, mix_norm_pre, w_in, ssm_a_re, ssm_a_im, ssm_log_dt, ssm_b_re, ssm_b_im, ssm_c_re, ssm_c_im, ssm_d, ssm_w_glu, ssm_b_glu, swa_sinks, hgrn_lower_bounds, out_norm_a, out_norm_b, out_norm_c, out_norm_d, w_out, mix_norm_post, ffn2_norm_pre, ffn2_w_gate, ffn2_w_up, ffn2_w_down, ffn2_norm_post):
    P = dict(ffn1_norm_pre=ffn1_norm_pre, ffn1_w_gate=ffn1_w_gate, ffn1_w_up=ffn1_w_up, ffn1_w_down=ffn1_w_down,
             ffn1_norm_post=ffn1_norm_post, mix_norm_pre=mix_norm_pre, w_in=w_in, ssm_a_re=ssm_a_re,
             ssm_a_im=ssm_a_im, ssm_log_dt=ssm_log_dt, ssm_b_re=ssm_b_re, ssm_b_im=ssm_b_im, ssm_c_re=ssm_c_re,
             ssm_c_im=ssm_c_im, ssm_d=ssm_d, ssm_w_glu=ssm_w_glu, ssm_b_glu=ssm_b_glu, swa_sinks=swa_sinks,
             hgrn_lower_bounds=hgrn_lower_bounds, out_norm_a=out_norm_a, out_norm_b=out_norm_b,
             out_norm_c=out_norm_c, out_norm_d=out_norm_d, w_out=w_out, mix_norm_post=mix_norm_post,
             ffn2_norm_pre=ffn2_norm_pre, ffn2_w_gate=ffn2_w_gate, ffn2_w_up=ffn2_w_up, ffn2_w_down=ffn2_w_down,
             ffn2_norm_post=ffn2_norm_post)
    caches = (state_ssm, cache_swa_kv, cache_dil0_kv, cache_dil1_kv, cache_dil2_kv, state_hgrn)
    return _forward(x_prompt, x_sample, caches, P)
```
